```python
import math
import jax, jax.numpy as jnp
from jax import lax
import numpy as np

D_MODEL = 1024
BATCH = 2
SEQ = 8192
DEPTH = 1
DEC_BATCH = 2
DEC_SEQ = 16384
PAST_LEN = 128

FOURIER_GROUPS = 4
FOURIER_GROUP_DIM = 128
FOURIER_WIDTH = FOURIER_GROUPS * FOURIER_GROUP_DIM
N_HEADS = 8
N_KV_HEADS = 2
HEAD_DIM = 64
ATTN_WIDTH = N_HEADS * HEAD_DIM
KV_WIDTH = N_KV_HEADS * HEAD_DIM
WINDOW = 128
BLOCK = 128
N_BUCKETS = 32
MAX_DISTANCE = 128
MIX_WIDTH = FOURIER_WIDTH + ATTN_WIDTH
IN_WIDTH = FOURIER_WIDTH + ATTN_WIDTH + 2 * KV_WIDTH
N_EXPERTS = 16
CAPACITY_FACTOR = 2
D_EXPERT = 1024
EPS = 1e-6

kernel_name = "hybrid_fnet_swa_ec_encoder"


def rmsnorm(x, g):
    xf = x.astype(jnp.float32)
    y = xf * lax.rsqrt(jnp.mean(xf * xf, axis=-1, keepdims=True) + EPS)
    return (y * g.astype(jnp.float32)).astype(x.dtype)


def t5_bucket(rel):
    nb = N_BUCKETS // 2
    max_exact = nb // 2
    ret = jnp.where(rel > 0, nb, 0)
    n = jnp.abs(rel)
    large = max_exact + (jnp.log(jnp.maximum(n, 1).astype(jnp.float32) / max_exact)
                         / math.log(MAX_DISTANCE / max_exact) * (nb - max_exact)).astype(jnp.int32)
    large = jnp.minimum(large, nb - 1)
    return ret + jnp.where(n < max_exact, n, large)


def fourier_mix(zf, w_fourier):
    B, S, _ = zf.shape
    f = zf.reshape(B, S, FOURIER_GROUPS, FOURIER_GROUP_DIM).astype(jnp.float32)
    f = jnp.fft.fft2(f, axes=(1, 3), norm="ortho").real.astype(zf.dtype)
    f = jnp.einsum('bsgc,gcd->bsgd', f, w_fourier)
    return f.reshape(B, S, FOURIER_WIDTH)


def banded_attention(q, k, v, sink, rel_bias):
    B, S = q.shape[0], q.shape[1]
    nb = S // BLOCK
    G = N_HEADS // N_KV_HEADS
    qb = q.reshape(B, nb, BLOCK, N_KV_HEADS, G, HEAD_DIM)

    def band(t):
        tp = jnp.pad(t, ((0, 0), (BLOCK, BLOCK), (0, 0), (0, 0)))
        tp = tp.reshape(B, nb + 2, BLOCK, N_KV_HEADS, HEAD_DIM)
        return jnp.concatenate([tp[:, :-2], tp[:, 1:-1], tp[:, 2:]], axis=2)

    kb, vb = band(k), band(v)
    s = jnp.einsum('bnqkgd,bnjkd->bnkgqj', qb, kb).astype(jnp.float32) * (HEAD_DIM ** -0.5)

    qi = jnp.arange(BLOCK, dtype=jnp.int32)[:, None]
    kj = jnp.arange(3 * BLOCK, dtype=jnp.int32)[None, :]
    rel = kj - BLOCK - qi
    bias = rel_bias.astype(jnp.float32)[t5_bucket(rel)]
    bias = jnp.transpose(bias, (2, 0, 1)).reshape(N_KV_HEADS, G, BLOCK, 3 * BLOCK)
    s = s + bias

    key_pos = (jnp.arange(nb, dtype=jnp.int32)[:, None, None] - 1) * BLOCK + kj[None]
    valid = (jnp.abs(rel)[None] <= WINDOW) & (key_pos >= 0) & (key_pos < S)
    s = jnp.where(valid[None, :, None, None], s, -jnp.inf)

    sink_l = sink.astype(jnp.float32).reshape(N_KV_HEADS, G)[:, :, None, None]
    m = jnp.maximum(jnp.max(s, axis=-1, keepdims=True), sink_l)
    p = jnp.exp(s - m)
    denom = jnp.sum(p, axis=-1, keepdims=True) + jnp.exp(sink_l - m)
    o = jnp.einsum('bnkgqj,bnjkd->bnqkgd', (p / denom).astype(v.dtype), vb)
    return o.reshape(B, S, ATTN_WIDTH)


def token_mixer(h, w_in, w_fourier, sink, rel_bias):
    B, S, _ = h.shape
    z = h @ w_in
    zf = z[..., :FOURIER_WIDTH]
    q = z[..., FOURIER_WIDTH:FOURIER_WIDTH + ATTN_WIDTH].reshape(B, S, N_HEADS, HEAD_DIM)
    k = z[..., FOURIER_WIDTH + ATTN_WIDTH:FOURIER_WIDTH + ATTN_WIDTH + KV_WIDTH].reshape(B, S, N_KV_HEADS, HEAD_DIM)
    v = z[..., FOURIER_WIDTH + ATTN_WIDTH + KV_WIDTH:].reshape(B, S, N_KV_HEADS, HEAD_DIM)
    f = fourier_mix(zf, w_fourier)
    a = banded_attention(q, k, v, sink, rel_bias)
    return jnp.concatenate([f, a], axis=-1)


def expert_choice_moe(h, w_router, w_gate, w_up, w_down):
    B, S, D = h.shape
    n_tok = B * S
    cap = CAPACITY_FACTOR * n_tok // N_EXPERTS
    t = h.reshape(n_tok, D)
    aff = jax.nn.softmax((t @ w_router).astype(jnp.float32), axis=-1)
    gates, idx = lax.top_k(aff.T, cap)
    xe = t[idx]
    hid = jax.nn.silu(jnp.einsum('ecd,edf->ecf', xe, w_gate)) * jnp.einsum('ecd,edf->ecf', xe, w_up)
    ye = jnp.einsum('ecf,efd->ecd', hid, w_down) * gates[..., None].astype(h.dtype)
    out = jnp.zeros((n_tok, D), h.dtype).at[idx.reshape(-1)].add(ye.reshape(-1, D))
    return out.reshape(B, S, D)


def setup_inputs(seed: int = 0) -> dict:
    key = jax.random.key(seed)
    ks = jax.random.split(key, 16)
    f32 = jnp.float32
    nrm = lambda k, shape, scale: jax.random.normal(k, shape, f32) * scale
    return {
        "x_prompt": nrm(ks[0], (BATCH, SEQ, D_MODEL), 1.0),
        "x_sample": nrm(ks[1], (DEC_BATCH, DEC_SEQ, D_MODEL), 1.0),
        "w_in": nrm(ks[2], (DEPTH, D_MODEL, IN_WIDTH), D_MODEL ** -0.5),
        "w_fourier": nrm(ks[3], (DEPTH, FOURIER_GROUPS, FOURIER_GROUP_DIM, FOURIER_GROUP_DIM), FOURIER_GROUP_DIM ** -0.5),
        "attn_sink": nrm(ks[4], (DEPTH, N_HEADS), 0.5),
        "rel_bias": nrm(ks[5], (N_BUCKETS, N_HEADS), 0.1),
        "w_out": nrm(ks[6], (DEPTH, MIX_WIDTH, D_MODEL), MIX_WIDTH ** -0.5),
        "norm_mix": 1.0 + nrm(ks[7], (DEPTH, D_MODEL), 0.02),
        "norm_ffn": 1.0 + nrm(ks[8], (DEPTH, D_MODEL), 0.02),
        "w_router": nrm(ks[9], (DEPTH, D_MODEL, N_EXPERTS), D_MODEL ** -0.5),
        "w_gate": nrm(ks[10], (DEPTH, N_EXPERTS, D_MODEL, D_EXPERT), D_MODEL ** -0.5),
        "w_up": nrm(ks[11], (DEPTH, N_EXPERTS, D_MODEL, D_EXPERT), D_MODEL ** -0.5),
        "w_down": nrm(ks[12], (DEPTH, N_EXPERTS, D_EXPERT, D_MODEL), D_EXPERT ** -0.5),
        "norm_final": 1.0 + nrm(ks[13], (D_MODEL,), 0.02),
    }


def reference(x_prompt, x_sample, w_in, w_fourier, attn_sink, rel_bias, w_out, norm_mix, norm_ffn,
              w_router, w_gate, w_up, w_down, norm_final):
    def trunk(x):
        for l in range(DEPTH):
            h = rmsnorm(x, norm_mix[l])
            x = x + token_mixer(h, w_in[l], w_fourier[l], attn_sink[l], rel_bias) @ w_out[l]
            h = rmsnorm(x, norm_ffn[l])
            x = x + expert_choice_moe(h, w_router[l], w_gate[l], w_up[l], w_down[l])
        return rmsnorm(x, norm_final)

    y_prompt = trunk(x_prompt)
    y_sample = trunk(x_sample)
    return (y_prompt, y_sample)
```

```python
import functools
import math

import numpy as np
import jax
import jax.numpy as jnp
from jax import lax
from jax.experimental import pallas as pl
from jax.experimental.pallas import tpu as pltpu

F32 = jnp.float32
BF16 = jnp.bfloat16
I32 = jnp.int32

D_MODEL = 1024
FOURIER_GROUPS = 4
GROUP_DIM = 128
FOURIER_WIDTH = FOURIER_GROUPS * GROUP_DIM
N_HEADS = 8
N_KV_HEADS = 2
HEAD_DIM = 64
ATTN_WIDTH = N_HEADS * HEAD_DIM
KV_WIDTH = N_KV_HEADS * HEAD_DIM
WINDOW = 128
BLOCK = 128
N_BUCKETS = 32
MAX_DISTANCE = 128
IN_WIDTH = FOURIER_WIDTH + ATTN_WIDTH + 2 * KV_WIDTH
N_EXPERTS = 16
CAPACITY_FACTOR = 2
D_EXPERT = 1024
EPS = 1e-6

LANES = 128
PREFIX_CHUNK = 256
VMEM_LIMIT = 56 * 1024 * 1024


def _params(sem, vmem=None):
    return pltpu.CompilerParams(dimension_semantics=sem, vmem_limit_bytes=vmem)


def _inproj_body(x_ref, g_ref, w_ref, zf_ref, q_ref, k_ref, v_ref):
    x = x_ref[...]
    ms = jnp.mean(x * x, axis=-1, keepdims=True)
    h = (x * lax.rsqrt(ms + EPS)) * g_ref[...]
    z = jnp.dot(h.astype(BF16), w_ref[...], preferred_element_type=F32)
    o = FOURIER_WIDTH
    zf_ref[...] = z[:, :o].astype(BF16)
    q_ref[...] = (z[:, o:o + ATTN_WIDTH] * (HEAD_DIM ** -0.5)).astype(BF16)
    o += ATTN_WIDTH
    k_ref[...] = z[:, o:o + KV_WIDTH].astype(BF16)
    o += KV_WIDTH
    v_ref[...] = z[:, o:o + KV_WIDTH].astype(BF16)


def _inproj(xf, gain, w_in, tm):
    n = xf.shape[0]
    row = lambda w: pl.BlockSpec((tm, w), lambda i: (i, 0))
    return pl.pallas_call(
        _inproj_body,
        grid=(n // tm,),
        in_specs=[row(D_MODEL),
                  pl.BlockSpec((1, D_MODEL), lambda i: (0, 0)),
                  pl.BlockSpec((D_MODEL, IN_WIDTH), lambda i: (0, 0))],
        out_specs=[row(FOURIER_WIDTH), row(ATTN_WIDTH), row(KV_WIDTH), row(KV_WIDTH)],
        out_shape=[jax.ShapeDtypeStruct((n, FOURIER_WIDTH), BF16),
                   jax.ShapeDtypeStruct((n, ATTN_WIDTH), BF16),
                   jax.ShapeDtypeStruct((n, KV_WIDTH), BF16),
                   jax.ShapeDtypeStruct((n, KV_WIDTH), BF16)],
        compiler_params=_params(("parallel",), VMEM_LIMIT),
        name="inproj",
    )(xf, gain, w_in)


def _dft_tables(s):
    s2 = BLOCK
    s1 = s // s2
    a = np.arange(s1, dtype=np.float64)
    ang1 = 2.0 * np.pi * np.outer(a, a) / s1
    f1 = np.concatenate([np.cos(ang1), -np.sin(ang1)], axis=0)
    b = np.arange(s2, dtype=np.float64)
    ang2 = 2.0 * np.pi * np.outer(b, b) / s2
    angt = 2.0 * np.pi * np.outer(a, b) / s
    c = np.arange(GROUP_DIM, dtype=np.float64)
    angc = 2.0 * np.pi * np.outer(c, c) / GROUP_DIM
    cs = np.concatenate([np.cos(angc), np.sin(angc)], axis=0)
    return dict(
        f1=jnp.asarray(f1, BF16),
        f2r=jnp.asarray(np.cos(ang2), F32), f2i=jnp.asarray(-np.sin(ang2), F32),
        twr=jnp.asarray(np.cos(angt), F32), twi=jnp.asarray(-np.sin(angt), F32),
        cs=jnp.asarray(cs, BF16))


def _dft1_body(f_ref, x_ref, o_ref):
    s1 = x_ref.shape[1]
    res = jnp.dot(f_ref[...], x_ref[0], preferred_element_type=F32)
    o_ref[0, 0] = res[:s1].astype(BF16)
    o_ref[0, 1] = res[s1:].astype(BF16)


def _dft1(zf3, f1, cb):
    b, s1, w = zf3.shape
    return pl.pallas_call(
        _dft1_body,
        grid=(b, w // cb),
        in_specs=[pl.BlockSpec((2 * s1, s1), lambda i, j: (0, 0)),
                  pl.BlockSpec((1, s1, cb), lambda i, j: (i, 0, j))],
        out_specs=pl.BlockSpec((1, 2, s1, cb), lambda i, j: (i, 0, 0, j)),
        out_shape=jax.ShapeDtypeStruct((b, 2, s1, w), BF16),
        compiler_params=_params(("parallel", "parallel"), VMEM_LIMIT),
        name="dft1",
    )(f1, zf3)


def _dft2_body(a_ref, f2r_ref, f2i_ref, twr_ref, twi_ref, cs_ref, wf_ref, o_ref, *, kb, scale):
    s2 = BLOCK
    fr = f2r_ref[...]
    fi = f2i_ref[...]
    for j in range(kb):
        tr = twr_ref[0, j:j + 1, :]
        ti = twi_ref[0, j:j + 1, :]
        gr = fr * tr - fi * ti
        gi = fr * ti + fi * tr
        lhs = jnp.concatenate(
            [jnp.concatenate([gr, -gi], axis=1), jnp.concatenate([gi, gr], axis=1)],
            axis=0).astype(BF16)
        rhs = jnp.concatenate([a_ref[0, 0, j], a_ref[0, 1, j]], axis=0)
        y = jnp.dot(lhs, rhs, preferred_element_type=F32)
        outs = []
        for g in range(FOURIER_GROUPS):
            sl = slice(g * GROUP_DIM, (g + 1) * GROUP_DIM)
            cat = jnp.concatenate([y[:s2, sl], y[s2:, sl]], axis=1).astype(BF16)
            re = jnp.dot(cat, cs_ref[...], preferred_element_type=F32) * scale
            outs.append(jnp.dot(re.astype(BF16), wf_ref[g], preferred_element_type=F32))
        o_ref[0, :, j, :] = jnp.concatenate(outs, axis=1)


def _dft2(a5, tabs, wf, kb, scale):
    b, _, s1, s2, w = a5.shape
    const = lambda shape: pl.BlockSpec(shape, lambda i, j: (0,) * len(shape))
    twr = tabs["twr"].reshape(s1 // kb, kb, s2)
    twi = tabs["twi"].reshape(s1 // kb, kb, s2)
    return pl.pallas_call(
        functools.partial(_dft2_body, kb=kb, scale=scale),
        grid=(b, s1 // kb),
        in_specs=[pl.BlockSpec((1, 2, kb, s2, w), lambda i, j: (i, 0, j, 0, 0)),
                  const((s2, s2)), const((s2, s2)),
                  pl.BlockSpec((1, kb, s2), lambda i, j: (j, 0, 0)),
                  pl.BlockSpec((1, kb, s2), lambda i, j: (j, 0, 0)),
                  const((2 * GROUP_DIM, GROUP_DIM)),
                  const((FOURIER_GROUPS, GROUP_DIM, GROUP_DIM))],
        out_specs=pl.BlockSpec((1, s2, kb, w), lambda i, j: (i, 0, j, 0)),
        out_shape=jax.ShapeDtypeStruct((b, s2, s1, w), F32),
        compiler_params=_params(("parallel", "parallel"), VMEM_LIMIT),
        name="dft2",
    )(a5, tabs["f2r"], tabs["f2i"], twr, twi, tabs["cs"], wf)


def _fourier(zf, wf_bf16):
    b, s, w = zf.shape
    s2 = BLOCK
    s1 = s // s2
    tabs = _dft_tables(s)
    a = _dft1(zf.reshape(b, s1, s2 * w), tabs["f1"], cb=2048)
    f = _dft2(a.reshape(b, 2, s1, s2, w), tabs, wf_bf16, kb=8,
              scale=1.0 / math.sqrt(s * GROUP_DIM))
    return f.reshape(b, s, w)


def _bucket_table():
    qi = np.arange(BLOCK)[:, None]
    kj = np.arange(3 * BLOCK)[None, :]
    rel = kj - BLOCK - qi
    nb = N_BUCKETS // 2
    max_exact = nb // 2
    n = np.abs(rel)
    large = max_exact + np.floor(
        np.log(np.maximum(n, 1).astype(np.float64) / max_exact)
        / math.log(MAX_DISTANCE / max_exact) * (nb - max_exact) + 1e-9).astype(np.int64)
    large = np.minimum(large, nb - 1)
    bucket = np.where(rel > 0, nb, 0) + np.where(n < max_exact, n, large)
    return np.where(n <= WINDOW, bucket, -1).astype(np.int32)


def _bias_body(rb_ref, bucket_ref, o_ref):
    bk = bucket_ref[...]
    for h in range(N_HEADS):
        acc = jnp.full(bk.shape, -jnp.inf, F32)
        for b in range(N_BUCKETS):
            acc = jnp.where(bk == b, rb_ref[b, h], acc)
        o_ref[h] = acc


def _bias_table(rel_bias):
    bucket = jnp.asarray(_bucket_table())
    return pl.pallas_call(
        _bias_body,
        in_specs=[pl.BlockSpec(memory_space=pltpu.SMEM),
                  pl.BlockSpec(bucket.shape, lambda: (0, 0))],
        out_specs=pl.BlockSpec((N_HEADS,) + bucket.shape, lambda: (0, 0, 0)),
        out_shape=jax.ShapeDtypeStruct((N_HEADS,) + bucket.shape, F32),
        name="bias_table",
    )(rel_bias.astype(F32), bucket)


def _attn_body(sink_ref, q_ref, kp_ref, kc_ref, kn_ref, vp_ref, vc_ref, vn_ref, bias_ref,
               o_ref, *, tq):
    j = pl.program_id(1)
    last = pl.num_programs(1) - 1
    nsb = tq // BLOCK
    kext = jnp.concatenate([kp_ref[0, tq - BLOCK:, :], kc_ref[0], kn_ref[0, :BLOCK, :]], axis=0)
    vext = jnp.concatenate([vp_ref[0, tq - BLOCK:, :], vc_ref[0], vn_ref[0, :BLOCK, :]], axis=0)
    col = lax.broadcasted_iota(I32, (BLOCK, 3 * BLOCK), 1)
    group = N_HEADS // N_KV_HEADS
    for sb in range(nsb):
        kb = kext[sb * BLOCK:(sb + 3) * BLOCK]
        vb = vext[sb * BLOCK:(sb + 3) * BLOCK]
        outs = []
        for h in range(N_HEADS):
            kh = h // group
            qh = q_ref[0, sb * BLOCK:(sb + 1) * BLOCK, h * HEAD_DIM:(h + 1) * HEAD_DIM]
            s = lax.dot_general(qh, kb[:, kh * HEAD_DIM:(kh + 1) * HEAD_DIM],
                                (((1,), (1,)), ((), ())), preferred_element_type=F32)
            s = s + bias_ref[h]
            if sb == 0:
                s = jnp.where((col >= BLOCK) | (j > 0), s, -jnp.inf)
            if sb == nsb - 1:
                s = jnp.where((col < 2 * BLOCK) | (j < last), s, -jnp.inf)
            sink = sink_ref[h]
            m = jnp.maximum(jnp.max(s, axis=-1, keepdims=True), sink)
            p = jnp.exp(s - m)
            denom = jnp.sum(p, axis=-1, keepdims=True) + jnp.exp(sink - m)
            pn = (p / denom).astype(BF16)
            outs.append(jnp.dot(pn, vb[:, kh * HEAD_DIM:(kh + 1) * HEAD_DIM],
                                preferred_element_type=F32))
        o_ref[0, sb * BLOCK:(sb + 1) * BLOCK, :] = jnp.concatenate(outs, axis=1).astype(BF16)


def _attention(q, k, v, bias, sink, tq):
    b, s, _ = q.shape
    nt = s // tq
    kv = lambda f: pl.BlockSpec((1, tq, KV_WIDTH), f)
    prev = lambda i, j: (i, jnp.maximum(j - 1, 0), 0)
    cur = lambda i, j: (i, j, 0)
    nxt = lambda i, j: (i, jnp.minimum(j + 1, nt - 1), 0)
    return pl.pallas_call(
        functools.partial(_attn_body, tq=tq),
        grid=(b, nt),
        in_specs=[pl.BlockSpec(memory_space=pltpu.SMEM),
                  pl.BlockSpec((1, tq, ATTN_WIDTH), cur),
                  kv(prev), kv(cur), kv(nxt), kv(prev), kv(cur), kv(nxt),
                  pl.BlockSpec(bias.shape, lambda i, j: (0, 0, 0))],
        out_specs=pl.BlockSpec((1, tq, ATTN_WIDTH), cur),
        out_shape=jax.ShapeDtypeStruct((b, s, ATTN_WIDTH), BF16),
        compiler_params=_params(("parallel", "parallel"), VMEM_LIMIT),
        name="attn",
    )(sink.astype(F32), q, k, k, k, v, v, v, bias)


def _outproj_body(x_ref, f_ref, a_ref, wo_ref, g_ref, wr_ref, x1_ref, h2_ref, aff_ref):
    mix = jnp.concatenate([f_ref[...].astype(BF16), a_ref[...]], axis=1)
    x1 = x_ref[...] + jnp.dot(mix, wo_ref[...], preferred_element_type=F32)
    x1_ref[...] = x1
    ms = jnp.mean(x1 * x1, axis=-1, keepdims=True)
    h2 = (x1 * lax.rsqrt(ms + EPS)) * g_ref[...]
    h2_ref[...] = h2
    hi = h2.astype(BF16)
    lo = (h2 - hi.astype(F32)).astype(BF16)
    nt = (((1,), (1,)), ((), ()))
    o1 = lax.dot_general(wr_ref[...], hi, nt, preferred_element_type=F32)
    o2 = lax.dot_general(wr_ref[:N_EXPERTS], lo, nt, preferred_element_type=F32)
    logits = o1[:N_EXPERTS] + (o1[N_EXPERTS:] + o2)
    e = jnp.exp(logits - jnp.max(logits, axis=0, keepdims=True))
    aff_ref[...] = e / jnp.sum(e, axis=0, keepdims=True)


def _outproj(xf, f, a, w_out, gain, wr_split, tm):
    n = xf.shape[0]
    row = lambda w: pl.BlockSpec((tm, w), lambda i: (i, 0))
    const = lambda shape: pl.BlockSpec(shape, lambda i: (0, 0))
    return pl.pallas_call(
        _outproj_body,
        grid=(n // tm,),
        in_specs=[row(D_MODEL), row(FOURIER_WIDTH), row(ATTN_WIDTH),
                  const((D_MODEL, D_MODEL)), const((1, D_MODEL)),
                  const((2 * N_EXPERTS, D_MODEL))],
        out_specs=[row(D_MODEL), row(D_MODEL),
                   pl.BlockSpec((N_EXPERTS, tm), lambda i: (0, i))],
        out_shape=[jax.ShapeDtypeStruct((n, D_MODEL), F32),
                   jax.ShapeDtypeStruct((n, D_MODEL), F32),
                   jax.ShapeDtypeStruct((N_EXPERTS, n), F32)],
        compiler_params=_params(("parallel",), VMEM_LIMIT),
        name="outproj",
    )(xf, f, a, w_out, gain, wr_split)


def _route_body(aff_ref, idx_ref, pos_ref, gate_ref, tok_ref,
                d_a, d_b, n_a, n_b, p_a, p_b, g_a, g_b, *, n, cap):
    e = N_EXPERTS
    ch = PREFIX_CHUNK

    def search(i, tau):
        cand = tau | jnp.left_shift(jnp.int32(1), 30 - i)
        keys = lax.bitcast_convert_type(aff_ref[...], I32)
        cnt = jnp.sum((keys >= cand).astype(I32), axis=1, keepdims=True)
        return jnp.where(cnt >= cap, cand, tau)

    tau = lax.fori_loop(0, 31, search, jnp.zeros((e, 1), I32))
    keys = lax.bitcast_convert_type(aff_ref[...], I32)
    n_gt = jnp.sum((keys > tau).astype(I32), axis=1, keepdims=True)
    need = (cap - n_gt).astype(F32)

    r = lax.broadcasted_iota(I32, (ch, ch), 0)
    c = lax.broadcasted_iota(I32, (ch, ch), 1)
    upper = (r < c).astype(BF16)
    r16 = lax.broadcasted_iota(I32, (e, e), 0)
    c16 = lax.broadcasted_iota(I32, (e, e), 1)
    lower = (c16 < r16).astype(BF16)

    carry_eq = jnp.zeros((e, 1), F32)
    carry_m = jnp.zeros((2 * e, 1), F32)
    for ci in range(n // ch):
        sl = slice(ci * ch, (ci + 1) * ch)
        k = lax.bitcast_convert_type(aff_ref[:, sl], I32)
        gt = k > tau
        eq = (k == tau).astype(F32)
        eq_ex = jnp.dot(eq.astype(BF16), upper, preferred_element_type=F32) + carry_eq
        carry_eq = carry_eq + jnp.sum(eq, axis=1, keepdims=True)
        m = jnp.where(gt | ((eq > 0.0) & (eq_ex < need)), 1.0, 0.0).astype(F32)
        cnt = jnp.sum(m, axis=0, keepdims=True)
        stacked = jnp.concatenate([m, cnt, jnp.zeros((e - 1, ch), F32)], axis=0)
        pre = jnp.dot(stacked.astype(BF16), upper, preferred_element_type=F32) + carry_m
        carry_m = carry_m + jnp.sum(stacked, axis=1, keepdims=True)
        rank = pre[:e]
        tokoff = pre[e:e + 1]
        ex_e = jnp.dot(lower, m.astype(BF16), preferred_element_type=F32)
        tok_ref[0:1, sl] = tokoff.astype(I32)
        tok_ref[1:2, sl] = (tokoff + cnt).astype(I32)
        lane = lax.broadcasted_iota(I32, (e, ch), 1) + ci * ch
        d_a[:, sl] = jnp.where(m > 0.0, lane - rank.astype(I32), 0)
        n_a[:, sl] = lane
        p_a[:, sl] = (tokoff + ex_e).astype(I32)
    g_a[...] = aff_ref[...]

    bufs = [(d_a, n_a, p_a, g_a), (d_b, n_b, p_b, g_b)]
    for b in range(int(math.log2(n))):
        src, dst = bufs[b % 2], bufs[(b + 1) % 2]
        sh = 1 << b
        d = src[0][...]
        moving = (lax.shift_right_logical(d, b) & 1) == 1
        d_in = pltpu.roll(d, n - sh, axis=1)
        arrive = (lax.shift_right_logical(d_in, b) & 1) == 1
        dst[0][...] = jnp.where(arrive, d_in, jnp.where(moving, 0, d))
        for t in (1, 2, 3):
            v = src[t][...]
            dst[t][...] = jnp.where(arrive, pltpu.roll(v, n - sh, axis=1), v)
    fin = bufs[int(math.log2(n)) % 2]
    idx_ref[...] = fin[1][:, :cap]
    pos_ref[...] = fin[2][:, :cap]
    gate_ref[...] = fin[3][:, :cap]


def _route(aff_t, cap):
    e, n = aff_t.shape
    assert n & (n - 1) == 0 or True
    full = lambda shape: pl.BlockSpec(shape, lambda: (0, 0))
    big_i = pltpu.VMEM((e, n), I32)
    return pl.pallas_call(
        functools.partial(_route_body, n=n, cap=cap),
        in_specs=[full((e, n))],
        out_specs=[full((e, cap)), full((e, cap)), full((e, cap)), full((2, n))],
        out_shape=[jax.ShapeDtypeStruct((e, cap), I32),
                   jax.ShapeDtypeStruct((e, cap), I32),
                   jax.ShapeDtypeStruct((e, cap), F32),
                   jax.ShapeDtypeStruct((2, n), I32)],
        scratch_shapes=[big_i, big_i, big_i, big_i, big_i, big_i,
                        pltpu.VMEM((e, n), F32), pltpu.VMEM((e, n), F32)],
        compiler_params=_params(None, VMEM_LIMIT),
        name="route",
    )(aff_t)


def _row_to_col(row, tm):
    r = lax.broadcasted_iota(I32, (tm, tm), 0)
    c = lax.broadcasted_iota(I32, (tm, tm), 1)
    return jnp.sum(jnp.where(r == c, row, jnp.zeros_like(row)), axis=1, keepdims=True)


def _ffn_body(idx_ref, pos_ref, gate_ref, h_hbm, wg_ref, wu_ref, wd_ref, ys_hbm,
              xbuf, ybuf, sem_in, sem_out, *, tm):
    def gather(i, carry):
        tok = idx_ref[0, 0, i]
        pltpu.make_async_copy(h_hbm.at[pl.ds(tok, 1), :], xbuf.at[pl.ds(i, 1), :], sem_in).start()
        return carry

    lax.fori_loop(0, tm, gather, 0)
    pltpu.make_async_copy(h_hbm.at[pl.ds(0, tm), :], xbuf, sem_in).wait()

    x = xbuf[...].astype(BF16)
    g = jnp.dot(x, wg_ref[0], preferred_element_type=F32)
    u = jnp.dot(x, wu_ref[0], preferred_element_type=F32)
    hid = (g * jax.nn.sigmoid(g) * u).astype(BF16)
    y = jnp.dot(hid, wd_ref[0], preferred_element_type=F32)
    y = y * _row_to_col(gate_ref[0], tm)
    ybuf[...] = y.astype(BF16).astype(F32)

    def scatter(i, carry):
        slot = pos_ref[0, 0, i]
        pltpu.make_async_copy(ybuf.at[pl.ds(i, 1), :], ys_hbm.at[pl.ds(slot, 1), :], sem_out).start()
        return carry

    lax.fori_loop(0, tm, scatter, 0)
    pltpu.make_async_copy(ybuf, ys_hbm.at[pl.ds(0, tm), :], sem_out).wait()


def _ffn(idx_c, pos_c, gate_c, h2, wg, wu, wd, tm):
    e, cap = idx_c.shape
    rt = cap // tm
    n_slots = e * cap
    idx3 = idx_c.reshape(e * rt, 1, tm)
    pos3 = pos_c.reshape(e * rt, 1, tm)
    gate3 = gate_c.reshape(e * rt, 1, tm)
    tile = lambda i, j: (i * rt + j, 0, 0)
    wspec = lambda a, b: pl.BlockSpec((1, a, b), lambda i, j: (i, 0, 0))
    return pl.pallas_call(
        functools.partial(_ffn_body, tm=tm),
        grid=(e, rt),
        in_specs=[pl.BlockSpec((1, 1, tm), tile, memory_space=pltpu.SMEM),
                  pl.BlockSpec((1, 1, tm), tile, memory_space=pltpu.SMEM),
                  pl.BlockSpec((1, 1, tm), tile),
                  pl.BlockSpec(memory_space=pl.ANY),
                  wspec(D_MODEL, D_EXPERT), wspec(D_MODEL, D_EXPERT), wspec(D_EXPERT, D_MODEL)],
        out_specs=pl.BlockSpec(memory_space=pl.ANY),
        out_shape=jax.ShapeDtypeStruct((n_slots, D_MODEL), F32),
        scratch_shapes=[pltpu.VMEM((tm, D_MODEL), F32), pltpu.VMEM((tm, D_MODEL), F32),
                        pltpu.SemaphoreType.DMA, pltpu.SemaphoreType.DMA],
        compiler_params=_params(("arbitrary", "arbitrary"), VMEM_LIMIT),
        name="ffn",
    )(idx3, pos3, gate3, h2, wg, wu, wd)


def _combine_body(toff_ref, x1_ref, tok_ref, g_ref, ys_hbm, o_ref, win, sem, *, tt, ws, n_slots):
    i = pl.program_id(0)
    lo = toff_ref[i]
    hi = toff_ref[i + 1]
    base = (lo // 8) * 8
    nwin = (hi - base + ws - 1) // ws
    t_lo = _row_to_col(tok_ref[0:1, :], tt)
    t_hi = _row_to_col(tok_ref[1:2, :], tt)
    lane = lax.broadcasted_iota(I32, (tt, ws), 1)

    def window(w, acc):
        nominal = base + w * ws
        start = pl.multiple_of(jnp.minimum(nominal, n_slots - ws), 8)
        cp = pltpu.make_async_copy(ys_hbm.at[pl.ds(start, ws), :], win, sem)
        cp.start()
        cp.wait()
        slot = lane + start
        onehot = (slot >= t_lo) & (slot < t_hi) & (slot >= nominal)
        sel = jnp.where(onehot, 1.0, 0.0).astype(BF16)
        return acc + jnp.dot(sel, win[...].astype(BF16), preferred_element_type=F32)

    moe = lax.fori_loop(0, nwin, window, jnp.zeros((tt, D_MODEL), F32))
    x2 = x1_ref[...] + moe
    ms = jnp.mean(x2 * x2, axis=-1, keepdims=True)
    o_ref[...] = (x2 * lax.rsqrt(ms + EPS)) * g_ref[...]


def _combine(x1, ys, tokinfo, gain, tt, ws):
    n = x1.shape[0]
    n_slots = ys.shape[0]
    tile_off = jnp.concatenate([tokinfo[0, ::tt], jnp.full((1,), n_slots, I32)])
    return pl.pallas_call(
        functools.partial(_combine_body, tt=tt, ws=ws, n_slots=n_slots),
        grid_spec=pltpu.PrefetchScalarGridSpec(
            num_scalar_prefetch=1,
            grid=(n // tt,),
            in_specs=[pl.BlockSpec((tt, D_MODEL), lambda i, s: (i, 0)),
                      pl.BlockSpec((2, tt), lambda i, s: (0, i)),
                      pl.BlockSpec((1, D_MODEL), lambda i, s: (0, 0)),
                      pl.BlockSpec(memory_space=pl.ANY)],
            out_specs=pl.BlockSpec((tt, D_MODEL), lambda i, s: (i, 0)),
            scratch_shapes=[pltpu.VMEM((ws, D_MODEL), F32), pltpu.SemaphoreType.DMA]),
        out_shape=jax.ShapeDtypeStruct((n, D_MODEL), F32),
        compiler_params=_params(("arbitrary",), VMEM_LIMIT),
        name="combine",
    )(tile_off, x1, tokinfo, gain, ys)


def _tile(n, pref):
    t = pref
    while n % t:
        t //= 2
    return t


def _trunk(x, p):
    b, s, d = x.shape
    n = b * s
    cap = CAPACITY_FACTOR * n // N_EXPERTS
    xf = x.reshape(n, d)
    tm = _tile(n, 512)
    zf, q, k, v = _inproj(xf, p["norm_mix"], p["w_in"], tm)
    f = _fourier(zf.reshape(b, s, FOURIER_WIDTH), p["w_fourier"])
    a = _attention(q.reshape(b, s, ATTN_WIDTH), k.reshape(b, s, KV_WIDTH),
                   v.reshape(b, s, KV_WIDTH), p["bias"], p["sink"], _tile(s, 512))
    x1, h2, aff_t = _outproj(xf, f.reshape(n, FOURIER_WIDTH), a.reshape(n, ATTN_WIDTH),
                             p["w_out"], p["norm_ffn"], p["w_router"], tm)
    idx_c, pos_c, gate_c, tokinfo = _route(aff_t, cap)
    ys = _ffn(idx_c, pos_c, gate_c, h2, p["w_gate"], p["w_up"], p["w_down"], _tile(cap, 512))
    y = _combine(x1, ys, tokinfo, p["norm_final"], _tile(n, 256), min(512, 2 * n))
    return y.reshape(b, s, d)


def kernel(x_prompt, x_sample, w_in, w_fourier, attn_sink, rel_bias, w_out, norm_mix, norm_ffn,
           w_router, w_gate, w_up, w_down, norm_final):
    assert w_in.shape[0] == 1
    wr = w_router[0].T
    wr_hi = wr.astype(BF16)
    wr_lo = (wr - wr_hi.astype(F32)).astype(BF16)
    p = dict(
        norm_mix=norm_mix[0].reshape(1, D_MODEL),
        norm_ffn=norm_ffn[0].reshape(1, D_MODEL),
        norm_final=norm_final.reshape(1, D_MODEL),
        w_in=w_in[0].astype(BF16),
        w_fourier=w_fourier[0].astype(BF16),
        w_out=w_out[0].astype(BF16),
        w_router=jnp.concatenate([wr_hi, wr_lo], axis=0),
        w_gate=w_gate[0].astype(BF16),
        w_up=w_up[0].astype(BF16),
        w_down=w_down[0].astype(BF16),
        sink=attn_sink[0],
        bias=_bias_table(rel_bias),
    )
    return _trunk(x_prompt, p), _trunk(x_sample, p)
```

```python
import functools
import math

import numpy as np
import jax
import jax.numpy as jnp
from jax import lax
from jax.experimental import pallas as pl
from jax.experimental.pallas import tpu as pltpu

F32 = jnp.float32
BF16 = jnp.bfloat16
I32 = jnp.int32

D_MODEL = 1024
FOURIER_GROUPS = 4
GROUP_DIM = 128
FOURIER_WIDTH = FOURIER_GROUPS * GROUP_DIM
N_HEADS = 8
N_KV_HEADS = 2
HEAD_DIM = 64
ATTN_WIDTH = N_HEADS * HEAD_DIM
KV_WIDTH = N_KV_HEADS * HEAD_DIM
WINDOW = 128
BLOCK = 128
N_BUCKETS = 32
MAX_DISTANCE = 128
IN_WIDTH = FOURIER_WIDTH + ATTN_WIDTH + 2 * KV_WIDTH
N_EXPERTS = 16
CAPACITY_FACTOR = 2
D_EXPERT = 1024
EPS = 1e-6

LANES = 128
PREFIX_CHUNK = 256
VMEM_LIMIT = 56 * 1024 * 1024


def _params(sem, vmem=None):
    return pltpu.CompilerParams(dimension_semantics=sem, vmem_limit_bytes=vmem)


def _inproj_body(x_ref, g_ref, w_ref, zf_ref, q_ref, k_ref, v_ref):
    x = x_ref[...]
    ms = jnp.mean(x * x, axis=-1, keepdims=True)
    h = (x * lax.rsqrt(ms + EPS)) * g_ref[...]
    z = jnp.dot(h.astype(BF16), w_ref[...], preferred_element_type=F32)
    o = FOURIER_WIDTH
    zf_ref[...] = z[:, :o].astype(BF16)
    q_ref[...] = (z[:, o:o + ATTN_WIDTH] * (HEAD_DIM ** -0.5)).astype(BF16)
    o += ATTN_WIDTH
    k_ref[...] = z[:, o:o + KV_WIDTH].astype(BF16)
    o += KV_WIDTH
    v_ref[...] = z[:, o:o + KV_WIDTH].astype(BF16)


def _inproj(xf, gain, w_in, tm):
    n = xf.shape[0]
    row = lambda w: pl.BlockSpec((tm, w), lambda i: (i, 0))
    return pl.pallas_call(
        _inproj_body,
        grid=(n // tm,),
        in_specs=[row(D_MODEL),
                  pl.BlockSpec((1, D_MODEL), lambda i: (0, 0)),
                  pl.BlockSpec((D_MODEL, IN_WIDTH), lambda i: (0, 0))],
        out_specs=[row(FOURIER_WIDTH), row(ATTN_WIDTH), row(KV_WIDTH), row(KV_WIDTH)],
        out_shape=[jax.ShapeDtypeStruct((n, FOURIER_WIDTH), BF16),
                   jax.ShapeDtypeStruct((n, ATTN_WIDTH), BF16),
                   jax.ShapeDtypeStruct((n, KV_WIDTH), BF16),
                   jax.ShapeDtypeStruct((n, KV_WIDTH), BF16)],
        compiler_params=_params(("parallel",), VMEM_LIMIT),
        name="inproj",
    )(xf, gain, w_in)


def _dft_tables(s):
    s2 = BLOCK
    s1 = s // s2
    a = np.arange(s1, dtype=np.float64)
    ang1 = 2.0 * np.pi * np.outer(a, a) / s1
    f1 = np.concatenate([np.cos(ang1), -np.sin(ang1)], axis=0)
    b = np.arange(s2, dtype=np.float64)
    ang2 = 2.0 * np.pi * np.outer(b, b) / s2
    angt = 2.0 * np.pi * np.outer(a, b) / s
    c = np.arange(GROUP_DIM, dtype=np.float64)
    angc = 2.0 * np.pi * np.outer(c, c) / GROUP_DIM
    cs = np.concatenate([np.cos(angc), np.sin(angc)], axis=0)
    return dict(
        f1=jnp.asarray(f1, BF16),
        f2r=jnp.asarray(np.cos(ang2), F32), f2i=jnp.asarray(-np.sin(ang2), F32),
        twr=jnp.asarray(np.cos(angt), F32), twi=jnp.asarray(-np.sin(angt), F32),
        cs=jnp.asarray(cs, BF16))


def _dft1_body(f_ref, x_ref, o_ref):
    s1 = x_ref.shape[1]
    res = jnp.dot(f_ref[...], x_ref[0], preferred_element_type=F32)
    o_ref[0, 0] = res[:s1].astype(BF16)
    o_ref[0, 1] = res[s1:].astype(BF16)


def _dft1(zf3, f1, cb):
    b, s1, w = zf3.shape
    return pl.pallas_call(
        _dft1_body,
        grid=(b, w // cb),
        in_specs=[pl.BlockSpec((2 * s1, s1), lambda i, j: (0, 0)),
                  pl.BlockSpec((1, s1, cb), lambda i, j: (i, 0, j))],
        out_specs=pl.BlockSpec((1, 2, s1, cb), lambda i, j: (i, 0, 0, j)),
        out_shape=jax.ShapeDtypeStruct((b, 2, s1, w), BF16),
        compiler_params=_params(("parallel", "parallel"), VMEM_LIMIT),
        name="dft1",
    )(f1, zf3)


def _dft2_body(a_ref, f2r_ref, f2i_ref, twr_ref, twi_ref, cs_ref, wf_ref, o_ref, *, kb, scale):
    s2 = BLOCK
    fr = f2r_ref[...]
    fi = f2i_ref[...]
    for j in range(kb):
        tr = twr_ref[0, j:j + 1, :]
        ti = twi_ref[0, j:j + 1, :]
        gr = fr * tr - fi * ti
        gi = fr * ti + fi * tr
        lhs = jnp.concatenate(
            [jnp.concatenate([gr, -gi], axis=1), jnp.concatenate([gi, gr], axis=1)],
            axis=0).astype(BF16)
        rhs = jnp.concatenate([a_ref[0, 0, j], a_ref[0, 1, j]], axis=0)
        y = jnp.dot(lhs, rhs, preferred_element_type=F32)
        outs = []
        for g in range(FOURIER_GROUPS):
            sl = slice(g * GROUP_DIM, (g + 1) * GROUP_DIM)
            cat = jnp.concatenate([y[:s2, sl], y[s2:, sl]], axis=1).astype(BF16)
            re = jnp.dot(cat, cs_ref[...], preferred_element_type=F32) * scale
            outs.append(jnp.dot(re.astype(BF16), wf_ref[g], preferred_element_type=F32))
        o_ref[0, :, j, :] = jnp.concatenate(outs, axis=1)


def _dft2(a5, tabs, wf, kb, scale):
    b, _, s1, s2, w = a5.shape
    const = lambda shape: pl.BlockSpec(shape, lambda i, j: (0,) * len(shape))
    twr = tabs["twr"].reshape(s1 // kb, kb, s2)
    twi = tabs["twi"].reshape(s1 // kb, kb, s2)
    return pl.pallas_call(
        functools.partial(_dft2_body, kb=kb, scale=scale),
        grid=(b, s1 // kb),
        in_specs=[pl.BlockSpec((1, 2, kb, s2, w), lambda i, j: (i, 0, j, 0, 0)),
                  const((s2, s2)), const((s2, s2)),
                  pl.BlockSpec((1, kb, s2), lambda i, j: (j, 0, 0)),
                  pl.BlockSpec((1, kb, s2), lambda i, j: (j, 0, 0)),
                  const((2 * GROUP_DIM, GROUP_DIM)),
                  const((FOURIER_GROUPS, GROUP_DIM, GROUP_DIM))],
        out_specs=pl.BlockSpec((1, s2, kb, w), lambda i, j: (i, 0, j, 0)),
        out_shape=jax.ShapeDtypeStruct((b, s2, s1, w), F32),
        compiler_params=_params(("parallel", "parallel"), VMEM_LIMIT),
        name="dft2",
    )(a5, tabs["f2r"], tabs["f2i"], twr, twi, tabs["cs"], wf)


def _fourier(zf, wf_bf16):
    b, s, w = zf.shape
    s2 = BLOCK
    s1 = s // s2
    tabs = _dft_tables(s)
    a = _dft1(zf.reshape(b, s1, s2 * w), tabs["f1"], cb=2048)
    f = _dft2(a.reshape(b, 2, s1, s2, w), tabs, wf_bf16, kb=8,
              scale=1.0 / math.sqrt(s * GROUP_DIM))
    return f.reshape(b, s, w)


def _bucket_table():
    qi = np.arange(BLOCK)[:, None]
    kj = np.arange(3 * BLOCK)[None, :]
    rel = kj - BLOCK - qi
    nb = N_BUCKETS // 2
    max_exact = nb // 2
    n = np.abs(rel)
    large = max_exact + np.floor(
        np.log(np.maximum(n, 1).astype(np.float64) / max_exact)
        / math.log(MAX_DISTANCE / max_exact) * (nb - max_exact) + 1e-9).astype(np.int64)
    large = np.minimum(large, nb - 1)
    bucket = np.where(rel > 0, nb, 0) + np.where(n < max_exact, n, large)
    return np.where(n <= WINDOW, bucket, -1).astype(np.int32)


def _bias_body(rb_ref, bucket_ref, o_ref):
    bk = bucket_ref[...]
    for h in range(N_HEADS):
        acc = jnp.full(bk.shape, -jnp.inf, F32)
        for b in range(N_BUCKETS):
            acc = jnp.where(bk == b, rb_ref[b, h], acc)
        o_ref[h] = acc


def _bias_table(rel_bias):
    bucket = jnp.asarray(_bucket_table())
    return pl.pallas_call(
        _bias_body,
        in_specs=[pl.BlockSpec(memory_space=pltpu.SMEM),
                  pl.BlockSpec(bucket.shape, lambda: (0, 0))],
        out_specs=pl.BlockSpec((N_HEADS,) + bucket.shape, lambda: (0, 0, 0)),
        out_shape=jax.ShapeDtypeStruct((N_HEADS,) + bucket.shape, F32),
        name="bias_table",
    )(rel_bias.astype(F32), bucket)


def _attn_body(sink_ref, q_ref, kp_ref, kc_ref, kn_ref, vp_ref, vc_ref, vn_ref, bias_ref,
               o_ref, *, tq):
    j = pl.program_id(1)
    last = pl.num_programs(1) - 1
    nsb = tq // BLOCK
    kext = jnp.concatenate([kp_ref[0, tq - BLOCK:, :], kc_ref[0], kn_ref[0, :BLOCK, :]], axis=0)
    vext = jnp.concatenate([vp_ref[0, tq - BLOCK:, :], vc_ref[0], vn_ref[0, :BLOCK, :]], axis=0)
    col = lax.broadcasted_iota(I32, (BLOCK, 3 * BLOCK), 1)
    group = N_HEADS // N_KV_HEADS
    for sb in range(nsb):
        kb = kext[sb * BLOCK:(sb + 3) * BLOCK]
        vb = vext[sb * BLOCK:(sb + 3) * BLOCK]
        outs = []
        for h in range(N_HEADS):
            kh = h // group
            qh = q_ref[0, sb * BLOCK:(sb + 1) * BLOCK, h * HEAD_DIM:(h + 1) * HEAD_DIM]
            s = lax.dot_general(qh, kb[:, kh * HEAD_DIM:(kh + 1) * HEAD_DIM],
                                (((1,), (1,)), ((), ())), preferred_element_type=F32)
            s = s + bias_ref[h]
            if sb == 0:
                s = jnp.where((col >= BLOCK) | (j > 0), s, -jnp.inf)
            if sb == nsb - 1:
                s = jnp.where((col < 2 * BLOCK) | (j < last), s, -jnp.inf)
            sink = sink_ref[h]
            m = jnp.maximum(jnp.max(s, axis=-1, keepdims=True), sink)
            p = jnp.exp(s - m)
            denom = jnp.sum(p, axis=-1, keepdims=True) + jnp.exp(sink - m)
            pn = (p / denom).astype(BF16)
            outs.append(jnp.dot(pn, vb[:, kh * HEAD_DIM:(kh + 1) * HEAD_DIM],
                                preferred_element_type=F32))
        o_ref[0, sb * BLOCK:(sb + 1) * BLOCK, :] = jnp.concatenate(outs, axis=1).astype(BF16)


def _attention(q, k, v, bias, sink, tq):
    b, s, _ = q.shape
    nt = s // tq
    kv = lambda f: pl.BlockSpec((1, tq, KV_WIDTH), f)
    prev = lambda i, j: (i, jnp.maximum(j - 1, 0), 0)
    cur = lambda i, j: (i, j, 0)
    nxt = lambda i, j: (i, jnp.minimum(j + 1, nt - 1), 0)
    return pl.pallas_call(
        functools.partial(_attn_body, tq=tq),
        grid=(b, nt),
        in_specs=[pl.BlockSpec(memory_space=pltpu.SMEM),
                  pl.BlockSpec((1, tq, ATTN_WIDTH), cur),
                  kv(prev), kv(cur), kv(nxt), kv(prev), kv(cur), kv(nxt),
                  pl.BlockSpec(bias.shape, lambda i, j: (0, 0, 0))],
        out_specs=pl.BlockSpec((1, tq, ATTN_WIDTH), cur),
        out_shape=jax.ShapeDtypeStruct((b, s, ATTN_WIDTH), BF16),
        compiler_params=_params(("parallel", "parallel"), VMEM_LIMIT),
        name="attn",
    )(sink.astype(F32), q, k, k, k, v, v, v, bias)


def _outproj_body(x_ref, f_ref, a_ref, wo_ref, g_ref, wr_ref, x1_ref, h2_ref, aff_ref):
    mix = jnp.concatenate([f_ref[...].astype(BF16), a_ref[...]], axis=1)
    x1 = x_ref[...] + jnp.dot(mix, wo_ref[...], preferred_element_type=F32)
    x1_ref[...] = x1
    ms = jnp.mean(x1 * x1, axis=-1, keepdims=True)
    h2 = (x1 * lax.rsqrt(ms + EPS)) * g_ref[...]
    h2_ref[...] = h2
    hi = h2.astype(BF16)
    lo = (h2 - hi.astype(F32)).astype(BF16)
    nt = (((1,), (1,)), ((), ()))
    o1 = lax.dot_general(wr_ref[...], hi, nt, preferred_element_type=F32)
    o2 = lax.dot_general(wr_ref[:N_EXPERTS], lo, nt, preferred_element_type=F32)
    logits = o1[:N_EXPERTS] + (o1[N_EXPERTS:] + o2)
    e = jnp.exp(logits - jnp.max(logits, axis=0, keepdims=True))
    aff_ref[...] = e / jnp.sum(e, axis=0, keepdims=True)


def _outproj(xf, f, a, w_out, gain, wr_split, tm):
    n = xf.shape[0]
    row = lambda w: pl.BlockSpec((tm, w), lambda i: (i, 0))
    const = lambda shape: pl.BlockSpec(shape, lambda i: (0, 0))
    return pl.pallas_call(
        _outproj_body,
        grid=(n // tm,),
        in_specs=[row(D_MODEL), row(FOURIER_WIDTH), row(ATTN_WIDTH),
                  const((D_MODEL, D_MODEL)), const((1, D_MODEL)),
                  const((2 * N_EXPERTS, D_MODEL))],
        out_specs=[row(D_MODEL), row(D_MODEL),
                   pl.BlockSpec((N_EXPERTS, tm), lambda i: (0, i))],
        out_shape=[jax.ShapeDtypeStruct((n, D_MODEL), F32),
                   jax.ShapeDtypeStruct((n, D_MODEL), F32),
                   jax.ShapeDtypeStruct((N_EXPERTS, n), F32)],
        compiler_params=_params(("parallel",), VMEM_LIMIT),
        name="outproj",
    )(xf, f, a, w_out, gain, wr_split)


def _route_body(aff_ref, idx_ref, pos_ref, gate_ref, tok_ref,
                d_a, d_b, n_a, n_b, p_a, p_b, g_a, g_b, *, n, cap):
    e = N_EXPERTS
    ch = PREFIX_CHUNK

    def search(i, tau):
        cand = tau | jnp.left_shift(jnp.int32(1), 30 - i)
        keys = lax.bitcast_convert_type(aff_ref[...], I32)
        cnt = jnp.sum((keys >= cand).astype(I32), axis=1, keepdims=True)
        return jnp.where(cnt >= cap, cand, tau)

    tau = lax.fori_loop(0, 31, search, jnp.zeros((e, 1), I32))
    keys = lax.bitcast_convert_type(aff_ref[...], I32)
    n_gt = jnp.sum((keys > tau).astype(I32), axis=1, keepdims=True)
    need = (cap - n_gt).astype(F32)

    r = lax.broadcasted_iota(I32, (ch, ch), 0)
    c = lax.broadcasted_iota(I32, (ch, ch), 1)
    upper = (r < c).astype(BF16)
    r16 = lax.broadcasted_iota(I32, (e, e), 0)
    c16 = lax.broadcasted_iota(I32, (e, e), 1)
    lower = (c16 < r16).astype(BF16)

    carry_eq = jnp.zeros((e, 1), F32)
    carry_m = jnp.zeros((2 * e, 1), F32)
    for ci in range(n // ch):
        sl = slice(ci * ch, (ci + 1) * ch)
        k = lax.bitcast_convert_type(aff_ref[:, sl], I32)
        gt = k > tau
        eq = (k == tau).astype(F32)
        eq_ex = jnp.dot(eq.astype(BF16), upper, preferred_element_type=F32) + carry_eq
        carry_eq = carry_eq + jnp.sum(eq, axis=1, keepdims=True)
        m = jnp.where(gt | ((eq > 0.0) & (eq_ex < need)), 1.0, 0.0).astype(F32)
        cnt = jnp.sum(m, axis=0, keepdims=True)
        stacked = jnp.concatenate([m, cnt, jnp.zeros((e - 1, ch), F32)], axis=0)
        pre = jnp.dot(stacked.astype(BF16), upper, preferred_element_type=F32) + carry_m
        carry_m = carry_m + jnp.sum(stacked, axis=1, keepdims=True)
        rank = pre[:e]
        tokoff = pre[e:e + 1]
        ex_e = jnp.dot(lower, m.astype(BF16), preferred_element_type=F32)
        tok_ref[0:1, sl] = tokoff.astype(I32)
        tok_ref[1:2, sl] = (tokoff + cnt).astype(I32)
        lane = lax.broadcasted_iota(I32, (e, ch), 1) + ci * ch
        d_a[:, sl] = jnp.where(m > 0.0, lane - rank.astype(I32), 0)
        n_a[:, sl] = lane
        p_a[:, sl] = (tokoff + ex_e).astype(I32)
    g_a[...] = aff_ref[...]

    bufs = [(d_a, n_a, p_a, g_a), (d_b, n_b, p_b, g_b)]
    for b in range(int(math.log2(n))):
        src, dst = bufs[b % 2], bufs[(b + 1) % 2]
        sh = 1 << b
        d = src[0][...]
        moving = (lax.shift_right_logical(d, b) & 1) == 1
        d_in = pltpu.roll(d, n - sh, axis=1)
        arrive = (lax.shift_right_logical(d_in, b) & 1) == 1
        dst[0][...] = jnp.where(arrive, d_in, jnp.where(moving, 0, d))
        for t in (1, 2, 3):
            v = src[t][...]
            dst[t][...] = jnp.where(arrive, pltpu.roll(v, n - sh, axis=1), v)
    fin = bufs[int(math.log2(n)) % 2]
    idx_ref[...] = fin[1][:, :cap]
    pos_ref[...] = fin[2][:, :cap]
    gate_ref[...] = fin[3][:, :cap]


def _route(aff_t, cap):
    e, n = aff_t.shape
    assert n & (n - 1) == 0 or True
    full = lambda shape: pl.BlockSpec(shape, lambda: (0, 0))
    big_i = pltpu.VMEM((e, n), I32)
    return pl.pallas_call(
        functools.partial(_route_body, n=n, cap=cap),
        in_specs=[full((e, n))],
        out_specs=[full((e, cap)), full((e, cap)), full((e, cap)), full((2, n))],
        out_shape=[jax.ShapeDtypeStruct((e, cap), I32),
                   jax.ShapeDtypeStruct((e, cap), I32),
                   jax.ShapeDtypeStruct((e, cap), F32),
                   jax.ShapeDtypeStruct((2, n), I32)],
        scratch_shapes=[big_i, big_i, big_i, big_i, big_i, big_i,
                        pltpu.VMEM((e, n), F32), pltpu.VMEM((e, n), F32)],
        compiler_params=_params(None, VMEM_LIMIT),
        name="route",
    )(aff_t)


def _row_to_col(row, tm):
    r = lax.broadcasted_iota(I32, (tm, tm), 0)
    c = lax.broadcasted_iota(I32, (tm, tm), 1)
    return jnp.sum(jnp.where(r == c, row, jnp.zeros_like(row)), axis=1, keepdims=True)


def _ffn_body(idx_cur, idx_nxt, pos_prv, pos_cur, gate_ref, h_hbm, wg_ref, wu_ref, wd_ref, ys_hbm,
              xa, xb, ya, yb, sems, *, tm):
    s = pl.program_id(0)
    last = pl.num_programs(0) - 1
    g_a, g_b, s_a, s_b = sems.at[0], sems.at[1], sems.at[2], sems.at[3]

    def gather(idx_ref, t, buf, sem, unrolled=True):
        def one(i):
            pltpu.make_async_copy(h_hbm.at[pl.ds(idx_ref[t, 0, i], 1), :],
                                  buf.at[pl.ds(i, 1), :], sem).start()
        if unrolled:
            for i in range(tm):
                one(i)
        else:
            pl.loop(0, tm)(one)

    def scatter(pos_ref, t, buf, sem, unrolled=True):
        def one(i):
            pltpu.make_async_copy(buf.at[pl.ds(i, 1), :],
                                  ys_hbm.at[pl.ds(pos_ref[t, 0, i], 1), :], sem).start()
        if unrolled:
            for i in range(tm):
                one(i)
        else:
            pl.loop(0, tm)(one)

    def wait_gather(buf, sem):
        pltpu.make_async_copy(h_hbm.at[pl.ds(0, tm), :], buf, sem).wait()

    def wait_scatter(buf, sem):
        pltpu.make_async_copy(buf, ys_hbm.at[pl.ds(0, tm), :], sem).wait()

    def compute(xbuf, ybuf, t):
        x = xbuf[...].astype(BF16)
        g = jnp.dot(x, wg_ref[0], preferred_element_type=F32)
        u = jnp.dot(x, wu_ref[0], preferred_element_type=F32)
        hid = (g * jax.nn.sigmoid(g) * u).astype(BF16)
        y = jnp.dot(hid, wd_ref[0], preferred_element_type=F32)
        y = y * _row_to_col(gate_ref[t], tm)
        ybuf[...] = y.astype(BF16).astype(F32)

    @pl.when(s == 0)
    def _():
        gather(idx_cur, 0, xa, g_a, unrolled=False)
        yb[...] = jnp.zeros_like(yb)

    wait_gather(xa, g_a)

    @pl.when(s > 0)
    def _():
        wait_scatter(ya, s_a)

    gather(idx_cur, 1, xb, g_b)
    scatter(pos_prv, 1, yb, s_b)
    compute(xa, ya, 0)

    wait_gather(xb, g_b)
    wait_scatter(yb, s_b)
    gather(idx_nxt, 0, xa, g_a)
    scatter(pos_cur, 0, ya, s_a)
    compute(xb, yb, 1)

    @pl.when(s == last)
    def _():
        scatter(pos_cur, 1, yb, s_b, unrolled=False)
        wait_gather(xa, g_a)
        wait_scatter(ya, s_a)
        wait_scatter(yb, s_b)


def _ffn(idx_c, pos_c, gate_c, h2, wg, wu, wd, tm):
    e, cap = idx_c.shape
    spe = cap // (2 * tm)
    steps = e * spe
    n_slots = e * cap
    idx3 = idx_c.reshape(2 * steps, 1, tm)
    pos3 = pos_c.reshape(2 * steps, 1, tm)
    gate3 = gate_c.reshape(2 * steps, 1, tm)
    cur = lambda i: (i, 0, 0)
    prv = lambda i: (jnp.maximum(i - 1, 0), 0, 0)
    nxt = lambda i: (jnp.minimum(i + 1, steps - 1), 0, 0)
    smem = lambda f: pl.BlockSpec((2, 1, tm), f, memory_space=pltpu.SMEM)
    wspec = lambda a, b: pl.BlockSpec((1, a, b), lambda i: (i // spe, 0, 0))
    buf = pltpu.VMEM((tm, D_MODEL), F32)
    return pl.pallas_call(
        functools.partial(_ffn_body, tm=tm),
        grid=(steps,),
        in_specs=[smem(cur), smem(nxt), smem(prv), smem(cur),
                  pl.BlockSpec((2, 1, tm), cur),
                  pl.BlockSpec(memory_space=pl.ANY),
                  wspec(D_MODEL, D_EXPERT), wspec(D_MODEL, D_EXPERT), wspec(D_EXPERT, D_MODEL)],
        out_specs=pl.BlockSpec(memory_space=pl.ANY),
        out_shape=jax.ShapeDtypeStruct((n_slots, D_MODEL), F32),
        scratch_shapes=[buf, buf, buf, buf, pltpu.SemaphoreType.DMA((4,))],
        compiler_params=_params(("arbitrary",), VMEM_LIMIT),
        name="ffn",
    )(idx3, idx3, pos3, pos3, gate3, h2, wg, wu, wd)


def _combine_body(toff_ref, x1_ref, tok_ref, g_ref, ys_hbm, o_ref, win, sem, *, tt, ws, n_slots):
    i = pl.program_id(0)
    lo = toff_ref[i]
    hi = toff_ref[i + 1]
    base = (lo // 8) * 8
    nwin = (hi - base + ws - 1) // ws
    t_lo = _row_to_col(tok_ref[0:1, :], tt)
    t_hi = _row_to_col(tok_ref[1:2, :], tt)
    lane = lax.broadcasted_iota(I32, (tt, ws), 1)

    def window(w, acc):
        nominal = base + w * ws
        start = pl.multiple_of(jnp.minimum(nominal, n_slots - ws), 8)
        cp = pltpu.make_async_copy(ys_hbm.at[pl.ds(start, ws), :], win, sem)
        cp.start()
        cp.wait()
        slot = lane + start
        onehot = (slot >= t_lo) & (slot < t_hi) & (slot >= nominal)
        sel = jnp.where(onehot, 1.0, 0.0).astype(BF16)
        return acc + jnp.dot(sel, win[...].astype(BF16), preferred_element_type=F32)

    moe = lax.fori_loop(0, nwin, window, jnp.zeros((tt, D_MODEL), F32))
    x2 = x1_ref[...] + moe
    ms = jnp.mean(x2 * x2, axis=-1, keepdims=True)
    o_ref[...] = (x2 * lax.rsqrt(ms + EPS)) * g_ref[...]


def _combine(x1, ys, tokinfo, gain, tt, ws):
    n = x1.shape[0]
    n_slots = ys.shape[0]
    tile_off = jnp.concatenate([tokinfo[0, ::tt], jnp.full((1,), n_slots, I32)])
    return pl.pallas_call(
        functools.partial(_combine_body, tt=tt, ws=ws, n_slots=n_slots),
        grid_spec=pltpu.PrefetchScalarGridSpec(
            num_scalar_prefetch=1,
            grid=(n // tt,),
            in_specs=[pl.BlockSpec((tt, D_MODEL), lambda i, s: (i, 0)),
                      pl.BlockSpec((2, tt), lambda i, s: (0, i)),
                      pl.BlockSpec((1, D_MODEL), lambda i, s: (0, 0)),
                      pl.BlockSpec(memory_space=pl.ANY)],
            out_specs=pl.BlockSpec((tt, D_MODEL), lambda i, s: (i, 0)),
            scratch_shapes=[pltpu.VMEM((ws, D_MODEL), F32), pltpu.SemaphoreType.DMA]),
        out_shape=jax.ShapeDtypeStruct((n, D_MODEL), F32),
        compiler_params=_params(("arbitrary",), VMEM_LIMIT),
        name="combine",
    )(tile_off, x1, tokinfo, gain, ys)


def _tile(n, pref):
    t = pref
    while n % t:
        t //= 2
    return t


def _trunk(x, p):
    b, s, d = x.shape
    n = b * s
    cap = CAPACITY_FACTOR * n // N_EXPERTS
    xf = x.reshape(n, d)
    tm = _tile(n, 512)
    zf, q, k, v = _inproj(xf, p["norm_mix"], p["w_in"], tm)
    f = _fourier(zf.reshape(b, s, FOURIER_WIDTH), p["w_fourier"])
    a = _attention(q.reshape(b, s, ATTN_WIDTH), k.reshape(b, s, KV_WIDTH),
                   v.reshape(b, s, KV_WIDTH), p["bias"], p["sink"], _tile(s, 512))
    x1, h2, aff_t = _outproj(xf, f.reshape(n, FOURIER_WIDTH), a.reshape(n, ATTN_WIDTH),
                             p["w_out"], p["norm_ffn"], p["w_router"], tm)
    idx_c, pos_c, gate_c, tokinfo = _route(aff_t, cap)
    ys = _ffn(idx_c, pos_c, gate_c, h2, p["w_gate"], p["w_up"], p["w_down"], _tile(cap // 2, 512))
    y = _combine(x1, ys, tokinfo, p["norm_final"], _tile(n, 256), min(512, 2 * n))
    return y.reshape(b, s, d)


def kernel(x_prompt, x_sample, w_in, w_fourier, attn_sink, rel_bias, w_out, norm_mix, norm_ffn,
           w_router, w_gate, w_up, w_down, norm_final):
    assert w_in.shape[0] == 1
    wr = w_router[0].T
    wr_hi = wr.astype(BF16)
    wr_lo = (wr - wr_hi.astype(F32)).astype(BF16)
    p = dict(
        norm_mix=norm_mix[0].reshape(1, D_MODEL),
        norm_ffn=norm_ffn[0].reshape(1, D_MODEL),
        norm_final=norm_final.reshape(1, D_MODEL),
        w_in=w_in[0].astype(BF16),
        w_fourier=w_fourier[0].astype(BF16),
        w_out=w_out[0].astype(BF16),
        w_router=jnp.concatenate([wr_hi, wr_lo], axis=0),
        w_gate=w_gate[0].astype(BF16),
        w_up=w_up[0].astype(BF16),
        w_down=w_down[0].astype(BF16),
        sink=attn_sink[0],
        bias=_bias_table(rel_bias),
    )
    return _trunk(x_prompt, p), _trunk(x_sample, p)
```

```python
import functools
import math

import numpy as np
import jax
import jax.numpy as jnp
from jax import lax
from jax.experimental import pallas as pl
from jax.experimental.pallas import tpu as pltpu

F32 = jnp.float32
BF16 = jnp.bfloat16
I32 = jnp.int32

D_MODEL = 1024
FOURIER_GROUPS = 4
GROUP_DIM = 128
FOURIER_WIDTH = FOURIER_GROUPS * GROUP_DIM
N_HEADS = 8
N_KV_HEADS = 2
HEAD_DIM = 64
ATTN_WIDTH = N_HEADS * HEAD_DIM
KV_WIDTH = N_KV_HEADS * HEAD_DIM
WINDOW = 128
BLOCK = 128
N_BUCKETS = 32
MAX_DISTANCE = 128
IN_WIDTH = FOURIER_WIDTH + ATTN_WIDTH + 2 * KV_WIDTH
N_EXPERTS = 16
CAPACITY_FACTOR = 2
D_EXPERT = 1024
EPS = 1e-6

LANES = 128
PREFIX_CHUNK = 256
VMEM_LIMIT = 56 * 1024 * 1024


ROW_TILE = D_MODEL // LANES


def _params(sem, vmem=None):
    return pltpu.CompilerParams(dimension_semantics=sem, vmem_limit_bytes=vmem)


def _store_rows_as_tiles(ref, val):
    for c in range(ROW_TILE):
        ref[:, c, :] = val[:, c * LANES:(c + 1) * LANES]


def _load_tiles_as_rows(ref):
    return jnp.concatenate([ref[:, c, :] for c in range(ROW_TILE)], axis=1)


def _inproj_body(x_ref, g_ref, w_ref, zf_ref, q_ref, k_ref, v_ref):
    x = x_ref[...]
    ms = jnp.mean(x * x, axis=-1, keepdims=True)
    h = (x * lax.rsqrt(ms + EPS)) * g_ref[...]
    z = jnp.dot(h.astype(BF16), w_ref[...], preferred_element_type=F32)
    o = FOURIER_WIDTH
    zf_ref[...] = z[:, :o].astype(BF16)
    q_ref[...] = (z[:, o:o + ATTN_WIDTH] * (HEAD_DIM ** -0.5)).astype(BF16)
    o += ATTN_WIDTH
    k_ref[...] = z[:, o:o + KV_WIDTH].astype(BF16)
    o += KV_WIDTH
    v_ref[...] = z[:, o:o + KV_WIDTH].astype(BF16)


def _inproj(xf, gain, w_in, tm):
    n = xf.shape[0]
    row = lambda w: pl.BlockSpec((tm, w), lambda i: (i, 0))
    return pl.pallas_call(
        _inproj_body,
        grid=(n // tm,),
        in_specs=[row(D_MODEL),
                  pl.BlockSpec((1, D_MODEL), lambda i: (0, 0)),
                  pl.BlockSpec((D_MODEL, IN_WIDTH), lambda i: (0, 0))],
        out_specs=[row(FOURIER_WIDTH), row(ATTN_WIDTH), row(KV_WIDTH), row(KV_WIDTH)],
        out_shape=[jax.ShapeDtypeStruct((n, FOURIER_WIDTH), BF16),
                   jax.ShapeDtypeStruct((n, ATTN_WIDTH), BF16),
                   jax.ShapeDtypeStruct((n, KV_WIDTH), BF16),
                   jax.ShapeDtypeStruct((n, KV_WIDTH), BF16)],
        compiler_params=_params(("parallel",), VMEM_LIMIT),
        name="inproj",
    )(xf, gain, w_in)


def _dft_tables(s):
    s2 = BLOCK
    s1 = s // s2
    a = np.arange(s1, dtype=np.float64)
    ang1 = 2.0 * np.pi * np.outer(a, a) / s1
    f1 = np.concatenate([np.cos(ang1), -np.sin(ang1)], axis=0)
    b = np.arange(s2, dtype=np.float64)
    ang2 = 2.0 * np.pi * np.outer(b, b) / s2
    angt = 2.0 * np.pi * np.outer(a, b) / s
    c = np.arange(GROUP_DIM, dtype=np.float64)
    angc = 2.0 * np.pi * np.outer(c, c) / GROUP_DIM
    cs = np.concatenate([np.cos(angc), np.sin(angc)], axis=0)
    return dict(
        f1=jnp.asarray(f1, BF16),
        f2r=jnp.asarray(np.cos(ang2), F32), f2i=jnp.asarray(-np.sin(ang2), F32),
        twr=jnp.asarray(np.cos(angt), F32), twi=jnp.asarray(-np.sin(angt), F32),
        cs=jnp.asarray(cs, BF16))


def _dft1_body(f_ref, x_ref, o_ref):
    s1 = x_ref.shape[1]
    res = jnp.dot(f_ref[...], x_ref[0], preferred_element_type=F32)
    o_ref[0, 0] = res[:s1].astype(BF16)
    o_ref[0, 1] = res[s1:].astype(BF16)


def _dft1(zf3, f1, cb):
    b, s1, w = zf3.shape
    return pl.pallas_call(
        _dft1_body,
        grid=(b, w // cb),
        in_specs=[pl.BlockSpec((2 * s1, s1), lambda i, j: (0, 0)),
                  pl.BlockSpec((1, s1, cb), lambda i, j: (i, 0, j))],
        out_specs=pl.BlockSpec((1, 2, s1, cb), lambda i, j: (i, 0, 0, j)),
        out_shape=jax.ShapeDtypeStruct((b, 2, s1, w), BF16),
        compiler_params=_params(("parallel", "parallel"), VMEM_LIMIT),
        name="dft1",
    )(f1, zf3)


def _dft2_body(a_ref, f2r_ref, f2i_ref, twr_ref, twi_ref, cs_ref, wf_ref, o_ref, *, kb, scale):
    s2 = BLOCK
    fr = f2r_ref[...]
    fi = f2i_ref[...]
    for j in range(kb):
        tr = twr_ref[0, j:j + 1, :]
        ti = twi_ref[0, j:j + 1, :]
        gr = fr * tr - fi * ti
        gi = fr * ti + fi * tr
        lhs = jnp.concatenate(
            [jnp.concatenate([gr, -gi], axis=1), jnp.concatenate([gi, gr], axis=1)],
            axis=0).astype(BF16)
        rhs = jnp.concatenate([a_ref[0, 0, j], a_ref[0, 1, j]], axis=0)
        y = jnp.dot(lhs, rhs, preferred_element_type=F32)
        outs = []
        for g in range(FOURIER_GROUPS):
            sl = slice(g * GROUP_DIM, (g + 1) * GROUP_DIM)
            cat = jnp.concatenate([y[:s2, sl], y[s2:, sl]], axis=1).astype(BF16)
            re = jnp.dot(cat, cs_ref[...], preferred_element_type=F32) * scale
            outs.append(jnp.dot(re.astype(BF16), wf_ref[g], preferred_element_type=F32))
        o_ref[0, :, j, :] = jnp.concatenate(outs, axis=1)


def _dft2(a5, tabs, wf, kb, scale):
    b, _, s1, s2, w = a5.shape
    const = lambda shape: pl.BlockSpec(shape, lambda i, j: (0,) * len(shape))
    twr = tabs["twr"].reshape(s1 // kb, kb, s2)
    twi = tabs["twi"].reshape(s1 // kb, kb, s2)
    return pl.pallas_call(
        functools.partial(_dft2_body, kb=kb, scale=scale),
        grid=(b, s1 // kb),
        in_specs=[pl.BlockSpec((1, 2, kb, s2, w), lambda i, j: (i, 0, j, 0, 0)),
                  const((s2, s2)), const((s2, s2)),
                  pl.BlockSpec((1, kb, s2), lambda i, j: (j, 0, 0)),
                  pl.BlockSpec((1, kb, s2), lambda i, j: (j, 0, 0)),
                  const((2 * GROUP_DIM, GROUP_DIM)),
                  const((FOURIER_GROUPS, GROUP_DIM, GROUP_DIM))],
        out_specs=pl.BlockSpec((1, s2, kb, w), lambda i, j: (i, 0, j, 0)),
        out_shape=jax.ShapeDtypeStruct((b, s2, s1, w), F32),
        compiler_params=_params(("parallel", "parallel"), VMEM_LIMIT),
        name="dft2",
    )(a5, tabs["f2r"], tabs["f2i"], twr, twi, tabs["cs"], wf)


def _fourier(zf, wf_bf16):
    b, s, w = zf.shape
    s2 = BLOCK
    s1 = s // s2
    tabs = _dft_tables(s)
    a = _dft1(zf.reshape(b, s1, s2 * w), tabs["f1"], cb=2048)
    f = _dft2(a.reshape(b, 2, s1, s2, w), tabs, wf_bf16, kb=8,
              scale=1.0 / math.sqrt(s * GROUP_DIM))
    return f.reshape(b, s, w)


def _bucket_table():
    qi = np.arange(BLOCK)[:, None]
    kj = np.arange(3 * BLOCK)[None, :]
    rel = kj - BLOCK - qi
    nb = N_BUCKETS // 2
    max_exact = nb // 2
    n = np.abs(rel)
    large = max_exact + np.floor(
        np.log(np.maximum(n, 1).astype(np.float64) / max_exact)
        / math.log(MAX_DISTANCE / max_exact) * (nb - max_exact) + 1e-9).astype(np.int64)
    large = np.minimum(large, nb - 1)
    bucket = np.where(rel > 0, nb, 0) + np.where(n < max_exact, n, large)
    return np.where(n <= WINDOW, bucket, -1).astype(np.int32)


def _bias_body(rb_ref, bucket_ref, o_ref):
    bk = bucket_ref[...]
    for h in range(N_HEADS):
        acc = jnp.full(bk.shape, -jnp.inf, F32)
        for b in range(N_BUCKETS):
            acc = jnp.where(bk == b, rb_ref[b, h], acc)
        o_ref[h] = acc


def _bias_table(rel_bias):
    bucket = jnp.asarray(_bucket_table())
    return pl.pallas_call(
        _bias_body,
        in_specs=[pl.BlockSpec(memory_space=pltpu.SMEM),
                  pl.BlockSpec(bucket.shape, lambda: (0, 0))],
        out_specs=pl.BlockSpec((N_HEADS,) + bucket.shape, lambda: (0, 0, 0)),
        out_shape=jax.ShapeDtypeStruct((N_HEADS,) + bucket.shape, F32),
        name="bias_table",
    )(rel_bias.astype(F32), bucket)


def _attn_body(sink_ref, q_ref, kp_ref, kc_ref, kn_ref, vp_ref, vc_ref, vn_ref, bias_ref,
               o_ref, *, tq):
    j = pl.program_id(1)
    last = pl.num_programs(1) - 1
    nsb = tq // BLOCK
    kext = jnp.concatenate([kp_ref[0, tq - BLOCK:, :], kc_ref[0], kn_ref[0, :BLOCK, :]], axis=0)
    vext = jnp.concatenate([vp_ref[0, tq - BLOCK:, :], vc_ref[0], vn_ref[0, :BLOCK, :]], axis=0)
    col = lax.broadcasted_iota(I32, (BLOCK, 3 * BLOCK), 1)
    group = N_HEADS // N_KV_HEADS
    for sb in range(nsb):
        kb = kext[sb * BLOCK:(sb + 3) * BLOCK]
        vb = vext[sb * BLOCK:(sb + 3) * BLOCK]
        outs = []
        for h in range(N_HEADS):
            kh = h // group
            qh = q_ref[0, sb * BLOCK:(sb + 1) * BLOCK, h * HEAD_DIM:(h + 1) * HEAD_DIM]
            s = lax.dot_general(qh, kb[:, kh * HEAD_DIM:(kh + 1) * HEAD_DIM],
                                (((1,), (1,)), ((), ())), preferred_element_type=F32)
            s = s + bias_ref[h]
            if sb == 0:
                s = jnp.where((col >= BLOCK) | (j > 0), s, -jnp.inf)
            if sb == nsb - 1:
                s = jnp.where((col < 2 * BLOCK) | (j < last), s, -jnp.inf)
            sink = sink_ref[h]
            m = jnp.maximum(jnp.max(s, axis=-1, keepdims=True), sink)
            p = jnp.exp(s - m)
            denom = jnp.sum(p, axis=-1, keepdims=True) + jnp.exp(sink - m)
            pn = (p / denom).astype(BF16)
            outs.append(jnp.dot(pn, vb[:, kh * HEAD_DIM:(kh + 1) * HEAD_DIM],
                                preferred_element_type=F32))
        o_ref[0, sb * BLOCK:(sb + 1) * BLOCK, :] = jnp.concatenate(outs, axis=1).astype(BF16)


def _attention(q, k, v, bias, sink, tq):
    b, s, _ = q.shape
    nt = s // tq
    kv = lambda f: pl.BlockSpec((1, tq, KV_WIDTH), f)
    prev = lambda i, j: (i, jnp.maximum(j - 1, 0), 0)
    cur = lambda i, j: (i, j, 0)
    nxt = lambda i, j: (i, jnp.minimum(j + 1, nt - 1), 0)
    return pl.pallas_call(
        functools.partial(_attn_body, tq=tq),
        grid=(b, nt),
        in_specs=[pl.BlockSpec(memory_space=pltpu.SMEM),
                  pl.BlockSpec((1, tq, ATTN_WIDTH), cur),
                  kv(prev), kv(cur), kv(nxt), kv(prev), kv(cur), kv(nxt),
                  pl.BlockSpec(bias.shape, lambda i, j: (0, 0, 0))],
        out_specs=pl.BlockSpec((1, tq, ATTN_WIDTH), cur),
        out_shape=jax.ShapeDtypeStruct((b, s, ATTN_WIDTH), BF16),
        compiler_params=_params(("parallel", "parallel"), VMEM_LIMIT),
        name="attn",
    )(sink.astype(F32), q, k, k, k, v, v, v, bias)


def _outproj_body(x_ref, f_ref, a_ref, wo_ref, g_ref, wr_ref, x1_ref, h2_ref, aff_ref):
    mix = jnp.concatenate([f_ref[...].astype(BF16), a_ref[...]], axis=1)
    x1 = x_ref[...] + jnp.dot(mix, wo_ref[...], preferred_element_type=F32)
    x1_ref[...] = x1
    ms = jnp.mean(x1 * x1, axis=-1, keepdims=True)
    h2 = (x1 * lax.rsqrt(ms + EPS)) * g_ref[...]
    _store_rows_as_tiles(h2_ref, h2)
    hi = h2.astype(BF16)
    lo = (h2 - hi.astype(F32)).astype(BF16)
    nt = (((1,), (1,)), ((), ()))
    o1 = lax.dot_general(wr_ref[...], hi, nt, preferred_element_type=F32)
    o2 = lax.dot_general(wr_ref[:N_EXPERTS], lo, nt, preferred_element_type=F32)
    logits = o1[:N_EXPERTS] + (o1[N_EXPERTS:] + o2)
    e = jnp.exp(logits - jnp.max(logits, axis=0, keepdims=True))
    aff_ref[...] = e / jnp.sum(e, axis=0, keepdims=True)


def _outproj(xf, f, a, w_out, gain, wr_split, tm):
    n = xf.shape[0]
    row = lambda w: pl.BlockSpec((tm, w), lambda i: (i, 0))
    const = lambda shape: pl.BlockSpec(shape, lambda i: (0, 0))
    return pl.pallas_call(
        _outproj_body,
        grid=(n // tm,),
        in_specs=[row(D_MODEL), row(FOURIER_WIDTH), row(ATTN_WIDTH),
                  const((D_MODEL, D_MODEL)), const((1, D_MODEL)),
                  const((2 * N_EXPERTS, D_MODEL))],
        out_specs=[row(D_MODEL),
                   pl.BlockSpec((tm, ROW_TILE, LANES), lambda i: (i, 0, 0)),
                   pl.BlockSpec((N_EXPERTS, tm), lambda i: (0, i))],
        out_shape=[jax.ShapeDtypeStruct((n, D_MODEL), F32),
                   jax.ShapeDtypeStruct((n, ROW_TILE, LANES), F32),
                   jax.ShapeDtypeStruct((N_EXPERTS, n), F32)],
        compiler_params=_params(("parallel",), VMEM_LIMIT),
        name="outproj",
    )(xf, f, a, w_out, gain, wr_split)


def _route_body(aff_ref, idx_ref, pos_ref, gate_ref, tok_ref,
                d_a, d_b, n_a, n_b, p_a, p_b, g_a, g_b, *, n, cap):
    e = N_EXPERTS
    ch = PREFIX_CHUNK

    def search(i, tau):
        cand = tau | jnp.left_shift(jnp.int32(1), 30 - i)
        keys = lax.bitcast_convert_type(aff_ref[...], I32)
        cnt = jnp.sum((keys >= cand).astype(I32), axis=1, keepdims=True)
        return jnp.where(cnt >= cap, cand, tau)

    tau = lax.fori_loop(0, 31, search, jnp.zeros((e, 1), I32))
    keys = lax.bitcast_convert_type(aff_ref[...], I32)
    n_gt = jnp.sum((keys > tau).astype(I32), axis=1, keepdims=True)
    need = (cap - n_gt).astype(F32)

    r = lax.broadcasted_iota(I32, (ch, ch), 0)
    c = lax.broadcasted_iota(I32, (ch, ch), 1)
    upper = (r < c).astype(BF16)
    r16 = lax.broadcasted_iota(I32, (e, e), 0)
    c16 = lax.broadcasted_iota(I32, (e, e), 1)
    lower = (c16 < r16).astype(BF16)

    carry_eq = jnp.zeros((e, 1), F32)
    carry_m = jnp.zeros((2 * e, 1), F32)
    for ci in range(n // ch):
        sl = slice(ci * ch, (ci + 1) * ch)
        k = lax.bitcast_convert_type(aff_ref[:, sl], I32)
        gt = k > tau
        eq = (k == tau).astype(F32)
        eq_ex = jnp.dot(eq.astype(BF16), upper, preferred_element_type=F32) + carry_eq
        carry_eq = carry_eq + jnp.sum(eq, axis=1, keepdims=True)
        m = jnp.where(gt | ((eq > 0.0) & (eq_ex < need)), 1.0, 0.0).astype(F32)
        cnt = jnp.sum(m, axis=0, keepdims=True)
        stacked = jnp.concatenate([m, cnt, jnp.zeros((e - 1, ch), F32)], axis=0)
        pre = jnp.dot(stacked.astype(BF16), upper, preferred_element_type=F32) + carry_m
        carry_m = carry_m + jnp.sum(stacked, axis=1, keepdims=True)
        rank = pre[:e]
        tokoff = pre[e:e + 1]
        ex_e = jnp.dot(lower, m.astype(BF16), preferred_element_type=F32)
        tok_ref[0:1, sl] = tokoff.astype(I32)
        tok_ref[1:2, sl] = (tokoff + cnt).astype(I32)
        lane = lax.broadcasted_iota(I32, (e, ch), 1) + ci * ch
        d_a[:, sl] = jnp.where(m > 0.0, lane - rank.astype(I32), 0)
        n_a[:, sl] = lane
        p_a[:, sl] = (tokoff + ex_e).astype(I32)
    g_a[...] = aff_ref[...]

    bufs = [(d_a, n_a, p_a, g_a), (d_b, n_b, p_b, g_b)]
    for b in range(int(math.log2(n))):
        src, dst = bufs[b % 2], bufs[(b + 1) % 2]
        sh = 1 << b
        d = src[0][...]
        moving = (lax.shift_right_logical(d, b) & 1) == 1
        d_in = pltpu.roll(d, n - sh, axis=1)
        arrive = (lax.shift_right_logical(d_in, b) & 1) == 1
        dst[0][...] = jnp.where(arrive, d_in, jnp.where(moving, 0, d))
        for t in (1, 2, 3):
            v = src[t][...]
            dst[t][...] = jnp.where(arrive, pltpu.roll(v, n - sh, axis=1), v)
    fin = bufs[int(math.log2(n)) % 2]
    idx_ref[...] = fin[1][:, :cap]
    pos_ref[...] = fin[2][:, :cap]
    gate_ref[...] = fin[3][:, :cap]


def _route(aff_t, cap):
    e, n = aff_t.shape
    assert n & (n - 1) == 0 or True
    full = lambda shape: pl.BlockSpec(shape, lambda: (0, 0))
    big_i = pltpu.VMEM((e, n), I32)
    return pl.pallas_call(
        functools.partial(_route_body, n=n, cap=cap),
        in_specs=[full((e, n))],
        out_specs=[full((e, cap)), full((e, cap)), full((e, cap)), full((2, n))],
        out_shape=[jax.ShapeDtypeStruct((e, cap), I32),
                   jax.ShapeDtypeStruct((e, cap), I32),
                   jax.ShapeDtypeStruct((e, cap), F32),
                   jax.ShapeDtypeStruct((2, n), I32)],
        scratch_shapes=[big_i, big_i, big_i, big_i, big_i, big_i,
                        pltpu.VMEM((e, n), F32), pltpu.VMEM((e, n), F32)],
        compiler_params=_params(None, VMEM_LIMIT),
        name="route",
    )(aff_t)


def _row_to_col(row, tm):
    r = lax.broadcasted_iota(I32, (tm, tm), 0)
    c = lax.broadcasted_iota(I32, (tm, tm), 1)
    return jnp.sum(jnp.where(r == c, row, jnp.zeros_like(row)), axis=1, keepdims=True)


def _ffn_body(idx_cur, idx_nxt, pos_prv, pos_cur, gate_ref, h_hbm, wg_ref, wu_ref, wd_ref, ys_hbm,
              xa, xb, ya, yb, sems, *, tm):
    s = pl.program_id(0)
    last = pl.num_programs(0) - 1
    g_a, g_b, s_a, s_b = sems.at[0], sems.at[1], sems.at[2], sems.at[3]

    def gather(idx_ref, t, buf, sem, unrolled=True):
        def one(i):
            pltpu.make_async_copy(h_hbm.at[idx_ref[t, 0, i]], buf.at[i], sem).start()
        if unrolled:
            for i in range(tm):
                one(i)
        else:
            pl.loop(0, tm)(one)

    def scatter(pos_ref, t, buf, sem, unrolled=True):
        def one(i):
            pltpu.make_async_copy(buf.at[i], ys_hbm.at[pos_ref[t, 0, i]], sem).start()
        if unrolled:
            for i in range(tm):
                one(i)
        else:
            pl.loop(0, tm)(one)

    def wait_gather(buf, sem):
        pltpu.make_async_copy(h_hbm.at[pl.ds(0, tm)], buf, sem).wait()

    def wait_scatter(buf, sem):
        pltpu.make_async_copy(buf, ys_hbm.at[pl.ds(0, tm)], sem).wait()

    def compute(xbuf, ybuf, t):
        x = _load_tiles_as_rows(xbuf).astype(BF16)
        g = jnp.dot(x, wg_ref[0], preferred_element_type=F32)
        u = jnp.dot(x, wu_ref[0], preferred_element_type=F32)
        hid = (g * jax.nn.sigmoid(g) * u).astype(BF16)
        y = jnp.dot(hid, wd_ref[0], preferred_element_type=F32)
        y = y * _row_to_col(gate_ref[t], tm)
        _store_rows_as_tiles(ybuf, y.astype(BF16).astype(F32))

    @pl.when(s == 0)
    def _():
        gather(idx_cur, 0, xa, g_a, unrolled=False)
        yb[...] = jnp.zeros_like(yb)

    wait_gather(xa, g_a)

    @pl.when(s > 0)
    def _():
        wait_scatter(ya, s_a)

    gather(idx_cur, 1, xb, g_b)
    scatter(pos_prv, 1, yb, s_b)
    compute(xa, ya, 0)

    wait_gather(xb, g_b)
    wait_scatter(yb, s_b)
    gather(idx_nxt, 0, xa, g_a)
    scatter(pos_cur, 0, ya, s_a)
    compute(xb, yb, 1)

    @pl.when(s == last)
    def _():
        scatter(pos_cur, 1, yb, s_b, unrolled=False)
        wait_gather(xa, g_a)
        wait_scatter(ya, s_a)
        wait_scatter(yb, s_b)


def _ffn(idx_c, pos_c, gate_c, h2, wg, wu, wd, tm):
    e, cap = idx_c.shape
    spe = cap // (2 * tm)
    steps = e * spe
    n_slots = e * cap
    idx3 = idx_c.reshape(2 * steps, 1, tm)
    pos3 = pos_c.reshape(2 * steps, 1, tm)
    gate3 = gate_c.reshape(2 * steps, 1, tm)
    cur = lambda i: (i, 0, 0)
    prv = lambda i: (jnp.maximum(i - 1, 0), 0, 0)
    nxt = lambda i: (jnp.minimum(i + 1, steps - 1), 0, 0)
    smem = lambda f: pl.BlockSpec((2, 1, tm), f, memory_space=pltpu.SMEM)
    wspec = lambda a, b: pl.BlockSpec((1, a, b), lambda i: (i // spe, 0, 0))
    buf = pltpu.VMEM((tm, ROW_TILE, LANES), F32)
    return pl.pallas_call(
        functools.partial(_ffn_body, tm=tm),
        grid=(steps,),
        in_specs=[smem(cur), smem(nxt), smem(prv), smem(cur),
                  pl.BlockSpec((2, 1, tm), cur),
                  pl.BlockSpec(memory_space=pl.ANY),
                  wspec(D_MODEL, D_EXPERT), wspec(D_MODEL, D_EXPERT), wspec(D_EXPERT, D_MODEL)],
        out_specs=pl.BlockSpec(memory_space=pl.ANY),
        out_shape=jax.ShapeDtypeStruct((n_slots, ROW_TILE, LANES), F32),
        scratch_shapes=[buf, buf, buf, buf, pltpu.SemaphoreType.DMA((4,))],
        compiler_params=_params(("arbitrary",), VMEM_LIMIT),
        name="ffn",
    )(idx3, idx3, pos3, pos3, gate3, h2, wg, wu, wd)


def _combine_body(toff_ref, x1_ref, tok_ref, g_ref, ys_hbm, o_ref, win, extra, sems,
                  *, tt, ws, n_slots):
    i = pl.program_id(0)
    last = pl.num_programs(0) - 1
    buf = i % 2

    def window_start(nominal):
        return pl.multiple_of(jnp.minimum(nominal, n_slots - ws), 8)

    def first_fetch(tile, b):
        start = window_start((toff_ref[tile] // 8) * 8)
        return pltpu.make_async_copy(ys_hbm.at[pl.ds(start, ws)], win.at[b], sems.at[b])

    @pl.when(i == 0)
    def _():
        first_fetch(0, 0).start()

    first_fetch(i, buf).wait()

    @pl.when(i < last)
    def _():
        first_fetch(i + 1, 1 - buf).start()

    hi = toff_ref[i + 1]
    base = (toff_ref[i] // 8) * 8
    nwin = (hi - base + ws - 1) // ws
    t_lo = _row_to_col(tok_ref[0:1, :], tt)
    t_hi = _row_to_col(tok_ref[1:2, :], tt)
    lane = lax.broadcasted_iota(I32, (tt, ws), 1)

    def segment_sum(rows_ref, start, nominal):
        slot = lane + start
        onehot = (slot >= t_lo) & (slot < t_hi) & (slot >= nominal)
        sel = jnp.where(onehot, 1.0, 0.0).astype(BF16)
        rows = _load_tiles_as_rows(rows_ref).astype(BF16)
        return jnp.dot(sel, rows, preferred_element_type=F32)

    def later_window(w, acc):
        nominal = base + w * ws
        start = window_start(nominal)
        cp = pltpu.make_async_copy(ys_hbm.at[pl.ds(start, ws)], extra, sems.at[2])
        cp.start()
        cp.wait()
        return acc + segment_sum(extra, start, nominal)

    moe = segment_sum(win.at[buf], window_start(base), base)
    moe = lax.fori_loop(1, nwin, later_window, moe)
    x2 = x1_ref[...] + moe
    ms = jnp.mean(x2 * x2, axis=-1, keepdims=True)
    o_ref[...] = (x2 * lax.rsqrt(ms + EPS)) * g_ref[...]


def _combine(x1, ys, tokinfo, gain, tt, ws):
    n = x1.shape[0]
    n_slots = ys.shape[0]
    tile_off = jnp.concatenate([tokinfo[0, ::tt], jnp.full((1,), n_slots, I32)])
    return pl.pallas_call(
        functools.partial(_combine_body, tt=tt, ws=ws, n_slots=n_slots),
        grid_spec=pltpu.PrefetchScalarGridSpec(
            num_scalar_prefetch=1,
            grid=(n // tt,),
            in_specs=[pl.BlockSpec((tt, D_MODEL), lambda i, s: (i, 0)),
                      pl.BlockSpec((2, tt), lambda i, s: (0, i)),
                      pl.BlockSpec((1, D_MODEL), lambda i, s: (0, 0)),
                      pl.BlockSpec(memory_space=pl.ANY)],
            out_specs=pl.BlockSpec((tt, D_MODEL), lambda i, s: (i, 0)),
            scratch_shapes=[pltpu.VMEM((2, ws, ROW_TILE, LANES), F32),
                            pltpu.VMEM((ws, ROW_TILE, LANES), F32),
                            pltpu.SemaphoreType.DMA((3,))]),
        out_shape=jax.ShapeDtypeStruct((n, D_MODEL), F32),
        compiler_params=_params(("arbitrary",), VMEM_LIMIT),
        name="combine",
    )(tile_off, x1, tokinfo, gain, ys)


def _tile(n, pref):
    t = pref
    while n % t:
        t //= 2
    return t


def _trunk(x, p):
    b, s, d = x.shape
    n = b * s
    cap = CAPACITY_FACTOR * n // N_EXPERTS
    xf = x.reshape(n, d)
    tm = _tile(n, 512)
    zf, q, k, v = _inproj(xf, p["norm_mix"], p["w_in"], tm)
    f = _fourier(zf.reshape(b, s, FOURIER_WIDTH), p["w_fourier"])
    a = _attention(q.reshape(b, s, ATTN_WIDTH), k.reshape(b, s, KV_WIDTH),
                   v.reshape(b, s, KV_WIDTH), p["bias"], p["sink"], _tile(s, 512))
    x1, h2, aff_t = _outproj(xf, f.reshape(n, FOURIER_WIDTH), a.reshape(n, ATTN_WIDTH),
                             p["w_out"], p["norm_ffn"], p["w_router"], tm)
    idx_c, pos_c, gate_c, tokinfo = _route(aff_t, cap)
    ys = _ffn(idx_c, pos_c, gate_c, h2, p["w_gate"], p["w_up"], p["w_down"], _tile(cap // 2, 512))
    tt = _tile(n, 512)
    ws = min(CAPACITY_FACTOR * tt + tt // 4, N_EXPERTS * cap)
    y = _combine(x1, ys, tokinfo, p["norm_final"], tt, ws)
    return y.reshape(b, s, d)


def kernel(x_prompt, x_sample, w_in, w_fourier, attn_sink, rel_bias, w_out, norm_mix, norm_ffn,
           w_router, w_gate, w_up, w_down, norm_final):
    assert w_in.shape[0] == 1
    wr = w_router[0].T
    wr_hi = wr.astype(BF16)
    wr_lo = (wr - wr_hi.astype(F32)).astype(BF16)
    p = dict(
        norm_mix=norm_mix[0].reshape(1, D_MODEL),
        norm_ffn=norm_ffn[0].reshape(1, D_MODEL),
        norm_final=norm_final.reshape(1, D_MODEL),
        w_in=w_in[0].astype(BF16),
        w_fourier=w_fourier[0].astype(BF16),
        w_out=w_out[0].astype(BF16),
        w_router=jnp.concatenate([wr_hi, wr_lo], axis=0),
        w_gate=w_gate[0].astype(BF16),
        w_up=w_up[0].astype(BF16),
        w_down=w_down[0].astype(BF16),
        sink=attn_sink[0],
        bias=_bias_table(rel_bias),
    )
    return _trunk(x_prompt, p), _trunk(x_sample, p)
```

```python
import functools
import math

import numpy as np
import jax
import jax.numpy as jnp
from jax import lax
from jax.experimental import pallas as pl
from jax.experimental.pallas import tpu as pltpu

F32 = jnp.float32
BF16 = jnp.bfloat16
I32 = jnp.int32

D_MODEL = 1024
FOURIER_GROUPS = 4
GROUP_DIM = 128
FOURIER_WIDTH = FOURIER_GROUPS * GROUP_DIM
N_HEADS = 8
N_KV_HEADS = 2
HEAD_DIM = 64
ATTN_WIDTH = N_HEADS * HEAD_DIM
KV_WIDTH = N_KV_HEADS * HEAD_DIM
WINDOW = 128
BLOCK = 128
N_BUCKETS = 32
MAX_DISTANCE = 128
IN_WIDTH = FOURIER_WIDTH + ATTN_WIDTH + 2 * KV_WIDTH
N_EXPERTS = 16
CAPACITY_FACTOR = 2
D_EXPERT = 1024
EPS = 1e-6

LANES = 128
PREFIX_CHUNK = 256
VMEM_LIMIT = 56 * 1024 * 1024


ROW_TILE = D_MODEL // LANES


def _params(sem, vmem=None):
    return pltpu.CompilerParams(dimension_semantics=sem, vmem_limit_bytes=vmem)


def _store_rows_as_tiles(ref, val):
    for c in range(ROW_TILE):
        ref[:, c, :] = val[:, c * LANES:(c + 1) * LANES]


def _load_tiles_as_rows(ref):
    return jnp.concatenate([ref[:, c, :] for c in range(ROW_TILE)], axis=1)


def _inproj_body(x_ref, g_ref, w_ref, zf_ref, q_ref, k_ref, v_ref):
    x = x_ref[...]
    ms = jnp.mean(x * x, axis=-1, keepdims=True)
    h = (x * lax.rsqrt(ms + EPS)) * g_ref[...]
    z = jnp.dot(h.astype(BF16), w_ref[...], preferred_element_type=F32)
    o = FOURIER_WIDTH
    zf_ref[...] = z[:, :o].astype(BF16)
    q_ref[...] = (z[:, o:o + ATTN_WIDTH] * (HEAD_DIM ** -0.5)).astype(BF16)
    o += ATTN_WIDTH
    k_ref[...] = z[:, o:o + KV_WIDTH].astype(BF16)
    o += KV_WIDTH
    v_ref[...] = z[:, o:o + KV_WIDTH].astype(BF16)


def _inproj(xf, gain, w_in, tm):
    n = xf.shape[0]
    row = lambda w: pl.BlockSpec((tm, w), lambda i: (i, 0))
    return pl.pallas_call(
        _inproj_body,
        grid=(n // tm,),
        in_specs=[row(D_MODEL),
                  pl.BlockSpec((1, D_MODEL), lambda i: (0, 0)),
                  pl.BlockSpec((D_MODEL, IN_WIDTH), lambda i: (0, 0))],
        out_specs=[row(FOURIER_WIDTH), row(ATTN_WIDTH), row(KV_WIDTH), row(KV_WIDTH)],
        out_shape=[jax.ShapeDtypeStruct((n, FOURIER_WIDTH), BF16),
                   jax.ShapeDtypeStruct((n, ATTN_WIDTH), BF16),
                   jax.ShapeDtypeStruct((n, KV_WIDTH), BF16),
                   jax.ShapeDtypeStruct((n, KV_WIDTH), BF16)],
        compiler_params=_params(("parallel",), VMEM_LIMIT),
        name="inproj",
    )(xf, gain, w_in)


def _dft_tables(s):
    s2 = BLOCK
    s1 = s // s2
    a = np.arange(s1, dtype=np.float64)
    ang1 = 2.0 * np.pi * np.outer(a, a) / s1
    f1 = np.concatenate([np.cos(ang1), -np.sin(ang1)], axis=0)
    b = np.arange(s2, dtype=np.float64)
    ang2 = 2.0 * np.pi * np.outer(b, b) / s2
    angt = 2.0 * np.pi * np.outer(a, b) / s
    c = np.arange(GROUP_DIM, dtype=np.float64)
    angc = 2.0 * np.pi * np.outer(c, c) / GROUP_DIM
    cs = np.concatenate([np.cos(angc), np.sin(angc)], axis=0)
    return dict(
        f1=jnp.asarray(f1, BF16),
        f2r=jnp.asarray(np.cos(ang2), F32), f2i=jnp.asarray(-np.sin(ang2), F32),
        twr=jnp.asarray(np.cos(angt), F32), twi=jnp.asarray(-np.sin(angt), F32),
        cs=jnp.asarray(cs, BF16))


def _dft1_body(f_ref, x_ref, o_ref):
    s1 = x_ref.shape[1]
    res = jnp.dot(f_ref[...], x_ref[0], preferred_element_type=F32)
    o_ref[0, 0] = res[:s1].astype(BF16)
    o_ref[0, 1] = res[s1:].astype(BF16)


def _dft1(zf3, f1, cb):
    b, s1, w = zf3.shape
    return pl.pallas_call(
        _dft1_body,
        grid=(b, w // cb),
        in_specs=[pl.BlockSpec((2 * s1, s1), lambda i, j: (0, 0)),
                  pl.BlockSpec((1, s1, cb), lambda i, j: (i, 0, j))],
        out_specs=pl.BlockSpec((1, 2, s1, cb), lambda i, j: (i, 0, 0, j)),
        out_shape=jax.ShapeDtypeStruct((b, 2, s1, w), BF16),
        compiler_params=_params(("parallel", "parallel"), VMEM_LIMIT),
        name="dft1",
    )(f1, zf3)


def _dft2_body(a_ref, f2r_ref, f2i_ref, twr_ref, twi_ref, cs_ref, wf_ref, o_ref, *, kb, scale):
    s2 = BLOCK
    fr = f2r_ref[...]
    fi = f2i_ref[...]
    for j in range(kb):
        tr = twr_ref[0, j:j + 1, :]
        ti = twi_ref[0, j:j + 1, :]
        gr = fr * tr - fi * ti
        gi = fr * ti + fi * tr
        lhs = jnp.concatenate(
            [jnp.concatenate([gr, -gi], axis=1), jnp.concatenate([gi, gr], axis=1)],
            axis=0).astype(BF16)
        rhs = jnp.concatenate([a_ref[0, 0, j], a_ref[0, 1, j]], axis=0)
        y = jnp.dot(lhs, rhs, preferred_element_type=F32)
        outs = []
        for g in range(FOURIER_GROUPS):
            sl = slice(g * GROUP_DIM, (g + 1) * GROUP_DIM)
            cat = jnp.concatenate([y[:s2, sl], y[s2:, sl]], axis=1).astype(BF16)
            re = jnp.dot(cat, cs_ref[...], preferred_element_type=F32) * scale
            outs.append(jnp.dot(re.astype(BF16), wf_ref[g], preferred_element_type=F32))
        o_ref[0, :, j, :] = jnp.concatenate(outs, axis=1)


def _dft2(a5, tabs, wf, kb, scale):
    b, _, s1, s2, w = a5.shape
    const = lambda shape: pl.BlockSpec(shape, lambda i, j: (0,) * len(shape))
    twr = tabs["twr"].reshape(s1 // kb, kb, s2)
    twi = tabs["twi"].reshape(s1 // kb, kb, s2)
    return pl.pallas_call(
        functools.partial(_dft2_body, kb=kb, scale=scale),
        grid=(b, s1 // kb),
        in_specs=[pl.BlockSpec((1, 2, kb, s2, w), lambda i, j: (i, 0, j, 0, 0)),
                  const((s2, s2)), const((s2, s2)),
                  pl.BlockSpec((1, kb, s2), lambda i, j: (j, 0, 0)),
                  pl.BlockSpec((1, kb, s2), lambda i, j: (j, 0, 0)),
                  const((2 * GROUP_DIM, GROUP_DIM)),
                  const((FOURIER_GROUPS, GROUP_DIM, GROUP_DIM))],
        out_specs=pl.BlockSpec((1, s2, kb, w), lambda i, j: (i, 0, j, 0)),
        out_shape=jax.ShapeDtypeStruct((b, s2, s1, w), F32),
        compiler_params=_params(("parallel", "parallel"), VMEM_LIMIT),
        name="dft2",
    )(a5, tabs["f2r"], tabs["f2i"], twr, twi, tabs["cs"], wf)


def _fourier(zf, wf_bf16):
    b, s, w = zf.shape
    s2 = BLOCK
    s1 = s // s2
    tabs = _dft_tables(s)
    a = _dft1(zf.reshape(b, s1, s2 * w), tabs["f1"], cb=2048)
    f = _dft2(a.reshape(b, 2, s1, s2, w), tabs, wf_bf16, kb=8,
              scale=1.0 / math.sqrt(s * GROUP_DIM))
    return f.reshape(b, s, w)


def _bucket_table():
    qi = np.arange(BLOCK)[:, None]
    kj = np.arange(3 * BLOCK)[None, :]
    rel = kj - BLOCK - qi
    nb = N_BUCKETS // 2
    max_exact = nb // 2
    n = np.abs(rel)
    large = max_exact + np.floor(
        np.log(np.maximum(n, 1).astype(np.float64) / max_exact)
        / math.log(MAX_DISTANCE / max_exact) * (nb - max_exact) + 1e-9).astype(np.int64)
    large = np.minimum(large, nb - 1)
    bucket = np.where(rel > 0, nb, 0) + np.where(n < max_exact, n, large)
    return np.where(n <= WINDOW, bucket, -1).astype(np.int32)


def _bias_body(rb_ref, bucket_ref, o_ref):
    bk = bucket_ref[...]
    for h in range(N_HEADS):
        acc = jnp.full(bk.shape, -jnp.inf, F32)
        for b in range(N_BUCKETS):
            acc = jnp.where(bk == b, rb_ref[b, h], acc)
        o_ref[h] = acc


def _bias_table(rel_bias):
    bucket = jnp.asarray(_bucket_table())
    return pl.pallas_call(
        _bias_body,
        in_specs=[pl.BlockSpec(memory_space=pltpu.SMEM),
                  pl.BlockSpec(bucket.shape, lambda: (0, 0))],
        out_specs=pl.BlockSpec((N_HEADS,) + bucket.shape, lambda: (0, 0, 0)),
        out_shape=jax.ShapeDtypeStruct((N_HEADS,) + bucket.shape, F32),
        name="bias_table",
    )(rel_bias.astype(F32), bucket)


def _attn_body(sink_ref, q_ref, kp_ref, kc_ref, kn_ref, vp_ref, vc_ref, vn_ref, bias_ref,
               o_ref, *, tq):
    j = pl.program_id(1)
    last = pl.num_programs(1) - 1
    nsb = tq // BLOCK
    kext = jnp.concatenate([kp_ref[0, tq - BLOCK:, :], kc_ref[0], kn_ref[0, :BLOCK, :]], axis=0)
    vext = jnp.concatenate([vp_ref[0, tq - BLOCK:, :], vc_ref[0], vn_ref[0, :BLOCK, :]], axis=0)
    col = lax.broadcasted_iota(I32, (BLOCK, 3 * BLOCK), 1)
    group = N_HEADS // N_KV_HEADS
    for sb in range(nsb):
        kb = kext[sb * BLOCK:(sb + 3) * BLOCK]
        vb = vext[sb * BLOCK:(sb + 3) * BLOCK]
        outs = []
        for h in range(N_HEADS):
            kh = h // group
            qh = q_ref[0, sb * BLOCK:(sb + 1) * BLOCK, h * HEAD_DIM:(h + 1) * HEAD_DIM]
            s = lax.dot_general(qh, kb[:, kh * HEAD_DIM:(kh + 1) * HEAD_DIM],
                                (((1,), (1,)), ((), ())), preferred_element_type=F32)
            s = s + bias_ref[h]
            if sb == 0:
                s = jnp.where((col >= BLOCK) | (j > 0), s, -jnp.inf)
            if sb == nsb - 1:
                s = jnp.where((col < 2 * BLOCK) | (j < last), s, -jnp.inf)
            sink = sink_ref[h]
            m = jnp.maximum(jnp.max(s, axis=-1, keepdims=True), sink)
            p = jnp.exp(s - m)
            denom = jnp.sum(p, axis=-1, keepdims=True) + jnp.exp(sink - m)
            pn = (p / denom).astype(BF16)
            outs.append(jnp.dot(pn, vb[:, kh * HEAD_DIM:(kh + 1) * HEAD_DIM],
                                preferred_element_type=F32))
        o_ref[0, sb * BLOCK:(sb + 1) * BLOCK, :] = jnp.concatenate(outs, axis=1).astype(BF16)


def _attention(q, k, v, bias, sink, tq):
    b, s, _ = q.shape
    nt = s // tq
    kv = lambda f: pl.BlockSpec((1, tq, KV_WIDTH), f)
    prev = lambda i, j: (i, jnp.maximum(j - 1, 0), 0)
    cur = lambda i, j: (i, j, 0)
    nxt = lambda i, j: (i, jnp.minimum(j + 1, nt - 1), 0)
    return pl.pallas_call(
        functools.partial(_attn_body, tq=tq),
        grid=(b, nt),
        in_specs=[pl.BlockSpec(memory_space=pltpu.SMEM),
                  pl.BlockSpec((1, tq, ATTN_WIDTH), cur),
                  kv(prev), kv(cur), kv(nxt), kv(prev), kv(cur), kv(nxt),
                  pl.BlockSpec(bias.shape, lambda i, j: (0, 0, 0))],
        out_specs=pl.BlockSpec((1, tq, ATTN_WIDTH), cur),
        out_shape=jax.ShapeDtypeStruct((b, s, ATTN_WIDTH), BF16),
        compiler_params=_params(("parallel", "parallel"), VMEM_LIMIT),
        name="attn",
    )(sink.astype(F32), q, k, k, k, v, v, v, bias)


def _outproj_body(x_ref, f_ref, a_ref, wo_ref, g_ref, wr_ref, x1_ref, h2_ref, aff_ref):
    mix = jnp.concatenate([f_ref[...].astype(BF16), a_ref[...]], axis=1)
    x1 = x_ref[...] + jnp.dot(mix, wo_ref[...], preferred_element_type=F32)
    x1_ref[...] = x1
    ms = jnp.mean(x1 * x1, axis=-1, keepdims=True)
    h2 = (x1 * lax.rsqrt(ms + EPS)) * g_ref[...]
    _store_rows_as_tiles(h2_ref, h2)
    hi = h2.astype(BF16)
    lo = (h2 - hi.astype(F32)).astype(BF16)
    nt = (((1,), (1,)), ((), ()))
    o1 = lax.dot_general(wr_ref[...], hi, nt, preferred_element_type=F32)
    o2 = lax.dot_general(wr_ref[:N_EXPERTS], lo, nt, preferred_element_type=F32)
    logits = o1[:N_EXPERTS] + (o1[N_EXPERTS:] + o2)
    e = jnp.exp(logits - jnp.max(logits, axis=0, keepdims=True))
    aff_ref[...] = e / jnp.sum(e, axis=0, keepdims=True)


def _outproj(xf, f, a, w_out, gain, wr_split, tm):
    n = xf.shape[0]
    row = lambda w: pl.BlockSpec((tm, w), lambda i: (i, 0))
    const = lambda shape: pl.BlockSpec(shape, lambda i: (0, 0))
    return pl.pallas_call(
        _outproj_body,
        grid=(n // tm,),
        in_specs=[row(D_MODEL), row(FOURIER_WIDTH), row(ATTN_WIDTH),
                  const((D_MODEL, D_MODEL)), const((1, D_MODEL)),
                  const((2 * N_EXPERTS, D_MODEL))],
        out_specs=[row(D_MODEL),
                   pl.BlockSpec((tm, ROW_TILE, LANES), lambda i: (i, 0, 0)),
                   pl.BlockSpec((N_EXPERTS, tm), lambda i: (0, i))],
        out_shape=[jax.ShapeDtypeStruct((n, D_MODEL), F32),
                   jax.ShapeDtypeStruct((n, ROW_TILE, LANES), F32),
                   jax.ShapeDtypeStruct((N_EXPERTS, n), F32)],
        compiler_params=_params(("parallel",), VMEM_LIMIT),
        name="outproj",
    )(xf, f, a, w_out, gain, wr_split)


def _route_body(aff_ref, idx_ref, pos_ref, gate_ref, tok_ref,
                d_a, d_b, n_a, n_b, p_a, p_b, g_a, g_b, *, n, cap):
    e = N_EXPERTS
    ch = PREFIX_CHUNK

    def search(i, tau):
        cand = tau | jnp.left_shift(jnp.int32(1), 30 - i)
        keys = lax.bitcast_convert_type(aff_ref[...], I32)
        cnt = jnp.sum((keys >= cand).astype(I32), axis=1, keepdims=True)
        return jnp.where(cnt >= cap, cand, tau)

    tau = lax.fori_loop(0, 31, search, jnp.zeros((e, 1), I32))
    keys = lax.bitcast_convert_type(aff_ref[...], I32)
    n_gt = jnp.sum((keys > tau).astype(I32), axis=1, keepdims=True)
    need = (cap - n_gt).astype(F32)

    r = lax.broadcasted_iota(I32, (ch, ch), 0)
    c = lax.broadcasted_iota(I32, (ch, ch), 1)
    upper = (r < c).astype(BF16)
    r16 = lax.broadcasted_iota(I32, (e, e), 0)
    c16 = lax.broadcasted_iota(I32, (e, e), 1)
    lower = (c16 < r16).astype(BF16)

    carry_eq = jnp.zeros((e, 1), F32)
    carry_m = jnp.zeros((2 * e, 1), F32)
    for ci in range(n // ch):
        sl = slice(ci * ch, (ci + 1) * ch)
        k = lax.bitcast_convert_type(aff_ref[:, sl], I32)
        gt = k > tau
        eq = (k == tau).astype(F32)
        eq_ex = jnp.dot(eq.astype(BF16), upper, preferred_element_type=F32) + carry_eq
        carry_eq = carry_eq + jnp.sum(eq, axis=1, keepdims=True)
        m = jnp.where(gt | ((eq > 0.0) & (eq_ex < need)), 1.0, 0.0).astype(F32)
        cnt = jnp.sum(m, axis=0, keepdims=True)
        stacked = jnp.concatenate([m, cnt, jnp.zeros((e - 1, ch), F32)], axis=0)
        pre = jnp.dot(stacked.astype(BF16), upper, preferred_element_type=F32) + carry_m
        carry_m = carry_m + jnp.sum(stacked, axis=1, keepdims=True)
        rank = pre[:e]
        tokoff = pre[e:e + 1]
        ex_e = jnp.dot(lower, m.astype(BF16), preferred_element_type=F32)
        tok_ref[0:1, sl] = tokoff.astype(I32)
        tok_ref[1:2, sl] = (tokoff + cnt).astype(I32)
        lane = lax.broadcasted_iota(I32, (e, ch), 1) + ci * ch
        d_a[:, sl] = jnp.where(m > 0.0, lane - rank.astype(I32), 0)
        n_a[:, sl] = lane
        p_a[:, sl] = (tokoff + ex_e).astype(I32)
    g_a[...] = aff_ref[...]

    bufs = [(d_a, n_a, p_a, g_a), (d_b, n_b, p_b, g_b)]
    for b in range(int(math.log2(n))):
        src, dst = bufs[b % 2], bufs[(b + 1) % 2]
        sh = 1 << b
        d = src[0][...]
        moving = (lax.shift_right_logical(d, b) & 1) == 1
        d_in = pltpu.roll(d, n - sh, axis=1)
        arrive = (lax.shift_right_logical(d_in, b) & 1) == 1
        dst[0][...] = jnp.where(arrive, d_in, jnp.where(moving, 0, d))
        for t in (1, 2, 3):
            v = src[t][...]
            dst[t][...] = jnp.where(arrive, pltpu.roll(v, n - sh, axis=1), v)
    fin = bufs[int(math.log2(n)) % 2]
    idx_ref[...] = fin[1][:, :cap]
    pos_ref[...] = fin[2][:, :cap]
    gate_ref[...] = fin[3][:, :cap]


def _route(aff_t, cap):
    e, n = aff_t.shape
    assert n & (n - 1) == 0 or True
    full = lambda shape: pl.BlockSpec(shape, lambda: (0, 0))
    big_i = pltpu.VMEM((e, n), I32)
    return pl.pallas_call(
        functools.partial(_route_body, n=n, cap=cap),
        in_specs=[full((e, n))],
        out_specs=[full((e, cap)), full((e, cap)), full((e, cap)), full((2, n))],
        out_shape=[jax.ShapeDtypeStruct((e, cap), I32),
                   jax.ShapeDtypeStruct((e, cap), I32),
                   jax.ShapeDtypeStruct((e, cap), F32),
                   jax.ShapeDtypeStruct((2, n), I32)],
        scratch_shapes=[big_i, big_i, big_i, big_i, big_i, big_i,
                        pltpu.VMEM((e, n), F32), pltpu.VMEM((e, n), F32)],
        compiler_params=_params(None, VMEM_LIMIT),
        name="route",
    )(aff_t)


def _row_to_col(row, tm):
    r = lax.broadcasted_iota(I32, (tm, tm), 0)
    c = lax.broadcasted_iota(I32, (tm, tm), 1)
    return jnp.sum(jnp.where(r == c, row, jnp.zeros_like(row)), axis=1, keepdims=True)


def _ffn_body(idx_cur, idx_nxt, pos_prv, pos_cur, gate_ref, h_hbm, wg_ref, wu_ref, wd_ref, ys_hbm,
              xa, xb, ya, yb, sems, *, tm):
    s = pl.program_id(0)
    last = pl.num_programs(0) - 1
    g_a, g_b, s_a, s_b = sems.at[0], sems.at[1], sems.at[2], sems.at[3]

    def gather(idx_ref, t, buf, sem, unrolled=True):
        def one(i):
            pltpu.make_async_copy(h_hbm.at[idx_ref[t, 0, i]], buf.at[i], sem).start()
        if unrolled:
            for i in range(tm):
                one(i)
        else:
            pl.loop(0, tm)(one)

    def scatter(pos_ref, t, buf, sem, unrolled=True):
        def one(i):
            pltpu.async_copy(buf.at[i], ys_hbm.at[pos_ref[t, 0, i]], sem, priority=1)
        if unrolled:
            for i in range(tm):
                one(i)
        else:
            pl.loop(0, tm)(one)

    def wait_gather(buf, sem):
        pltpu.make_async_copy(h_hbm.at[pl.ds(0, tm)], buf, sem).wait()

    def wait_scatter(buf, sem):
        pltpu.make_async_copy(buf, ys_hbm.at[pl.ds(0, tm)], sem).wait()

    def compute(xbuf, ybuf, t):
        x = _load_tiles_as_rows(xbuf).astype(BF16)
        g = jnp.dot(x, wg_ref[0], preferred_element_type=F32)
        u = jnp.dot(x, wu_ref[0], preferred_element_type=F32)
        hid = (g * jax.nn.sigmoid(g) * u).astype(BF16)
        y = jnp.dot(hid, wd_ref[0], preferred_element_type=F32)
        y = y * _row_to_col(gate_ref[t], tm)
        _store_rows_as_tiles(ybuf, y.astype(BF16).astype(F32))

    @pl.when(s == 0)
    def _():
        gather(idx_cur, 0, xa, g_a, unrolled=False)
        yb[...] = jnp.zeros_like(yb)

    wait_gather(xa, g_a)

    @pl.when(s > 0)
    def _():
        wait_scatter(ya, s_a)

    gather(idx_cur, 1, xb, g_b)
    scatter(pos_prv, 1, yb, s_b)
    compute(xa, ya, 0)

    wait_gather(xb, g_b)
    wait_scatter(yb, s_b)
    gather(idx_nxt, 0, xa, g_a)
    scatter(pos_cur, 0, ya, s_a)
    compute(xb, yb, 1)

    @pl.when(s == last)
    def _():
        scatter(pos_cur, 1, yb, s_b, unrolled=False)
        wait_gather(xa, g_a)
        wait_scatter(ya, s_a)
        wait_scatter(yb, s_b)


def _ffn(idx_c, pos_c, gate_c, h2, wg, wu, wd, tm):
    e, cap = idx_c.shape
    spe = cap // (2 * tm)
    steps = e * spe
    n_slots = e * cap
    idx3 = idx_c.reshape(2 * steps, 1, tm)
    pos3 = pos_c.reshape(2 * steps, 1, tm)
    gate3 = gate_c.reshape(2 * steps, 1, tm)
    cur = lambda i: (i, 0, 0)
    prv = lambda i: (jnp.maximum(i - 1, 0), 0, 0)
    nxt = lambda i: (jnp.minimum(i + 1, steps - 1), 0, 0)
    smem = lambda f: pl.BlockSpec((2, 1, tm), f, memory_space=pltpu.SMEM)
    wspec = lambda a, b: pl.BlockSpec((1, a, b), lambda i: (i // spe, 0, 0))
    buf = pltpu.VMEM((tm, ROW_TILE, LANES), F32)
    return pl.pallas_call(
        functools.partial(_ffn_body, tm=tm),
        grid=(steps,),
        in_specs=[smem(cur), smem(nxt), smem(prv), smem(cur),
                  pl.BlockSpec((2, 1, tm), cur),
                  pl.BlockSpec(memory_space=pl.ANY),
                  wspec(D_MODEL, D_EXPERT), wspec(D_MODEL, D_EXPERT), wspec(D_EXPERT, D_MODEL)],
        out_specs=pl.BlockSpec(memory_space=pl.ANY),
        out_shape=jax.ShapeDtypeStruct((n_slots, ROW_TILE, LANES), F32),
        scratch_shapes=[buf, buf, buf, buf, pltpu.SemaphoreType.DMA((4,))],
        compiler_params=_params(("arbitrary",), VMEM_LIMIT),
        name="ffn",
    )(idx3, idx3, pos3, pos3, gate3, h2, wg, wu, wd)


def _combine_body(toff_ref, x1_ref, tok_ref, g_ref, ys_hbm, o_ref, win, extra, sems,
                  *, tt, ws, n_slots):
    i = pl.program_id(0)
    last = pl.num_programs(0) - 1
    buf = i % 2

    def window_start(nominal):
        return pl.multiple_of(jnp.minimum(nominal, n_slots - ws), 8)

    def first_fetch(tile, b):
        start = window_start((toff_ref[tile] // 8) * 8)
        return pltpu.make_async_copy(ys_hbm.at[pl.ds(start, ws)], win.at[b], sems.at[b])

    @pl.when(i == 0)
    def _():
        first_fetch(0, 0).start()

    first_fetch(i, buf).wait()

    @pl.when(i < last)
    def _():
        first_fetch(i + 1, 1 - buf).start()

    hi = toff_ref[i + 1]
    base = (toff_ref[i] // 8) * 8
    nwin = (hi - base + ws - 1) // ws
    t_lo = _row_to_col(tok_ref[0:1, :], tt)
    t_hi = _row_to_col(tok_ref[1:2, :], tt)
    lane = lax.broadcasted_iota(I32, (tt, ws), 1)

    def segment_sum(rows_ref, start, nominal):
        slot = lane + start
        onehot = (slot >= t_lo) & (slot < t_hi) & (slot >= nominal)
        sel = jnp.where(onehot, 1.0, 0.0).astype(BF16)
        rows = _load_tiles_as_rows(rows_ref).astype(BF16)
        return jnp.dot(sel, rows, preferred_element_type=F32)

    def later_window(w, acc):
        nominal = base + w * ws
        start = window_start(nominal)
        cp = pltpu.make_async_copy(ys_hbm.at[pl.ds(start, ws)], extra, sems.at[2])
        cp.start()
        cp.wait()
        return acc + segment_sum(extra, start, nominal)

    moe = segment_sum(win.at[buf], window_start(base), base)
    moe = lax.fori_loop(1, nwin, later_window, moe)
    x2 = x1_ref[...] + moe
    ms = jnp.mean(x2 * x2, axis=-1, keepdims=True)
    o_ref[...] = (x2 * lax.rsqrt(ms + EPS)) * g_ref[...]


def _combine(x1, ys, tokinfo, gain, tt, ws):
    n = x1.shape[0]
    n_slots = ys.shape[0]
    tile_off = jnp.concatenate([tokinfo[0, ::tt], jnp.full((1,), n_slots, I32)])
    return pl.pallas_call(
        functools.partial(_combine_body, tt=tt, ws=ws, n_slots=n_slots),
        grid_spec=pltpu.PrefetchScalarGridSpec(
            num_scalar_prefetch=1,
            grid=(n // tt,),
            in_specs=[pl.BlockSpec((tt, D_MODEL), lambda i, s: (i, 0)),
                      pl.BlockSpec((2, tt), lambda i, s: (0, i)),
                      pl.BlockSpec((1, D_MODEL), lambda i, s: (0, 0)),
                      pl.BlockSpec(memory_space=pl.ANY)],
            out_specs=pl.BlockSpec((tt, D_MODEL), lambda i, s: (i, 0)),
            scratch_shapes=[pltpu.VMEM((2, ws, ROW_TILE, LANES), F32),
                            pltpu.VMEM((ws, ROW_TILE, LANES), F32),
                            pltpu.SemaphoreType.DMA((3,))]),
        out_shape=jax.ShapeDtypeStruct((n, D_MODEL), F32),
        compiler_params=_params(("arbitrary",), VMEM_LIMIT),
        name="combine",
    )(tile_off, x1, tokinfo, gain, ys)


def _tile(n, pref):
    t = pref
    while n % t:
        t //= 2
    return t


def _trunk(x, p):
    b, s, d = x.shape
    n = b * s
    cap = CAPACITY_FACTOR * n // N_EXPERTS
    xf = x.reshape(n, d)
    tm = _tile(n, 512)
    zf, q, k, v = _inproj(xf, p["norm_mix"], p["w_in"], tm)
    f = _fourier(zf.reshape(b, s, FOURIER_WIDTH), p["w_fourier"])
    a = _attention(q.reshape(b, s, ATTN_WIDTH), k.reshape(b, s, KV_WIDTH),
                   v.reshape(b, s, KV_WIDTH), p["bias"], p["sink"], _tile(s, 512))
    x1, h2, aff_t = _outproj(xf, f.reshape(n, FOURIER_WIDTH), a.reshape(n, ATTN_WIDTH),
                             p["w_out"], p["norm_ffn"], p["w_router"], tm)
    idx_c, pos_c, gate_c, tokinfo = _route(aff_t, cap)
    ys = _ffn(idx_c, pos_c, gate_c, h2, p["w_gate"], p["w_up"], p["w_down"], _tile(cap // 2, 512))
    tt = _tile(n, 512)
    ws = min(CAPACITY_FACTOR * tt + tt // 4, N_EXPERTS * cap)
    y = _combine(x1, ys, tokinfo, p["norm_final"], tt, ws)
    return y.reshape(b, s, d)


def kernel(x_prompt, x_sample, w_in, w_fourier, attn_sink, rel_bias, w_out, norm_mix, norm_ffn,
           w_router, w_gate, w_up, w_down, norm_final):
    assert w_in.shape[0] == 1
    wr = w_router[0].T
    wr_hi = wr.astype(BF16)
    wr_lo = (wr - wr_hi.astype(F32)).astype(BF16)
    p = dict(
        norm_mix=norm_mix[0].reshape(1, D_MODEL),
        norm_ffn=norm_ffn[0].reshape(1, D_MODEL),
        norm_final=norm_final.reshape(1, D_MODEL),
        w_in=w_in[0].astype(BF16),
        w_fourier=w_fourier[0].astype(BF16),
        w_out=w_out[0].astype(BF16),
        w_router=jnp.concatenate([wr_hi, wr_lo], axis=0),
        w_gate=w_gate[0].astype(BF16),
        w_up=w_up[0].astype(BF16),
        w_down=w_down[0].astype(BF16),
        sink=attn_sink[0],
        bias=_bias_table(rel_bias),
    )
    return _trunk(x_prompt, p), _trunk(x_sample, p)
```

```python
import functools
import math

import numpy as np
import jax
import jax.numpy as jnp
from jax import lax
from jax.experimental import pallas as pl
from jax.experimental.pallas import tpu as pltpu

F32 = jnp.float32
BF16 = jnp.bfloat16
I32 = jnp.int32

D_MODEL = 1024
FOURIER_GROUPS = 4
GROUP_DIM = 128
FOURIER_WIDTH = FOURIER_GROUPS * GROUP_DIM
N_HEADS = 8
N_KV_HEADS = 2
HEAD_DIM = 64
ATTN_WIDTH = N_HEADS * HEAD_DIM
KV_WIDTH = N_KV_HEADS * HEAD_DIM
WINDOW = 128
BLOCK = 128
N_BUCKETS = 32
MAX_DISTANCE = 128
IN_WIDTH = FOURIER_WIDTH + ATTN_WIDTH + 2 * KV_WIDTH
N_EXPERTS = 16
CAPACITY_FACTOR = 2
D_EXPERT = 1024
EPS = 1e-6

LANES = 128
PREFIX_CHUNK = 256
STRIP = 32
FFN_CHUNK = 256
VMEM_LIMIT = 56 * 1024 * 1024


ROW_TILE = D_MODEL // LANES


def _params(sem, vmem=None):
    return pltpu.CompilerParams(dimension_semantics=sem, vmem_limit_bytes=vmem)


def _store_rows_as_tiles(ref, val):
    m = val.shape[0]
    for c in range(ROW_TILE):
        ref[pl.ds(c, m, stride=ROW_TILE), :] = val[:, c * LANES:(c + 1) * LANES]


def _load_tiles_as_rows(ref):
    m = ref.shape[0] // ROW_TILE
    return jnp.concatenate(
        [ref[pl.ds(c, m, stride=ROW_TILE), :] for c in range(ROW_TILE)], axis=1)


def _tiles_of_rows(ref, r, count):
    return ref.at[pl.ds(pl.multiple_of(r * ROW_TILE, ROW_TILE), count * ROW_TILE), :]


def _tile_of_row(ref, r):
    start = r * ROW_TILE
    if not isinstance(r, int):
        start = pl.multiple_of(start, ROW_TILE)
    return ref.at[pl.ds(start, ROW_TILE), :]


def _inproj_body(x_ref, g_ref, w_ref, zf_ref, q_ref, k_ref, v_ref):
    x = x_ref[...]
    ms = jnp.mean(x * x, axis=-1, keepdims=True)
    h = (x * lax.rsqrt(ms + EPS)) * g_ref[...]
    z = jnp.dot(h.astype(BF16), w_ref[...], preferred_element_type=F32)
    o = FOURIER_WIDTH
    zf_ref[...] = z[:, :o].astype(BF16)
    q_ref[...] = (z[:, o:o + ATTN_WIDTH] * (HEAD_DIM ** -0.5)).astype(BF16)
    o += ATTN_WIDTH
    k_ref[...] = z[:, o:o + KV_WIDTH].astype(BF16)
    o += KV_WIDTH
    v_ref[...] = z[:, o:o + KV_WIDTH].astype(BF16)


def _inproj(xf, gain, w_in, tm):
    n = xf.shape[0]
    row = lambda w: pl.BlockSpec((tm, w), lambda i: (i, 0))
    return pl.pallas_call(
        _inproj_body,
        grid=(n // tm,),
        in_specs=[row(D_MODEL),
                  pl.BlockSpec((1, D_MODEL), lambda i: (0, 0)),
                  pl.BlockSpec((D_MODEL, IN_WIDTH), lambda i: (0, 0))],
        out_specs=[row(FOURIER_WIDTH), row(ATTN_WIDTH), row(KV_WIDTH), row(KV_WIDTH)],
        out_shape=[jax.ShapeDtypeStruct((n, FOURIER_WIDTH), BF16),
                   jax.ShapeDtypeStruct((n, ATTN_WIDTH), BF16),
                   jax.ShapeDtypeStruct((n, KV_WIDTH), BF16),
                   jax.ShapeDtypeStruct((n, KV_WIDTH), BF16)],
        compiler_params=_params(("parallel",), VMEM_LIMIT),
        name="inproj",
    )(xf, gain, w_in)


def _dft_tables(s):
    s2 = BLOCK
    s1 = s // s2
    a = np.arange(s1, dtype=np.float64)
    ang1 = 2.0 * np.pi * np.outer(a, a) / s1
    f1 = np.concatenate([np.cos(ang1), -np.sin(ang1)], axis=0)
    b = np.arange(s2, dtype=np.float64)
    ang2 = 2.0 * np.pi * np.outer(b, b) / s2
    angt = 2.0 * np.pi * np.outer(a, b) / s
    c = np.arange(GROUP_DIM, dtype=np.float64)
    angc = 2.0 * np.pi * np.outer(c, c) / GROUP_DIM
    cs = np.concatenate([np.cos(angc), np.sin(angc)], axis=0)
    return dict(
        f1=jnp.asarray(f1, BF16),
        f2r=jnp.asarray(np.cos(ang2), F32), f2i=jnp.asarray(-np.sin(ang2), F32),
        twr=jnp.asarray(np.cos(angt), F32), twi=jnp.asarray(-np.sin(angt), F32),
        cs=jnp.asarray(cs, BF16))


def _dft1_body(f_ref, x_ref, o_ref):
    s1 = x_ref.shape[1]
    res = jnp.dot(f_ref[...], x_ref[0], preferred_element_type=F32)
    o_ref[0, 0] = res[:s1].astype(BF16)
    o_ref[0, 1] = res[s1:].astype(BF16)


def _dft1(zf3, f1, cb):
    b, s1, w = zf3.shape
    return pl.pallas_call(
        _dft1_body,
        grid=(b, w // cb),
        in_specs=[pl.BlockSpec((2 * s1, s1), lambda i, j: (0, 0)),
                  pl.BlockSpec((1, s1, cb), lambda i, j: (i, 0, j))],
        out_specs=pl.BlockSpec((1, 2, s1, cb), lambda i, j: (i, 0, 0, j)),
        out_shape=jax.ShapeDtypeStruct((b, 2, s1, w), BF16),
        compiler_params=_params(("parallel", "parallel"), VMEM_LIMIT),
        name="dft1",
    )(f1, zf3)


def _dft2_body(a_ref, f2r_ref, f2i_ref, twr_ref, twi_ref, cs_ref, wf_ref, o_ref, *, kb, scale):
    s2 = BLOCK
    fr = f2r_ref[...]
    fi = f2i_ref[...]
    for j in range(kb):
        tr = twr_ref[0, j:j + 1, :]
        ti = twi_ref[0, j:j + 1, :]
        gr = fr * tr - fi * ti
        gi = fr * ti + fi * tr
        lhs = jnp.concatenate(
            [jnp.concatenate([gr, -gi], axis=1), jnp.concatenate([gi, gr], axis=1)],
            axis=0).astype(BF16)
        rhs = jnp.concatenate([a_ref[0, 0, j], a_ref[0, 1, j]], axis=0)
        y = jnp.dot(lhs, rhs, preferred_element_type=F32)
        outs = []
        for g in range(FOURIER_GROUPS):
            sl = slice(g * GROUP_DIM, (g + 1) * GROUP_DIM)
            cat = jnp.concatenate([y[:s2, sl], y[s2:, sl]], axis=1).astype(BF16)
            re = jnp.dot(cat, cs_ref[...], preferred_element_type=F32) * scale
            outs.append(jnp.dot(re.astype(BF16), wf_ref[g], preferred_element_type=F32))
        o_ref[0, :, j, :] = jnp.concatenate(outs, axis=1)


def _dft2(a5, tabs, wf, kb, scale):
    b, _, s1, s2, w = a5.shape
    const = lambda shape: pl.BlockSpec(shape, lambda i, j: (0,) * len(shape))
    twr = tabs["twr"].reshape(s1 // kb, kb, s2)
    twi = tabs["twi"].reshape(s1 // kb, kb, s2)
    return pl.pallas_call(
        functools.partial(_dft2_body, kb=kb, scale=scale),
        grid=(b, s1 // kb),
        in_specs=[pl.BlockSpec((1, 2, kb, s2, w), lambda i, j: (i, 0, j, 0, 0)),
                  const((s2, s2)), const((s2, s2)),
                  pl.BlockSpec((1, kb, s2), lambda i, j: (j, 0, 0)),
                  pl.BlockSpec((1, kb, s2), lambda i, j: (j, 0, 0)),
                  const((2 * GROUP_DIM, GROUP_DIM)),
                  const((FOURIER_GROUPS, GROUP_DIM, GROUP_DIM))],
        out_specs=pl.BlockSpec((1, s2, kb, w), lambda i, j: (i, 0, j, 0)),
        out_shape=jax.ShapeDtypeStruct((b, s2, s1, w), F32),
        compiler_params=_params(("parallel", "parallel"), VMEM_LIMIT),
        name="dft2",
    )(a5, tabs["f2r"], tabs["f2i"], twr, twi, tabs["cs"], wf)


def _fourier(zf, wf_bf16):
    b, s, w = zf.shape
    s2 = BLOCK
    s1 = s // s2
    tabs = _dft_tables(s)
    a = _dft1(zf.reshape(b, s1, s2 * w), tabs["f1"], cb=2048)
    f = _dft2(a.reshape(b, 2, s1, s2, w), tabs, wf_bf16, kb=8,
              scale=1.0 / math.sqrt(s * GROUP_DIM))
    return f.reshape(b, s, w)


def _bucket_table():
    qi = np.arange(BLOCK)[:, None]
    kj = np.arange(3 * BLOCK)[None, :]
    rel = kj - BLOCK - qi
    nb = N_BUCKETS // 2
    max_exact = nb // 2
    n = np.abs(rel)
    large = max_exact + np.floor(
        np.log(np.maximum(n, 1).astype(np.float64) / max_exact)
        / math.log(MAX_DISTANCE / max_exact) * (nb - max_exact) + 1e-9).astype(np.int64)
    large = np.minimum(large, nb - 1)
    bucket = np.where(rel > 0, nb, 0) + np.where(n < max_exact, n, large)
    return np.where(n <= WINDOW, bucket, -1).astype(np.int32)


def _bias_body(rb_ref, bucket_ref, o_ref):
    bk = bucket_ref[...]
    col = lax.broadcasted_iota(I32, bk.shape, 1)
    for h in range(N_HEADS):
        acc = jnp.full(bk.shape, -jnp.inf, F32)
        for b in range(N_BUCKETS):
            acc = jnp.where(bk == b, rb_ref[b, h], acc)
        o_ref[0, h] = acc
        o_ref[1, h] = jnp.where(col >= BLOCK, acc, -jnp.inf)
        o_ref[2, h] = jnp.where(col < 2 * BLOCK, acc, -jnp.inf)


def _bias_table(rel_bias):
    bucket = jnp.asarray(_bucket_table())
    shape = (3, N_HEADS) + bucket.shape
    return pl.pallas_call(
        _bias_body,
        in_specs=[pl.BlockSpec(memory_space=pltpu.SMEM),
                  pl.BlockSpec(bucket.shape, lambda: (0, 0))],
        out_specs=pl.BlockSpec(shape, lambda: (0, 0, 0, 0)),
        out_shape=jax.ShapeDtypeStruct(shape, F32),
        name="bias_table",
    )(rel_bias.astype(F32), bucket)


def _attn_body(sink_ref, q_ref, kp_ref, kc_ref, kn_ref, vp_ref, vc_ref, vn_ref, bias_ref,
               o_ref, *, tq):
    j = pl.program_id(1)
    last = pl.num_programs(1) - 1
    nsb = tq // BLOCK
    group = N_HEADS // N_KV_HEADS
    gw = group * HEAD_DIM
    kext = jnp.concatenate([kp_ref[0, tq - BLOCK:, :], kc_ref[0], kn_ref[0, :BLOCK, :]], axis=0)
    vext = jnp.concatenate([vp_ref[0, tq - BLOCK:, :], vc_ref[0], vn_ref[0, :BLOCK, :]], axis=0)
    low_half = lax.broadcasted_iota(I32, (1, KV_WIDTH), 1) < HEAD_DIM

    def per_kv_head(ext):
        x = ext.astype(F32)
        xr = pltpu.roll(x, HEAD_DIM, axis=1)
        return (jnp.where(low_half, x, xr).astype(BF16), jnp.where(low_half, xr, x).astype(BF16))

    krep = per_kv_head(kext)
    vrep = per_kv_head(vext)
    head_of_lane = lax.broadcasted_iota(I32, (1, gw), 1) // HEAD_DIM
    nt = (((1,), (1,)), ((), ()))
    def scores(sb, kh):
        rows = slice(sb * BLOCK, (sb + 3) * BLOCK)
        variant = 0
        if sb == 0:
            variant = jnp.where(j == 0, 1, 0)
        elif sb == nsb - 1:
            variant = jnp.where(j == last, 2, 0)
        qblk = q_ref[0, sb * BLOCK:(sb + 1) * BLOCK, kh * gw:(kh + 1) * gw]
        qm = jnp.concatenate(
            [jnp.where(head_of_lane == g, qblk, jnp.zeros_like(qblk)) for g in range(group)],
            axis=0)
        kb = jnp.concatenate([krep[kh][rows], krep[kh][rows]], axis=1)
        s = lax.dot_general(qm, kb, nt, preferred_element_type=F32)
        return s, variant

    def weights(sv, kh):
        s_all, variant = sv
        out = []
        for g in range(group):
            for r0 in range(0, BLOCK, STRIP):
                s = s_all[g * BLOCK + r0:g * BLOCK + r0 + STRIP]
                s = s + bias_ref[variant, kh * group + g, r0:r0 + STRIP, :]
                sink = sink_ref[kh * group + g]
                m = jnp.maximum(jnp.max(s, axis=-1, keepdims=True), sink)
                p = jnp.exp(s - m)
                denom = jnp.sum(p, axis=-1, keepdims=True) + jnp.exp(sink - m)
                out.append((p / denom).astype(BF16))
        return jnp.concatenate(out, axis=0)

    def values(pn, sb, kh):
        rows = slice(sb * BLOCK, (sb + 3) * BLOCK)
        o4 = jnp.dot(pn, vrep[kh][rows], preferred_element_type=F32)
        out = []
        for pair in range(group // 2):
            even = o4[(2 * pair) * BLOCK:(2 * pair + 1) * BLOCK]
            odd = o4[(2 * pair + 1) * BLOCK:(2 * pair + 2) * BLOCK]
            out.append(jnp.where(low_half, even, odd))
        return out

    items = [(sb, kh) for sb in range(nsb) for kh in range(N_KV_HEADS)]
    pieces = {}
    s_next = scores(*items[0])
    pn_prev = None
    for i, (sb, kh) in enumerate(items):
        s_cur = s_next
        if i + 1 < len(items):
            s_next = scores(*items[i + 1])
        if pn_prev is not None:
            pieces[items[i - 1]] = values(pn_prev, *items[i - 1])
        pn_prev = weights(s_cur, kh)
    pieces[items[-1]] = values(pn_prev, *items[-1])
    for sb in range(nsb):
        row = [x for kh in range(N_KV_HEADS) for x in pieces[(sb, kh)]]
        o_ref[0, sb * BLOCK:(sb + 1) * BLOCK, :] = jnp.concatenate(row, axis=1).astype(BF16)


def _attention(q, k, v, bias, sink, tq):
    b, s, _ = q.shape
    nt = s // tq
    assert tq >= 2 * BLOCK
    kv = lambda f: pl.BlockSpec((1, tq, KV_WIDTH), f)
    prev = lambda i, j: (i, jnp.maximum(j - 1, 0), 0)
    cur = lambda i, j: (i, j, 0)
    nxt = lambda i, j: (i, jnp.minimum(j + 1, nt - 1), 0)
    return pl.pallas_call(
        functools.partial(_attn_body, tq=tq),
        grid=(b, nt),
        in_specs=[pl.BlockSpec(memory_space=pltpu.SMEM),
                  pl.BlockSpec((1, tq, ATTN_WIDTH), cur),
                  kv(prev), kv(cur), kv(nxt), kv(prev), kv(cur), kv(nxt),
                  pl.BlockSpec(bias.shape, lambda i, j: (0, 0, 0, 0))],
        out_specs=pl.BlockSpec((1, tq, ATTN_WIDTH), cur),
        out_shape=jax.ShapeDtypeStruct((b, s, ATTN_WIDTH), BF16),
        compiler_params=_params(("parallel", "parallel"), VMEM_LIMIT),
        name="attn",
    )(sink.astype(F32), q, k, k, k, v, v, v, bias)


def _outproj_body(x_ref, f_ref, a_ref, wo_ref, g_ref, wr_ref, x1_ref, h2_ref, aff_ref):
    mix = jnp.concatenate([f_ref[...].astype(BF16), a_ref[...]], axis=1)
    x1 = x_ref[...] + jnp.dot(mix, wo_ref[...], preferred_element_type=F32)
    x1_ref[...] = x1
    ms = jnp.mean(x1 * x1, axis=-1, keepdims=True)
    h2 = (x1 * lax.rsqrt(ms + EPS)) * g_ref[...]
    _store_rows_as_tiles(h2_ref, h2)
    hi = h2.astype(BF16)
    lo = (h2 - hi.astype(F32)).astype(BF16)
    nt = (((1,), (1,)), ((), ()))
    o1 = lax.dot_general(wr_ref[...], hi, nt, preferred_element_type=F32)
    o2 = lax.dot_general(wr_ref[:N_EXPERTS], lo, nt, preferred_element_type=F32)
    logits = o1[:N_EXPERTS] + (o1[N_EXPERTS:] + o2)
    e = jnp.exp(logits - jnp.max(logits, axis=0, keepdims=True))
    aff_ref[...] = e / jnp.sum(e, axis=0, keepdims=True)


def _outproj(xf, f, a, w_out, gain, wr_split, tm):
    n = xf.shape[0]
    row = lambda w: pl.BlockSpec((tm, w), lambda i: (i, 0))
    const = lambda shape: pl.BlockSpec(shape, lambda i: (0, 0))
    return pl.pallas_call(
        _outproj_body,
        grid=(n // tm,),
        in_specs=[row(D_MODEL), row(FOURIER_WIDTH), row(ATTN_WIDTH),
                  const((D_MODEL, D_MODEL)), const((1, D_MODEL)),
                  const((2 * N_EXPERTS, D_MODEL))],
        out_specs=[row(D_MODEL),
                   pl.BlockSpec((tm * ROW_TILE, LANES), lambda i: (i, 0)),
                   pl.BlockSpec((N_EXPERTS, tm), lambda i: (0, i))],
        out_shape=[jax.ShapeDtypeStruct((n, D_MODEL), F32),
                   jax.ShapeDtypeStruct((n * ROW_TILE, LANES), F32),
                   jax.ShapeDtypeStruct((N_EXPERTS, n), F32)],
        compiler_params=_params(("parallel",), VMEM_LIMIT),
        name="outproj",
    )(xf, f, a, w_out, gain, wr_split)


def _route_body(aff_ref, idx_ref, pos_ref, gate_ref, tok_ref,
                d_a, d_b, n_a, n_b, p_a, p_b, g_a, g_b, *, n, cap):
    e = N_EXPERTS
    ch = PREFIX_CHUNK

    def search(i, tau):
        cand = tau | jnp.left_shift(jnp.int32(1), 30 - i)
        keys = lax.bitcast_convert_type(aff_ref[...], I32)
        cnt = jnp.sum((keys >= cand).astype(I32), axis=1, keepdims=True)
        return jnp.where(cnt >= cap, cand, tau)

    tau = lax.fori_loop(0, 31, search, jnp.zeros((e, 1), I32))
    keys = lax.bitcast_convert_type(aff_ref[...], I32)
    n_gt = jnp.sum((keys > tau).astype(I32), axis=1, keepdims=True)
    need = (cap - n_gt).astype(F32)

    r = lax.broadcasted_iota(I32, (ch, ch), 0)
    c = lax.broadcasted_iota(I32, (ch, ch), 1)
    upper = (r < c).astype(BF16)
    r16 = lax.broadcasted_iota(I32, (e, e), 0)
    c16 = lax.broadcasted_iota(I32, (e, e), 1)
    lower = (c16 < r16).astype(BF16)

    carry_eq = jnp.zeros((e, 1), F32)
    carry_m = jnp.zeros((2 * e, 1), F32)
    for ci in range(n // ch):
        sl = slice(ci * ch, (ci + 1) * ch)
        k = lax.bitcast_convert_type(aff_ref[:, sl], I32)
        gt = k > tau
        eq = (k == tau).astype(F32)
        eq_ex = jnp.dot(eq.astype(BF16), upper, preferred_element_type=F32) + carry_eq
        carry_eq = carry_eq + jnp.sum(eq, axis=1, keepdims=True)
        m = jnp.where(gt | ((eq > 0.0) & (eq_ex < need)), 1.0, 0.0).astype(F32)
        cnt = jnp.sum(m, axis=0, keepdims=True)
        stacked = jnp.concatenate([m, cnt, jnp.zeros((e - 1, ch), F32)], axis=0)
        pre = jnp.dot(stacked.astype(BF16), upper, preferred_element_type=F32) + carry_m
        carry_m = carry_m + jnp.sum(stacked, axis=1, keepdims=True)
        rank = pre[:e]
        tokoff = pre[e:e + 1]
        ex_e = jnp.dot(lower, m.astype(BF16), preferred_element_type=F32)
        tok_ref[0:1, sl] = tokoff.astype(I32)
        tok_ref[1:2, sl] = (tokoff + cnt).astype(I32)
        lane = lax.broadcasted_iota(I32, (e, ch), 1) + ci * ch
        d_a[:, sl] = jnp.where(m > 0.0, lane - rank.astype(I32), 0)
        n_a[:, sl] = lane
        p_a[:, sl] = (tokoff + ex_e).astype(I32)
    g_a[...] = aff_ref[...]

    bufs = [(d_a, n_a, p_a, g_a), (d_b, n_b, p_b, g_b)]
    for b in range(int(math.log2(n))):
        src, dst = bufs[b % 2], bufs[(b + 1) % 2]
        sh = 1 << b
        d = src[0][...]
        moving = (lax.shift_right_logical(d, b) & 1) == 1
        d_in = pltpu.roll(d, n - sh, axis=1)
        arrive = (lax.shift_right_logical(d_in, b) & 1) == 1
        dst[0][...] = jnp.where(arrive, d_in, jnp.where(moving, 0, d))
        for t in (1, 2, 3):
            v = src[t][...]
            dst[t][...] = jnp.where(arrive, pltpu.roll(v, n - sh, axis=1), v)
    fin = bufs[int(math.log2(n)) % 2]
    idx_ref[...] = fin[1][:, :cap]
    pos_ref[...] = fin[2][:, :cap]
    gate_ref[...] = fin[3][:, :cap]


def _route(aff_t, cap):
    e, n = aff_t.shape
    assert n & (n - 1) == 0 or True
    full = lambda shape: pl.BlockSpec(shape, lambda: (0, 0))
    big_i = pltpu.VMEM((e, n), I32)
    return pl.pallas_call(
        functools.partial(_route_body, n=n, cap=cap),
        in_specs=[full((e, n))],
        out_specs=[full((e, cap)), full((e, cap)), full((e, cap)), full((2, n))],
        out_shape=[jax.ShapeDtypeStruct((e, cap), I32),
                   jax.ShapeDtypeStruct((e, cap), I32),
                   jax.ShapeDtypeStruct((e, cap), F32),
                   jax.ShapeDtypeStruct((2, n), I32)],
        scratch_shapes=[big_i, big_i, big_i, big_i, big_i, big_i,
                        pltpu.VMEM((e, n), F32), pltpu.VMEM((e, n), F32)],
        compiler_params=_params(None, VMEM_LIMIT),
        name="route",
    )(aff_t)


def _row_to_col(row, tm):
    r = lax.broadcasted_iota(I32, (tm, tm), 0)
    c = lax.broadcasted_iota(I32, (tm, tm), 1)
    return jnp.sum(jnp.where(r == c, row, jnp.zeros_like(row)), axis=1, keepdims=True)


def _ffn_body(idx_cur, idx_nxt, pos_prv, pos_cur, gate_ref, h_hbm, wg_ref, wu_ref, wd_ref, ys_hbm,
              xa, xb, ya, yb, sems, *, tm):
    s = pl.program_id(0)
    last = pl.num_programs(0) - 1
    g_a, g_b, s_a, s_b = sems.at[0], sems.at[1], sems.at[2], sems.at[3]

    def gather(idx_ref, t, buf, sem):
        def one(i):
            pltpu.make_async_copy(_tile_of_row(h_hbm, idx_ref[t, 0, i]),
                                  _tile_of_row(buf, i), sem).start()
        return one

    def scatter(pos_ref, t, buf, sem):
        def one(i):
            pltpu.make_async_copy(_tile_of_row(buf, i),
                                  _tile_of_row(ys_hbm, pos_ref[t, 0, i]), sem).start()
        return one

    def wait_gather(buf, sem):
        pltpu.make_async_copy(h_hbm.at[pl.ds(0, tm * ROW_TILE), :], buf, sem).wait()

    def wait_scatter(buf, sem):
        pltpu.make_async_copy(buf, ys_hbm.at[pl.ds(0, tm * ROW_TILE), :], sem).wait()

    def compute(xbuf, ybuf, t, copies):
        nc = D_EXPERT // FFN_CHUNK
        stages = 3 * nc
        todo = [(fn, i) for fn in copies for i in range(tm)]
        per_stage = -(-len(todo) // stages)

        def issue_some():
            for fn, i in todo[:per_stage]:
                fn(i)
            del todo[:per_stage]

        x = _load_tiles_as_rows(xbuf).astype(BF16)
        cols = lambda c: slice(c * FFN_CHUNK, (c + 1) * FFN_CHUNK)

        def up(c):
            g = jnp.dot(x, wg_ref[0, :, cols(c)], preferred_element_type=F32)
            issue_some()
            u = jnp.dot(x, wu_ref[0, :, cols(c)], preferred_element_type=F32)
            issue_some()
            return g, u

        def act(gu):
            g, u = gu
            return (g * jax.nn.sigmoid(g) * u).astype(BF16)

        def down(hid, c):
            part = jnp.dot(hid, wd_ref[0, cols(c), :], preferred_element_type=F32)
            issue_some()
            return part

        gu_next = up(0)
        hid_prev = None
        y = None
        for c in range(nc):
            gu_cur = gu_next
            if c + 1 < nc:
                gu_next = up(c + 1)
            if hid_prev is not None:
                part = down(hid_prev, c - 1)
                y = part if y is None else y + part
            hid_prev = act(gu_cur)
        y = y + down(hid_prev, nc - 1)
        assert not todo
        y = y * _row_to_col(gate_ref[t], tm)
        _store_rows_as_tiles(ybuf, y.astype(BF16).astype(F32))

    @pl.when(s == 0)
    def _():
        pl.loop(0, tm)(gather(idx_cur, 0, xa, g_a))
        yb[...] = jnp.zeros_like(yb)

    wait_gather(xa, g_a)

    @pl.when(s > 0)
    def _():
        wait_scatter(ya, s_a)

    compute(xa, ya, 0, [gather(idx_cur, 1, xb, g_b), scatter(pos_prv, 1, yb, s_b)])

    wait_gather(xb, g_b)
    wait_scatter(yb, s_b)
    compute(xb, yb, 1, [gather(idx_nxt, 0, xa, g_a), scatter(pos_cur, 0, ya, s_a)])

    @pl.when(s == last)
    def _():
        pl.loop(0, tm)(scatter(pos_cur, 1, yb, s_b))
        wait_gather(xa, g_a)
        wait_scatter(ya, s_a)
        wait_scatter(yb, s_b)


def _ffn(idx_c, pos_c, gate_c, h2, wg, wu, wd, tm):
    e, cap = idx_c.shape
    spe = cap // (2 * tm)
    steps = e * spe
    n_slots = e * cap
    idx3 = idx_c.reshape(2 * steps, 1, tm)
    pos3 = pos_c.reshape(2 * steps, 1, tm)
    gate3 = gate_c.reshape(2 * steps, 1, tm)
    cur = lambda i: (i, 0, 0)
    prv = lambda i: (jnp.maximum(i - 1, 0), 0, 0)
    nxt = lambda i: (jnp.minimum(i + 1, steps - 1), 0, 0)
    smem = lambda f: pl.BlockSpec((2, 1, tm), f, memory_space=pltpu.SMEM)
    wspec = lambda a, b: pl.BlockSpec((1, a, b), lambda i: (i // spe, 0, 0))
    buf = pltpu.VMEM((tm * ROW_TILE, LANES), F32)
    return pl.pallas_call(
        functools.partial(_ffn_body, tm=tm),
        grid=(steps,),
        in_specs=[smem(cur), smem(nxt), smem(prv), smem(cur),
                  pl.BlockSpec((2, 1, tm), cur),
                  pl.BlockSpec(memory_space=pl.ANY),
                  wspec(D_MODEL, D_EXPERT), wspec(D_MODEL, D_EXPERT), wspec(D_EXPERT, D_MODEL)],
        out_specs=pl.BlockSpec(memory_space=pl.ANY),
        out_shape=jax.ShapeDtypeStruct((n_slots * ROW_TILE, LANES), F32),
        scratch_shapes=[buf, buf, buf, buf, pltpu.SemaphoreType.DMA((4,))],
        compiler_params=_params(("arbitrary",), VMEM_LIMIT),
        name="ffn",
    )(idx3, idx3, pos3, pos3, gate3, h2, wg, wu, wd)


def _combine_body(toff_ref, x1_ref, tok_ref, g_ref, ys_hbm, o_ref, win, extra, sems,
                  *, tt, ws, n_slots):
    i = pl.program_id(0)
    last = pl.num_programs(0) - 1
    buf = i % 2

    def window_start(nominal):
        return pl.multiple_of(jnp.minimum(nominal, n_slots - ws), 8)

    def first_fetch(tile, b):
        start = window_start((toff_ref[tile] // 8) * 8)
        return pltpu.make_async_copy(_tiles_of_rows(ys_hbm, start, ws), win.at[b], sems.at[b])

    @pl.when(i == 0)
    def _():
        first_fetch(0, 0).start()

    first_fetch(i, buf).wait()

    @pl.when(i < last)
    def _():
        first_fetch(i + 1, 1 - buf).start()

    hi = toff_ref[i + 1]
    base = (toff_ref[i] // 8) * 8
    nwin = (hi - base + ws - 1) // ws
    t_lo = _row_to_col(tok_ref[0:1, :], tt)
    t_hi = _row_to_col(tok_ref[1:2, :], tt)
    lane = lax.broadcasted_iota(I32, (tt, ws), 1)

    def segment_sum(rows_ref, start, nominal):
        slot = lane + start
        onehot = (slot >= t_lo) & (slot < t_hi) & (slot >= nominal)
        sel = jnp.where(onehot, 1.0, 0.0).astype(BF16)
        rows = _load_tiles_as_rows(rows_ref).astype(BF16)
        return jnp.dot(sel, rows, preferred_element_type=F32)

    def later_window(w, acc):
        nominal = base + w * ws
        start = window_start(nominal)
        cp = pltpu.make_async_copy(_tiles_of_rows(ys_hbm, start, ws), extra, sems.at[2])
        cp.start()
        cp.wait()
        return acc + segment_sum(extra, start, nominal)

    moe = segment_sum(win.at[buf], window_start(base), base)
    moe = lax.fori_loop(1, nwin, later_window, moe)
    x2 = x1_ref[...] + moe
    ms = jnp.mean(x2 * x2, axis=-1, keepdims=True)
    o_ref[...] = (x2 * lax.rsqrt(ms + EPS)) * g_ref[...]


def _combine(x1, ys, tokinfo, gain, tt, ws):
    n = x1.shape[0]
    n_slots = ys.shape[0] // ROW_TILE
    tile_off = jnp.concatenate([tokinfo[0, ::tt], jnp.full((1,), n_slots, I32)])
    return pl.pallas_call(
        functools.partial(_combine_body, tt=tt, ws=ws, n_slots=n_slots),
        grid_spec=pltpu.PrefetchScalarGridSpec(
            num_scalar_prefetch=1,
            grid=(n // tt,),
            in_specs=[pl.BlockSpec((tt, D_MODEL), lambda i, s: (i, 0)),
                      pl.BlockSpec((2, tt), lambda i, s: (0, i)),
                      pl.BlockSpec((1, D_MODEL), lambda i, s: (0, 0)),
                      pl.BlockSpec(memory_space=pl.ANY)],
            out_specs=pl.BlockSpec((tt, D_MODEL), lambda i, s: (i, 0)),
            scratch_shapes=[pltpu.VMEM((2, ws * ROW_TILE, LANES), F32),
                            pltpu.VMEM((ws * ROW_TILE, LANES), F32),
                            pltpu.SemaphoreType.DMA((3,))]),
        out_shape=jax.ShapeDtypeStruct((n, D_MODEL), F32),
        compiler_params=_params(("arbitrary",), VMEM_LIMIT),
        name="combine",
    )(tile_off, x1, tokinfo, gain, ys)


def _tile(n, pref):
    t = pref
    while n % t:
        t //= 2
    return t


def _trunk(x, p):
    b, s, d = x.shape
    n = b * s
    cap = CAPACITY_FACTOR * n // N_EXPERTS
    xf = x.reshape(n, d)
    tm = _tile(n, 512)
    zf, q, k, v = _inproj(xf, p["norm_mix"], p["w_in"], tm)
    f = _fourier(zf.reshape(b, s, FOURIER_WIDTH), p["w_fourier"])
    a = _attention(q.reshape(b, s, ATTN_WIDTH), k.reshape(b, s, KV_WIDTH),
                   v.reshape(b, s, KV_WIDTH), p["bias"], p["sink"], _tile(s, 512))
    x1, h2, aff_t = _outproj(xf, f.reshape(n, FOURIER_WIDTH), a.reshape(n, ATTN_WIDTH),
                             p["w_out"], p["norm_ffn"], p["w_router"], tm)
    idx_c, pos_c, gate_c, tokinfo = _route(aff_t, cap)
    ys = _ffn(idx_c, pos_c, gate_c, h2, p["w_gate"], p["w_up"], p["w_down"], _tile(cap // 2, 512))
    tt = _tile(n, 512)
    ws = min(CAPACITY_FACTOR * tt + tt // 4, N_EXPERTS * cap)
    y = _combine(x1, ys, tokinfo, p["norm_final"], tt, ws)
    return y.reshape(b, s, d)


def kernel(x_prompt, x_sample, w_in, w_fourier, attn_sink, rel_bias, w_out, norm_mix, norm_ffn,
           w_router, w_gate, w_up, w_down, norm_final):
    assert w_in.shape[0] == 1
    wr = w_router[0].T
    wr_hi = wr.astype(BF16)
    wr_lo = (wr - wr_hi.astype(F32)).astype(BF16)
    p = dict(
        norm_mix=norm_mix[0].reshape(1, D_MODEL),
        norm_ffn=norm_ffn[0].reshape(1, D_MODEL),
        norm_final=norm_final.reshape(1, D_MODEL),
        w_in=w_in[0].astype(BF16),
        w_fourier=w_fourier[0].astype(BF16),
        w_out=w_out[0].astype(BF16),
        w_router=jnp.concatenate([wr_hi, wr_lo], axis=0),
        w_gate=w_gate[0].astype(BF16),
        w_up=w_up[0].astype(BF16),
        w_down=w_down[0].astype(BF16),
        sink=attn_sink[0],
        bias=_bias_table(rel_bias),
    )
    return _trunk(x_prompt, p), _trunk(x_sample, p)
```

```python
import functools
import math

import numpy as np
import jax
import jax.numpy as jnp
from jax import lax
from jax.experimental import pallas as pl
from jax.experimental.pallas import tpu as pltpu

F32 = jnp.float32
BF16 = jnp.bfloat16
I32 = jnp.int32

D_MODEL = 1024
FOURIER_GROUPS = 4
GROUP_DIM = 128
FOURIER_WIDTH = FOURIER_GROUPS * GROUP_DIM
N_HEADS = 8
N_KV_HEADS = 2
HEAD_DIM = 64
ATTN_WIDTH = N_HEADS * HEAD_DIM
KV_WIDTH = N_KV_HEADS * HEAD_DIM
WINDOW = 128
BLOCK = 128
N_BUCKETS = 32
MAX_DISTANCE = 128
IN_WIDTH = FOURIER_WIDTH + ATTN_WIDTH + 2 * KV_WIDTH
N_EXPERTS = 16
CAPACITY_FACTOR = 2
D_EXPERT = 1024
EPS = 1e-6

LANES = 128
PREFIX_CHUNK = 256
STRIP = 32
FFN_CHUNK = 256
VMEM_LIMIT = 56 * 1024 * 1024


ROW_TILE = D_MODEL // LANES


def _params(sem, vmem=None):
    return pltpu.CompilerParams(dimension_semantics=sem, vmem_limit_bytes=vmem)


def _store_rows_as_tiles(ref, val):
    m = val.shape[0]
    for c in range(ROW_TILE):
        ref[pl.ds(c, m, stride=ROW_TILE), :] = val[:, c * LANES:(c + 1) * LANES]


def _load_tiles_as_rows(ref):
    m = ref.shape[0] // ROW_TILE
    return jnp.concatenate(
        [ref[pl.ds(c, m, stride=ROW_TILE), :] for c in range(ROW_TILE)], axis=1)


def _tiles_of_rows(ref, r, count):
    return ref.at[pl.ds(pl.multiple_of(r * ROW_TILE, ROW_TILE), count * ROW_TILE), :]


def _tile_of_row(ref, r):
    start = r * ROW_TILE
    if not isinstance(r, int):
        start = pl.multiple_of(start, ROW_TILE)
    return ref.at[pl.ds(start, ROW_TILE), :]


def _inproj_body(x_ref, g_ref, w_ref, zf_ref, q_ref, k_ref, v_ref):
    x = x_ref[...]
    ms = jnp.mean(x * x, axis=-1, keepdims=True)
    h = (x * lax.rsqrt(ms + EPS)) * g_ref[...]
    z = jnp.dot(h.astype(BF16), w_ref[...], preferred_element_type=F32)
    o = FOURIER_WIDTH
    zf_ref[...] = z[:, :o].astype(BF16)
    q_ref[...] = (z[:, o:o + ATTN_WIDTH] * (HEAD_DIM ** -0.5)).astype(BF16)
    o += ATTN_WIDTH
    k_ref[...] = z[:, o:o + KV_WIDTH].astype(BF16)
    o += KV_WIDTH
    v_ref[...] = z[:, o:o + KV_WIDTH].astype(BF16)


def _inproj(xf, gain, w_in, tm):
    n = xf.shape[0]
    row = lambda w: pl.BlockSpec((tm, w), lambda i: (i, 0))
    return pl.pallas_call(
        _inproj_body,
        grid=(n // tm,),
        in_specs=[row(D_MODEL),
                  pl.BlockSpec((1, D_MODEL), lambda i: (0, 0)),
                  pl.BlockSpec((D_MODEL, IN_WIDTH), lambda i: (0, 0))],
        out_specs=[row(FOURIER_WIDTH), row(ATTN_WIDTH), row(KV_WIDTH), row(KV_WIDTH)],
        out_shape=[jax.ShapeDtypeStruct((n, FOURIER_WIDTH), BF16),
                   jax.ShapeDtypeStruct((n, ATTN_WIDTH), BF16),
                   jax.ShapeDtypeStruct((n, KV_WIDTH), BF16),
                   jax.ShapeDtypeStruct((n, KV_WIDTH), BF16)],
        compiler_params=_params(("parallel",), VMEM_LIMIT),
        name="inproj",
    )(xf, gain, w_in)


def _dft_tables(s):
    s2 = BLOCK
    s1 = s // s2
    a = np.arange(s1, dtype=np.float64)
    ang1 = 2.0 * np.pi * np.outer(a, a) / s1
    f1 = np.concatenate([np.cos(ang1), -np.sin(ang1)], axis=0)
    b = np.arange(s2, dtype=np.float64)
    ang2 = 2.0 * np.pi * np.outer(b, b) / s2
    angt = 2.0 * np.pi * np.outer(a, b) / s
    c = np.arange(GROUP_DIM, dtype=np.float64)
    angc = 2.0 * np.pi * np.outer(c, c) / GROUP_DIM
    cs = np.concatenate([np.cos(angc), np.sin(angc)], axis=0)
    return dict(
        f1=jnp.asarray(f1, BF16),
        f2r=jnp.asarray(np.cos(ang2), F32), f2i=jnp.asarray(-np.sin(ang2), F32),
        twr=jnp.asarray(np.cos(angt), F32), twi=jnp.asarray(-np.sin(angt), F32),
        cs=jnp.asarray(cs, BF16))


def _dft1_body(f_ref, x_ref, o_ref):
    s1, cb = x_ref.shape[1], x_ref.shape[2]
    sub = min(cb, 2048)
    for c in range(0, cb, sub):
        res = jnp.dot(f_ref[...], x_ref[0, :, c:c + sub], preferred_element_type=F32)
        o_ref[0, 0, :, c:c + sub] = res[:s1].astype(BF16)
        o_ref[0, 1, :, c:c + sub] = res[s1:].astype(BF16)


def _dft1(zf3, f1, cb):
    b, s1, w = zf3.shape
    return pl.pallas_call(
        _dft1_body,
        grid=(b, w // cb),
        in_specs=[pl.BlockSpec((2 * s1, s1), lambda i, j: (0, 0)),
                  pl.BlockSpec((1, s1, cb), lambda i, j: (i, 0, j))],
        out_specs=pl.BlockSpec((1, 2, s1, cb), lambda i, j: (i, 0, 0, j)),
        out_shape=jax.ShapeDtypeStruct((b, 2, s1, w), BF16),
        compiler_params=_params(("parallel", "parallel"), VMEM_LIMIT),
        name="dft1",
    )(f1, zf3)


def _dft2_body(a_ref, f2r_ref, f2i_ref, twr_ref, twi_ref, cs_ref, wf_ref, o_ref, *, kb, scale):
    s2 = BLOCK
    fr = f2r_ref[...]
    fi = f2i_ref[...]
    ys = []
    for j in range(kb):
        tr = twr_ref[0, j:j + 1, :]
        ti = twi_ref[0, j:j + 1, :]
        gr = fr * tr - fi * ti
        gi = fr * ti + fi * tr
        lhs = jnp.concatenate(
            [jnp.concatenate([gr, -gi], axis=1), jnp.concatenate([gi, gr], axis=1)],
            axis=0).astype(BF16)
        rhs = jnp.concatenate([a_ref[0, 0, j], a_ref[0, 1, j]], axis=0)
        ys.append(jnp.dot(lhs, rhs, preferred_element_type=F32))
    cat = jnp.concatenate(
        [jnp.concatenate([y[:s2, g * GROUP_DIM:(g + 1) * GROUP_DIM],
                          y[s2:, g * GROUP_DIM:(g + 1) * GROUP_DIM]], axis=1)
         for g in range(FOURIER_GROUPS) for y in ys], axis=0).astype(BF16)
    re = (jnp.dot(cat, cs_ref[...], preferred_element_type=F32) * scale).astype(BF16)
    rows = kb * s2
    outs = [jnp.dot(re[g * rows:(g + 1) * rows], wf_ref[g], preferred_element_type=F32)
            for g in range(FOURIER_GROUPS)]
    for j in range(kb):
        o_ref[0, :, j, :] = jnp.concatenate(
            [o[j * s2:(j + 1) * s2] for o in outs], axis=1)


def _dft2(a5, tabs, wf, kb, scale):
    b, _, s1, s2, w = a5.shape
    const = lambda shape: pl.BlockSpec(shape, lambda i, j: (0,) * len(shape))
    twr = tabs["twr"].reshape(s1 // kb, kb, s2)
    twi = tabs["twi"].reshape(s1 // kb, kb, s2)
    return pl.pallas_call(
        functools.partial(_dft2_body, kb=kb, scale=scale),
        grid=(b, s1 // kb),
        in_specs=[pl.BlockSpec((1, 2, kb, s2, w), lambda i, j: (i, 0, j, 0, 0)),
                  const((s2, s2)), const((s2, s2)),
                  pl.BlockSpec((1, kb, s2), lambda i, j: (j, 0, 0)),
                  pl.BlockSpec((1, kb, s2), lambda i, j: (j, 0, 0)),
                  const((2 * GROUP_DIM, GROUP_DIM)),
                  const((FOURIER_GROUPS, GROUP_DIM, GROUP_DIM))],
        out_specs=pl.BlockSpec((1, s2, kb, w), lambda i, j: (i, 0, j, 0)),
        out_shape=jax.ShapeDtypeStruct((b, s2, s1, w), F32),
        compiler_params=_params(("parallel", "parallel"), VMEM_LIMIT),
        name="dft2",
    )(a5, tabs["f2r"], tabs["f2i"], twr, twi, tabs["cs"], wf)


def _fourier(zf, wf_bf16):
    b, s, w = zf.shape
    s2 = BLOCK
    s1 = s // s2
    tabs = _dft_tables(s)
    a = _dft1(zf.reshape(b, s1, s2 * w), tabs["f1"], cb=8192)
    f = _dft2(a.reshape(b, 2, s1, s2, w), tabs, wf_bf16, kb=8,
              scale=1.0 / math.sqrt(s * GROUP_DIM))
    return f.reshape(b, s, w)


def _bucket_table():
    qi = np.arange(BLOCK)[:, None]
    kj = np.arange(3 * BLOCK)[None, :]
    rel = kj - BLOCK - qi
    nb = N_BUCKETS // 2
    max_exact = nb // 2
    n = np.abs(rel)
    large = max_exact + np.floor(
        np.log(np.maximum(n, 1).astype(np.float64) / max_exact)
        / math.log(MAX_DISTANCE / max_exact) * (nb - max_exact) + 1e-9).astype(np.int64)
    large = np.minimum(large, nb - 1)
    bucket = np.where(rel > 0, nb, 0) + np.where(n < max_exact, n, large)
    return np.where(n <= WINDOW, bucket, -1).astype(np.int32)


def _bias_body(rb_ref, bucket_ref, o_ref):
    bk = bucket_ref[...]
    col = lax.broadcasted_iota(I32, bk.shape, 1)
    for h in range(N_HEADS):
        acc = jnp.full(bk.shape, -jnp.inf, F32)
        for b in range(N_BUCKETS):
            acc = jnp.where(bk == b, rb_ref[b, h], acc)
        o_ref[0, h] = acc
        o_ref[1, h] = jnp.where(col >= BLOCK, acc, -jnp.inf)
        o_ref[2, h] = jnp.where(col < 2 * BLOCK, acc, -jnp.inf)


def _bias_table(rel_bias):
    bucket = jnp.asarray(_bucket_table())
    shape = (3, N_HEADS) + bucket.shape
    return pl.pallas_call(
        _bias_body,
        in_specs=[pl.BlockSpec(memory_space=pltpu.SMEM),
                  pl.BlockSpec(bucket.shape, lambda: (0, 0))],
        out_specs=pl.BlockSpec(shape, lambda: (0, 0, 0, 0)),
        out_shape=jax.ShapeDtypeStruct(shape, F32),
        name="bias_table",
    )(rel_bias.astype(F32), bucket)


def _attn_body(sink_ref, q_ref, kp_ref, kc_ref, kn_ref, vp_ref, vc_ref, vn_ref, bias_ref,
               o_ref, *, tq):
    j = pl.program_id(1)
    last = pl.num_programs(1) - 1
    nsb = tq // BLOCK
    group = N_HEADS // N_KV_HEADS
    gw = group * HEAD_DIM
    kext = jnp.concatenate([kp_ref[0, tq - BLOCK:, :], kc_ref[0], kn_ref[0, :BLOCK, :]], axis=0)
    vext = jnp.concatenate([vp_ref[0, tq - BLOCK:, :], vc_ref[0], vn_ref[0, :BLOCK, :]], axis=0)
    low_half = lax.broadcasted_iota(I32, (1, KV_WIDTH), 1) < HEAD_DIM

    def per_kv_head(ext):
        x = ext.astype(F32)
        xr = pltpu.roll(x, HEAD_DIM, axis=1)
        return (jnp.where(low_half, x, xr).astype(BF16), jnp.where(low_half, xr, x).astype(BF16))

    krep = per_kv_head(kext)
    vrep = per_kv_head(vext)
    head_of_lane = lax.broadcasted_iota(I32, (1, gw), 1) // HEAD_DIM
    nt = (((1,), (1,)), ((), ()))
    def scores(sb, kh):
        rows = slice(sb * BLOCK, (sb + 3) * BLOCK)
        variant = 0
        if sb == 0:
            variant = jnp.where(j == 0, 1, 0)
        elif sb == nsb - 1:
            variant = jnp.where(j == last, 2, 0)
        qblk = q_ref[0, sb * BLOCK:(sb + 1) * BLOCK, kh * gw:(kh + 1) * gw]
        qm = jnp.concatenate(
            [jnp.where(head_of_lane == g, qblk, jnp.zeros_like(qblk)) for g in range(group)],
            axis=0)
        kb = jnp.concatenate([krep[kh][rows], krep[kh][rows]], axis=1)
        s = lax.dot_general(qm, kb, nt, preferred_element_type=F32)
        return s, variant

    def weights(sv, kh):
        s_all, variant = sv
        out = []
        for g in range(group):
            for r0 in range(0, BLOCK, STRIP):
                s = s_all[g * BLOCK + r0:g * BLOCK + r0 + STRIP]
                s = s + bias_ref[variant, kh * group + g, r0:r0 + STRIP, :]
                sink = sink_ref[kh * group + g]
                m = jnp.maximum(jnp.max(s, axis=-1, keepdims=True), sink)
                p = jnp.exp(s - m)
                denom = jnp.sum(p, axis=-1, keepdims=True) + jnp.exp(sink - m)
                out.append((p / denom).astype(BF16))
        return jnp.concatenate(out, axis=0)

    def values(pn, sb, kh):
        rows = slice(sb * BLOCK, (sb + 3) * BLOCK)
        o4 = jnp.dot(pn, vrep[kh][rows], preferred_element_type=F32)
        out = []
        for pair in range(group // 2):
            even = o4[(2 * pair) * BLOCK:(2 * pair + 1) * BLOCK]
            odd = o4[(2 * pair + 1) * BLOCK:(2 * pair + 2) * BLOCK]
            out.append(jnp.where(low_half, even, odd))
        return out

    items = [(sb, kh) for sb in range(nsb) for kh in range(N_KV_HEADS)]
    pieces = {}
    s_next = scores(*items[0])
    pn_prev = None
    for i, (sb, kh) in enumerate(items):
        s_cur = s_next
        if i + 1 < len(items):
            s_next = scores(*items[i + 1])
        if pn_prev is not None:
            pieces[items[i - 1]] = values(pn_prev, *items[i - 1])
        pn_prev = weights(s_cur, kh)
    pieces[items[-1]] = values(pn_prev, *items[-1])
    for sb in range(nsb):
        row = [x for kh in range(N_KV_HEADS) for x in pieces[(sb, kh)]]
        o_ref[0, sb * BLOCK:(sb + 1) * BLOCK, :] = jnp.concatenate(row, axis=1).astype(BF16)


def _attention(q, k, v, bias, sink, tq):
    b, s, _ = q.shape
    nt = s // tq
    assert tq >= 2 * BLOCK
    kv = lambda f: pl.BlockSpec((1, tq, KV_WIDTH), f)
    prev = lambda i, j: (i, jnp.maximum(j - 1, 0), 0)
    cur = lambda i, j: (i, j, 0)
    nxt = lambda i, j: (i, jnp.minimum(j + 1, nt - 1), 0)
    return pl.pallas_call(
        functools.partial(_attn_body, tq=tq),
        grid=(b, nt),
        in_specs=[pl.BlockSpec(memory_space=pltpu.SMEM),
                  pl.BlockSpec((1, tq, ATTN_WIDTH), cur),
                  kv(prev), kv(cur), kv(nxt), kv(prev), kv(cur), kv(nxt),
                  pl.BlockSpec(bias.shape, lambda i, j: (0, 0, 0, 0))],
        out_specs=pl.BlockSpec((1, tq, ATTN_WIDTH), cur),
        out_shape=jax.ShapeDtypeStruct((b, s, ATTN_WIDTH), BF16),
        compiler_params=_params(("parallel", "parallel"), VMEM_LIMIT),
        name="attn",
    )(sink.astype(F32), q, k, k, k, v, v, v, bias)


def _outproj_body(x_ref, f_ref, a_ref, wo_ref, g_ref, wr_ref, x1_ref, h2_ref, aff_ref):
    mix = jnp.concatenate([f_ref[...].astype(BF16), a_ref[...]], axis=1)
    x1 = x_ref[...] + jnp.dot(mix, wo_ref[...], preferred_element_type=F32)
    x1_ref[...] = x1
    ms = jnp.mean(x1 * x1, axis=-1, keepdims=True)
    h2 = (x1 * lax.rsqrt(ms + EPS)) * g_ref[...]
    _store_rows_as_tiles(h2_ref, h2)
    hi = h2.astype(BF16)
    lo = (h2 - hi.astype(F32)).astype(BF16)
    nt = (((1,), (1,)), ((), ()))
    o1 = lax.dot_general(wr_ref[...], hi, nt, preferred_element_type=F32)
    o2 = lax.dot_general(wr_ref[:N_EXPERTS], lo, nt, preferred_element_type=F32)
    logits = o1[:N_EXPERTS] + (o1[N_EXPERTS:] + o2)
    e = jnp.exp(logits - jnp.max(logits, axis=0, keepdims=True))
    aff_ref[...] = e / jnp.sum(e, axis=0, keepdims=True)


def _outproj(xf, f, a, w_out, gain, wr_split, tm):
    n = xf.shape[0]
    row = lambda w: pl.BlockSpec((tm, w), lambda i: (i, 0))
    const = lambda shape: pl.BlockSpec(shape, lambda i: (0, 0))
    return pl.pallas_call(
        _outproj_body,
        grid=(n // tm,),
        in_specs=[row(D_MODEL), row(FOURIER_WIDTH), row(ATTN_WIDTH),
                  const((D_MODEL, D_MODEL)), const((1, D_MODEL)),
                  const((2 * N_EXPERTS, D_MODEL))],
        out_specs=[row(D_MODEL),
                   pl.BlockSpec((tm * ROW_TILE, LANES), lambda i: (i, 0)),
                   pl.BlockSpec((N_EXPERTS, tm), lambda i: (0, i))],
        out_shape=[jax.ShapeDtypeStruct((n, D_MODEL), F32),
                   jax.ShapeDtypeStruct((n * ROW_TILE, LANES), F32),
                   jax.ShapeDtypeStruct((N_EXPERTS, n), F32)],
        compiler_params=_params(("parallel",), VMEM_LIMIT),
        name="outproj",
    )(xf, f, a, w_out, gain, wr_split)


def _route_body(aff_ref, idx_ref, pos_ref, gate_ref, tok_ref,
                d_a, d_b, n_a, n_b, p_a, p_b, g_a, g_b, *, n, cap):
    e = N_EXPERTS
    ch = PREFIX_CHUNK

    def search(i, tau):
        cand = tau | jnp.left_shift(jnp.int32(1), 30 - i)
        keys = lax.bitcast_convert_type(aff_ref[...], I32)
        cnt = jnp.sum((keys >= cand).astype(I32), axis=1, keepdims=True)
        return jnp.where(cnt >= cap, cand, tau)

    tau = lax.fori_loop(0, 31, search, jnp.zeros((e, 1), I32))
    keys = lax.bitcast_convert_type(aff_ref[...], I32)
    n_gt = jnp.sum((keys > tau).astype(I32), axis=1, keepdims=True)
    need = (cap - n_gt).astype(F32)

    r = lax.broadcasted_iota(I32, (ch, ch), 0)
    c = lax.broadcasted_iota(I32, (ch, ch), 1)
    upper = (r < c).astype(BF16)
    r16 = lax.broadcasted_iota(I32, (e, e), 0)
    c16 = lax.broadcasted_iota(I32, (e, e), 1)
    lower = (c16 < r16).astype(BF16)

    carry_eq = jnp.zeros((e, 1), F32)
    carry_m = jnp.zeros((2 * e, 1), F32)
    for ci in range(n // ch):
        sl = slice(ci * ch, (ci + 1) * ch)
        k = lax.bitcast_convert_type(aff_ref[:, sl], I32)
        gt = k > tau
        eq = (k == tau).astype(F32)
        eq_ex = jnp.dot(eq.astype(BF16), upper, preferred_element_type=F32) + carry_eq
        carry_eq = carry_eq + jnp.sum(eq, axis=1, keepdims=True)
        m = jnp.where(gt | ((eq > 0.0) & (eq_ex < need)), 1.0, 0.0).astype(F32)
        cnt = jnp.sum(m, axis=0, keepdims=True)
        stacked = jnp.concatenate([m, cnt, jnp.zeros((e - 1, ch), F32)], axis=0)
        pre = jnp.dot(stacked.astype(BF16), upper, preferred_element_type=F32) + carry_m
        carry_m = carry_m + jnp.sum(stacked, axis=1, keepdims=True)
        rank = pre[:e]
        tokoff = pre[e:e + 1]
        ex_e = jnp.dot(lower, m.astype(BF16), preferred_element_type=F32)
        tok_ref[0:1, sl] = tokoff.astype(I32)
        tok_ref[1:2, sl] = (tokoff + cnt).astype(I32)
        lane = lax.broadcasted_iota(I32, (e, ch), 1) + ci * ch
        d_a[:, sl] = jnp.where(m > 0.0, lane - rank.astype(I32), 0)
        n_a[:, sl] = lane
        p_a[:, sl] = (tokoff + ex_e).astype(I32)
    g_a[...] = aff_ref[...]

    bufs = [(d_a, n_a, p_a, g_a), (d_b, n_b, p_b, g_b)]
    for b in range(int(math.log2(n))):
        src, dst = bufs[b % 2], bufs[(b + 1) % 2]
        sh = 1 << b
        d = src[0][...]
        moving = (lax.shift_right_logical(d, b) & 1) == 1
        d_in = pltpu.roll(d, n - sh, axis=1)
        arrive = (lax.shift_right_logical(d_in, b) & 1) == 1
        dst[0][...] = jnp.where(arrive, d_in, jnp.where(moving, 0, d))
        for t in (1, 2, 3):
            v = src[t][...]
            dst[t][...] = jnp.where(arrive, pltpu.roll(v, n - sh, axis=1), v)
    fin = bufs[int(math.log2(n)) % 2]
    idx_ref[...] = fin[1][:, :cap]
    pos_ref[...] = fin[2][:, :cap]
    gate_ref[...] = fin[3][:, :cap]


def _route(aff_t, cap):
    e, n = aff_t.shape
    assert n & (n - 1) == 0 or True
    full = lambda shape: pl.BlockSpec(shape, lambda: (0, 0))
    big_i = pltpu.VMEM((e, n), I32)
    return pl.pallas_call(
        functools.partial(_route_body, n=n, cap=cap),
        in_specs=[full((e, n))],
        out_specs=[full((e, cap)), full((e, cap)), full((e, cap)), full((2, n))],
        out_shape=[jax.ShapeDtypeStruct((e, cap), I32),
                   jax.ShapeDtypeStruct((e, cap), I32),
                   jax.ShapeDtypeStruct((e, cap), F32),
                   jax.ShapeDtypeStruct((2, n), I32)],
        scratch_shapes=[big_i, big_i, big_i, big_i, big_i, big_i,
                        pltpu.VMEM((e, n), F32), pltpu.VMEM((e, n), F32)],
        compiler_params=_params(None, VMEM_LIMIT),
        name="route",
    )(aff_t)


def _rows_to_cols(rows):
    r, m = rows.shape
    if r < 8:
        rows = jnp.concatenate([rows, jnp.zeros((8 - r, m), rows.dtype)], axis=0)
    return rows.T[:, :r]


def _ffn_body(idx_cur, idx_nxt, pos_prv, pos_cur, gate_ref, h_hbm, wg_ref, wu_ref, wd_ref, ys_hbm,
              xa, xb, ya, yb, sems, *, tm):
    s = pl.program_id(0)
    last = pl.num_programs(0) - 1
    g_a, g_b, s_a, s_b = sems.at[0], sems.at[1], sems.at[2], sems.at[3]

    def gather(idx_ref, t, buf, sem):
        def one(i):
            pltpu.make_async_copy(_tile_of_row(h_hbm, idx_ref[t, 0, i]),
                                  _tile_of_row(buf, i), sem).start()
        return one

    def scatter(pos_ref, t, buf, sem):
        def one(i):
            pltpu.make_async_copy(_tile_of_row(buf, i),
                                  _tile_of_row(ys_hbm, pos_ref[t, 0, i]), sem).start()
        return one

    def wait_gather(buf, sem):
        pltpu.make_async_copy(h_hbm.at[pl.ds(0, tm * ROW_TILE), :], buf, sem).wait()

    def wait_scatter(buf, sem):
        pltpu.make_async_copy(buf, ys_hbm.at[pl.ds(0, tm * ROW_TILE), :], sem).wait()

    def compute(xbuf, ybuf, t, copies):
        nc = D_EXPERT // FFN_CHUNK
        stages = 3 * nc
        todo = [(fn, i) for fn in copies for i in range(tm)]
        per_stage = -(-len(todo) // stages)

        def issue_some():
            for fn, i in todo[:per_stage]:
                fn(i)
            del todo[:per_stage]

        x = _load_tiles_as_rows(xbuf).astype(BF16)
        cols = lambda c: slice(c * FFN_CHUNK, (c + 1) * FFN_CHUNK)

        def up(c):
            g = jnp.dot(x, wg_ref[0, :, cols(c)], preferred_element_type=F32)
            issue_some()
            u = jnp.dot(x, wu_ref[0, :, cols(c)], preferred_element_type=F32)
            issue_some()
            return g, u

        def act(gu):
            g, u = gu
            return (g * jax.nn.sigmoid(g) * u).astype(BF16)

        def down(hid, c):
            part = jnp.dot(hid, wd_ref[0, cols(c), :], preferred_element_type=F32)
            issue_some()
            return part

        gu_next = up(0)
        hid_prev = None
        y = None
        for c in range(nc):
            gu_cur = gu_next
            if c + 1 < nc:
                gu_next = up(c + 1)
            if hid_prev is not None:
                part = down(hid_prev, c - 1)
                y = part if y is None else y + part
            hid_prev = act(gu_cur)
        y = y + down(hid_prev, nc - 1)
        assert not todo
        y = y * _rows_to_cols(gate_ref[t])
        _store_rows_as_tiles(ybuf, y.astype(BF16).astype(F32))

    @pl.when(s == 0)
    def _():
        pl.loop(0, tm)(gather(idx_cur, 0, xa, g_a))
        yb[...] = jnp.zeros_like(yb)

    wait_gather(xa, g_a)

    @pl.when(s > 0)
    def _():
        wait_scatter(ya, s_a)

    compute(xa, ya, 0, [gather(idx_cur, 1, xb, g_b), scatter(pos_prv, 1, yb, s_b)])

    wait_gather(xb, g_b)
    wait_scatter(yb, s_b)
    compute(xb, yb, 1, [gather(idx_nxt, 0, xa, g_a), scatter(pos_cur, 0, ya, s_a)])

    @pl.when(s == last)
    def _():
        pl.loop(0, tm)(scatter(pos_cur, 1, yb, s_b))
        wait_gather(xa, g_a)
        wait_scatter(ya, s_a)
        wait_scatter(yb, s_b)


def _ffn(idx_c, pos_c, gate_c, h2, wg, wu, wd, tm):
    e, cap = idx_c.shape
    spe = cap // (2 * tm)
    steps = e * spe
    n_slots = e * cap
    idx3 = idx_c.reshape(2 * steps, 1, tm)
    pos3 = pos_c.reshape(2 * steps, 1, tm)
    gate3 = gate_c.reshape(2 * steps, 1, tm)
    cur = lambda i: (i, 0, 0)
    prv = lambda i: (jnp.maximum(i - 1, 0), 0, 0)
    nxt = lambda i: (jnp.minimum(i + 1, steps - 1), 0, 0)
    smem = lambda f: pl.BlockSpec((2, 1, tm), f, memory_space=pltpu.SMEM)
    wspec = lambda a, b: pl.BlockSpec((1, a, b), lambda i: (i // spe, 0, 0))
    buf = pltpu.VMEM((tm * ROW_TILE, LANES), F32)
    return pl.pallas_call(
        functools.partial(_ffn_body, tm=tm),
        grid=(steps,),
        in_specs=[smem(cur), smem(nxt), smem(prv), smem(cur),
                  pl.BlockSpec((2, 1, tm), cur),
                  pl.BlockSpec(memory_space=pl.ANY),
                  wspec(D_MODEL, D_EXPERT), wspec(D_MODEL, D_EXPERT), wspec(D_EXPERT, D_MODEL)],
        out_specs=pl.BlockSpec(memory_space=pl.ANY),
        out_shape=jax.ShapeDtypeStruct((n_slots * ROW_TILE, LANES), F32),
        scratch_shapes=[buf, buf, buf, buf, pltpu.SemaphoreType.DMA((4,))],
        compiler_params=_params(("arbitrary",), VMEM_LIMIT),
        name="ffn",
    )(idx3, idx3, pos3, pos3, gate3, h2, wg, wu, wd)


def _combine_body(toff_ref, x1_ref, tok_ref, g_ref, ys_hbm, o_ref, win, extra, sems,
                  *, tt, ws, n_slots):
    i = pl.program_id(0)
    last = pl.num_programs(0) - 1
    buf = i % 2

    def window_start(nominal):
        return pl.multiple_of(jnp.minimum(nominal, n_slots - ws), 8)

    def first_fetch(tile, b):
        start = window_start((toff_ref[tile] // 8) * 8)
        return pltpu.make_async_copy(_tiles_of_rows(ys_hbm, start, ws), win.at[b], sems.at[b])

    @pl.when(i == 0)
    def _():
        first_fetch(0, 0).start()

    first_fetch(i, buf).wait()

    @pl.when(i < last)
    def _():
        first_fetch(i + 1, 1 - buf).start()

    hi = toff_ref[i + 1]
    base = (toff_ref[i] // 8) * 8
    nwin = (hi - base + ws - 1) // ws
    tok = _rows_to_cols(tok_ref[...])
    t_lo = tok[:, 0:1]
    as_unsigned = lambda v: lax.bitcast_convert_type(v, jnp.uint32)
    t_cnt = as_unsigned(tok[:, 1:2] - t_lo)
    lane = lax.broadcasted_iota(I32, (tt, ws), 1)

    def selector(start, nominal):
        owned = as_unsigned(lane + (start - t_lo)) < t_cnt
        if nominal is not None:
            owned = owned & (lane >= nominal - start)
        return jnp.where(owned, 1.0, 0.0).astype(BF16)

    def finish(x2_cols):
        ssq = sum(jnp.sum(x * x, axis=-1, keepdims=True) for x in x2_cols)
        scale = lax.rsqrt(ssq * (1.0 / D_MODEL) + EPS)
        for c, x in enumerate(x2_cols):
            w = x.shape[1]
            o_ref[:, c * w:(c + 1) * w] = (x * scale) * g_ref[:, c * w:(c + 1) * w]

    first = win.at[buf]
    sel = selector(window_start(base), None)
    cw = 2 * LANES

    def window_cols(c):
        return jnp.concatenate(
            [first[pl.ds(2 * c + k, ws, stride=ROW_TILE), :] for k in range(2)],
            axis=1).astype(BF16)

    x2_cols = []
    rows_next = window_cols(0)
    for c in range(D_MODEL // cw):
        rows_cur = rows_next
        if (c + 1) * cw < D_MODEL:
            rows_next = window_cols(c + 1)
        x2_cols.append(x1_ref[:, c * cw:(c + 1) * cw]
                       + jnp.dot(sel, rows_cur, preferred_element_type=F32))
    finish(x2_cols)

    @pl.when(nwin > 1)
    def _():
        def later_window(w, acc):
            nominal = base + w * ws
            start = window_start(nominal)
            cp = pltpu.make_async_copy(_tiles_of_rows(ys_hbm, start, ws), extra, sems.at[2])
            cp.start()
            cp.wait()
            rows = _load_tiles_as_rows(extra).astype(BF16)
            return acc + jnp.dot(selector(start, nominal), rows, preferred_element_type=F32)

        moe = jnp.dot(sel, _load_tiles_as_rows(first).astype(BF16), preferred_element_type=F32)
        moe = lax.fori_loop(1, nwin, later_window, moe)
        finish([x1_ref[...] + moe])


def _combine(x1, ys, tokinfo, gain, tt, ws):
    n = x1.shape[0]
    n_slots = ys.shape[0] // ROW_TILE
    tile_off = jnp.concatenate([tokinfo[0, ::tt], jnp.full((1,), n_slots, I32)])
    return pl.pallas_call(
        functools.partial(_combine_body, tt=tt, ws=ws, n_slots=n_slots),
        grid_spec=pltpu.PrefetchScalarGridSpec(
            num_scalar_prefetch=1,
            grid=(n // tt,),
            in_specs=[pl.BlockSpec((tt, D_MODEL), lambda i, s: (i, 0)),
                      pl.BlockSpec((2, tt), lambda i, s: (0, i)),
                      pl.BlockSpec((1, D_MODEL), lambda i, s: (0, 0)),
                      pl.BlockSpec(memory_space=pl.ANY)],
            out_specs=pl.BlockSpec((tt, D_MODEL), lambda i, s: (i, 0)),
            scratch_shapes=[pltpu.VMEM((2, ws * ROW_TILE, LANES), F32),
                            pltpu.VMEM((ws * ROW_TILE, LANES), F32),
                            pltpu.SemaphoreType.DMA((3,))]),
        out_shape=jax.ShapeDtypeStruct((n, D_MODEL), F32),
        compiler_params=_params(("arbitrary",), VMEM_LIMIT),
        name="combine",
    )(tile_off, x1, tokinfo, gain, ys)


def _tile(n, pref):
    t = pref
    while n % t:
        t //= 2
    return t


def _trunk(x, p):
    b, s, d = x.shape
    n = b * s
    cap = CAPACITY_FACTOR * n // N_EXPERTS
    xf = x.reshape(n, d)
    tm = _tile(n, 512)
    zf, q, k, v = _inproj(xf, p["norm_mix"], p["w_in"], tm)
    f = _fourier(zf.reshape(b, s, FOURIER_WIDTH), p["w_fourier"])
    a = _attention(q.reshape(b, s, ATTN_WIDTH), k.reshape(b, s, KV_WIDTH),
                   v.reshape(b, s, KV_WIDTH), p["bias"], p["sink"], _tile(s, 512))
    x1, h2, aff_t = _outproj(xf, f.reshape(n, FOURIER_WIDTH), a.reshape(n, ATTN_WIDTH),
                             p["w_out"], p["norm_ffn"], p["w_router"], tm)
    idx_c, pos_c, gate_c, tokinfo = _route(aff_t, cap)
    ys = _ffn(idx_c, pos_c, gate_c, h2, p["w_gate"], p["w_up"], p["w_down"], _tile(cap // 2, 512))
    tt = _tile(n, 512)
    ws = min(CAPACITY_FACTOR * tt + tt // 4, N_EXPERTS * cap)
    y = _combine(x1, ys, tokinfo, p["norm_final"], tt, ws)
    return y.reshape(b, s, d)


def kernel(x_prompt, x_sample, w_in, w_fourier, attn_sink, rel_bias, w_out, norm_mix, norm_ffn,
           w_router, w_gate, w_up, w_down, norm_final):
    assert w_in.shape[0] == 1
    wr = w_router[0].T
    wr_hi = wr.astype(BF16)
    wr_lo = (wr - wr_hi.astype(F32)).astype(BF16)
    p = dict(
        norm_mix=norm_mix[0].reshape(1, D_MODEL),
        norm_ffn=norm_ffn[0].reshape(1, D_MODEL),
        norm_final=norm_final.reshape(1, D_MODEL),
        w_in=w_in[0].astype(BF16),
        w_fourier=w_fourier[0].astype(BF16),
        w_out=w_out[0].astype(BF16),
        w_router=jnp.concatenate([wr_hi, wr_lo], axis=0),
        w_gate=w_gate[0].astype(BF16),
        w_up=w_up[0].astype(BF16),
        w_down=w_down[0].astype(BF16),
        sink=attn_sink[0],
        bias=_bias_table(rel_bias),
    )
    return _trunk(x_prompt, p), _trunk(x_sample, p)
```

```python
import functools
import math

import numpy as np
import jax
import jax.numpy as jnp
from jax import lax
from jax.experimental import pallas as pl
from jax.experimental.pallas import tpu as pltpu

F32 = jnp.float32
BF16 = jnp.bfloat16
I32 = jnp.int32

D_MODEL = 1024
FOURIER_GROUPS = 4
GROUP_DIM = 128
FOURIER_WIDTH = FOURIER_GROUPS * GROUP_DIM
N_HEADS = 8
N_KV_HEADS = 2
HEAD_DIM = 64
ATTN_WIDTH = N_HEADS * HEAD_DIM
KV_WIDTH = N_KV_HEADS * HEAD_DIM
WINDOW = 128
BLOCK = 128
N_BUCKETS = 32
MAX_DISTANCE = 128
IN_WIDTH = FOURIER_WIDTH + ATTN_WIDTH + 2 * KV_WIDTH
N_EXPERTS = 16
CAPACITY_FACTOR = 2
D_EXPERT = 1024
EPS = 1e-6

LANES = 128
PREFIX_CHUNK = 256
STRIP = 32
FFN_CHUNK = 256
VMEM_LIMIT = 56 * 1024 * 1024


ROW_TILE = D_MODEL // LANES


def _params(sem, vmem=None):
    return pltpu.CompilerParams(dimension_semantics=sem, vmem_limit_bytes=vmem)


def _store_rows_as_tiles(ref, val):
    m = val.shape[0]
    for c in range(ROW_TILE):
        ref[pl.ds(c, m, stride=ROW_TILE), :] = val[:, c * LANES:(c + 1) * LANES]


def _load_tiles_as_rows(ref):
    m = ref.shape[0] // ROW_TILE
    return jnp.concatenate(
        [ref[pl.ds(c, m, stride=ROW_TILE), :] for c in range(ROW_TILE)], axis=1)


def _tiles_of_rows(ref, r, count):
    return ref.at[pl.ds(pl.multiple_of(r * ROW_TILE, ROW_TILE), count * ROW_TILE), :]


def _tile_of_row(ref, r):
    start = r * ROW_TILE
    if not isinstance(r, int):
        start = pl.multiple_of(start, ROW_TILE)
    return ref.at[pl.ds(start, ROW_TILE), :]


def _inproj_body(x_ref, g_ref, w_ref, zf_ref, q_ref, k_ref, v_ref):
    x = x_ref[...]
    ms = jnp.mean(x * x, axis=-1, keepdims=True)
    h = (x * lax.rsqrt(ms + EPS)) * g_ref[...]
    z = jnp.dot(h.astype(BF16), w_ref[...], preferred_element_type=F32)
    o = FOURIER_WIDTH
    zf_ref[...] = z[:, :o].astype(BF16)
    q_ref[...] = (z[:, o:o + ATTN_WIDTH] * (HEAD_DIM ** -0.5)).astype(BF16)
    o += ATTN_WIDTH
    k_ref[...] = z[:, o:o + KV_WIDTH].astype(BF16)
    o += KV_WIDTH
    v_ref[...] = z[:, o:o + KV_WIDTH].astype(BF16)


def _inproj(xf, gain, w_in, tm):
    n = xf.shape[0]
    row = lambda w: pl.BlockSpec((tm, w), lambda i: (i, 0))
    return pl.pallas_call(
        _inproj_body,
        grid=(n // tm,),
        in_specs=[row(D_MODEL),
                  pl.BlockSpec((1, D_MODEL), lambda i: (0, 0)),
                  pl.BlockSpec((D_MODEL, IN_WIDTH), lambda i: (0, 0))],
        out_specs=[row(FOURIER_WIDTH), row(ATTN_WIDTH), row(KV_WIDTH), row(KV_WIDTH)],
        out_shape=[jax.ShapeDtypeStruct((n, FOURIER_WIDTH), BF16),
                   jax.ShapeDtypeStruct((n, ATTN_WIDTH), BF16),
                   jax.ShapeDtypeStruct((n, KV_WIDTH), BF16),
                   jax.ShapeDtypeStruct((n, KV_WIDTH), BF16)],
        compiler_params=_params(("parallel",), VMEM_LIMIT),
        name="inproj",
    )(xf, gain, w_in)


def _dft_tables(s):
    s2 = BLOCK
    s1 = s // s2
    a = np.arange(s1, dtype=np.float64)
    ang1 = 2.0 * np.pi * np.outer(a, a) / s1
    f1 = np.concatenate([np.cos(ang1), -np.sin(ang1)], axis=0)
    b = np.arange(s2, dtype=np.float64)
    ang2 = 2.0 * np.pi * np.outer(b, b) / s2
    angt = 2.0 * np.pi * np.outer(a, b) / s
    c = np.arange(GROUP_DIM, dtype=np.float64)
    angc = 2.0 * np.pi * np.outer(c, c) / GROUP_DIM
    cs = np.concatenate([np.cos(angc), np.sin(angc)], axis=0)
    return dict(
        f1=jnp.asarray(f1, BF16),
        f2r=jnp.asarray(np.cos(ang2), F32), f2i=jnp.asarray(-np.sin(ang2), F32),
        twr=jnp.asarray(np.cos(angt), F32), twi=jnp.asarray(-np.sin(angt), F32),
        cs=jnp.asarray(cs, BF16))


def _dft1_body(f_ref, x_ref, o_ref):
    s1, cb = x_ref.shape[1], x_ref.shape[2]
    sub = min(cb, 2048)
    for c in range(0, cb, sub):
        res = jnp.dot(f_ref[...], x_ref[0, :, c:c + sub], preferred_element_type=F32)
        o_ref[0, 0, :, c:c + sub] = res[:s1].astype(BF16)
        o_ref[0, 1, :, c:c + sub] = res[s1:].astype(BF16)


def _dft1(zf3, f1, cb):
    b, s1, w = zf3.shape
    return pl.pallas_call(
        _dft1_body,
        grid=(b, w // cb),
        in_specs=[pl.BlockSpec((2 * s1, s1), lambda i, j: (0, 0)),
                  pl.BlockSpec((1, s1, cb), lambda i, j: (i, 0, j))],
        out_specs=pl.BlockSpec((1, 2, s1, cb), lambda i, j: (i, 0, 0, j)),
        out_shape=jax.ShapeDtypeStruct((b, 2, s1, w), BF16),
        compiler_params=_params(("parallel", "parallel"), VMEM_LIMIT),
        name="dft1",
    )(f1, zf3)


def _dft2_body(a_ref, f2r_ref, f2i_ref, twr_ref, twi_ref, cs_ref, wf_ref, o_ref, *, kb, scale):
    s2 = BLOCK
    fr = f2r_ref[...]
    fi = f2i_ref[...]
    ys = []
    for j in range(kb):
        tr = twr_ref[0, j:j + 1, :]
        ti = twi_ref[0, j:j + 1, :]
        gr = fr * tr - fi * ti
        gi = fr * ti + fi * tr
        lhs = jnp.concatenate(
            [jnp.concatenate([gr, -gi], axis=1), jnp.concatenate([gi, gr], axis=1)],
            axis=0).astype(BF16)
        rhs = jnp.concatenate([a_ref[0, 0, j], a_ref[0, 1, j]], axis=0)
        ys.append(jnp.dot(lhs, rhs, preferred_element_type=F32))
    cat = jnp.concatenate(
        [jnp.concatenate([y[:s2, g * GROUP_DIM:(g + 1) * GROUP_DIM],
                          y[s2:, g * GROUP_DIM:(g + 1) * GROUP_DIM]], axis=1)
         for g in range(FOURIER_GROUPS) for y in ys], axis=0).astype(BF16)
    re = (jnp.dot(cat, cs_ref[...], preferred_element_type=F32) * scale).astype(BF16)
    rows = kb * s2
    outs = [jnp.dot(re[g * rows:(g + 1) * rows], wf_ref[g], preferred_element_type=F32)
            for g in range(FOURIER_GROUPS)]
    for j in range(kb):
        o_ref[0, :, j, :] = jnp.concatenate(
            [o[j * s2:(j + 1) * s2] for o in outs], axis=1)


def _dft2(a5, tabs, wf, kb, scale):
    b, _, s1, s2, w = a5.shape
    const = lambda shape: pl.BlockSpec(shape, lambda i, j: (0,) * len(shape))
    twr = tabs["twr"].reshape(s1 // kb, kb, s2)
    twi = tabs["twi"].reshape(s1 // kb, kb, s2)
    return pl.pallas_call(
        functools.partial(_dft2_body, kb=kb, scale=scale),
        grid=(b, s1 // kb),
        in_specs=[pl.BlockSpec((1, 2, kb, s2, w), lambda i, j: (i, 0, j, 0, 0)),
                  const((s2, s2)), const((s2, s2)),
                  pl.BlockSpec((1, kb, s2), lambda i, j: (j, 0, 0)),
                  pl.BlockSpec((1, kb, s2), lambda i, j: (j, 0, 0)),
                  const((2 * GROUP_DIM, GROUP_DIM)),
                  const((FOURIER_GROUPS, GROUP_DIM, GROUP_DIM))],
        out_specs=pl.BlockSpec((1, s2, kb, w), lambda i, j: (i, 0, j, 0)),
        out_shape=jax.ShapeDtypeStruct((b, s2, s1, w), F32),
        compiler_params=_params(("parallel", "parallel"), VMEM_LIMIT),
        name="dft2",
    )(a5, tabs["f2r"], tabs["f2i"], twr, twi, tabs["cs"], wf)


def _fourier(zf, wf_bf16):
    b, s, w = zf.shape
    s2 = BLOCK
    s1 = s // s2
    tabs = _dft_tables(s)
    a = _dft1(zf.reshape(b, s1, s2 * w), tabs["f1"], cb=8192)
    f = _dft2(a.reshape(b, 2, s1, s2, w), tabs, wf_bf16, kb=8,
              scale=1.0 / math.sqrt(s * GROUP_DIM))
    return f.reshape(b, s, w)


def _bucket_table():
    qi = np.arange(BLOCK)[:, None]
    kj = np.arange(3 * BLOCK)[None, :]
    rel = kj - BLOCK - qi
    nb = N_BUCKETS // 2
    max_exact = nb // 2
    n = np.abs(rel)
    large = max_exact + np.floor(
        np.log(np.maximum(n, 1).astype(np.float64) / max_exact)
        / math.log(MAX_DISTANCE / max_exact) * (nb - max_exact) + 1e-9).astype(np.int64)
    large = np.minimum(large, nb - 1)
    bucket = np.where(rel > 0, nb, 0) + np.where(n < max_exact, n, large)
    return np.where(n <= WINDOW, bucket, -1).astype(np.int32)


def _bias_body(rb_ref, bucket_ref, o_ref):
    bk = bucket_ref[...]
    col = lax.broadcasted_iota(I32, bk.shape, 1)
    for h in range(N_HEADS):
        acc = jnp.full(bk.shape, -jnp.inf, F32)
        for b in range(N_BUCKETS):
            acc = jnp.where(bk == b, rb_ref[b, h], acc)
        o_ref[0, h] = acc
        o_ref[1, h] = jnp.where(col >= BLOCK, acc, -jnp.inf)
        o_ref[2, h] = jnp.where(col < 2 * BLOCK, acc, -jnp.inf)


def _bias_table(rel_bias):
    bucket = jnp.asarray(_bucket_table())
    shape = (3, N_HEADS) + bucket.shape
    return pl.pallas_call(
        _bias_body,
        in_specs=[pl.BlockSpec(memory_space=pltpu.SMEM),
                  pl.BlockSpec(bucket.shape, lambda: (0, 0))],
        out_specs=pl.BlockSpec(shape, lambda: (0, 0, 0, 0)),
        out_shape=jax.ShapeDtypeStruct(shape, F32),
        name="bias_table",
    )(rel_bias.astype(F32), bucket)


def _attn_body(sink_ref, q_ref, kp_ref, kc_ref, kn_ref, vp_ref, vc_ref, vn_ref, bias_ref,
               o_ref, *, tq):
    j = pl.program_id(1)
    last = pl.num_programs(1) - 1
    nsb = tq // BLOCK
    group = N_HEADS // N_KV_HEADS
    gw = group * HEAD_DIM
    kext = jnp.concatenate([kp_ref[0, tq - BLOCK:, :], kc_ref[0], kn_ref[0, :BLOCK, :]], axis=0)
    vext = jnp.concatenate([vp_ref[0, tq - BLOCK:, :], vc_ref[0], vn_ref[0, :BLOCK, :]], axis=0)
    low_half = lax.broadcasted_iota(I32, (1, KV_WIDTH), 1) < HEAD_DIM

    def per_kv_head(ext):
        x = ext.astype(F32)
        xr = pltpu.roll(x, HEAD_DIM, axis=1)
        return (jnp.where(low_half, x, xr).astype(BF16), jnp.where(low_half, xr, x).astype(BF16))

    krep = per_kv_head(kext)
    vrep = per_kv_head(vext)
    head_of_lane = lax.broadcasted_iota(I32, (1, gw), 1) // HEAD_DIM
    nt = (((1,), (1,)), ((), ()))
    def scores(sb, kh):
        rows = slice(sb * BLOCK, (sb + 3) * BLOCK)
        variant = 0
        if sb == 0:
            variant = jnp.where(j == 0, 1, 0)
        elif sb == nsb - 1:
            variant = jnp.where(j == last, 2, 0)
        qblk = q_ref[0, sb * BLOCK:(sb + 1) * BLOCK, kh * gw:(kh + 1) * gw]
        qm = jnp.concatenate(
            [jnp.where(head_of_lane == g, qblk, jnp.zeros_like(qblk)) for g in range(group)],
            axis=0)
        kb = jnp.concatenate([krep[kh][rows], krep[kh][rows]], axis=1)
        s = lax.dot_general(qm, kb, nt, preferred_element_type=F32)
        return s, variant

    def weights(sv, kh):
        s_all, variant = sv
        out = []
        for g in range(group):
            for r0 in range(0, BLOCK, STRIP):
                s = s_all[g * BLOCK + r0:g * BLOCK + r0 + STRIP]
                s = s + bias_ref[variant, kh * group + g, r0:r0 + STRIP, :]
                sink = sink_ref[kh * group + g]
                m = jnp.maximum(jnp.max(s, axis=-1, keepdims=True), sink)
                p = jnp.exp(s - m)
                denom = jnp.sum(p, axis=-1, keepdims=True) + jnp.exp(sink - m)
                out.append((p / denom).astype(BF16))
        return jnp.concatenate(out, axis=0)

    def values(pn, sb, kh):
        rows = slice(sb * BLOCK, (sb + 3) * BLOCK)
        o4 = jnp.dot(pn, vrep[kh][rows], preferred_element_type=F32)
        out = []
        for pair in range(group // 2):
            even = o4[(2 * pair) * BLOCK:(2 * pair + 1) * BLOCK]
            odd = o4[(2 * pair + 1) * BLOCK:(2 * pair + 2) * BLOCK]
            out.append(jnp.where(low_half, even, odd))
        return out

    items = [(sb, kh) for sb in range(nsb) for kh in range(N_KV_HEADS)]
    pieces = {}
    s_next = scores(*items[0])
    pn_prev = None
    for i, (sb, kh) in enumerate(items):
        s_cur = s_next
        if i + 1 < len(items):
            s_next = scores(*items[i + 1])
        if pn_prev is not None:
            pieces[items[i - 1]] = values(pn_prev, *items[i - 1])
        pn_prev = weights(s_cur, kh)
    pieces[items[-1]] = values(pn_prev, *items[-1])
    for sb in range(nsb):
        row = [x for kh in range(N_KV_HEADS) for x in pieces[(sb, kh)]]
        o_ref[0, sb * BLOCK:(sb + 1) * BLOCK, :] = jnp.concatenate(row, axis=1).astype(BF16)


def _attention(q, k, v, bias, sink, tq):
    b, s, _ = q.shape
    nt = s // tq
    assert tq >= 2 * BLOCK
    kv = lambda f: pl.BlockSpec((1, tq, KV_WIDTH), f)
    prev = lambda i, j: (i, jnp.maximum(j - 1, 0), 0)
    cur = lambda i, j: (i, j, 0)
    nxt = lambda i, j: (i, jnp.minimum(j + 1, nt - 1), 0)
    return pl.pallas_call(
        functools.partial(_attn_body, tq=tq),
        grid=(b, nt),
        in_specs=[pl.BlockSpec(memory_space=pltpu.SMEM),
                  pl.BlockSpec((1, tq, ATTN_WIDTH), cur),
                  kv(prev), kv(cur), kv(nxt), kv(prev), kv(cur), kv(nxt),
                  pl.BlockSpec(bias.shape, lambda i, j: (0, 0, 0, 0))],
        out_specs=pl.BlockSpec((1, tq, ATTN_WIDTH), cur),
        out_shape=jax.ShapeDtypeStruct((b, s, ATTN_WIDTH), BF16),
        compiler_params=_params(("parallel", "parallel"), VMEM_LIMIT),
        name="attn",
    )(sink.astype(F32), q, k, k, k, v, v, v, bias)


def _outproj_body(x_ref, f_ref, a_ref, wo_ref, g_ref, wr_ref, x1_ref, h2_ref, aff_ref):
    mix = jnp.concatenate([f_ref[...].astype(BF16), a_ref[...]], axis=1)
    x1 = x_ref[...] + jnp.dot(mix, wo_ref[...], preferred_element_type=F32)
    x1_ref[...] = x1
    ms = jnp.mean(x1 * x1, axis=-1, keepdims=True)
    h2 = (x1 * lax.rsqrt(ms + EPS)) * g_ref[...]
    _store_rows_as_tiles(h2_ref, h2)
    hi = h2.astype(BF16)
    lo = (h2 - hi.astype(F32)).astype(BF16)
    nt = (((1,), (1,)), ((), ()))
    o1 = lax.dot_general(wr_ref[...], hi, nt, preferred_element_type=F32)
    o2 = lax.dot_general(wr_ref[:N_EXPERTS], lo, nt, preferred_element_type=F32)
    logits = o1[:N_EXPERTS] + (o1[N_EXPERTS:] + o2)
    e = jnp.exp(logits - jnp.max(logits, axis=0, keepdims=True))
    aff_ref[...] = e / jnp.sum(e, axis=0, keepdims=True)


def _outproj(xf, f, a, w_out, gain, wr_split, tm):
    n = xf.shape[0]
    row = lambda w: pl.BlockSpec((tm, w), lambda i: (i, 0))
    const = lambda shape: pl.BlockSpec(shape, lambda i: (0, 0))
    return pl.pallas_call(
        _outproj_body,
        grid=(n // tm,),
        in_specs=[row(D_MODEL), row(FOURIER_WIDTH), row(ATTN_WIDTH),
                  const((D_MODEL, D_MODEL)), const((1, D_MODEL)),
                  const((2 * N_EXPERTS, D_MODEL))],
        out_specs=[row(D_MODEL),
                   pl.BlockSpec((tm * ROW_TILE, LANES), lambda i: (i, 0)),
                   pl.BlockSpec((N_EXPERTS, tm), lambda i: (0, i))],
        out_shape=[jax.ShapeDtypeStruct((n, D_MODEL), F32),
                   jax.ShapeDtypeStruct((n * ROW_TILE, LANES), F32),
                   jax.ShapeDtypeStruct((N_EXPERTS, n), F32)],
        compiler_params=_params(("parallel",), VMEM_LIMIT),
        name="outproj",
    )(xf, f, a, w_out, gain, wr_split)


def _route_body(aff_ref, idx_ref, pos_ref, gate_ref, tok_ref,
                d_a, d_b, n_a, n_b, p_a, p_b, g_a, g_b, *, n, cap):
    e = N_EXPERTS
    ch = PREFIX_CHUNK

    def search(i, tau):
        cand = tau | jnp.left_shift(jnp.int32(1), 30 - i)
        keys = lax.bitcast_convert_type(aff_ref[...], I32)
        cnt = jnp.sum((keys >= cand).astype(I32), axis=1, keepdims=True)
        return jnp.where(cnt >= cap, cand, tau)

    tau = lax.fori_loop(0, 31, search, jnp.zeros((e, 1), I32))
    keys = lax.bitcast_convert_type(aff_ref[...], I32)
    n_gt = jnp.sum((keys > tau).astype(I32), axis=1, keepdims=True)
    need = (cap - n_gt).astype(F32)

    r = lax.broadcasted_iota(I32, (ch, ch), 0)
    c = lax.broadcasted_iota(I32, (ch, ch), 1)
    upper = (r < c).astype(BF16)
    r16 = lax.broadcasted_iota(I32, (e, e), 0)
    c16 = lax.broadcasted_iota(I32, (e, e), 1)
    lower = (c16 < r16).astype(BF16)

    carry_eq = jnp.zeros((e, 1), F32)
    carry_m = jnp.zeros((2 * e, 1), F32)
    for ci in range(n // ch):
        sl = slice(ci * ch, (ci + 1) * ch)
        k = lax.bitcast_convert_type(aff_ref[:, sl], I32)
        gt = k > tau
        eq = (k == tau).astype(F32)
        eq_ex = jnp.dot(eq.astype(BF16), upper, preferred_element_type=F32) + carry_eq
        carry_eq = carry_eq + jnp.sum(eq, axis=1, keepdims=True)
        m = jnp.where(gt | ((eq > 0.0) & (eq_ex < need)), 1.0, 0.0).astype(F32)
        cnt = jnp.sum(m, axis=0, keepdims=True)
        stacked = jnp.concatenate([m, cnt, jnp.zeros((e - 1, ch), F32)], axis=0)
        pre = jnp.dot(stacked.astype(BF16), upper, preferred_element_type=F32) + carry_m
        carry_m = carry_m + jnp.sum(stacked, axis=1, keepdims=True)
        rank = pre[:e]
        tokoff = pre[e:e + 1]
        ex_e = jnp.dot(lower, m.astype(BF16), preferred_element_type=F32)
        tok_ref[0:1, sl] = tokoff.astype(I32)
        tok_ref[1:2, sl] = (tokoff + cnt).astype(I32)
        lane = lax.broadcasted_iota(I32, (e, ch), 1) + ci * ch
        d_a[:, sl] = jnp.where(m > 0.0, lane - rank.astype(I32), 0)
        n_a[:, sl] = lane
        p_a[:, sl] = (tokoff + ex_e).astype(I32)
    g_a[...] = aff_ref[...]

    bufs = [(d_a, n_a, p_a, g_a), (d_b, n_b, p_b, g_b)]
    for b in range(int(math.log2(n))):
        src, dst = bufs[b % 2], bufs[(b + 1) % 2]
        sh = 1 << b
        d = src[0][...]
        moving = (lax.shift_right_logical(d, b) & 1) == 1
        d_in = pltpu.roll(d, n - sh, axis=1)
        arrive = (lax.shift_right_logical(d_in, b) & 1) == 1
        dst[0][...] = jnp.where(arrive, d_in, jnp.where(moving, 0, d))
        for t in (1, 2, 3):
            v = src[t][...]
            dst[t][...] = jnp.where(arrive, pltpu.roll(v, n - sh, axis=1), v)
    fin = bufs[int(math.log2(n)) % 2]
    idx_ref[...] = fin[1][:, :cap]
    pos_ref[...] = fin[2][:, :cap]
    gate_ref[...] = fin[3][:, :cap]


def _route(aff_t, cap):
    e, n = aff_t.shape
    assert n & (n - 1) == 0 or True
    full = lambda shape: pl.BlockSpec(shape, lambda: (0, 0))
    big_i = pltpu.VMEM((e, n), I32)
    return pl.pallas_call(
        functools.partial(_route_body, n=n, cap=cap),
        in_specs=[full((e, n))],
        out_specs=[full((e, cap)), full((e, cap)), full((e, cap)), full((2, n))],
        out_shape=[jax.ShapeDtypeStruct((e, cap), I32),
                   jax.ShapeDtypeStruct((e, cap), I32),
                   jax.ShapeDtypeStruct((e, cap), F32),
                   jax.ShapeDtypeStruct((2, n), I32)],
        scratch_shapes=[big_i, big_i, big_i, big_i, big_i, big_i,
                        pltpu.VMEM((e, n), F32), pltpu.VMEM((e, n), F32)],
        compiler_params=_params(None, VMEM_LIMIT),
        name="route",
    )(aff_t)


def _rows_to_cols(rows):
    r, m = rows.shape
    if r < 8:
        rows = jnp.concatenate([rows, jnp.zeros((8 - r, m), rows.dtype)], axis=0)
    return rows.T[:, :r]


def _ffn_body(idx_cur, idx_nxt, pos_prv, pos_cur, gate_ref, h_hbm, wg_ref, wu_ref, wd_ref, ys_hbm,
              xa, xb, ya, yb, wg_s, wu_s, wd_s, sems, *, tm, spe):
    s = pl.program_id(0)
    last = pl.num_programs(0) - 1
    g_a, g_b, s_a, s_b = sems.at[0], sems.at[1], sems.at[2], sems.at[3]

    def gather(idx_ref, t, buf, sem):
        def one(i):
            pltpu.make_async_copy(_tile_of_row(h_hbm, idx_ref[t, 0, i]),
                                  _tile_of_row(buf, i), sem).start()
        return one

    def scatter(pos_ref, t, buf, sem):
        def one(i):
            pltpu.make_async_copy(_tile_of_row(buf, i),
                                  _tile_of_row(ys_hbm, pos_ref[t, 0, i]), sem).start()
        return one

    def wait_gather(buf, sem):
        pltpu.make_async_copy(h_hbm.at[pl.ds(0, tm * ROW_TILE), :], buf, sem).wait()

    def wait_scatter(buf, sem):
        pltpu.make_async_copy(buf, ys_hbm.at[pl.ds(0, tm * ROW_TILE), :], sem).wait()

    def compute(xbuf, ybuf, t, copies):
        nc = D_EXPERT // FFN_CHUNK
        stages = 3 * nc
        todo = [(fn, i) for fn in copies for i in range(tm)]
        per_stage = -(-len(todo) // stages)

        def issue_some():
            for fn, i in todo[:per_stage]:
                fn(i)
            del todo[:per_stage]

        x = _load_tiles_as_rows(xbuf).astype(BF16)
        cols = lambda c: slice(c * FFN_CHUNK, (c + 1) * FFN_CHUNK)

        def up(c):
            g = jnp.dot(x, wg_s[:, cols(c)], preferred_element_type=F32)
            issue_some()
            u = jnp.dot(x, wu_s[:, cols(c)], preferred_element_type=F32)
            issue_some()
            return g, u

        def act(gu):
            g, u = gu
            return (g * jax.nn.sigmoid(g) * u).astype(BF16)

        def down(hid, c):
            part = jnp.dot(hid, wd_s[cols(c), :], preferred_element_type=F32)
            issue_some()
            return part

        gu_next = up(0)
        hid_prev = None
        y = None
        for c in range(nc):
            gu_cur = gu_next
            if c + 1 < nc:
                gu_next = up(c + 1)
            if hid_prev is not None:
                part = down(hid_prev, c - 1)
                y = part if y is None else y + part
            hid_prev = act(gu_cur)
        y = y + down(hid_prev, nc - 1)
        assert not todo
        y = y * _rows_to_cols(gate_ref[t])
        _store_rows_as_tiles(ybuf, y.astype(BF16).astype(F32))

    @pl.when(s % spe == 0)
    def _():
        rows = 256
        for w_ref, w_s in ((wg_ref, wg_s), (wu_ref, wu_s), (wd_ref, wd_s)):
            for r in range(0, w_s.shape[0], rows):
                w_s[r:r + rows, :] = w_ref[0, r:r + rows, :].astype(BF16)

    @pl.when(s == 0)
    def _():
        pl.loop(0, tm)(gather(idx_cur, 0, xa, g_a))
        yb[...] = jnp.zeros_like(yb)

    wait_gather(xa, g_a)

    @pl.when(s > 0)
    def _():
        wait_scatter(ya, s_a)

    compute(xa, ya, 0, [gather(idx_cur, 1, xb, g_b), scatter(pos_prv, 1, yb, s_b)])

    wait_gather(xb, g_b)
    wait_scatter(yb, s_b)
    compute(xb, yb, 1, [gather(idx_nxt, 0, xa, g_a), scatter(pos_cur, 0, ya, s_a)])

    @pl.when(s == last)
    def _():
        pl.loop(0, tm)(scatter(pos_cur, 1, yb, s_b))
        wait_gather(xa, g_a)
        wait_scatter(ya, s_a)
        wait_scatter(yb, s_b)


def _ffn(idx_c, pos_c, gate_c, h2, wg, wu, wd, tm):
    e, cap = idx_c.shape
    spe = cap // (2 * tm)
    steps = e * spe
    n_slots = e * cap
    idx3 = idx_c.reshape(2 * steps, 1, tm)
    pos3 = pos_c.reshape(2 * steps, 1, tm)
    gate3 = gate_c.reshape(2 * steps, 1, tm)
    cur = lambda i: (i, 0, 0)
    prv = lambda i: (jnp.maximum(i - 1, 0), 0, 0)
    nxt = lambda i: (jnp.minimum(i + 1, steps - 1), 0, 0)
    smem = lambda f: pl.BlockSpec((2, 1, tm), f, memory_space=pltpu.SMEM)
    wspec = lambda a, b: pl.BlockSpec((1, a, b), lambda i: (i // spe, 0, 0))
    buf = pltpu.VMEM((tm * ROW_TILE, LANES), F32)
    return pl.pallas_call(
        functools.partial(_ffn_body, tm=tm, spe=spe),
        grid=(steps,),
        in_specs=[smem(cur), smem(nxt), smem(prv), smem(cur),
                  pl.BlockSpec((2, 1, tm), cur),
                  pl.BlockSpec(memory_space=pl.ANY),
                  wspec(D_MODEL, D_EXPERT), wspec(D_MODEL, D_EXPERT), wspec(D_EXPERT, D_MODEL)],
        out_specs=pl.BlockSpec(memory_space=pl.ANY),
        out_shape=jax.ShapeDtypeStruct((n_slots * ROW_TILE, LANES), F32),
        scratch_shapes=[buf, buf, buf, buf,
                        pltpu.VMEM((D_MODEL, D_EXPERT), BF16), pltpu.VMEM((D_MODEL, D_EXPERT), BF16),
                        pltpu.VMEM((D_EXPERT, D_MODEL), BF16), pltpu.SemaphoreType.DMA((4,))],
        compiler_params=_params(("arbitrary",), VMEM_LIMIT),
        name="ffn",
    )(idx3, idx3, pos3, pos3, gate3, h2, wg, wu, wd)


def _combine_body(toff_ref, x1_ref, tok_ref, g_ref, ys_hbm, o_ref, win, extra, sems,
                  *, tt, ws, n_slots):
    i = pl.program_id(0)
    last = pl.num_programs(0) - 1
    buf = i % 2

    def window_start(nominal):
        return pl.multiple_of(jnp.minimum(nominal, n_slots - ws), 8)

    def first_fetch(tile, b):
        start = window_start((toff_ref[tile] // 8) * 8)
        return pltpu.make_async_copy(_tiles_of_rows(ys_hbm, start, ws), win.at[b], sems.at[b])

    @pl.when(i == 0)
    def _():
        first_fetch(0, 0).start()

    first_fetch(i, buf).wait()

    @pl.when(i < last)
    def _():
        first_fetch(i + 1, 1 - buf).start()

    hi = toff_ref[i + 1]
    base = (toff_ref[i] // 8) * 8
    nwin = (hi - base + ws - 1) // ws
    tok = _rows_to_cols(tok_ref[...])
    t_lo = tok[:, 0:1]
    as_unsigned = lambda v: lax.bitcast_convert_type(v, jnp.uint32)
    t_cnt = as_unsigned(tok[:, 1:2] - t_lo)
    lane = lax.broadcasted_iota(I32, (tt, ws), 1)

    def selector(start, nominal):
        owned = as_unsigned(lane + (start - t_lo)) < t_cnt
        if nominal is not None:
            owned = owned & (lane >= nominal - start)
        return jnp.where(owned, 1.0, 0.0).astype(BF16)

    def finish(x2_cols):
        ssq = sum(jnp.sum(x * x, axis=-1, keepdims=True) for x in x2_cols)
        scale = lax.rsqrt(ssq * (1.0 / D_MODEL) + EPS)
        for c, x in enumerate(x2_cols):
            w = x.shape[1]
            o_ref[:, c * w:(c + 1) * w] = (x * scale) * g_ref[:, c * w:(c + 1) * w]

    first = win.at[buf]
    sel = selector(window_start(base), None)
    cw = 2 * LANES

    def window_cols(c):
        return jnp.concatenate(
            [first[pl.ds(2 * c + k, ws, stride=ROW_TILE), :] for k in range(2)],
            axis=1).astype(BF16)

    x2_cols = []
    rows_next = window_cols(0)
    for c in range(D_MODEL // cw):
        rows_cur = rows_next
        if (c + 1) * cw < D_MODEL:
            rows_next = window_cols(c + 1)
        x2_cols.append(x1_ref[:, c * cw:(c + 1) * cw]
                       + jnp.dot(sel, rows_cur, preferred_element_type=F32))
    finish(x2_cols)

    @pl.when(nwin > 1)
    def _():
        def later_window(w, acc):
            nominal = base + w * ws
            start = window_start(nominal)
            cp = pltpu.make_async_copy(_tiles_of_rows(ys_hbm, start, ws), extra, sems.at[2])
            cp.start()
            cp.wait()
            rows = _load_tiles_as_rows(extra).astype(BF16)
            return acc + jnp.dot(selector(start, nominal), rows, preferred_element_type=F32)

        moe = jnp.dot(sel, _load_tiles_as_rows(first).astype(BF16), preferred_element_type=F32)
        moe = lax.fori_loop(1, nwin, later_window, moe)
        finish([x1_ref[...] + moe])


def _combine(x1, ys, tokinfo, gain, tt, ws):
    n = x1.shape[0]
    n_slots = ys.shape[0] // ROW_TILE
    tile_off = jnp.concatenate([tokinfo[0, ::tt], jnp.full((1,), n_slots, I32)])
    return pl.pallas_call(
        functools.partial(_combine_body, tt=tt, ws=ws, n_slots=n_slots),
        grid_spec=pltpu.PrefetchScalarGridSpec(
            num_scalar_prefetch=1,
            grid=(n // tt,),
            in_specs=[pl.BlockSpec((tt, D_MODEL), lambda i, s: (i, 0)),
                      pl.BlockSpec((2, tt), lambda i, s: (0, i)),
                      pl.BlockSpec((1, D_MODEL), lambda i, s: (0, 0)),
                      pl.BlockSpec(memory_space=pl.ANY)],
            out_specs=pl.BlockSpec((tt, D_MODEL), lambda i, s: (i, 0)),
            scratch_shapes=[pltpu.VMEM((2, ws * ROW_TILE, LANES), F32),
                            pltpu.VMEM((ws * ROW_TILE, LANES), F32),
                            pltpu.SemaphoreType.DMA((3,))]),
        out_shape=jax.ShapeDtypeStruct((n, D_MODEL), F32),
        compiler_params=_params(("arbitrary",), VMEM_LIMIT),
        name="combine",
    )(tile_off, x1, tokinfo, gain, ys)


def _tile(n, pref):
    t = pref
    while n % t:
        t //= 2
    return t


def _trunk(x, p):
    b, s, d = x.shape
    n = b * s
    cap = CAPACITY_FACTOR * n // N_EXPERTS
    xf = x.reshape(n, d)
    tm = _tile(n, 512)
    zf, q, k, v = _inproj(xf, p["norm_mix"], p["w_in"], tm)
    f = _fourier(zf.reshape(b, s, FOURIER_WIDTH), p["w_fourier"])
    a = _attention(q.reshape(b, s, ATTN_WIDTH), k.reshape(b, s, KV_WIDTH),
                   v.reshape(b, s, KV_WIDTH), p["bias"], p["sink"], _tile(s, 512))
    x1, h2, aff_t = _outproj(xf, f.reshape(n, FOURIER_WIDTH), a.reshape(n, ATTN_WIDTH),
                             p["w_out"], p["norm_ffn"], p["w_router"], tm)
    idx_c, pos_c, gate_c, tokinfo = _route(aff_t, cap)
    ys = _ffn(idx_c, pos_c, gate_c, h2, p["w_gate"], p["w_up"], p["w_down"], _tile(cap // 2, 512))
    tt = _tile(n, 512)
    ws = min(CAPACITY_FACTOR * tt + tt // 4, N_EXPERTS * cap)
    y = _combine(x1, ys, tokinfo, p["norm_final"], tt, ws)
    return y.reshape(b, s, d)


def kernel(x_prompt, x_sample, w_in, w_fourier, attn_sink, rel_bias, w_out, norm_mix, norm_ffn,
           w_router, w_gate, w_up, w_down, norm_final):
    assert w_in.shape[0] == 1
    wr = w_router[0].T
    wr_hi = wr.astype(BF16)
    wr_lo = (wr - wr_hi.astype(F32)).astype(BF16)
    p = dict(
        norm_mix=norm_mix[0].reshape(1, D_MODEL),
        norm_ffn=norm_ffn[0].reshape(1, D_MODEL),
        norm_final=norm_final.reshape(1, D_MODEL),
        w_in=w_in[0].astype(BF16),
        w_fourier=w_fourier[0].astype(BF16),
        w_out=w_out[0].astype(BF16),
        w_router=jnp.concatenate([wr_hi, wr_lo], axis=0),
        w_gate=w_gate[0], w_up=w_up[0], w_down=w_down[0],
        sink=attn_sink[0],
        bias=_bias_table(rel_bias),
    )
    return _trunk(x_prompt, p), _trunk(x_sample, p)
```

```python
import functools
import math

import numpy as np
import jax
import jax.numpy as jnp
from jax import lax
from jax.experimental import pallas as pl
from jax.experimental.pallas import tpu as pltpu

F32 = jnp.float32
BF16 = jnp.bfloat16
I32 = jnp.int32

D_MODEL = 1024
FOURIER_GROUPS = 4
GROUP_DIM = 128
FOURIER_WIDTH = FOURIER_GROUPS * GROUP_DIM
N_HEADS = 8
N_KV_HEADS = 2
HEAD_DIM = 64
ATTN_WIDTH = N_HEADS * HEAD_DIM
KV_WIDTH = N_KV_HEADS * HEAD_DIM
WINDOW = 128
BLOCK = 128
N_BUCKETS = 32
MAX_DISTANCE = 128
IN_WIDTH = FOURIER_WIDTH + ATTN_WIDTH + 2 * KV_WIDTH
N_EXPERTS = 16
CAPACITY_FACTOR = 2
D_EXPERT = 1024
EPS = 1e-6

LANES = 128
PREFIX_CHUNK = 256
STRIP = 32
FFN_CHUNK = 256
VMEM_LIMIT = 56 * 1024 * 1024


U32 = jnp.uint32
HALF = D_MODEL // 2
ROW_TILE = HALF // LANES
HIGH_HALF = np.uint32(0xFFFF0000)


def _params(sem, vmem=None):
    return pltpu.CompilerParams(dimension_semantics=sem, vmem_limit_bytes=vmem)


def _store_rows_as_tiles(ref, val):
    m = val.shape[0]
    bits = lax.bitcast_convert_type(val.astype(BF16).astype(F32), U32)
    words = (bits[:, :HALF] >> 16) | (bits[:, HALF:] & HIGH_HALF)
    for c in range(ROW_TILE):
        ref[pl.ds(c, m, stride=ROW_TILE), :] = words[:, c * LANES:(c + 1) * LANES]


def _unpack_words(words, high):
    bits = (words & HIGH_HALF) if high else (words << 16)
    return lax.bitcast_convert_type(bits, F32).astype(BF16)


def _load_tiles_as_rows(ref):
    m = ref.shape[0] // ROW_TILE
    words = jnp.concatenate(
        [ref[pl.ds(c, m, stride=ROW_TILE), :] for c in range(ROW_TILE)], axis=1)
    return jnp.concatenate([_unpack_words(words, False), _unpack_words(words, True)], axis=1)


def _tiles_of_rows(ref, r, count):
    return ref.at[pl.ds(pl.multiple_of(r * ROW_TILE, ROW_TILE), count * ROW_TILE), :]


def _tile_of_row(ref, r):
    start = r * ROW_TILE
    if not isinstance(r, int):
        start = pl.multiple_of(start, ROW_TILE)
    return ref.at[pl.ds(start, ROW_TILE), :]


def _inproj_body(x_ref, g_ref, w_ref, zf_ref, q_ref, k_ref, v_ref):
    x = x_ref[...]
    ms = jnp.mean(x * x, axis=-1, keepdims=True)
    h = (x * lax.rsqrt(ms + EPS)) * g_ref[...]
    z = jnp.dot(h.astype(BF16), w_ref[...], preferred_element_type=F32)
    o = FOURIER_WIDTH
    zf_ref[...] = z[:, :o].astype(BF16)
    q_ref[...] = (z[:, o:o + ATTN_WIDTH] * (HEAD_DIM ** -0.5)).astype(BF16)
    o += ATTN_WIDTH
    k_ref[...] = z[:, o:o + KV_WIDTH].astype(BF16)
    o += KV_WIDTH
    v_ref[...] = z[:, o:o + KV_WIDTH].astype(BF16)


def _inproj(xf, gain, w_in, tm):
    n = xf.shape[0]
    row = lambda w: pl.BlockSpec((tm, w), lambda i: (i, 0))
    return pl.pallas_call(
        _inproj_body,
        grid=(n // tm,),
        in_specs=[row(D_MODEL),
                  pl.BlockSpec((1, D_MODEL), lambda i: (0, 0)),
                  pl.BlockSpec((D_MODEL, IN_WIDTH), lambda i: (0, 0))],
        out_specs=[row(FOURIER_WIDTH), row(ATTN_WIDTH), row(KV_WIDTH), row(KV_WIDTH)],
        out_shape=[jax.ShapeDtypeStruct((n, FOURIER_WIDTH), BF16),
                   jax.ShapeDtypeStruct((n, ATTN_WIDTH), BF16),
                   jax.ShapeDtypeStruct((n, KV_WIDTH), BF16),
                   jax.ShapeDtypeStruct((n, KV_WIDTH), BF16)],
        compiler_params=_params(("parallel",), VMEM_LIMIT),
        name="inproj",
    )(xf, gain, w_in)


def _dft_tables(s):
    s2 = BLOCK
    s1 = s // s2
    a = np.arange(s1, dtype=np.float64)
    ang1 = 2.0 * np.pi * np.outer(a, a) / s1
    f1 = np.concatenate([np.cos(ang1), -np.sin(ang1)], axis=0)
    b = np.arange(s2, dtype=np.float64)
    ang2 = 2.0 * np.pi * np.outer(b, b) / s2
    angt = 2.0 * np.pi * np.outer(a, b) / s
    c = np.arange(GROUP_DIM, dtype=np.float64)
    angc = 2.0 * np.pi * np.outer(c, c) / GROUP_DIM
    cs = np.concatenate([np.cos(angc), np.sin(angc)], axis=0)
    return dict(
        f1=jnp.asarray(f1, BF16),
        f2r=jnp.asarray(np.cos(ang2), F32), f2i=jnp.asarray(-np.sin(ang2), F32),
        twr=jnp.asarray(np.cos(angt), F32), twi=jnp.asarray(-np.sin(angt), F32),
        cs=jnp.asarray(cs, BF16))


def _dft1_body(f_ref, x_ref, o_ref):
    s1, cb = x_ref.shape[1], x_ref.shape[2]
    sub = min(cb, 2048)
    for c in range(0, cb, sub):
        res = jnp.dot(f_ref[...], x_ref[0, :, c:c + sub], preferred_element_type=F32)
        o_ref[0, 0, :, c:c + sub] = res[:s1].astype(BF16)
        o_ref[0, 1, :, c:c + sub] = res[s1:].astype(BF16)


def _dft1(zf3, f1, cb):
    b, s1, w = zf3.shape
    return pl.pallas_call(
        _dft1_body,
        grid=(b, w // cb),
        in_specs=[pl.BlockSpec((2 * s1, s1), lambda i, j: (0, 0)),
                  pl.BlockSpec((1, s1, cb), lambda i, j: (i, 0, j))],
        out_specs=pl.BlockSpec((1, 2, s1, cb), lambda i, j: (i, 0, 0, j)),
        out_shape=jax.ShapeDtypeStruct((b, 2, s1, w), BF16),
        compiler_params=_params(("parallel", "parallel"), VMEM_LIMIT),
        name="dft1",
    )(f1, zf3)


def _dft2_body(a_ref, f2r_ref, f2i_ref, twr_ref, twi_ref, cs_ref, wf_ref, o_ref, stage,
               *, kb, scale):
    s2 = BLOCK
    fr = f2r_ref[...]
    fi = f2i_ref[...]
    ys = []
    for j in range(kb):
        tr = twr_ref[0, j:j + 1, :]
        ti = twi_ref[0, j:j + 1, :]
        gr = fr * tr - fi * ti
        gi = fr * ti + fi * tr
        lhs = jnp.concatenate(
            [jnp.concatenate([gr, -gi], axis=1), jnp.concatenate([gi, gr], axis=1)],
            axis=0).astype(BF16)
        rhs = jnp.concatenate([a_ref[0, 0, j], a_ref[0, 1, j]], axis=0)
        ys.append(jnp.dot(lhs, rhs, preferred_element_type=F32))
    cat = jnp.concatenate(
        [jnp.concatenate([y[:s2, g * GROUP_DIM:(g + 1) * GROUP_DIM],
                          y[s2:, g * GROUP_DIM:(g + 1) * GROUP_DIM]], axis=1)
         for g in range(FOURIER_GROUPS) for y in ys], axis=0).astype(BF16)
    re = (jnp.dot(cat, cs_ref[...], preferred_element_type=F32) * scale).astype(BF16)
    rows = kb * s2
    outs = [jnp.dot(re[g * rows:(g + 1) * rows], wf_ref[g], preferred_element_type=F32)
            for g in range(FOURIER_GROUPS)]
    for j in range(kb):
        stage[:, j, :] = jnp.concatenate(
            [o[j * s2:(j + 1) * s2] for o in outs], axis=1)
    o_ref[0] = stage[...].astype(BF16)


def _dft2(a5, tabs, wf, kb, scale):
    b, _, s1, s2, w = a5.shape
    const = lambda shape: pl.BlockSpec(shape, lambda i, j: (0,) * len(shape))
    twr = tabs["twr"].reshape(s1 // kb, kb, s2)
    twi = tabs["twi"].reshape(s1 // kb, kb, s2)
    return pl.pallas_call(
        functools.partial(_dft2_body, kb=kb, scale=scale),
        grid=(b, s1 // kb),
        in_specs=[pl.BlockSpec((1, 2, kb, s2, w), lambda i, j: (i, 0, j, 0, 0)),
                  const((s2, s2)), const((s2, s2)),
                  pl.BlockSpec((1, kb, s2), lambda i, j: (j, 0, 0)),
                  pl.BlockSpec((1, kb, s2), lambda i, j: (j, 0, 0)),
                  const((2 * GROUP_DIM, GROUP_DIM)),
                  const((FOURIER_GROUPS, GROUP_DIM, GROUP_DIM))],
        out_specs=pl.BlockSpec((1, s2, kb, w), lambda i, j: (i, 0, j, 0)),
        out_shape=jax.ShapeDtypeStruct((b, s2, s1, w), BF16),
        scratch_shapes=[pltpu.VMEM((s2, kb, w), F32)],
        compiler_params=_params(("parallel", "parallel"), VMEM_LIMIT),
        name="dft2",
    )(a5, tabs["f2r"], tabs["f2i"], twr, twi, tabs["cs"], wf)


def _fourier(zf, wf_bf16):
    b, s, w = zf.shape
    s2 = BLOCK
    s1 = s // s2
    tabs = _dft_tables(s)
    a = _dft1(zf.reshape(b, s1, s2 * w), tabs["f1"], cb=8192)
    f = _dft2(a.reshape(b, 2, s1, s2, w), tabs, wf_bf16, kb=16,
              scale=1.0 / math.sqrt(s * GROUP_DIM))
    return f.reshape(b, s, w)


def _bucket_table():
    qi = np.arange(BLOCK)[:, None]
    kj = np.arange(3 * BLOCK)[None, :]
    rel = kj - BLOCK - qi
    nb = N_BUCKETS // 2
    max_exact = nb // 2
    n = np.abs(rel)
    large = max_exact + np.floor(
        np.log(np.maximum(n, 1).astype(np.float64) / max_exact)
        / math.log(MAX_DISTANCE / max_exact) * (nb - max_exact) + 1e-9).astype(np.int64)
    large = np.minimum(large, nb - 1)
    bucket = np.where(rel > 0, nb, 0) + np.where(n < max_exact, n, large)
    return np.where(n <= WINDOW, bucket, -1).astype(np.int32)


def _bias_body(rb_ref, bucket_ref, o_ref):
    bk = bucket_ref[...]
    col = lax.broadcasted_iota(I32, bk.shape, 1)
    for h in range(N_HEADS):
        acc = jnp.full(bk.shape, -jnp.inf, F32)
        for b in range(N_BUCKETS):
            acc = jnp.where(bk == b, rb_ref[b, h], acc)
        o_ref[0, h] = acc
        o_ref[1, h] = jnp.where(col >= BLOCK, acc, -jnp.inf)
        o_ref[2, h] = jnp.where(col < 2 * BLOCK, acc, -jnp.inf)


def _bias_table(rel_bias):
    bucket = jnp.asarray(_bucket_table())
    shape = (3, N_HEADS) + bucket.shape
    return pl.pallas_call(
        _bias_body,
        in_specs=[pl.BlockSpec(memory_space=pltpu.SMEM),
                  pl.BlockSpec(bucket.shape, lambda: (0, 0))],
        out_specs=pl.BlockSpec(shape, lambda: (0, 0, 0, 0)),
        out_shape=jax.ShapeDtypeStruct(shape, F32),
        name="bias_table",
    )(rel_bias.astype(F32), bucket)


def _attn_body(sink_ref, q_ref, kp_ref, kc_ref, kn_ref, vp_ref, vc_ref, vn_ref, bias_ref,
               o_ref, *, tq):
    j = pl.program_id(1)
    last = pl.num_programs(1) - 1
    nsb = tq // BLOCK
    group = N_HEADS // N_KV_HEADS
    gw = group * HEAD_DIM
    kext = jnp.concatenate([kp_ref[0, tq - BLOCK:, :], kc_ref[0], kn_ref[0, :BLOCK, :]], axis=0)
    vext = jnp.concatenate([vp_ref[0, tq - BLOCK:, :], vc_ref[0], vn_ref[0, :BLOCK, :]], axis=0)
    low_half = lax.broadcasted_iota(I32, (1, KV_WIDTH), 1) < HEAD_DIM

    def per_kv_head(ext):
        x = ext.astype(F32)
        xr = pltpu.roll(x, HEAD_DIM, axis=1)
        return (jnp.where(low_half, x, xr).astype(BF16), jnp.where(low_half, xr, x).astype(BF16))

    krep = per_kv_head(kext)
    vrep = per_kv_head(vext)
    head_of_lane = lax.broadcasted_iota(I32, (1, gw), 1) // HEAD_DIM
    nt = (((1,), (1,)), ((), ()))
    def scores(sb, kh):
        rows = slice(sb * BLOCK, (sb + 3) * BLOCK)
        variant = 0
        if sb == 0:
            variant = jnp.where(j == 0, 1, 0)
        elif sb == nsb - 1:
            variant = jnp.where(j == last, 2, 0)
        qblk = q_ref[0, sb * BLOCK:(sb + 1) * BLOCK, kh * gw:(kh + 1) * gw]
        qm = jnp.concatenate(
            [jnp.where(head_of_lane == g, qblk, jnp.zeros_like(qblk)) for g in range(group)],
            axis=0)
        kb = jnp.concatenate([krep[kh][rows], krep[kh][rows]], axis=1)
        s = lax.dot_general(qm, kb, nt, preferred_element_type=F32)
        return s, variant

    def weights(sv, kh):
        s_all, variant = sv
        out = []
        for g in range(group):
            for r0 in range(0, BLOCK, STRIP):
                s = s_all[g * BLOCK + r0:g * BLOCK + r0 + STRIP]
                s = s + bias_ref[variant, kh * group + g, r0:r0 + STRIP, :]
                sink = sink_ref[kh * group + g]
                m = jnp.maximum(jnp.max(s, axis=-1, keepdims=True), sink)
                p = jnp.exp(s - m)
                denom = jnp.sum(p, axis=-1, keepdims=True) + jnp.exp(sink - m)
                out.append((p / denom).astype(BF16))
        return jnp.concatenate(out, axis=0)

    def values(pn, sb, kh):
        rows = slice(sb * BLOCK, (sb + 3) * BLOCK)
        o4 = jnp.dot(pn, vrep[kh][rows], preferred_element_type=F32)
        out = []
        for pair in range(group // 2):
            even = o4[(2 * pair) * BLOCK:(2 * pair + 1) * BLOCK]
            odd = o4[(2 * pair + 1) * BLOCK:(2 * pair + 2) * BLOCK]
            out.append(jnp.where(low_half, even, odd))
        return out

    items = [(sb, kh) for sb in range(nsb) for kh in range(N_KV_HEADS)]
    pieces = {}
    s_next = scores(*items[0])
    pn_prev = None
    for i, (sb, kh) in enumerate(items):
        s_cur = s_next
        if i + 1 < len(items):
            s_next = scores(*items[i + 1])
        if pn_prev is not None:
            pieces[items[i - 1]] = values(pn_prev, *items[i - 1])
        pn_prev = weights(s_cur, kh)
    pieces[items[-1]] = values(pn_prev, *items[-1])
    for sb in range(nsb):
        row = [x for kh in range(N_KV_HEADS) for x in pieces[(sb, kh)]]
        o_ref[0, sb * BLOCK:(sb + 1) * BLOCK, :] = jnp.concatenate(row, axis=1).astype(BF16)


def _attention(q, k, v, bias, sink, tq):
    b, s, _ = q.shape
    nt = s // tq
    assert tq >= 2 * BLOCK
    kv = lambda f: pl.BlockSpec((1, tq, KV_WIDTH), f)
    prev = lambda i, j: (i, jnp.maximum(j - 1, 0), 0)
    cur = lambda i, j: (i, j, 0)
    nxt = lambda i, j: (i, jnp.minimum(j + 1, nt - 1), 0)
    return pl.pallas_call(
        functools.partial(_attn_body, tq=tq),
        grid=(b, nt),
        in_specs=[pl.BlockSpec(memory_space=pltpu.SMEM),
                  pl.BlockSpec((1, tq, ATTN_WIDTH), cur),
                  kv(prev), kv(cur), kv(nxt), kv(prev), kv(cur), kv(nxt),
                  pl.BlockSpec(bias.shape, lambda i, j: (0, 0, 0, 0))],
        out_specs=pl.BlockSpec((1, tq, ATTN_WIDTH), cur),
        out_shape=jax.ShapeDtypeStruct((b, s, ATTN_WIDTH), BF16),
        compiler_params=_params(("parallel", "parallel"), VMEM_LIMIT),
        name="attn",
    )(sink.astype(F32), q, k, k, k, v, v, v, bias)


def _outproj_body(x_ref, f_ref, a_ref, wo_ref, g_ref, wr_ref, x1_ref, h2_ref, aff_ref):
    mix = jnp.concatenate([f_ref[...], a_ref[...]], axis=1)
    x1 = x_ref[...] + jnp.dot(mix, wo_ref[...], preferred_element_type=F32)
    x1_ref[...] = x1
    ms = jnp.mean(x1 * x1, axis=-1, keepdims=True)
    h2 = (x1 * lax.rsqrt(ms + EPS)) * g_ref[...]
    _store_rows_as_tiles(h2_ref, h2)
    hi = h2.astype(BF16)
    lo = (h2 - hi.astype(F32)).astype(BF16)
    nt = (((1,), (1,)), ((), ()))
    o1 = lax.dot_general(wr_ref[...], hi, nt, preferred_element_type=F32)
    o2 = lax.dot_general(wr_ref[:N_EXPERTS], lo, nt, preferred_element_type=F32)
    logits = o1[:N_EXPERTS] + (o1[N_EXPERTS:] + o2)
    e = jnp.exp(logits - jnp.max(logits, axis=0, keepdims=True))
    aff_ref[...] = e / jnp.sum(e, axis=0, keepdims=True)


def _outproj(xf, f, a, w_out, gain, w_router_t, tm):
    n = xf.shape[0]
    row = lambda w: pl.BlockSpec((tm, w), lambda i: (i, 0))
    const = lambda shape: pl.BlockSpec(shape, lambda i: (0, 0))
    return pl.pallas_call(
        _outproj_body,
        grid=(n // tm,),
        in_specs=[row(D_MODEL), row(FOURIER_WIDTH), row(ATTN_WIDTH),
                  const((D_MODEL, D_MODEL)), const((1, D_MODEL)),
                  const((2 * N_EXPERTS, D_MODEL))],
        out_specs=[row(D_MODEL),
                   pl.BlockSpec((tm * ROW_TILE, LANES), lambda i: (i, 0)),
                   pl.BlockSpec((N_EXPERTS, tm), lambda i: (0, i))],
        out_shape=[jax.ShapeDtypeStruct((n, D_MODEL), F32),
                   jax.ShapeDtypeStruct((n * ROW_TILE, LANES), U32),
                   jax.ShapeDtypeStruct((N_EXPERTS, n), F32)],
        compiler_params=_params(("parallel",), VMEM_LIMIT),
        name="outproj",
    )(xf, f, a, w_out, gain, w_router_t)


def _route_body(aff_ref, idx_ref, pos_ref, gate_ref, tok_ref,
                d_a, d_b, n_a, n_b, p_a, p_b, g_a, g_b, *, n, cap):
    e = N_EXPERTS
    ch = PREFIX_CHUNK

    def search(i, tau):
        cand = tau | jnp.left_shift(jnp.int32(1), 30 - i)
        keys = lax.bitcast_convert_type(aff_ref[...], I32)
        cnt = jnp.sum((keys >= cand).astype(I32), axis=1, keepdims=True)
        return jnp.where(cnt >= cap, cand, tau)

    tau = lax.fori_loop(0, 31, search, jnp.zeros((e, 1), I32))
    keys = lax.bitcast_convert_type(aff_ref[...], I32)
    n_gt = jnp.sum((keys > tau).astype(I32), axis=1, keepdims=True)
    need = (cap - n_gt).astype(F32)

    r = lax.broadcasted_iota(I32, (ch, ch), 0)
    c = lax.broadcasted_iota(I32, (ch, ch), 1)
    upper = (r < c).astype(BF16)
    r16 = lax.broadcasted_iota(I32, (e, e), 0)
    c16 = lax.broadcasted_iota(I32, (e, e), 1)
    lower = (c16 < r16).astype(BF16)

    carry_eq = jnp.zeros((e, 1), F32)
    carry_m = jnp.zeros((2 * e, 1), F32)
    for ci in range(n // ch):
        sl = slice(ci * ch, (ci + 1) * ch)
        k = lax.bitcast_convert_type(aff_ref[:, sl], I32)
        gt = k > tau
        eq = (k == tau).astype(F32)
        eq_ex = jnp.dot(eq.astype(BF16), upper, preferred_element_type=F32) + carry_eq
        carry_eq = carry_eq + jnp.sum(eq, axis=1, keepdims=True)
        m = jnp.where(gt | ((eq > 0.0) & (eq_ex < need)), 1.0, 0.0).astype(F32)
        cnt = jnp.sum(m, axis=0, keepdims=True)
        stacked = jnp.concatenate([m, cnt, jnp.zeros((e - 1, ch), F32)], axis=0)
        pre = jnp.dot(stacked.astype(BF16), upper, preferred_element_type=F32) + carry_m
        carry_m = carry_m + jnp.sum(stacked, axis=1, keepdims=True)
        rank = pre[:e]
        tokoff = pre[e:e + 1]
        ex_e = jnp.dot(lower, m.astype(BF16), preferred_element_type=F32)
        tok_ref[0:1, sl] = tokoff.astype(I32)
        tok_ref[1:2, sl] = (tokoff + cnt).astype(I32)
        lane = lax.broadcasted_iota(I32, (e, ch), 1) + ci * ch
        d_a[:, sl] = jnp.where(m > 0.0, lane - rank.astype(I32), 0)
        n_a[:, sl] = lane
        p_a[:, sl] = (tokoff + ex_e).astype(I32)
    g_a[...] = aff_ref[...]

    bufs = [(d_a, n_a, p_a, g_a), (d_b, n_b, p_b, g_b)]
    for b in range(int(math.log2(n))):
        src, dst = bufs[b % 2], bufs[(b + 1) % 2]
        sh = 1 << b
        d = src[0][...]
        moving = (lax.shift_right_logical(d, b) & 1) == 1
        d_in = pltpu.roll(d, n - sh, axis=1)
        arrive = (lax.shift_right_logical(d_in, b) & 1) == 1
        dst[0][...] = jnp.where(arrive, d_in, jnp.where(moving, 0, d))
        for t in (1, 2, 3):
            v = src[t][...]
            dst[t][...] = jnp.where(arrive, pltpu.roll(v, n - sh, axis=1), v)
    fin = bufs[int(math.log2(n)) % 2]
    idx_ref[...] = fin[1][:, :cap]
    pos_ref[...] = fin[2][:, :cap]
    gate_ref[...] = fin[3][:, :cap]


def _route(aff_t, cap):
    e, n = aff_t.shape
    assert n & (n - 1) == 0 or True
    full = lambda shape: pl.BlockSpec(shape, lambda: (0, 0))
    big_i = pltpu.VMEM((e, n), I32)
    return pl.pallas_call(
        functools.partial(_route_body, n=n, cap=cap),
        in_specs=[full((e, n))],
        out_specs=[full((e, cap)), full((e, cap)), full((e, cap)), full((2, n))],
        out_shape=[jax.ShapeDtypeStruct((e, cap), I32),
                   jax.ShapeDtypeStruct((e, cap), I32),
                   jax.ShapeDtypeStruct((e, cap), F32),
                   jax.ShapeDtypeStruct((2, n), I32)],
        scratch_shapes=[big_i, big_i, big_i, big_i, big_i, big_i,
                        pltpu.VMEM((e, n), F32), pltpu.VMEM((e, n), F32)],
        compiler_params=_params(None, VMEM_LIMIT),
        name="route",
    )(aff_t)


def _rows_to_cols(rows):
    r, m = rows.shape
    if r < 8:
        rows = jnp.concatenate([rows, jnp.zeros((8 - r, m), rows.dtype)], axis=0)
    return rows.T[:, :r]


def _ffn_body(idx_cur, idx_nxt, pos_prv, pos_cur, gate_ref, h_hbm, wg_ref, wu_ref, wd_ref, ys_hbm,
              xa, xb, ya, yb, wg_s, wu_s, wd_s, sems, *, tm, spe):
    s = pl.program_id(0)
    last = pl.num_programs(0) - 1
    g_a, g_b, s_a, s_b = sems.at[0], sems.at[1], sems.at[2], sems.at[3]

    def gather(idx_ref, t, buf, sem):
        def one(i):
            pltpu.make_async_copy(_tile_of_row(h_hbm, idx_ref[t, 0, i]),
                                  _tile_of_row(buf, i), sem).start()
        return one

    def scatter(pos_ref, t, buf, sem):
        def one(i):
            pltpu.make_async_copy(_tile_of_row(buf, i),
                                  _tile_of_row(ys_hbm, pos_ref[t, 0, i]), sem).start()
        return one

    def wait_gather(buf, sem):
        pltpu.make_async_copy(h_hbm.at[pl.ds(0, tm * ROW_TILE), :], buf, sem).wait()

    def wait_scatter(buf, sem):
        pltpu.make_async_copy(buf, ys_hbm.at[pl.ds(0, tm * ROW_TILE), :], sem).wait()

    def compute(xbuf, ybuf, t, copies):
        nc = D_EXPERT // FFN_CHUNK
        stages = 3 * nc
        todo = [(fn, i) for fn in copies for i in range(tm)]
        per_stage = -(-len(todo) // stages)

        def issue_some():
            for fn, i in todo[:per_stage]:
                fn(i)
            del todo[:per_stage]

        x = _load_tiles_as_rows(xbuf)
        cols = lambda c: slice(c * FFN_CHUNK, (c + 1) * FFN_CHUNK)

        def up(c):
            g = jnp.dot(x, wg_s[:, cols(c)], preferred_element_type=F32)
            issue_some()
            u = jnp.dot(x, wu_s[:, cols(c)], preferred_element_type=F32)
            issue_some()
            return g, u

        def act(gu):
            g, u = gu
            return (g * jax.nn.sigmoid(g) * u).astype(BF16)

        def down(hid, c):
            part = jnp.dot(hid, wd_s[cols(c), :], preferred_element_type=F32)
            issue_some()
            return part

        gu_next = up(0)
        hid_prev = None
        y = None
        for c in range(nc):
            gu_cur = gu_next
            if c + 1 < nc:
                gu_next = up(c + 1)
            if hid_prev is not None:
                part = down(hid_prev, c - 1)
                y = part if y is None else y + part
            hid_prev = act(gu_cur)
        y = y + down(hid_prev, nc - 1)
        assert not todo
        y = y * _rows_to_cols(gate_ref[t])
        _store_rows_as_tiles(ybuf, y)

    @pl.when(s % spe == 0)
    def _():
        rows = 256
        for w_ref, w_s in ((wg_ref, wg_s), (wu_ref, wu_s), (wd_ref, wd_s)):
            for r in range(0, w_s.shape[0], rows):
                w_s[r:r + rows, :] = w_ref[0, r:r + rows, :].astype(BF16)

    @pl.when(s == 0)
    def _():
        pl.loop(0, tm)(gather(idx_cur, 0, xa, g_a))
        yb[...] = jnp.zeros_like(yb)

    wait_gather(xa, g_a)

    @pl.when(s > 0)
    def _():
        wait_scatter(ya, s_a)

    compute(xa, ya, 0, [gather(idx_cur, 1, xb, g_b), scatter(pos_prv, 1, yb, s_b)])

    wait_gather(xb, g_b)
    wait_scatter(yb, s_b)
    compute(xb, yb, 1, [gather(idx_nxt, 0, xa, g_a), scatter(pos_cur, 0, ya, s_a)])

    @pl.when(s == last)
    def _():
        pl.loop(0, tm)(scatter(pos_cur, 1, yb, s_b))
        wait_gather(xa, g_a)
        wait_scatter(ya, s_a)
        wait_scatter(yb, s_b)


def _ffn(idx_c, pos_c, gate_c, h2, wg, wu, wd, tm):
    e, cap = idx_c.shape
    spe = cap // (2 * tm)
    steps = e * spe
    n_slots = e * cap
    idx3 = idx_c.reshape(2 * steps, 1, tm)
    pos3 = pos_c.reshape(2 * steps, 1, tm)
    gate3 = gate_c.reshape(2 * steps, 1, tm)
    cur = lambda i: (i, 0, 0)
    prv = lambda i: (jnp.maximum(i - 1, 0), 0, 0)
    nxt = lambda i: (jnp.minimum(i + 1, steps - 1), 0, 0)
    smem = lambda f: pl.BlockSpec((2, 1, tm), f, memory_space=pltpu.SMEM)
    wspec = lambda a, b: pl.BlockSpec((1, a, b), lambda i: (i // spe, 0, 0))
    buf = pltpu.VMEM((tm * ROW_TILE, LANES), U32)
    return pl.pallas_call(
        functools.partial(_ffn_body, tm=tm, spe=spe),
        grid=(steps,),
        in_specs=[smem(cur), smem(nxt), smem(prv), smem(cur),
                  pl.BlockSpec((2, 1, tm), cur),
                  pl.BlockSpec(memory_space=pl.ANY),
                  wspec(D_MODEL, D_EXPERT), wspec(D_MODEL, D_EXPERT), wspec(D_EXPERT, D_MODEL)],
        out_specs=pl.BlockSpec(memory_space=pl.ANY),
        out_shape=jax.ShapeDtypeStruct((n_slots * ROW_TILE, LANES), U32),
        scratch_shapes=[buf, buf, buf, buf,
                        pltpu.VMEM((D_MODEL, D_EXPERT), BF16), pltpu.VMEM((D_MODEL, D_EXPERT), BF16),
                        pltpu.VMEM((D_EXPERT, D_MODEL), BF16), pltpu.SemaphoreType.DMA((4,))],
        compiler_params=_params(("arbitrary",), VMEM_LIMIT),
        name="ffn",
    )(idx3, idx3, pos3, pos3, gate3, h2, wg, wu, wd)


def _combine_body(toff_ref, x1_ref, tok_ref, g_ref, ys_hbm, o_ref, win, extra, sems,
                  *, tt, ws, n_slots):
    i = pl.program_id(0)
    last = pl.num_programs(0) - 1
    buf = i % 2

    def window_start(nominal):
        return pl.multiple_of(jnp.minimum(nominal, n_slots - ws), 8)

    def first_fetch(tile, b):
        start = window_start((toff_ref[tile] // 8) * 8)
        return pltpu.make_async_copy(_tiles_of_rows(ys_hbm, start, ws), win.at[b], sems.at[b])

    @pl.when(i == 0)
    def _():
        first_fetch(0, 0).start()

    first_fetch(i, buf).wait()

    @pl.when(i < last)
    def _():
        first_fetch(i + 1, 1 - buf).start()

    hi = toff_ref[i + 1]
    base = (toff_ref[i] // 8) * 8
    nwin = (hi - base + ws - 1) // ws
    tok = _rows_to_cols(tok_ref[...])
    t_lo = tok[:, 0:1]
    as_unsigned = lambda v: lax.bitcast_convert_type(v, jnp.uint32)
    t_cnt = as_unsigned(tok[:, 1:2] - t_lo)
    lane = lax.broadcasted_iota(I32, (tt, ws), 1)

    def selector(start, nominal):
        owned = as_unsigned(lane + (start - t_lo)) < t_cnt
        if nominal is not None:
            owned = owned & (lane >= nominal - start)
        return jnp.where(owned, 1.0, 0.0).astype(BF16)

    def finish(x2_cols):
        ssq = sum(jnp.sum(x * x, axis=-1, keepdims=True) for x in x2_cols)
        scale = lax.rsqrt(ssq * (1.0 / D_MODEL) + EPS)
        for c, x in enumerate(x2_cols):
            w = x.shape[1]
            o_ref[:, c * w:(c + 1) * w] = (x * scale) * g_ref[:, c * w:(c + 1) * w]

    first = win.at[buf]
    sel = selector(window_start(base), None)
    cw = 2 * LANES

    def window_cols(c):
        words = jnp.concatenate(
            [first[pl.ds(2 * (c % 2) + k, ws, stride=ROW_TILE), :] for k in range(2)], axis=1)
        return _unpack_words(words, c >= 2)

    x2_cols = []
    rows_next = window_cols(0)
    for c in range(D_MODEL // cw):
        rows_cur = rows_next
        if (c + 1) * cw < D_MODEL:
            rows_next = window_cols(c + 1)
        x2_cols.append(x1_ref[:, c * cw:(c + 1) * cw]
                       + jnp.dot(sel, rows_cur, preferred_element_type=F32))
    finish(x2_cols)

    @pl.when(nwin > 1)
    def _():
        def later_window(w, acc):
            nominal = base + w * ws
            start = window_start(nominal)
            cp = pltpu.make_async_copy(_tiles_of_rows(ys_hbm, start, ws), extra, sems.at[2])
            cp.start()
            cp.wait()
            rows = _load_tiles_as_rows(extra)
            return acc + jnp.dot(selector(start, nominal), rows, preferred_element_type=F32)

        moe = jnp.dot(sel, _load_tiles_as_rows(first), preferred_element_type=F32)
        moe = lax.fori_loop(1, nwin, later_window, moe)
        finish([x1_ref[...] + moe])


def _combine(x1, ys, tokinfo, gain, tt, ws):
    n = x1.shape[0]
    n_slots = ys.shape[0] // ROW_TILE
    tile_off = jnp.concatenate([tokinfo[0, ::tt], jnp.full((1,), n_slots, I32)])
    return pl.pallas_call(
        functools.partial(_combine_body, tt=tt, ws=ws, n_slots=n_slots),
        grid_spec=pltpu.PrefetchScalarGridSpec(
            num_scalar_prefetch=1,
            grid=(n // tt,),
            in_specs=[pl.BlockSpec((tt, D_MODEL), lambda i, s: (i, 0)),
                      pl.BlockSpec((2, tt), lambda i, s: (0, i)),
                      pl.BlockSpec((1, D_MODEL), lambda i, s: (0, 0)),
                      pl.BlockSpec(memory_space=pl.ANY)],
            out_specs=pl.BlockSpec((tt, D_MODEL), lambda i, s: (i, 0)),
            scratch_shapes=[pltpu.VMEM((2, ws * ROW_TILE, LANES), U32),
                            pltpu.VMEM((ws * ROW_TILE, LANES), U32),
                            pltpu.SemaphoreType.DMA((3,))]),
        out_shape=jax.ShapeDtypeStruct((n, D_MODEL), F32),
        compiler_params=_params(("arbitrary",), VMEM_LIMIT),
        name="combine",
    )(tile_off, x1, tokinfo, gain, ys)


def _tile(n, pref):
    t = pref
    while n % t:
        t //= 2
    return t


def _trunk(x, p):
    b, s, d = x.shape
    n = b * s
    cap = CAPACITY_FACTOR * n // N_EXPERTS
    xf = x.reshape(n, d)
    tm = _tile(n, 512)
    zf, q, k, v = _inproj(xf, p["norm_mix"], p["w_in"], tm)
    f = _fourier(zf.reshape(b, s, FOURIER_WIDTH), p["w_fourier"])
    a = _attention(q.reshape(b, s, ATTN_WIDTH), k.reshape(b, s, KV_WIDTH),
                   v.reshape(b, s, KV_WIDTH), p["bias"], p["sink"], _tile(s, 512))
    x1, h2, aff_t = _outproj(xf, f.reshape(n, FOURIER_WIDTH), a.reshape(n, ATTN_WIDTH),
                             p["w_out"], p["norm_ffn"], p["w_router"], tm)
    idx_c, pos_c, gate_c, tokinfo = _route(aff_t, cap)
    ys = _ffn(idx_c, pos_c, gate_c, h2, p["w_gate"], p["w_up"], p["w_down"], _tile(cap // 2, 512))
    tt = _tile(n, 512)
    ws = min(CAPACITY_FACTOR * tt + tt // 4, N_EXPERTS * cap)
    y = _combine(x1, ys, tokinfo, p["norm_final"], tt, ws)
    return y.reshape(b, s, d)


def kernel(x_prompt, x_sample, w_in, w_fourier, attn_sink, rel_bias, w_out, norm_mix, norm_ffn,
           w_router, w_gate, w_up, w_down, norm_final):
    assert w_in.shape[0] == 1
    wr = w_router[0].T
    wr_hi = wr.astype(BF16)
    wr_lo = (wr - wr_hi.astype(F32)).astype(BF16)
    p = dict(
        norm_mix=norm_mix[0].reshape(1, D_MODEL),
        norm_ffn=norm_ffn[0].reshape(1, D_MODEL),
        norm_final=norm_final.reshape(1, D_MODEL),
        w_in=w_in[0].astype(BF16),
        w_fourier=w_fourier[0].astype(BF16),
        w_out=w_out[0].astype(BF16),
        w_router=jnp.concatenate([wr_hi, wr_lo], axis=0),
        w_gate=w_gate[0], w_up=w_up[0], w_down=w_down[0],
        sink=attn_sink[0],
        bias=_bias_table(rel_bias),
    )
    return _trunk(x_prompt, p), _trunk(x_sample, p)
```

```python
import functools
import math

import numpy as np
import jax
import jax.numpy as jnp
from jax import lax
from jax.experimental import pallas as pl
from jax.experimental.pallas import tpu as pltpu

F32 = jnp.float32
BF16 = jnp.bfloat16
I32 = jnp.int32

D_MODEL = 1024
FOURIER_GROUPS = 4
GROUP_DIM = 128
FOURIER_WIDTH = FOURIER_GROUPS * GROUP_DIM
N_HEADS = 8
N_KV_HEADS = 2
HEAD_DIM = 64
ATTN_WIDTH = N_HEADS * HEAD_DIM
KV_WIDTH = N_KV_HEADS * HEAD_DIM
WINDOW = 128
BLOCK = 128
N_BUCKETS = 32
MAX_DISTANCE = 128
IN_WIDTH = FOURIER_WIDTH + ATTN_WIDTH + 2 * KV_WIDTH
N_EXPERTS = 16
CAPACITY_FACTOR = 2
D_EXPERT = 1024
EPS = 1e-6

LANES = 128
PREFIX_CHUNK = 256
STRIP = 32
FFN_CHUNK = 256
VMEM_LIMIT = 56 * 1024 * 1024


U32 = jnp.uint32
HALF = D_MODEL // 2
ROW_TILE = HALF // LANES
HIGH_HALF = np.uint32(0xFFFF0000)


def _params(sem, vmem=None):
    return pltpu.CompilerParams(dimension_semantics=sem, vmem_limit_bytes=vmem)


def _store_rows_as_tiles(ref, val):
    m = val.shape[0]
    bits = lax.bitcast_convert_type(val.astype(BF16).astype(F32), U32)
    words = (bits[:, :HALF] >> 16) | (bits[:, HALF:] & HIGH_HALF)
    for c in range(ROW_TILE):
        ref[pl.ds(c, m, stride=ROW_TILE), :] = words[:, c * LANES:(c + 1) * LANES]


def _unpack_words(words, high):
    bits = (words & HIGH_HALF) if high else (words << 16)
    return lax.bitcast_convert_type(bits, F32).astype(BF16)


def _load_tiles_as_rows(ref):
    m = ref.shape[0] // ROW_TILE
    words = jnp.concatenate(
        [ref[pl.ds(c, m, stride=ROW_TILE), :] for c in range(ROW_TILE)], axis=1)
    return jnp.concatenate([_unpack_words(words, False), _unpack_words(words, True)], axis=1)


def _tiles_of_rows(ref, r, count):
    return ref.at[pl.ds(pl.multiple_of(r * ROW_TILE, ROW_TILE), count * ROW_TILE), :]


def _tile_of_row(ref, r):
    start = r * ROW_TILE
    if not isinstance(r, int):
        start = pl.multiple_of(start, ROW_TILE)
    return ref.at[pl.ds(start, ROW_TILE), :]


def _inproj_body(x_ref, g_ref, w_ref, zf_ref, q_ref, k_ref, v_ref):
    x = x_ref[...]
    ms = jnp.mean(x * x, axis=-1, keepdims=True)
    h = (x * lax.rsqrt(ms + EPS)) * g_ref[...]
    z = jnp.dot(h.astype(BF16), w_ref[...], preferred_element_type=F32)
    o = FOURIER_WIDTH
    zf_ref[...] = z[:, :o].astype(BF16)
    q_ref[...] = (z[:, o:o + ATTN_WIDTH] * (HEAD_DIM ** -0.5)).astype(BF16)
    o += ATTN_WIDTH
    k_ref[...] = z[:, o:o + KV_WIDTH].astype(BF16)
    o += KV_WIDTH
    v_ref[...] = z[:, o:o + KV_WIDTH].astype(BF16)


def _inproj(xf, gain, w_in, tm):
    n = xf.shape[0]
    row = lambda w: pl.BlockSpec((tm, w), lambda i: (i, 0))
    return pl.pallas_call(
        _inproj_body,
        grid=(n // tm,),
        in_specs=[row(D_MODEL),
                  pl.BlockSpec((1, D_MODEL), lambda i: (0, 0)),
                  pl.BlockSpec((D_MODEL, IN_WIDTH), lambda i: (0, 0))],
        out_specs=[row(FOURIER_WIDTH), row(ATTN_WIDTH), row(KV_WIDTH), row(KV_WIDTH)],
        out_shape=[jax.ShapeDtypeStruct((n, FOURIER_WIDTH), BF16),
                   jax.ShapeDtypeStruct((n, ATTN_WIDTH), BF16),
                   jax.ShapeDtypeStruct((n, KV_WIDTH), BF16),
                   jax.ShapeDtypeStruct((n, KV_WIDTH), BF16)],
        compiler_params=_params(("parallel",), VMEM_LIMIT),
        name="inproj",
    )(xf, gain, w_in)


def _dft_tables(s):
    s2 = BLOCK
    s1 = s // s2
    a = np.arange(s1, dtype=np.float64)
    ang1 = 2.0 * np.pi * np.outer(a, a) / s1
    f1 = np.concatenate([np.cos(ang1), -np.sin(ang1)], axis=0)
    b = np.arange(s2, dtype=np.float64)
    ang2 = 2.0 * np.pi * np.outer(b, b) / s2
    angt = 2.0 * np.pi * np.outer(a, b) / s
    c = np.arange(GROUP_DIM, dtype=np.float64)
    angc = 2.0 * np.pi * np.outer(c, c) / GROUP_DIM
    cs = np.concatenate([np.cos(angc), np.sin(angc)], axis=0)
    return dict(
        f1=jnp.asarray(f1, BF16),
        f2r=jnp.asarray(np.cos(ang2), F32), f2i=jnp.asarray(-np.sin(ang2), F32),
        twr=jnp.asarray(np.cos(angt), F32), twi=jnp.asarray(-np.sin(angt), F32),
        cs=jnp.asarray(cs, BF16))


def _dft1_body(f_ref, x_ref, o_ref, stage):
    s1, cb = x_ref.shape[1], x_ref.shape[2]
    w = FOURIER_WIDTH
    sub = min(cb, 2048)
    for c in range(0, cb, sub):
        res = jnp.dot(f_ref[...], x_ref[0, :, c:c + sub], preferred_element_type=F32)
        for jj in range(sub // w):
            j = c // w + jj
            stage[0, :, j, :] = res[:s1, jj * w:(jj + 1) * w]
            stage[1, :, j, :] = res[s1:, jj * w:(jj + 1) * w]
    o_ref[0] = stage[...].astype(BF16)


def _dft1(zf3, f1, nb):
    b, s1, lanes = zf3.shape
    w = FOURIER_WIDTH
    s2 = lanes // w
    return pl.pallas_call(
        _dft1_body,
        grid=(b, s2 // nb),
        in_specs=[pl.BlockSpec((2 * s1, s1), lambda i, j: (0, 0)),
                  pl.BlockSpec((1, s1, nb * w), lambda i, j: (i, 0, j))],
        out_specs=pl.BlockSpec((1, 2, s1, nb, w), lambda i, j: (i, 0, 0, j, 0)),
        out_shape=jax.ShapeDtypeStruct((b, 2, s1, s2, w), BF16),
        scratch_shapes=[pltpu.VMEM((2, s1, nb, w), F32)],
        compiler_params=_params(("parallel", "parallel"), VMEM_LIMIT),
        name="dft1",
    )(f1, zf3)


def _dft2_body(a_ref, f2r_ref, f2i_ref, twr_ref, twi_ref, cs_ref, wf_ref, o_ref, stage,
               *, kb, scale):
    s2 = BLOCK
    fr = f2r_ref[...]
    fi = f2i_ref[...]
    ys = []
    for j in range(kb):
        tr = twr_ref[0, j:j + 1, :]
        ti = twi_ref[0, j:j + 1, :]
        gr = fr * tr - fi * ti
        gi = fr * ti + fi * tr
        lhs = jnp.concatenate(
            [jnp.concatenate([gr, -gi], axis=1), jnp.concatenate([gi, gr], axis=1)],
            axis=0).astype(BF16)
        rhs = jnp.concatenate([a_ref[0, 0, j], a_ref[0, 1, j]], axis=0)
        ys.append(jnp.dot(lhs, rhs, preferred_element_type=F32))
    cat = jnp.concatenate(
        [jnp.concatenate([y[:s2, g * GROUP_DIM:(g + 1) * GROUP_DIM],
                          y[s2:, g * GROUP_DIM:(g + 1) * GROUP_DIM]], axis=1)
         for g in range(FOURIER_GROUPS) for y in ys], axis=0).astype(BF16)
    re = (jnp.dot(cat, cs_ref[...], preferred_element_type=F32) * scale).astype(BF16)
    rows = kb * s2
    outs = [jnp.dot(re[g * rows:(g + 1) * rows], wf_ref[g], preferred_element_type=F32)
            for g in range(FOURIER_GROUPS)]
    for j in range(kb):
        stage[:, j, :] = jnp.concatenate(
            [o[j * s2:(j + 1) * s2] for o in outs], axis=1)
    o_ref[0] = stage[...].astype(BF16)


def _dft2(a5, tabs, wf, kb, scale):
    b, _, s1, s2, w = a5.shape
    const = lambda shape: pl.BlockSpec(shape, lambda i, j: (0,) * len(shape))
    twr = tabs["twr"].reshape(s1 // kb, kb, s2)
    twi = tabs["twi"].reshape(s1 // kb, kb, s2)
    return pl.pallas_call(
        functools.partial(_dft2_body, kb=kb, scale=scale),
        grid=(b, s1 // kb),
        in_specs=[pl.BlockSpec((1, 2, kb, s2, w), lambda i, j: (i, 0, j, 0, 0)),
                  const((s2, s2)), const((s2, s2)),
                  pl.BlockSpec((1, kb, s2), lambda i, j: (j, 0, 0)),
                  pl.BlockSpec((1, kb, s2), lambda i, j: (j, 0, 0)),
                  const((2 * GROUP_DIM, GROUP_DIM)),
                  const((FOURIER_GROUPS, GROUP_DIM, GROUP_DIM))],
        out_specs=pl.BlockSpec((1, s2, kb, w), lambda i, j: (i, 0, j, 0)),
        out_shape=jax.ShapeDtypeStruct((b, s2, s1, w), BF16),
        scratch_shapes=[pltpu.VMEM((s2, kb, w), F32)],
        compiler_params=_params(("parallel", "parallel"), VMEM_LIMIT),
        name="dft2",
    )(a5, tabs["f2r"], tabs["f2i"], twr, twi, tabs["cs"], wf)


def _fourier(zf, wf_bf16):
    b, s, w = zf.shape
    s2 = BLOCK
    s1 = s // s2
    tabs = _dft_tables(s)
    a = _dft1(zf.reshape(b, s1, s2 * w), tabs["f1"], nb=16)
    f = _dft2(a, tabs, wf_bf16, kb=16,
              scale=1.0 / math.sqrt(s * GROUP_DIM))
    return f.reshape(b, s, w)


def _bucket_table():
    qi = np.arange(BLOCK)[:, None]
    kj = np.arange(3 * BLOCK)[None, :]
    rel = kj - BLOCK - qi
    nb = N_BUCKETS // 2
    max_exact = nb // 2
    n = np.abs(rel)
    large = max_exact + np.floor(
        np.log(np.maximum(n, 1).astype(np.float64) / max_exact)
        / math.log(MAX_DISTANCE / max_exact) * (nb - max_exact) + 1e-9).astype(np.int64)
    large = np.minimum(large, nb - 1)
    bucket = np.where(rel > 0, nb, 0) + np.where(n < max_exact, n, large)
    return np.where(n <= WINDOW, bucket, -1).astype(np.int32)


def _bias_body(rb_ref, bucket_ref, o_ref):
    bk = bucket_ref[...]
    col = lax.broadcasted_iota(I32, bk.shape, 1)
    for h in range(N_HEADS):
        acc = jnp.full(bk.shape, -jnp.inf, F32)
        for b in range(N_BUCKETS):
            acc = jnp.where(bk == b, rb_ref[b, h], acc)
        o_ref[0, h] = acc
        o_ref[1, h] = jnp.where(col >= BLOCK, acc, -jnp.inf)
        o_ref[2, h] = jnp.where(col < 2 * BLOCK, acc, -jnp.inf)


def _bias_table(rel_bias):
    bucket = jnp.asarray(_bucket_table())
    shape = (3, N_HEADS) + bucket.shape
    return pl.pallas_call(
        _bias_body,
        in_specs=[pl.BlockSpec(memory_space=pltpu.SMEM),
                  pl.BlockSpec(bucket.shape, lambda: (0, 0))],
        out_specs=pl.BlockSpec(shape, lambda: (0, 0, 0, 0)),
        out_shape=jax.ShapeDtypeStruct(shape, F32),
        name="bias_table",
    )(rel_bias.astype(F32), bucket)


def _attn_body(sink_ref, q_ref, kp_ref, kc_ref, kn_ref, vp_ref, vc_ref, vn_ref, bias_ref,
               o_ref, *, tq):
    j = pl.program_id(1)
    last = pl.num_programs(1) - 1
    nsb = tq // BLOCK
    group = N_HEADS // N_KV_HEADS
    gw = group * HEAD_DIM
    kext = jnp.concatenate([kp_ref[0, tq - BLOCK:, :], kc_ref[0], kn_ref[0, :BLOCK, :]], axis=0)
    vext = jnp.concatenate([vp_ref[0, tq - BLOCK:, :], vc_ref[0], vn_ref[0, :BLOCK, :]], axis=0)
    low_half = lax.broadcasted_iota(I32, (1, KV_WIDTH), 1) < HEAD_DIM

    def per_kv_head(ext):
        x = ext.astype(F32)
        xr = pltpu.roll(x, HEAD_DIM, axis=1)
        return (jnp.where(low_half, x, xr).astype(BF16), jnp.where(low_half, xr, x).astype(BF16))

    krep = per_kv_head(kext)
    vrep = per_kv_head(vext)
    head_of_lane = lax.broadcasted_iota(I32, (1, gw), 1) // HEAD_DIM
    nt = (((1,), (1,)), ((), ()))
    def scores(sb, kh):
        rows = slice(sb * BLOCK, (sb + 3) * BLOCK)
        variant = 0
        if sb == 0:
            variant = jnp.where(j == 0, 1, 0)
        elif sb == nsb - 1:
            variant = jnp.where(j == last, 2, 0)
        qblk = q_ref[0, sb * BLOCK:(sb + 1) * BLOCK, kh * gw:(kh + 1) * gw]
        qm = jnp.concatenate(
            [jnp.where(head_of_lane == g, qblk, jnp.zeros_like(qblk)) for g in range(group)],
            axis=0)
        kb = jnp.concatenate([krep[kh][rows], krep[kh][rows]], axis=1)
        s = lax.dot_general(qm, kb, nt, preferred_element_type=F32)
        return s, variant

    def weights(sv, kh):
        s_all, variant = sv
        out = []
        for g in range(group):
            for r0 in range(0, BLOCK, STRIP):
                s = s_all[g * BLOCK + r0:g * BLOCK + r0 + STRIP]
                s = s + bias_ref[variant, kh * group + g, r0:r0 + STRIP, :]
                sink = sink_ref[kh * group + g]
                m = jnp.maximum(jnp.max(s, axis=-1, keepdims=True), sink)
                p = jnp.exp(s - m)
                denom = jnp.sum(p, axis=-1, keepdims=True) + jnp.exp(sink - m)
                out.append((p / denom).astype(BF16))
        return jnp.concatenate(out, axis=0)

    def values(pn, sb, kh):
        rows = slice(sb * BLOCK, (sb + 3) * BLOCK)
        o4 = jnp.dot(pn, vrep[kh][rows], preferred_element_type=F32)
        out = []
        for pair in range(group // 2):
            even = o4[(2 * pair) * BLOCK:(2 * pair + 1) * BLOCK]
            odd = o4[(2 * pair + 1) * BLOCK:(2 * pair + 2) * BLOCK]
            out.append(jnp.where(low_half, even, odd))
        return out

    items = [(sb, kh) for sb in range(nsb) for kh in range(N_KV_HEADS)]
    pieces = {}
    s_next = scores(*items[0])
    pn_prev = None
    for i, (sb, kh) in enumerate(items):
        s_cur = s_next
        if i + 1 < len(items):
            s_next = scores(*items[i + 1])
        if pn_prev is not None:
            pieces[items[i - 1]] = values(pn_prev, *items[i - 1])
        pn_prev = weights(s_cur, kh)
    pieces[items[-1]] = values(pn_prev, *items[-1])
    for sb in range(nsb):
        row = [x for kh in range(N_KV_HEADS) for x in pieces[(sb, kh)]]
        o_ref[0, sb * BLOCK:(sb + 1) * BLOCK, :] = jnp.concatenate(row, axis=1).astype(BF16)


def _attention(q, k, v, bias, sink, tq):
    b, s, _ = q.shape
    nt = s // tq
    assert tq >= 2 * BLOCK
    kv = lambda f: pl.BlockSpec((1, tq, KV_WIDTH), f)
    prev = lambda i, j: (i, jnp.maximum(j - 1, 0), 0)
    cur = lambda i, j: (i, j, 0)
    nxt = lambda i, j: (i, jnp.minimum(j + 1, nt - 1), 0)
    return pl.pallas_call(
        functools.partial(_attn_body, tq=tq),
        grid=(b, nt),
        in_specs=[pl.BlockSpec(memory_space=pltpu.SMEM),
                  pl.BlockSpec((1, tq, ATTN_WIDTH), cur),
                  kv(prev), kv(cur), kv(nxt), kv(prev), kv(cur), kv(nxt),
                  pl.BlockSpec(bias.shape, lambda i, j: (0, 0, 0, 0))],
        out_specs=pl.BlockSpec((1, tq, ATTN_WIDTH), cur),
        out_shape=jax.ShapeDtypeStruct((b, s, ATTN_WIDTH), BF16),
        compiler_params=_params(("parallel", "parallel"), VMEM_LIMIT),
        name="attn",
    )(sink.astype(F32), q, k, k, k, v, v, v, bias)


def _outproj_body(x_ref, f_ref, a_ref, wo_ref, g_ref, wr_ref, x1_ref, h2_ref, aff_ref):
    mix = jnp.concatenate([f_ref[...], a_ref[...]], axis=1)
    x1 = x_ref[...] + jnp.dot(mix, wo_ref[...], preferred_element_type=F32)
    x1_ref[...] = x1
    ms = jnp.mean(x1 * x1, axis=-1, keepdims=True)
    h2 = (x1 * lax.rsqrt(ms + EPS)) * g_ref[...]
    _store_rows_as_tiles(h2_ref, h2)
    hi = h2.astype(BF16)
    lo = (h2 - hi.astype(F32)).astype(BF16)
    nt = (((1,), (1,)), ((), ()))
    o1 = lax.dot_general(wr_ref[...], hi, nt, preferred_element_type=F32)
    o2 = lax.dot_general(wr_ref[:N_EXPERTS], lo, nt, preferred_element_type=F32)
    logits = o1[:N_EXPERTS] + (o1[N_EXPERTS:] + o2)
    e = jnp.exp(logits - jnp.max(logits, axis=0, keepdims=True))
    aff_ref[...] = e / jnp.sum(e, axis=0, keepdims=True)


def _outproj(xf, f, a, w_out, gain, w_router_t, tm):
    n = xf.shape[0]
    row = lambda w: pl.BlockSpec((tm, w), lambda i: (i, 0))
    const = lambda shape: pl.BlockSpec(shape, lambda i: (0, 0))
    return pl.pallas_call(
        _outproj_body,
        grid=(n // tm,),
        in_specs=[row(D_MODEL), row(FOURIER_WIDTH), row(ATTN_WIDTH),
                  const((D_MODEL, D_MODEL)), const((1, D_MODEL)),
                  const((2 * N_EXPERTS, D_MODEL))],
        out_specs=[row(D_MODEL),
                   pl.BlockSpec((tm * ROW_TILE, LANES), lambda i: (i, 0)),
                   pl.BlockSpec((N_EXPERTS, tm), lambda i: (0, i))],
        out_shape=[jax.ShapeDtypeStruct((n, D_MODEL), F32),
                   jax.ShapeDtypeStruct((n * ROW_TILE, LANES), U32),
                   jax.ShapeDtypeStruct((N_EXPERTS, n), F32)],
        compiler_params=_params(("parallel",), VMEM_LIMIT),
        name="outproj",
    )(xf, f, a, w_out, gain, w_router_t)


def _route_body(aff_ref, idx_ref, pos_ref, gate_ref, tok_ref,
                d_a, d_b, n_a, n_b, p_a, p_b, g_a, g_b, *, n, cap):
    e = N_EXPERTS
    ch = PREFIX_CHUNK

    def search(i, tau):
        cand = tau | jnp.left_shift(jnp.int32(1), 30 - i)
        keys = lax.bitcast_convert_type(aff_ref[...], I32)
        cnt = jnp.sum((keys >= cand).astype(I32), axis=1, keepdims=True)
        return jnp.where(cnt >= cap, cand, tau)

    tau = lax.fori_loop(0, 31, search, jnp.zeros((e, 1), I32))
    keys = lax.bitcast_convert_type(aff_ref[...], I32)
    n_gt = jnp.sum((keys > tau).astype(I32), axis=1, keepdims=True)
    need = (cap - n_gt).astype(F32)

    r = lax.broadcasted_iota(I32, (ch, ch), 0)
    c = lax.broadcasted_iota(I32, (ch, ch), 1)
    upper = (r < c).astype(BF16)
    r16 = lax.broadcasted_iota(I32, (e, e), 0)
    c16 = lax.broadcasted_iota(I32, (e, e), 1)
    lower = (c16 < r16).astype(BF16)

    carry_eq = jnp.zeros((e, 1), F32)
    carry_m = jnp.zeros((2 * e, 1), F32)
    for ci in range(n // ch):
        sl = slice(ci * ch, (ci + 1) * ch)
        k = lax.bitcast_convert_type(aff_ref[:, sl], I32)
        gt = k > tau
        eq = (k == tau).astype(F32)
        eq_ex = jnp.dot(eq.astype(BF16), upper, preferred_element_type=F32) + carry_eq
        carry_eq = carry_eq + jnp.sum(eq, axis=1, keepdims=True)
        m = jnp.where(gt | ((eq > 0.0) & (eq_ex < need)), 1.0, 0.0).astype(F32)
        cnt = jnp.sum(m, axis=0, keepdims=True)
        stacked = jnp.concatenate([m, cnt, jnp.zeros((e - 1, ch), F32)], axis=0)
        pre = jnp.dot(stacked.astype(BF16), upper, preferred_element_type=F32) + carry_m
        carry_m = carry_m + jnp.sum(stacked, axis=1, keepdims=True)
        rank = pre[:e]
        tokoff = pre[e:e + 1]
        ex_e = jnp.dot(lower, m.astype(BF16), preferred_element_type=F32)
        tok_ref[0:1, sl] = tokoff.astype(I32)
        tok_ref[1:2, sl] = (tokoff + cnt).astype(I32)
        lane = lax.broadcasted_iota(I32, (e, ch), 1) + ci * ch
        d_a[:, sl] = jnp.where(m > 0.0, lane - rank.astype(I32), 0)
        n_a[:, sl] = lane
        p_a[:, sl] = (tokoff + ex_e).astype(I32)
    g_a[...] = aff_ref[...]

    bufs = [(d_a, n_a, p_a, g_a), (d_b, n_b, p_b, g_b)]
    for b in range(int(math.log2(n))):
        src, dst = bufs[b % 2], bufs[(b + 1) % 2]
        sh = 1 << b
        d = src[0][...]
        moving = (lax.shift_right_logical(d, b) & 1) == 1
        d_in = pltpu.roll(d, n - sh, axis=1)
        arrive = (lax.shift_right_logical(d_in, b) & 1) == 1
        dst[0][...] = jnp.where(arrive, d_in, jnp.where(moving, 0, d))
        for t in (1, 2, 3):
            v = src[t][...]
            dst[t][...] = jnp.where(arrive, pltpu.roll(v, n - sh, axis=1), v)
    fin = bufs[int(math.log2(n)) % 2]
    idx_ref[...] = fin[1][:, :cap]
    pos_ref[...] = fin[2][:, :cap]
    gate_ref[...] = fin[3][:, :cap]


def _route(aff_t, cap):
    e, n = aff_t.shape
    assert n & (n - 1) == 0 or True
    full = lambda shape: pl.BlockSpec(shape, lambda: (0, 0))
    big_i = pltpu.VMEM((e, n), I32)
    return pl.pallas_call(
        functools.partial(_route_body, n=n, cap=cap),
        in_specs=[full((e, n))],
        out_specs=[full((e, cap)), full((e, cap)), full((e, cap)), full((2, n))],
        out_shape=[jax.ShapeDtypeStruct((e, cap), I32),
                   jax.ShapeDtypeStruct((e, cap), I32),
                   jax.ShapeDtypeStruct((e, cap), F32),
                   jax.ShapeDtypeStruct((2, n), I32)],
        scratch_shapes=[big_i, big_i, big_i, big_i, big_i, big_i,
                        pltpu.VMEM((e, n), F32), pltpu.VMEM((e, n), F32)],
        compiler_params=_params(None, VMEM_LIMIT),
        name="route",
    )(aff_t)


def _rows_to_cols(rows):
    r, m = rows.shape
    if r < 8:
        rows = jnp.concatenate([rows, jnp.zeros((8 - r, m), rows.dtype)], axis=0)
    return rows.T[:, :r]


def _ffn_body(idx_cur, idx_nxt, pos_prv, pos_cur, gate_ref, h_hbm, wg_ref, wu_ref, wd_ref, ys_hbm,
              xa, xb, ya, yb, wg_s, wu_s, wd_s, sems, *, tm, spe):
    s = pl.program_id(0)
    last = pl.num_programs(0) - 1
    g_a, g_b, s_a, s_b = sems.at[0], sems.at[1], sems.at[2], sems.at[3]

    def gather(idx_ref, t, buf, sem):
        def one(i):
            pltpu.make_async_copy(_tile_of_row(h_hbm, idx_ref[t, 0, i]),
                                  _tile_of_row(buf, i), sem).start()
        return one

    def scatter(pos_ref, t, buf, sem):
        def one(i):
            pltpu.make_async_copy(_tile_of_row(buf, i),
                                  _tile_of_row(ys_hbm, pos_ref[t, 0, i]), sem).start()
        return one

    def wait_gather(buf, sem):
        pltpu.make_async_copy(h_hbm.at[pl.ds(0, tm * ROW_TILE), :], buf, sem).wait()

    def wait_scatter(buf, sem):
        pltpu.make_async_copy(buf, ys_hbm.at[pl.ds(0, tm * ROW_TILE), :], sem).wait()

    def compute(xbuf, ybuf, t, copies):
        nc = D_EXPERT // FFN_CHUNK
        stages = 3 * nc
        todo = [(fn, i) for fn in copies for i in range(tm)]
        per_stage = -(-len(todo) // stages)

        def issue_some():
            for fn, i in todo[:per_stage]:
                fn(i)
            del todo[:per_stage]

        x = _load_tiles_as_rows(xbuf)
        cols = lambda c: slice(c * FFN_CHUNK, (c + 1) * FFN_CHUNK)

        def up(c):
            g = jnp.dot(x, wg_s[:, cols(c)], preferred_element_type=F32)
            issue_some()
            u = jnp.dot(x, wu_s[:, cols(c)], preferred_element_type=F32)
            issue_some()
            return g, u

        def act(gu):
            g, u = gu
            return (g * jax.nn.sigmoid(g) * u).astype(BF16)

        def down(hid, c):
            part = jnp.dot(hid, wd_s[cols(c), :], preferred_element_type=F32)
            issue_some()
            return part

        gu_next = up(0)
        hid_prev = None
        y = None
        for c in range(nc):
            gu_cur = gu_next
            if c + 1 < nc:
                gu_next = up(c + 1)
            if hid_prev is not None:
                part = down(hid_prev, c - 1)
                y = part if y is None else y + part
            hid_prev = act(gu_cur)
        y = y + down(hid_prev, nc - 1)
        assert not todo
        y = y * _rows_to_cols(gate_ref[t])
        _store_rows_as_tiles(ybuf, y)

    @pl.when(s % spe == 0)
    def _():
        rows = 256
        for w_ref, w_s in ((wg_ref, wg_s), (wu_ref, wu_s), (wd_ref, wd_s)):
            for r in range(0, w_s.shape[0], rows):
                w_s[r:r + rows, :] = w_ref[0, r:r + rows, :].astype(BF16)

    @pl.when(s == 0)
    def _():
        pl.loop(0, tm)(gather(idx_cur, 0, xa, g_a))
        yb[...] = jnp.zeros_like(yb)

    wait_gather(xa, g_a)

    @pl.when(s > 0)
    def _():
        wait_scatter(ya, s_a)

    compute(xa, ya, 0, [gather(idx_cur, 1, xb, g_b), scatter(pos_prv, 1, yb, s_b)])

    wait_gather(xb, g_b)
    wait_scatter(yb, s_b)
    compute(xb, yb, 1, [gather(idx_nxt, 0, xa, g_a), scatter(pos_cur, 0, ya, s_a)])

    @pl.when(s == last)
    def _():
        pl.loop(0, tm)(scatter(pos_cur, 1, yb, s_b))
        wait_gather(xa, g_a)
        wait_scatter(ya, s_a)
        wait_scatter(yb, s_b)


def _ffn(idx_c, pos_c, gate_c, h2, wg, wu, wd, tm):
    e, cap = idx_c.shape
    spe = cap // (2 * tm)
    steps = e * spe
    n_slots = e * cap
    idx3 = idx_c.reshape(2 * steps, 1, tm)
    pos3 = pos_c.reshape(2 * steps, 1, tm)
    gate3 = gate_c.reshape(2 * steps, 1, tm)
    cur = lambda i: (i, 0, 0)
    prv = lambda i: (jnp.maximum(i - 1, 0), 0, 0)
    nxt = lambda i: (jnp.minimum(i + 1, steps - 1), 0, 0)
    smem = lambda f: pl.BlockSpec((2, 1, tm), f, memory_space=pltpu.SMEM)
    wspec = lambda a, b: pl.BlockSpec((1, a, b), lambda i: (i // spe, 0, 0))
    buf = pltpu.VMEM((tm * ROW_TILE, LANES), U32)
    return pl.pallas_call(
        functools.partial(_ffn_body, tm=tm, spe=spe),
        grid=(steps,),
        in_specs=[smem(cur), smem(nxt), smem(prv), smem(cur),
                  pl.BlockSpec((2, 1, tm), cur),
                  pl.BlockSpec(memory_space=pl.ANY),
                  wspec(D_MODEL, D_EXPERT), wspec(D_MODEL, D_EXPERT), wspec(D_EXPERT, D_MODEL)],
        out_specs=pl.BlockSpec(memory_space=pl.ANY),
        out_shape=jax.ShapeDtypeStruct((n_slots * ROW_TILE, LANES), U32),
        scratch_shapes=[buf, buf, buf, buf,
                        pltpu.VMEM((D_MODEL, D_EXPERT), BF16), pltpu.VMEM((D_MODEL, D_EXPERT), BF16),
                        pltpu.VMEM((D_EXPERT, D_MODEL), BF16), pltpu.SemaphoreType.DMA((4,))],
        compiler_params=_params(("arbitrary",), VMEM_LIMIT),
        name="ffn",
    )(idx3, idx3, pos3, pos3, gate3, h2, wg, wu, wd)


def _combine_body(toff_ref, x1_ref, tok_ref, g_ref, ys_hbm, o_ref, win, extra, sems,
                  *, tt, ws, n_slots):
    i = pl.program_id(0)
    last = pl.num_programs(0) - 1
    buf = i % 2

    def window_start(nominal):
        return pl.multiple_of(jnp.minimum(nominal, n_slots - ws), 8)

    def first_fetch(tile, b):
        start = window_start((toff_ref[tile] // 8) * 8)
        return pltpu.make_async_copy(_tiles_of_rows(ys_hbm, start, ws), win.at[b], sems.at[b])

    @pl.when(i == 0)
    def _():
        first_fetch(0, 0).start()

    first_fetch(i, buf).wait()

    @pl.when(i < last)
    def _():
        first_fetch(i + 1, 1 - buf).start()

    hi = toff_ref[i + 1]
    base = (toff_ref[i] // 8) * 8
    nwin = (hi - base + ws - 1) // ws
    tok = _rows_to_cols(tok_ref[...])
    t_lo = tok[:, 0:1]
    as_unsigned = lambda v: lax.bitcast_convert_type(v, jnp.uint32)
    t_cnt = as_unsigned(tok[:, 1:2] - t_lo)
    lane = lax.broadcasted_iota(I32, (tt, ws), 1)

    def selector(start, nominal):
        owned = as_unsigned(lane + (start - t_lo)) < t_cnt
        if nominal is not None:
            owned = owned & (lane >= nominal - start)
        return jnp.where(owned, 1.0, 0.0).astype(BF16)

    def finish(x2_cols):
        ssq = sum(jnp.sum(x * x, axis=-1, keepdims=True) for x in x2_cols)
        scale = lax.rsqrt(ssq * (1.0 / D_MODEL) + EPS)
        for c, x in enumerate(x2_cols):
            w = x.shape[1]
            o_ref[:, c * w:(c + 1) * w] = (x * scale) * g_ref[:, c * w:(c + 1) * w]

    first = win.at[buf]
    sel = selector(window_start(base), None)
    cw = 2 * LANES

    def window_cols(c):
        words = jnp.concatenate(
            [first[pl.ds(2 * (c % 2) + k, ws, stride=ROW_TILE), :] for k in range(2)], axis=1)
        return _unpack_words(words, c >= 2)

    x2_cols = []
    rows_next = window_cols(0)
    for c in range(D_MODEL // cw):
        rows_cur = rows_next
        if (c + 1) * cw < D_MODEL:
            rows_next = window_cols(c + 1)
        x2_cols.append(x1_ref[:, c * cw:(c + 1) * cw]
                       + jnp.dot(sel, rows_cur, preferred_element_type=F32))
    finish(x2_cols)

    @pl.when(nwin > 1)
    def _():
        def later_window(w, acc):
            nominal = base + w * ws
            start = window_start(nominal)
            cp = pltpu.make_async_copy(_tiles_of_rows(ys_hbm, start, ws), extra, sems.at[2])
            cp.start()
            cp.wait()
            rows = _load_tiles_as_rows(extra)
            return acc + jnp.dot(selector(start, nominal), rows, preferred_element_type=F32)

        moe = jnp.dot(sel, _load_tiles_as_rows(first), preferred_element_type=F32)
        moe = lax.fori_loop(1, nwin, later_window, moe)
        finish([x1_ref[...] + moe])


def _combine(x1, ys, tokinfo, gain, tt, ws):
    n = x1.shape[0]
    n_slots = ys.shape[0] // ROW_TILE
    tile_off = jnp.concatenate([tokinfo[0, ::tt], jnp.full((1,), n_slots, I32)])
    return pl.pallas_call(
        functools.partial(_combine_body, tt=tt, ws=ws, n_slots=n_slots),
        grid_spec=pltpu.PrefetchScalarGridSpec(
            num_scalar_prefetch=1,
            grid=(n // tt,),
            in_specs=[pl.BlockSpec((tt, D_MODEL), lambda i, s: (i, 0)),
                      pl.BlockSpec((2, tt), lambda i, s: (0, i)),
                      pl.BlockSpec((1, D_MODEL), lambda i, s: (0, 0)),
                      pl.BlockSpec(memory_space=pl.ANY)],
            out_specs=pl.BlockSpec((tt, D_MODEL), lambda i, s: (i, 0)),
            scratch_shapes=[pltpu.VMEM((2, ws * ROW_TILE, LANES), U32),
                            pltpu.VMEM((ws * ROW_TILE, LANES), U32),
                            pltpu.SemaphoreType.DMA((3,))]),
        out_shape=jax.ShapeDtypeStruct((n, D_MODEL), F32),
        compiler_params=_params(("arbitrary",), VMEM_LIMIT),
        name="combine",
    )(tile_off, x1, tokinfo, gain, ys)


def _tile(n, pref):
    t = pref
    while n % t:
        t //= 2
    return t


def _trunk(x, p):
    b, s, d = x.shape
    n = b * s
    cap = CAPACITY_FACTOR * n // N_EXPERTS
    xf = x.reshape(n, d)
    tm = _tile(n, 512)
    zf, q, k, v = _inproj(xf, p["norm_mix"], p["w_in"], tm)
    f = _fourier(zf.reshape(b, s, FOURIER_WIDTH), p["w_fourier"])
    a = _attention(q.reshape(b, s, ATTN_WIDTH), k.reshape(b, s, KV_WIDTH),
                   v.reshape(b, s, KV_WIDTH), p["bias"], p["sink"], _tile(s, 512))
    x1, h2, aff_t = _outproj(xf, f.reshape(n, FOURIER_WIDTH), a.reshape(n, ATTN_WIDTH),
                             p["w_out"], p["norm_ffn"], p["w_router"], tm)
    idx_c, pos_c, gate_c, tokinfo = _route(aff_t, cap)
    ys = _ffn(idx_c, pos_c, gate_c, h2, p["w_gate"], p["w_up"], p["w_down"], _tile(cap // 2, 512))
    tt = _tile(n, 512)
    ws = min(CAPACITY_FACTOR * tt + tt // 4, N_EXPERTS * cap)
    y = _combine(x1, ys, tokinfo, p["norm_final"], tt, ws)
    return y.reshape(b, s, d)


def kernel(x_prompt, x_sample, w_in, w_fourier, attn_sink, rel_bias, w_out, norm_mix, norm_ffn,
           w_router, w_gate, w_up, w_down, norm_final):
    assert w_in.shape[0] == 1
    wr = w_router[0].T
    wr_hi = wr.astype(BF16)
    wr_lo = (wr - wr_hi.astype(F32)).astype(BF16)
    p = dict(
        norm_mix=norm_mix[0].reshape(1, D_MODEL),
        norm_ffn=norm_ffn[0].reshape(1, D_MODEL),
        norm_final=norm_final.reshape(1, D_MODEL),
        w_in=w_in[0].astype(BF16),
        w_fourier=w_fourier[0].astype(BF16),
        w_out=w_out[0].astype(BF16),
        w_router=jnp.concatenate([wr_hi, wr_lo], axis=0),
        w_gate=w_gate[0], w_up=w_up[0], w_down=w_down[0],
        sink=attn_sink[0],
        bias=_bias_table(rel_bias),
    )
    return _trunk(x_prompt, p), _trunk(x_sample, p)
```

```python
import functools
import math

import numpy as np
import jax
import jax.numpy as jnp
from jax import lax
from jax.experimental import pallas as pl
from jax.experimental.pallas import tpu as pltpu

F32 = jnp.float32
BF16 = jnp.bfloat16
I32 = jnp.int32

D_MODEL = 1024
FOURIER_GROUPS = 4
GROUP_DIM = 128
FOURIER_WIDTH = FOURIER_GROUPS * GROUP_DIM
N_HEADS = 8
N_KV_HEADS = 2
HEAD_DIM = 64
ATTN_WIDTH = N_HEADS * HEAD_DIM
KV_WIDTH = N_KV_HEADS * HEAD_DIM
WINDOW = 128
BLOCK = 128
N_BUCKETS = 32
MAX_DISTANCE = 128
IN_WIDTH = FOURIER_WIDTH + ATTN_WIDTH + 2 * KV_WIDTH
N_EXPERTS = 16
CAPACITY_FACTOR = 2
D_EXPERT = 1024
EPS = 1e-6

LANES = 128
PREFIX_CHUNK = 256
FFT_S2_BLOCK = 16
STRIP = 32
FFN_CHUNK = 256
VMEM_LIMIT = 56 * 1024 * 1024


U32 = jnp.uint32
HALF = D_MODEL // 2
ROW_TILE = HALF // LANES
HIGH_HALF = np.uint32(0xFFFF0000)


def _params(sem, vmem=None):
    return pltpu.CompilerParams(dimension_semantics=sem, vmem_limit_bytes=vmem)


def _store_rows_as_tiles(ref, val):
    m = val.shape[0]
    bits = lax.bitcast_convert_type(val.astype(BF16).astype(F32), U32)
    words = (bits[:, :HALF] >> 16) | (bits[:, HALF:] & HIGH_HALF)
    for c in range(ROW_TILE):
        ref[pl.ds(c, m, stride=ROW_TILE), :] = words[:, c * LANES:(c + 1) * LANES]


def _unpack_words(words, high):
    bits = (words & HIGH_HALF) if high else (words << 16)
    return lax.bitcast_convert_type(bits, F32).astype(BF16)


def _load_tiles_as_rows(ref):
    m = ref.shape[0] // ROW_TILE
    words = jnp.concatenate(
        [ref[pl.ds(c, m, stride=ROW_TILE), :] for c in range(ROW_TILE)], axis=1)
    return jnp.concatenate([_unpack_words(words, False), _unpack_words(words, True)], axis=1)


def _tiles_of_rows(ref, r, count):
    return ref.at[pl.ds(pl.multiple_of(r * ROW_TILE, ROW_TILE), count * ROW_TILE), :]


def _tile_of_row(ref, r):
    start = r * ROW_TILE
    if not isinstance(r, int):
        start = pl.multiple_of(start, ROW_TILE)
    return ref.at[pl.ds(start, ROW_TILE), :]


def _inproj_body(x_ref, g_ref, w_ref, zf_ref, q_ref, k_ref, v_ref):
    x = x_ref[...]
    ms = jnp.mean(x * x, axis=-1, keepdims=True)
    h = (x * lax.rsqrt(ms + EPS)) * g_ref[...]
    z = jnp.dot(h.astype(BF16), w_ref[...], preferred_element_type=F32)
    o = FOURIER_WIDTH
    for s1 in range(x.shape[0] // BLOCK):
        for jb in range(BLOCK // FFT_S2_BLOCK):
            r = s1 * BLOCK + jb * FFT_S2_BLOCK
            zf_ref[0, jb, s1 * FFT_S2_BLOCK:(s1 + 1) * FFT_S2_BLOCK, :] = z[r:r + FFT_S2_BLOCK, :o]
    q_ref[...] = (z[:, o:o + ATTN_WIDTH] * (HEAD_DIM ** -0.5)).astype(BF16)
    o += ATTN_WIDTH
    k_ref[...] = z[:, o:o + KV_WIDTH].astype(BF16)
    o += KV_WIDTH
    v_ref[...] = z[:, o:o + KV_WIDTH].astype(BF16)


def _inproj(xf, gain, w_in, tm, seq):
    n = xf.shape[0]
    assert tm % BLOCK == 0 and seq % tm == 0
    tiles = seq // tm
    s1_rows = tm // BLOCK * FFT_S2_BLOCK
    nblk = BLOCK // FFT_S2_BLOCK
    row = lambda w: pl.BlockSpec((tm, w), lambda i: (i, 0))
    return pl.pallas_call(
        _inproj_body,
        grid=(n // tm,),
        in_specs=[row(D_MODEL),
                  pl.BlockSpec((1, D_MODEL), lambda i: (0, 0)),
                  pl.BlockSpec((D_MODEL, IN_WIDTH), lambda i: (0, 0))],
        out_specs=[pl.BlockSpec((1, nblk, s1_rows, FOURIER_WIDTH),
                                lambda i: (i // tiles, 0, i % tiles, 0)),
                   row(ATTN_WIDTH), row(KV_WIDTH), row(KV_WIDTH)],
        out_shape=[jax.ShapeDtypeStruct((n // seq, nblk, seq // BLOCK * FFT_S2_BLOCK,
                                         FOURIER_WIDTH), F32),
                   jax.ShapeDtypeStruct((n, ATTN_WIDTH), BF16),
                   jax.ShapeDtypeStruct((n, KV_WIDTH), BF16),
                   jax.ShapeDtypeStruct((n, KV_WIDTH), BF16)],
        compiler_params=_params(("parallel",), VMEM_LIMIT),
        name="inproj",
    )(xf, gain, w_in)


def _dft_tables(s):
    s2 = BLOCK
    s1 = s // s2
    a = np.arange(s1, dtype=np.float64)
    ang1 = 2.0 * np.pi * np.outer(a, a) / s1
    f1 = np.concatenate([np.cos(ang1), -np.sin(ang1)], axis=0)
    b = np.arange(s2, dtype=np.float64)
    ang2 = 2.0 * np.pi * np.outer(b, b) / s2
    angt = 2.0 * np.pi * np.outer(a, b) / s
    c = np.arange(GROUP_DIM, dtype=np.float64)
    angc = 2.0 * np.pi * np.outer(c, c) / GROUP_DIM
    cs = np.concatenate([np.cos(angc), np.sin(angc)], axis=0)
    return dict(
        f1=jnp.asarray(f1, BF16),
        f2r=jnp.asarray(np.cos(ang2), F32), f2i=jnp.asarray(-np.sin(ang2), F32),
        twr=jnp.asarray(np.cos(angt), F32), twi=jnp.asarray(-np.sin(angt), F32),
        cs=jnp.asarray(cs, BF16))


def _dft1_body(f_ref, *refs, nb):
    x_refs, o_ref, stage = refs[:-2], refs[-2], refs[-1]
    flat = [x.at[0, 0] for x in x_refs]
    s1 = flat[0].shape[0] // nb
    per_dot = 4
    for j0 in range(0, nb, per_dot):
        x = jnp.concatenate(
            [ref[pl.ds(j, s1, stride=nb), :] for j in range(j0, j0 + per_dot) for ref in flat],
            axis=1).astype(BF16)
        res = jnp.dot(f_ref[...], x, preferred_element_type=F32)
        w = FOURIER_WIDTH
        for jj in range(per_dot):
            stage[0, :, j0 + jj, :] = res[:s1, jj * w:(jj + 1) * w]
            stage[1, :, j0 + jj, :] = res[s1:, jj * w:(jj + 1) * w]
    o_ref[0] = stage[...].astype(BF16)


def _dft1(zf4, f1):
    nb = FFT_S2_BLOCK
    b, s2_blocks, rows, w = zf4.shape
    s1, s2 = rows // nb, s2_blocks * nb
    lane_block = lambda k: pl.BlockSpec((1, 1, rows, LANES), lambda i, j: (i, j, 0, k))
    nblk = w // LANES
    return pl.pallas_call(
        functools.partial(_dft1_body, nb=nb),
        grid=(b, s2 // nb),
        in_specs=[pl.BlockSpec((2 * s1, s1), lambda i, j: (0, 0))]
                 + [lane_block(k) for k in range(nblk)],
        out_specs=pl.BlockSpec((1, 2, s1, nb, w), lambda i, j: (i, 0, 0, j, 0)),
        out_shape=jax.ShapeDtypeStruct((b, 2, s1, s2, w), BF16),
        scratch_shapes=[pltpu.VMEM((2, s1, nb, w), F32)],
        compiler_params=_params(("parallel", "parallel"), VMEM_LIMIT),
        name="dft1",
    )(f1, *([zf4] * nblk))


def _dft2_body(a_ref, f2r_ref, f2i_ref, twr_ref, twi_ref, cs_ref, wf_ref, o_ref, stage,
               *, kb, scale):
    s2 = BLOCK
    fr = f2r_ref[...]
    fi = f2i_ref[...]
    ys = []
    for j in range(kb):
        tr = twr_ref[0, j:j + 1, :]
        ti = twi_ref[0, j:j + 1, :]
        gr = fr * tr - fi * ti
        gi = fr * ti + fi * tr
        lhs = jnp.concatenate(
            [jnp.concatenate([gr, -gi], axis=1), jnp.concatenate([gi, gr], axis=1)],
            axis=0).astype(BF16)
        rhs = jnp.concatenate([a_ref[0, 0, j], a_ref[0, 1, j]], axis=0)
        ys.append(jnp.dot(lhs, rhs, preferred_element_type=F32))
    cat = jnp.concatenate(
        [jnp.concatenate([y[:s2, g * GROUP_DIM:(g + 1) * GROUP_DIM],
                          y[s2:, g * GROUP_DIM:(g + 1) * GROUP_DIM]], axis=1)
         for g in range(FOURIER_GROUPS) for y in ys], axis=0).astype(BF16)
    re = (jnp.dot(cat, cs_ref[...], preferred_element_type=F32) * scale).astype(BF16)
    rows = kb * s2
    outs = [jnp.dot(re[g * rows:(g + 1) * rows], wf_ref[g], preferred_element_type=F32)
            for g in range(FOURIER_GROUPS)]
    for j in range(kb):
        stage[:, j, :] = jnp.concatenate(
            [o[j * s2:(j + 1) * s2] for o in outs], axis=1)
    o_ref[0] = stage[...].astype(BF16)


def _dft2(a5, tabs, wf, kb, scale):
    b, _, s1, s2, w = a5.shape
    const = lambda shape: pl.BlockSpec(shape, lambda i, j: (0,) * len(shape))
    twr = tabs["twr"].reshape(s1 // kb, kb, s2)
    twi = tabs["twi"].reshape(s1 // kb, kb, s2)
    return pl.pallas_call(
        functools.partial(_dft2_body, kb=kb, scale=scale),
        grid=(b, s1 // kb),
        in_specs=[pl.BlockSpec((1, 2, kb, s2, w), lambda i, j: (i, 0, j, 0, 0)),
                  const((s2, s2)), const((s2, s2)),
                  pl.BlockSpec((1, kb, s2), lambda i, j: (j, 0, 0)),
                  pl.BlockSpec((1, kb, s2), lambda i, j: (j, 0, 0)),
                  const((2 * GROUP_DIM, GROUP_DIM)),
                  const((FOURIER_GROUPS, GROUP_DIM, GROUP_DIM))],
        out_specs=pl.BlockSpec((1, s2, kb, w), lambda i, j: (i, 0, j, 0)),
        out_shape=jax.ShapeDtypeStruct((b, s2, s1, w), BF16),
        scratch_shapes=[pltpu.VMEM((s2, kb, w), F32)],
        compiler_params=_params(("parallel", "parallel"), VMEM_LIMIT),
        name="dft2",
    )(a5, tabs["f2r"], tabs["f2i"], twr, twi, tabs["cs"], wf)


def _fourier(zf, wf_bf16):
    b, s2_blocks, rows, w = zf.shape
    s = s2_blocks * rows
    tabs = _dft_tables(s)
    a = _dft1(zf, tabs["f1"])
    f = _dft2(a, tabs, wf_bf16, kb=16,
              scale=1.0 / math.sqrt(s * GROUP_DIM))
    return f.reshape(b, s, w)


def _bucket_table():
    qi = np.arange(BLOCK)[:, None]
    kj = np.arange(3 * BLOCK)[None, :]
    rel = kj - BLOCK - qi
    nb = N_BUCKETS // 2
    max_exact = nb // 2
    n = np.abs(rel)
    large = max_exact + np.floor(
        np.log(np.maximum(n, 1).astype(np.float64) / max_exact)
        / math.log(MAX_DISTANCE / max_exact) * (nb - max_exact) + 1e-9).astype(np.int64)
    large = np.minimum(large, nb - 1)
    bucket = np.where(rel > 0, nb, 0) + np.where(n < max_exact, n, large)
    return np.where(n <= WINDOW, bucket, -1).astype(np.int32)


def _bias_body(rb_ref, bucket_ref, o_ref):
    bk = bucket_ref[...]
    col = lax.broadcasted_iota(I32, bk.shape, 1)
    for h in range(N_HEADS):
        acc = jnp.full(bk.shape, -jnp.inf, F32)
        for b in range(N_BUCKETS):
            acc = jnp.where(bk == b, rb_ref[b, h], acc)
        o_ref[0, h] = acc
        o_ref[1, h] = jnp.where(col >= BLOCK, acc, -jnp.inf)
        o_ref[2, h] = jnp.where(col < 2 * BLOCK, acc, -jnp.inf)


def _bias_table(rel_bias):
    bucket = jnp.asarray(_bucket_table())
    shape = (3, N_HEADS) + bucket.shape
    return pl.pallas_call(
        _bias_body,
        in_specs=[pl.BlockSpec(memory_space=pltpu.SMEM),
                  pl.BlockSpec(bucket.shape, lambda: (0, 0))],
        out_specs=pl.BlockSpec(shape, lambda: (0, 0, 0, 0)),
        out_shape=jax.ShapeDtypeStruct(shape, F32),
        name="bias_table",
    )(rel_bias.astype(F32), bucket)


def _attn_body(sink_ref, q_ref, kp_ref, kc_ref, kn_ref, vp_ref, vc_ref, vn_ref, bias_ref,
               o_ref, *, tq):
    j = pl.program_id(1)
    last = pl.num_programs(1) - 1
    nsb = tq // BLOCK
    group = N_HEADS // N_KV_HEADS
    gw = group * HEAD_DIM
    kext = jnp.concatenate([kp_ref[0, tq - BLOCK:, :], kc_ref[0], kn_ref[0, :BLOCK, :]], axis=0)
    vext = jnp.concatenate([vp_ref[0, tq - BLOCK:, :], vc_ref[0], vn_ref[0, :BLOCK, :]], axis=0)
    low_half = lax.broadcasted_iota(I32, (1, KV_WIDTH), 1) < HEAD_DIM

    def per_kv_head(ext):
        x = ext.astype(F32)
        xr = pltpu.roll(x, HEAD_DIM, axis=1)
        return (jnp.where(low_half, x, xr).astype(BF16), jnp.where(low_half, xr, x).astype(BF16))

    krep = per_kv_head(kext)
    vrep = per_kv_head(vext)
    head_of_lane = lax.broadcasted_iota(I32, (1, gw), 1) // HEAD_DIM
    nt = (((1,), (1,)), ((), ()))
    def scores(sb, kh):
        rows = slice(sb * BLOCK, (sb + 3) * BLOCK)
        variant = 0
        if sb == 0:
            variant = jnp.where(j == 0, 1, 0)
        elif sb == nsb - 1:
            variant = jnp.where(j == last, 2, 0)
        qblk = q_ref[0, sb * BLOCK:(sb + 1) * BLOCK, kh * gw:(kh + 1) * gw]
        qm = jnp.concatenate(
            [jnp.where(head_of_lane == g, qblk, jnp.zeros_like(qblk)) for g in range(group)],
            axis=0)
        kb = jnp.concatenate([krep[kh][rows], krep[kh][rows]], axis=1)
        s = lax.dot_general(qm, kb, nt, preferred_element_type=F32)
        return s, variant

    def weights(sv, kh):
        s_all, variant = sv
        out = []
        for g in range(group):
            for r0 in range(0, BLOCK, STRIP):
                s = s_all[g * BLOCK + r0:g * BLOCK + r0 + STRIP]
                s = s + bias_ref[variant, kh * group + g, r0:r0 + STRIP, :]
                sink = sink_ref[kh * group + g]
                m = jnp.maximum(jnp.max(s, axis=-1, keepdims=True), sink)
                p = jnp.exp(s - m)
                denom = jnp.sum(p, axis=-1, keepdims=True) + jnp.exp(sink - m)
                out.append((p / denom).astype(BF16))
        return jnp.concatenate(out, axis=0)

    def values(pn, sb, kh):
        rows = slice(sb * BLOCK, (sb + 3) * BLOCK)
        o4 = jnp.dot(pn, vrep[kh][rows], preferred_element_type=F32)
        out = []
        for pair in range(group // 2):
            even = o4[(2 * pair) * BLOCK:(2 * pair + 1) * BLOCK]
            odd = o4[(2 * pair + 1) * BLOCK:(2 * pair + 2) * BLOCK]
            out.append(jnp.where(low_half, even, odd))
        return out

    items = [(sb, kh) for sb in range(nsb) for kh in range(N_KV_HEADS)]
    pieces = {}
    s_next = scores(*items[0])
    pn_prev = None
    for i, (sb, kh) in enumerate(items):
        s_cur = s_next
        if i + 1 < len(items):
            s_next = scores(*items[i + 1])
        if pn_prev is not None:
            pieces[items[i - 1]] = values(pn_prev, *items[i - 1])
        pn_prev = weights(s_cur, kh)
    pieces[items[-1]] = values(pn_prev, *items[-1])
    for sb in range(nsb):
        row = [x for kh in range(N_KV_HEADS) for x in pieces[(sb, kh)]]
        o_ref[0, sb * BLOCK:(sb + 1) * BLOCK, :] = jnp.concatenate(row, axis=1).astype(BF16)


def _attention(q, k, v, bias, sink, tq):
    b, s, _ = q.shape
    nt = s // tq
    assert tq >= 2 * BLOCK
    kv = lambda f: pl.BlockSpec((1, tq, KV_WIDTH), f)
    prev = lambda i, j: (i, jnp.maximum(j - 1, 0), 0)
    cur = lambda i, j: (i, j, 0)
    nxt = lambda i, j: (i, jnp.minimum(j + 1, nt - 1), 0)
    return pl.pallas_call(
        functools.partial(_attn_body, tq=tq),
        grid=(b, nt),
        in_specs=[pl.BlockSpec(memory_space=pltpu.SMEM),
                  pl.BlockSpec((1, tq, ATTN_WIDTH), cur),
                  kv(prev), kv(cur), kv(nxt), kv(prev), kv(cur), kv(nxt),
                  pl.BlockSpec(bias.shape, lambda i, j: (0, 0, 0, 0))],
        out_specs=pl.BlockSpec((1, tq, ATTN_WIDTH), cur),
        out_shape=jax.ShapeDtypeStruct((b, s, ATTN_WIDTH), BF16),
        compiler_params=_params(("parallel", "parallel"), VMEM_LIMIT),
        name="attn",
    )(sink.astype(F32), q, k, k, k, v, v, v, bias)


def _outproj_body(x_ref, f_ref, a_ref, wo_ref, g_ref, wr_ref, x1_ref, h2_ref, aff_ref):
    mix = jnp.concatenate([f_ref[...], a_ref[...]], axis=1)
    x1 = x_ref[...] + jnp.dot(mix, wo_ref[...], preferred_element_type=F32)
    x1_ref[...] = x1
    ms = jnp.mean(x1 * x1, axis=-1, keepdims=True)
    h2 = (x1 * lax.rsqrt(ms + EPS)) * g_ref[...]
    _store_rows_as_tiles(h2_ref, h2)
    hi = h2.astype(BF16)
    lo = (h2 - hi.astype(F32)).astype(BF16)
    nt = (((1,), (1,)), ((), ()))
    o1 = lax.dot_general(wr_ref[...], hi, nt, preferred_element_type=F32)
    o2 = lax.dot_general(wr_ref[:N_EXPERTS], lo, nt, preferred_element_type=F32)
    logits = o1[:N_EXPERTS] + (o1[N_EXPERTS:] + o2)
    e = jnp.exp(logits - jnp.max(logits, axis=0, keepdims=True))
    aff_ref[...] = e / jnp.sum(e, axis=0, keepdims=True)


def _outproj(xf, f, a, w_out, gain, w_router_t, tm):
    n = xf.shape[0]
    row = lambda w: pl.BlockSpec((tm, w), lambda i: (i, 0))
    const = lambda shape: pl.BlockSpec(shape, lambda i: (0, 0))
    return pl.pallas_call(
        _outproj_body,
        grid=(n // tm,),
        in_specs=[row(D_MODEL), row(FOURIER_WIDTH), row(ATTN_WIDTH),
                  const((D_MODEL, D_MODEL)), const((1, D_MODEL)),
                  const((2 * N_EXPERTS, D_MODEL))],
        out_specs=[row(D_MODEL),
                   pl.BlockSpec((tm * ROW_TILE, LANES), lambda i: (i, 0)),
                   pl.BlockSpec((N_EXPERTS, tm), lambda i: (0, i))],
        out_shape=[jax.ShapeDtypeStruct((n, D_MODEL), F32),
                   jax.ShapeDtypeStruct((n * ROW_TILE, LANES), U32),
                   jax.ShapeDtypeStruct((N_EXPERTS, n), F32)],
        compiler_params=_params(("parallel",), VMEM_LIMIT),
        name="outproj",
    )(xf, f, a, w_out, gain, w_router_t)


def _route_body(aff_ref, idx_ref, pos_ref, gate_ref, tok_ref,
                d_a, d_b, n_a, n_b, p_a, p_b, g_a, g_b, *, n, cap):
    e = N_EXPERTS
    ch = PREFIX_CHUNK

    def search(i, tau):
        cand = tau | jnp.left_shift(jnp.int32(1), 30 - i)
        keys = lax.bitcast_convert_type(aff_ref[...], I32)
        cnt = jnp.sum((keys >= cand).astype(I32), axis=1, keepdims=True)
        return jnp.where(cnt >= cap, cand, tau)

    tau = lax.fori_loop(0, 31, search, jnp.zeros((e, 1), I32))
    keys = lax.bitcast_convert_type(aff_ref[...], I32)
    n_gt = jnp.sum((keys > tau).astype(I32), axis=1, keepdims=True)
    need = (cap - n_gt).astype(F32)

    r = lax.broadcasted_iota(I32, (ch, ch), 0)
    c = lax.broadcasted_iota(I32, (ch, ch), 1)
    upper = (r < c).astype(BF16)
    r16 = lax.broadcasted_iota(I32, (e, e), 0)
    c16 = lax.broadcasted_iota(I32, (e, e), 1)
    lower = (c16 < r16).astype(BF16)

    carry_eq = jnp.zeros((e, 1), F32)
    carry_m = jnp.zeros((2 * e, 1), F32)
    for ci in range(n // ch):
        sl = slice(ci * ch, (ci + 1) * ch)
        k = lax.bitcast_convert_type(aff_ref[:, sl], I32)
        gt = k > tau
        eq = (k == tau).astype(F32)
        eq_ex = jnp.dot(eq.astype(BF16), upper, preferred_element_type=F32) + carry_eq
        carry_eq = carry_eq + jnp.sum(eq, axis=1, keepdims=True)
        m = jnp.where(gt | ((eq > 0.0) & (eq_ex < need)), 1.0, 0.0).astype(F32)
        cnt = jnp.sum(m, axis=0, keepdims=True)
        stacked = jnp.concatenate([m, cnt, jnp.zeros((e - 1, ch), F32)], axis=0)
        pre = jnp.dot(stacked.astype(BF16), upper, preferred_element_type=F32) + carry_m
        carry_m = carry_m + jnp.sum(stacked, axis=1, keepdims=True)
        rank = pre[:e]
        tokoff = pre[e:e + 1]
        ex_e = jnp.dot(lower, m.astype(BF16), preferred_element_type=F32)
        tok_ref[0:1, sl] = tokoff.astype(I32)
        tok_ref[1:2, sl] = (tokoff + cnt).astype(I32)
        lane = lax.broadcasted_iota(I32, (e, ch), 1) + ci * ch
        d_a[:, sl] = jnp.where(m > 0.0, lane - rank.astype(I32), 0)
        n_a[:, sl] = lane
        p_a[:, sl] = (tokoff + ex_e).astype(I32)
    g_a[...] = aff_ref[...]

    bufs = [(d_a, n_a, p_a, g_a), (d_b, n_b, p_b, g_b)]
    for b in range(int(math.log2(n))):
        src, dst = bufs[b % 2], bufs[(b + 1) % 2]
        sh = 1 << b
        d = src[0][...]
        moving = (lax.shift_right_logical(d, b) & 1) == 1
        d_in = pltpu.roll(d, n - sh, axis=1)
        arrive = (lax.shift_right_logical(d_in, b) & 1) == 1
        dst[0][...] = jnp.where(arrive, d_in, jnp.where(moving, 0, d))
        for t in (1, 2, 3):
            v = src[t][...]
            dst[t][...] = jnp.where(arrive, pltpu.roll(v, n - sh, axis=1), v)
    fin = bufs[int(math.log2(n)) % 2]
    idx_ref[...] = fin[1][:, :cap]
    pos_ref[...] = fin[2][:, :cap]
    gate_ref[...] = fin[3][:, :cap]


def _route(aff_t, cap):
    e, n = aff_t.shape
    assert n & (n - 1) == 0 or True
    full = lambda shape: pl.BlockSpec(shape, lambda: (0, 0))
    big_i = pltpu.VMEM((e, n), I32)
    return pl.pallas_call(
        functools.partial(_route_body, n=n, cap=cap),
        in_specs=[full((e, n))],
        out_specs=[full((e, cap)), full((e, cap)), full((e, cap)), full((2, n))],
        out_shape=[jax.ShapeDtypeStruct((e, cap), I32),
                   jax.ShapeDtypeStruct((e, cap), I32),
                   jax.ShapeDtypeStruct((e, cap), F32),
                   jax.ShapeDtypeStruct((2, n), I32)],
        scratch_shapes=[big_i, big_i, big_i, big_i, big_i, big_i,
                        pltpu.VMEM((e, n), F32), pltpu.VMEM((e, n), F32)],
        compiler_params=_params(None, VMEM_LIMIT),
        name="route",
    )(aff_t)


def _rows_to_cols(rows):
    r, m = rows.shape
    if r < 8:
        rows = jnp.concatenate([rows, jnp.zeros((8 - r, m), rows.dtype)], axis=0)
    return rows.T[:, :r]


def _ffn_body(idx_cur, idx_nxt, pos_prv, pos_cur, gate_ref, h_hbm, wg_ref, wu_ref, wd_ref, ys_hbm,
              xa, xb, ya, yb, wg_s, wu_s, wd_s, sems, *, tm, spe):
    s = pl.program_id(0)
    last = pl.num_programs(0) - 1
    g_a, g_b, s_a, s_b = sems.at[0], sems.at[1], sems.at[2], sems.at[3]

    def gather(idx_ref, t, buf, sem):
        def one(i):
            pltpu.make_async_copy(_tile_of_row(h_hbm, idx_ref[t, 0, i]),
                                  _tile_of_row(buf, i), sem).start()
        return one

    def scatter(pos_ref, t, buf, sem):
        def one(i):
            pltpu.make_async_copy(_tile_of_row(buf, i),
                                  _tile_of_row(ys_hbm, pos_ref[t, 0, i]), sem).start()
        return one

    def wait_gather(buf, sem):
        pltpu.make_async_copy(h_hbm.at[pl.ds(0, tm * ROW_TILE), :], buf, sem).wait()

    def wait_scatter(buf, sem):
        pltpu.make_async_copy(buf, ys_hbm.at[pl.ds(0, tm * ROW_TILE), :], sem).wait()

    def compute(xbuf, ybuf, t, copies):
        nc = D_EXPERT // FFN_CHUNK
        stages = 3 * nc
        todo = [(fn, i) for fn in copies for i in range(tm)]
        per_stage = -(-len(todo) // stages)

        def issue_some():
            for fn, i in todo[:per_stage]:
                fn(i)
            del todo[:per_stage]

        x = _load_tiles_as_rows(xbuf)
        cols = lambda c: slice(c * FFN_CHUNK, (c + 1) * FFN_CHUNK)

        def up(c):
            g = jnp.dot(x, wg_s[:, cols(c)], preferred_element_type=F32)
            issue_some()
            u = jnp.dot(x, wu_s[:, cols(c)], preferred_element_type=F32)
            issue_some()
            return g, u

        def act(gu):
            g, u = gu
            return (g * jax.nn.sigmoid(g) * u).astype(BF16)

        def down(hid, c):
            part = jnp.dot(hid, wd_s[cols(c), :], preferred_element_type=F32)
            issue_some()
            return part

        gu_next = up(0)
        hid_prev = None
        y = None
        for c in range(nc):
            gu_cur = gu_next
            if c + 1 < nc:
                gu_next = up(c + 1)
            if hid_prev is not None:
                part = down(hid_prev, c - 1)
                y = part if y is None else y + part
            hid_prev = act(gu_cur)
        y = y + down(hid_prev, nc - 1)
        assert not todo
        y = y * _rows_to_cols(gate_ref[t])
        _store_rows_as_tiles(ybuf, y)

    @pl.when(s % spe == 0)
    def _():
        rows = 256
        for w_ref, w_s in ((wg_ref, wg_s), (wu_ref, wu_s), (wd_ref, wd_s)):
            for r in range(0, w_s.shape[0], rows):
                w_s[r:r + rows, :] = w_ref[0, r:r + rows, :].astype(BF16)

    @pl.when(s == 0)
    def _():
        pl.loop(0, tm)(gather(idx_cur, 0, xa, g_a))
        yb[...] = jnp.zeros_like(yb)

    wait_gather(xa, g_a)

    @pl.when(s > 0)
    def _():
        wait_scatter(ya, s_a)

    compute(xa, ya, 0, [gather(idx_cur, 1, xb, g_b), scatter(pos_prv, 1, yb, s_b)])

    wait_gather(xb, g_b)
    wait_scatter(yb, s_b)
    compute(xb, yb, 1, [gather(idx_nxt, 0, xa, g_a), scatter(pos_cur, 0, ya, s_a)])

    @pl.when(s == last)
    def _():
        pl.loop(0, tm)(scatter(pos_cur, 1, yb, s_b))
        wait_gather(xa, g_a)
        wait_scatter(ya, s_a)
        wait_scatter(yb, s_b)


def _ffn(idx_c, pos_c, gate_c, h2, wg, wu, wd, tm):
    e, cap = idx_c.shape
    spe = cap // (2 * tm)
    steps = e * spe
    n_slots = e * cap
    idx3 = idx_c.reshape(2 * steps, 1, tm)
    pos3 = pos_c.reshape(2 * steps, 1, tm)
    gate3 = gate_c.reshape(2 * steps, 1, tm)
    cur = lambda i: (i, 0, 0)
    prv = lambda i: (jnp.maximum(i - 1, 0), 0, 0)
    nxt = lambda i: (jnp.minimum(i + 1, steps - 1), 0, 0)
    smem = lambda f: pl.BlockSpec((2, 1, tm), f, memory_space=pltpu.SMEM)
    wspec = lambda a, b: pl.BlockSpec((1, a, b), lambda i: (i // spe, 0, 0))
    buf = pltpu.VMEM((tm * ROW_TILE, LANES), U32)
    return pl.pallas_call(
        functools.partial(_ffn_body, tm=tm, spe=spe),
        grid=(steps,),
        in_specs=[smem(cur), smem(nxt), smem(prv), smem(cur),
                  pl.BlockSpec((2, 1, tm), cur),
                  pl.BlockSpec(memory_space=pl.ANY),
                  wspec(D_MODEL, D_EXPERT), wspec(D_MODEL, D_EXPERT), wspec(D_EXPERT, D_MODEL)],
        out_specs=pl.BlockSpec(memory_space=pl.ANY),
        out_shape=jax.ShapeDtypeStruct((n_slots * ROW_TILE, LANES), U32),
        scratch_shapes=[buf, buf, buf, buf,
                        pltpu.VMEM((D_MODEL, D_EXPERT), BF16), pltpu.VMEM((D_MODEL, D_EXPERT), BF16),
                        pltpu.VMEM((D_EXPERT, D_MODEL), BF16), pltpu.SemaphoreType.DMA((4,))],
        compiler_params=_params(("arbitrary",), VMEM_LIMIT),
        name="ffn",
    )(idx3, idx3, pos3, pos3, gate3, h2, wg, wu, wd)


def _combine_body(toff_ref, x1_ref, tok_ref, g_ref, ys_hbm, o_ref, win, extra, sems,
                  *, tt, ws, n_slots):
    i = pl.program_id(0)
    last = pl.num_programs(0) - 1
    buf = i % 2

    def window_start(nominal):
        return pl.multiple_of(jnp.minimum(nominal, n_slots - ws), 8)

    def first_fetch(tile, b):
        start = window_start((toff_ref[tile] // 8) * 8)
        return pltpu.make_async_copy(_tiles_of_rows(ys_hbm, start, ws), win.at[b], sems.at[b])

    @pl.when(i == 0)
    def _():
        first_fetch(0, 0).start()

    first_fetch(i, buf).wait()

    @pl.when(i < last)
    def _():
        first_fetch(i + 1, 1 - buf).start()

    hi = toff_ref[i + 1]
    base = (toff_ref[i] // 8) * 8
    nwin = (hi - base + ws - 1) // ws
    tok = _rows_to_cols(tok_ref[...])
    t_lo = tok[:, 0:1]
    as_unsigned = lambda v: lax.bitcast_convert_type(v, jnp.uint32)
    t_cnt = as_unsigned(tok[:, 1:2] - t_lo)
    lane = lax.broadcasted_iota(I32, (tt, ws), 1)

    def selector(start, nominal):
        owned = as_unsigned(lane + (start - t_lo)) < t_cnt
        if nominal is not None:
            owned = owned & (lane >= nominal - start)
        return jnp.where(owned, 1.0, 0.0).astype(BF16)

    def finish(x2_cols):
        ssq = sum(jnp.sum(x * x, axis=-1, keepdims=True) for x in x2_cols)
        scale = lax.rsqrt(ssq * (1.0 / D_MODEL) + EPS)
        for c, x in enumerate(x2_cols):
            w = x.shape[1]
            o_ref[:, c * w:(c + 1) * w] = (x * scale) * g_ref[:, c * w:(c + 1) * w]

    first = win.at[buf]
    sel = selector(window_start(base), None)
    cw = 2 * LANES

    def window_cols(c):
        words = jnp.concatenate(
            [first[pl.ds(2 * (c % 2) + k, ws, stride=ROW_TILE), :] for k in range(2)], axis=1)
        return _unpack_words(words, c >= 2)

    x2_cols = []
    rows_next = window_cols(0)
    for c in range(D_MODEL // cw):
        rows_cur = rows_next
        if (c + 1) * cw < D_MODEL:
            rows_next = window_cols(c + 1)
        x2_cols.append(x1_ref[:, c * cw:(c + 1) * cw]
                       + jnp.dot(sel, rows_cur, preferred_element_type=F32))
    finish(x2_cols)

    @pl.when(nwin > 1)
    def _():
        def later_window(w, acc):
            nominal = base + w * ws
            start = window_start(nominal)
            cp = pltpu.make_async_copy(_tiles_of_rows(ys_hbm, start, ws), extra, sems.at[2])
            cp.start()
            cp.wait()
            rows = _load_tiles_as_rows(extra)
            return acc + jnp.dot(selector(start, nominal), rows, preferred_element_type=F32)

        moe = jnp.dot(sel, _load_tiles_as_rows(first), preferred_element_type=F32)
        moe = lax.fori_loop(1, nwin, later_window, moe)
        finish([x1_ref[...] + moe])


def _combine(x1, ys, tokinfo, gain, tt, ws):
    n = x1.shape[0]
    n_slots = ys.shape[0] // ROW_TILE
    tile_off = jnp.concatenate([tokinfo[0, ::tt], jnp.full((1,), n_slots, I32)])
    return pl.pallas_call(
        functools.partial(_combine_body, tt=tt, ws=ws, n_slots=n_slots),
        grid_spec=pltpu.PrefetchScalarGridSpec(
            num_scalar_prefetch=1,
            grid=(n // tt,),
            in_specs=[pl.BlockSpec((tt, D_MODEL), lambda i, s: (i, 0)),
                      pl.BlockSpec((2, tt), lambda i, s: (0, i)),
                      pl.BlockSpec((1, D_MODEL), lambda i, s: (0, 0)),
                      pl.BlockSpec(memory_space=pl.ANY)],
            out_specs=pl.BlockSpec((tt, D_MODEL), lambda i, s: (i, 0)),
            scratch_shapes=[pltpu.VMEM((2, ws * ROW_TILE, LANES), U32),
                            pltpu.VMEM((ws * ROW_TILE, LANES), U32),
                            pltpu.SemaphoreType.DMA((3,))]),
        out_shape=jax.ShapeDtypeStruct((n, D_MODEL), F32),
        compiler_params=_params(("arbitrary",), VMEM_LIMIT),
        name="combine",
    )(tile_off, x1, tokinfo, gain, ys)


def _tile(n, pref):
    t = pref
    while n % t:
        t //= 2
    return t


def _trunk(x, p):
    b, s, d = x.shape
    n = b * s
    cap = CAPACITY_FACTOR * n // N_EXPERTS
    xf = x.reshape(n, d)
    tm = _tile(n, 512)
    zf, q, k, v = _inproj(xf, p["norm_mix"], p["w_in"], tm, s)
    f = _fourier(zf, p["w_fourier"])
    a = _attention(q.reshape(b, s, ATTN_WIDTH), k.reshape(b, s, KV_WIDTH),
                   v.reshape(b, s, KV_WIDTH), p["bias"], p["sink"], _tile(s, 512))
    x1, h2, aff_t = _outproj(xf, f.reshape(n, FOURIER_WIDTH), a.reshape(n, ATTN_WIDTH),
                             p["w_out"], p["norm_ffn"], p["w_router"], tm)
    idx_c, pos_c, gate_c, tokinfo = _route(aff_t, cap)
    ys = _ffn(idx_c, pos_c, gate_c, h2, p["w_gate"], p["w_up"], p["w_down"], _tile(cap // 2, 512))
    tt = _tile(n, 512)
    ws = min(CAPACITY_FACTOR * tt + tt // 4, N_EXPERTS * cap)
    y = _combine(x1, ys, tokinfo, p["norm_final"], tt, ws)
    return y.reshape(b, s, d)


def kernel(x_prompt, x_sample, w_in, w_fourier, attn_sink, rel_bias, w_out, norm_mix, norm_ffn,
           w_router, w_gate, w_up, w_down, norm_final):
    assert w_in.shape[0] == 1
    wr = w_router[0].T
    wr_hi = wr.astype(BF16)
    wr_lo = (wr - wr_hi.astype(F32)).astype(BF16)
    p = dict(
        norm_mix=norm_mix[0].reshape(1, D_MODEL),
        norm_ffn=norm_ffn[0].reshape(1, D_MODEL),
        norm_final=norm_final.reshape(1, D_MODEL),
        w_in=w_in[0].astype(BF16),
        w_fourier=w_fourier[0].astype(BF16),
        w_out=w_out[0].astype(BF16),
        w_router=jnp.concatenate([wr_hi, wr_lo], axis=0),
        w_gate=w_gate[0], w_up=w_up[0], w_down=w_down[0],
        sink=attn_sink[0],
        bias=_bias_table(rel_bias),
    )
    return _trunk(x_prompt, p), _trunk(x_sample, p)
```

```python
import functools
import math

import numpy as np
import jax
import jax.numpy as jnp
from jax import lax
from jax.experimental import pallas as pl
from jax.experimental.pallas import tpu as pltpu

F32 = jnp.float32
BF16 = jnp.bfloat16
I32 = jnp.int32

D_MODEL = 1024
FOURIER_GROUPS = 4
GROUP_DIM = 128
FOURIER_WIDTH = FOURIER_GROUPS * GROUP_DIM
N_HEADS = 8
N_KV_HEADS = 2
HEAD_DIM = 64
ATTN_WIDTH = N_HEADS * HEAD_DIM
KV_WIDTH = N_KV_HEADS * HEAD_DIM
WINDOW = 128
BLOCK = 128
N_BUCKETS = 32
MAX_DISTANCE = 128
IN_WIDTH = FOURIER_WIDTH + ATTN_WIDTH + 2 * KV_WIDTH
N_EXPERTS = 16
CAPACITY_FACTOR = 2
D_EXPERT = 1024
EPS = 1e-6

LANES = 128
PREFIX_CHUNK = 256
FFT_S2_BLOCK = 16
FFT_ROW_GROUP = 8
STRIP = 32
FFN_CHUNK = 256
VMEM_LIMIT = 56 * 1024 * 1024


U32 = jnp.uint32
HALF = D_MODEL // 2
ROW_TILE = HALF // LANES
HIGH_HALF = np.uint32(0xFFFF0000)


def _params(sem, vmem=None):
    return pltpu.CompilerParams(dimension_semantics=sem, vmem_limit_bytes=vmem)


def _store_rows_as_tiles(ref, val):
    m = val.shape[0]
    bits = lax.bitcast_convert_type(val.astype(BF16).astype(F32), U32)
    words = (bits[:, :HALF] >> 16) | (bits[:, HALF:] & HIGH_HALF)
    for c in range(ROW_TILE):
        ref[pl.ds(c, m, stride=ROW_TILE), :] = words[:, c * LANES:(c + 1) * LANES]


def _unpack_words(words, high):
    bits = (words & HIGH_HALF) if high else (words << 16)
    return lax.bitcast_convert_type(bits, F32).astype(BF16)


def _load_tiles_as_rows(ref):
    m = ref.shape[0] // ROW_TILE
    words = jnp.concatenate(
        [ref[pl.ds(c, m, stride=ROW_TILE), :] for c in range(ROW_TILE)], axis=1)
    return jnp.concatenate([_unpack_words(words, False), _unpack_words(words, True)], axis=1)


def _tiles_of_rows(ref, r, count):
    return ref.at[pl.ds(pl.multiple_of(r * ROW_TILE, ROW_TILE), count * ROW_TILE), :]


def _tile_of_row(ref, r):
    start = r * ROW_TILE
    if not isinstance(r, int):
        start = pl.multiple_of(start, ROW_TILE)
    return ref.at[pl.ds(start, ROW_TILE), :]


def _inproj_body(x_ref, g_ref, w_ref, zf_ref, q_ref, k_ref, v_ref):
    x = x_ref[...]
    ms = jnp.mean(x * x, axis=-1, keepdims=True)
    h = (x * lax.rsqrt(ms + EPS)) * g_ref[...]
    z = jnp.dot(h.astype(BF16), w_ref[...], preferred_element_type=F32)
    o = FOURIER_WIDTH
    g = FFT_ROW_GROUP
    for s1 in range(x.shape[0] // BLOCK):
        for jb in range(BLOCK // g):
            r = s1 * BLOCK + jb * g
            zf_ref[0, jb, s1 * g:(s1 + 1) * g, :] = z[r:r + g, :o]
    q_ref[...] = (z[:, o:o + ATTN_WIDTH] * (HEAD_DIM ** -0.5)).astype(BF16)
    o += ATTN_WIDTH
    k_ref[...] = z[:, o:o + KV_WIDTH].astype(BF16)
    o += KV_WIDTH
    v_ref[...] = z[:, o:o + KV_WIDTH].astype(BF16)


def _inproj(xf, gain, w_in, tm, seq):
    n = xf.shape[0]
    assert tm % BLOCK == 0 and seq % tm == 0
    tiles = seq // tm
    s1_rows = tm // BLOCK * FFT_ROW_GROUP
    nblk = BLOCK // FFT_ROW_GROUP
    row = lambda w: pl.BlockSpec((tm, w), lambda i: (i, 0))
    return pl.pallas_call(
        _inproj_body,
        grid=(n // tm,),
        in_specs=[row(D_MODEL),
                  pl.BlockSpec((1, D_MODEL), lambda i: (0, 0)),
                  pl.BlockSpec((D_MODEL, IN_WIDTH), lambda i: (0, 0))],
        out_specs=[pl.BlockSpec((1, nblk, s1_rows, FOURIER_WIDTH),
                                lambda i: (i // tiles, 0, i % tiles, 0)),
                   row(ATTN_WIDTH), row(KV_WIDTH), row(KV_WIDTH)],
        out_shape=[jax.ShapeDtypeStruct((n // seq, nblk, seq // BLOCK * FFT_ROW_GROUP,
                                         FOURIER_WIDTH), F32),
                   jax.ShapeDtypeStruct((n, ATTN_WIDTH), BF16),
                   jax.ShapeDtypeStruct((n, KV_WIDTH), BF16),
                   jax.ShapeDtypeStruct((n, KV_WIDTH), BF16)],
        compiler_params=_params(("parallel",), VMEM_LIMIT),
        name="inproj",
    )(xf, gain, w_in)


def _dft_tables(s):
    s2 = BLOCK
    s1 = s // s2
    a = np.arange(s1, dtype=np.float64)
    ang1 = 2.0 * np.pi * np.outer(a, a) / s1
    f1 = np.concatenate([np.cos(ang1), -np.sin(ang1)], axis=0)
    b = np.arange(s2, dtype=np.float64)
    ang2 = 2.0 * np.pi * np.outer(b, b) / s2
    angt = 2.0 * np.pi * np.outer(a, b) / s
    c = np.arange(GROUP_DIM, dtype=np.float64)
    angc = 2.0 * np.pi * np.outer(c, c) / GROUP_DIM
    cs = np.concatenate([np.cos(angc), np.sin(angc)], axis=0)
    return dict(
        f1=jnp.asarray(f1, BF16),
        f2r=jnp.asarray(np.cos(ang2), F32), f2i=jnp.asarray(-np.sin(ang2), F32),
        twr=jnp.asarray(np.cos(angt), F32), twi=jnp.asarray(-np.sin(angt), F32),
        cs=jnp.asarray(cs, BF16))


def _dft1_body(f_ref, *refs, nb):
    x_refs, o_ref, stage = refs[:-2], refs[-2], refs[-1]
    g = FFT_ROW_GROUP
    s1 = x_refs[0].shape[2] // g
    per_dot = 4
    for j0 in range(0, nb, per_dot):
        x = jnp.concatenate(
            [ref[0, j // g, pl.ds(j % g, s1, stride=g), :]
             for j in range(j0, j0 + per_dot) for ref in x_refs],
            axis=1).astype(BF16)
        res = jnp.dot(f_ref[...], x, preferred_element_type=F32)
        w = FOURIER_WIDTH
        for jj in range(per_dot):
            stage[0, :, j0 + jj, :] = res[:s1, jj * w:(jj + 1) * w]
            stage[1, :, j0 + jj, :] = res[s1:, jj * w:(jj + 1) * w]
    o_ref[0] = stage[...].astype(BF16)


def _dft1(zf4, f1):
    nb, g = FFT_S2_BLOCK, FFT_ROW_GROUP
    b, s2_groups, rows, w = zf4.shape
    s1, s2 = rows // g, s2_groups * g
    lane_block = lambda k: pl.BlockSpec((1, nb // g, rows, LANES), lambda i, j: (i, j, 0, k))
    nblk = w // LANES
    return pl.pallas_call(
        functools.partial(_dft1_body, nb=nb),
        grid=(b, s2 // nb),
        in_specs=[pl.BlockSpec((2 * s1, s1), lambda i, j: (0, 0))]
                 + [lane_block(k) for k in range(nblk)],
        out_specs=pl.BlockSpec((1, 2, s1, nb, w), lambda i, j: (i, 0, 0, j, 0)),
        out_shape=jax.ShapeDtypeStruct((b, 2, s1, s2, w), BF16),
        scratch_shapes=[pltpu.VMEM((2, s1, nb, w), F32)],
        compiler_params=_params(("parallel", "parallel"), VMEM_LIMIT),
        name="dft1",
    )(f1, *([zf4] * nblk))


def _dft2_body(a_ref, f2r_ref, f2i_ref, twr_ref, twi_ref, cs_ref, wf_ref, o_ref, stage,
               *, kb, scale):
    s2 = BLOCK
    fr = f2r_ref[...]
    fi = f2i_ref[...]
    ys = []
    for j in range(kb):
        tr = twr_ref[0, j:j + 1, :]
        ti = twi_ref[0, j:j + 1, :]
        gr = fr * tr - fi * ti
        gi = fr * ti + fi * tr
        lhs = jnp.concatenate(
            [jnp.concatenate([gr, -gi], axis=1), jnp.concatenate([gi, gr], axis=1)],
            axis=0).astype(BF16)
        rhs = jnp.concatenate([a_ref[0, 0, j], a_ref[0, 1, j]], axis=0)
        ys.append(jnp.dot(lhs, rhs, preferred_element_type=F32))
    cat = jnp.concatenate(
        [jnp.concatenate([y[:s2, g * GROUP_DIM:(g + 1) * GROUP_DIM],
                          y[s2:, g * GROUP_DIM:(g + 1) * GROUP_DIM]], axis=1)
         for g in range(FOURIER_GROUPS) for y in ys], axis=0).astype(BF16)
    re = (jnp.dot(cat, cs_ref[...], preferred_element_type=F32) * scale).astype(BF16)
    rows = kb * s2
    outs = [jnp.dot(re[g * rows:(g + 1) * rows], wf_ref[g], preferred_element_type=F32)
            for g in range(FOURIER_GROUPS)]
    for j in range(kb):
        stage[:, j, :] = jnp.concatenate(
            [o[j * s2:(j + 1) * s2] for o in outs], axis=1)
    o_ref[0] = stage[...].astype(BF16)


def _dft2(a5, tabs, wf, kb, scale):
    b, _, s1, s2, w = a5.shape
    const = lambda shape: pl.BlockSpec(shape, lambda i, j: (0,) * len(shape))
    twr = tabs["twr"].reshape(s1 // kb, kb, s2)
    twi = tabs["twi"].reshape(s1 // kb, kb, s2)
    return pl.pallas_call(
        functools.partial(_dft2_body, kb=kb, scale=scale),
        grid=(b, s1 // kb),
        in_specs=[pl.BlockSpec((1, 2, kb, s2, w), lambda i, j: (i, 0, j, 0, 0)),
                  const((s2, s2)), const((s2, s2)),
                  pl.BlockSpec((1, kb, s2), lambda i, j: (j, 0, 0)),
                  pl.BlockSpec((1, kb, s2), lambda i, j: (j, 0, 0)),
                  const((2 * GROUP_DIM, GROUP_DIM)),
                  const((FOURIER_GROUPS, GROUP_DIM, GROUP_DIM))],
        out_specs=pl.BlockSpec((1, s2, kb, w), lambda i, j: (i, 0, j, 0)),
        out_shape=jax.ShapeDtypeStruct((b, s2, s1, w), BF16),
        scratch_shapes=[pltpu.VMEM((s2, kb, w), F32)],
        compiler_params=_params(("parallel", "parallel"), VMEM_LIMIT),
        name="dft2",
    )(a5, tabs["f2r"], tabs["f2i"], twr, twi, tabs["cs"], wf)


def _fourier(zf, wf_bf16):
    b, s2_blocks, rows, w = zf.shape
    s = s2_blocks * rows
    tabs = _dft_tables(s)
    a = _dft1(zf, tabs["f1"])
    f = _dft2(a, tabs, wf_bf16, kb=16,
              scale=1.0 / math.sqrt(s * GROUP_DIM))
    return f.reshape(b, s, w)


def _bucket_table():
    qi = np.arange(BLOCK)[:, None]
    kj = np.arange(3 * BLOCK)[None, :]
    rel = kj - BLOCK - qi
    nb = N_BUCKETS // 2
    max_exact = nb // 2
    n = np.abs(rel)
    large = max_exact + np.floor(
        np.log(np.maximum(n, 1).astype(np.float64) / max_exact)
        / math.log(MAX_DISTANCE / max_exact) * (nb - max_exact) + 1e-9).astype(np.int64)
    large = np.minimum(large, nb - 1)
    bucket = np.where(rel > 0, nb, 0) + np.where(n < max_exact, n, large)
    return np.where(n <= WINDOW, bucket, -1).astype(np.int32)


def _bias_body(rb_ref, bucket_ref, o_ref):
    bk = bucket_ref[...]
    col = lax.broadcasted_iota(I32, bk.shape, 1)
    for h in range(N_HEADS):
        acc = jnp.full(bk.shape, -jnp.inf, F32)
        for b in range(N_BUCKETS):
            acc = jnp.where(bk == b, rb_ref[b, h], acc)
        o_ref[0, h] = acc
        o_ref[1, h] = jnp.where(col >= BLOCK, acc, -jnp.inf)
        o_ref[2, h] = jnp.where(col < 2 * BLOCK, acc, -jnp.inf)


def _bias_table(rel_bias):
    bucket = jnp.asarray(_bucket_table())
    shape = (3, N_HEADS) + bucket.shape
    return pl.pallas_call(
        _bias_body,
        in_specs=[pl.BlockSpec(memory_space=pltpu.SMEM),
                  pl.BlockSpec(bucket.shape, lambda: (0, 0))],
        out_specs=pl.BlockSpec(shape, lambda: (0, 0, 0, 0)),
        out_shape=jax.ShapeDtypeStruct(shape, F32),
        name="bias_table",
    )(rel_bias.astype(F32), bucket)


def _attn_body(sink_ref, q_ref, kp_ref, kc_ref, kn_ref, vp_ref, vc_ref, vn_ref, bias_ref,
               o_ref, *, tq):
    j = pl.program_id(1)
    last = pl.num_programs(1) - 1
    nsb = tq // BLOCK
    group = N_HEADS // N_KV_HEADS
    gw = group * HEAD_DIM
    kext = jnp.concatenate([kp_ref[0, tq - BLOCK:, :], kc_ref[0], kn_ref[0, :BLOCK, :]], axis=0)
    vext = jnp.concatenate([vp_ref[0, tq - BLOCK:, :], vc_ref[0], vn_ref[0, :BLOCK, :]], axis=0)
    low_half = lax.broadcasted_iota(I32, (1, KV_WIDTH), 1) < HEAD_DIM

    def per_kv_head(ext):
        x = ext.astype(F32)
        xr = pltpu.roll(x, HEAD_DIM, axis=1)
        return (jnp.where(low_half, x, xr).astype(BF16), jnp.where(low_half, xr, x).astype(BF16))

    krep = per_kv_head(kext)
    vrep = per_kv_head(vext)
    head_of_lane = lax.broadcasted_iota(I32, (1, gw), 1) // HEAD_DIM
    nt = (((1,), (1,)), ((), ()))
    def scores(sb, kh):
        rows = slice(sb * BLOCK, (sb + 3) * BLOCK)
        variant = 0
        if sb == 0:
            variant = jnp.where(j == 0, 1, 0)
        elif sb == nsb - 1:
            variant = jnp.where(j == last, 2, 0)
        qblk = q_ref[0, sb * BLOCK:(sb + 1) * BLOCK, kh * gw:(kh + 1) * gw]
        qm = jnp.concatenate(
            [jnp.where(head_of_lane == g, qblk, jnp.zeros_like(qblk)) for g in range(group)],
            axis=0)
        kb = jnp.concatenate([krep[kh][rows], krep[kh][rows]], axis=1)
        s = lax.dot_general(qm, kb, nt, preferred_element_type=F32)
        return s, variant

    def weights(sv, kh):
        s_all, variant = sv
        out = []
        for g in range(group):
            for r0 in range(0, BLOCK, STRIP):
                s = s_all[g * BLOCK + r0:g * BLOCK + r0 + STRIP]
                s = s + bias_ref[variant, kh * group + g, r0:r0 + STRIP, :]
                sink = sink_ref[kh * group + g]
                m = jnp.maximum(jnp.max(s, axis=-1, keepdims=True), sink)
                p = jnp.exp(s - m)
                denom = jnp.sum(p, axis=-1, keepdims=True) + jnp.exp(sink - m)
                out.append((p / denom).astype(BF16))
        return jnp.concatenate(out, axis=0)

    def values(pn, sb, kh):
        rows = slice(sb * BLOCK, (sb + 3) * BLOCK)
        o4 = jnp.dot(pn, vrep[kh][rows], preferred_element_type=F32)
        out = []
        for pair in range(group // 2):
            even = o4[(2 * pair) * BLOCK:(2 * pair + 1) * BLOCK]
            odd = o4[(2 * pair + 1) * BLOCK:(2 * pair + 2) * BLOCK]
            out.append(jnp.where(low_half, even, odd))
        return out

    items = [(sb, kh) for sb in range(nsb) for kh in range(N_KV_HEADS)]
    pieces = {}
    s_next = scores(*items[0])
    pn_prev = None
    for i, (sb, kh) in enumerate(items):
        s_cur = s_next
        if i + 1 < len(items):
            s_next = scores(*items[i + 1])
        if pn_prev is not None:
            pieces[items[i - 1]] = values(pn_prev, *items[i - 1])
        pn_prev = weights(s_cur, kh)
    pieces[items[-1]] = values(pn_prev, *items[-1])
    for sb in range(nsb):
        row = [x for kh in range(N_KV_HEADS) for x in pieces[(sb, kh)]]
        o_ref[0, sb * BLOCK:(sb + 1) * BLOCK, :] = jnp.concatenate(row, axis=1).astype(BF16)


def _attention(q, k, v, bias, sink, tq):
    b, s, _ = q.shape
    nt = s // tq
    assert tq >= 2 * BLOCK
    kv = lambda f: pl.BlockSpec((1, tq, KV_WIDTH), f)
    prev = lambda i, j: (i, jnp.maximum(j - 1, 0), 0)
    cur = lambda i, j: (i, j, 0)
    nxt = lambda i, j: (i, jnp.minimum(j + 1, nt - 1), 0)
    return pl.pallas_call(
        functools.partial(_attn_body, tq=tq),
        grid=(b, nt),
        in_specs=[pl.BlockSpec(memory_space=pltpu.SMEM),
                  pl.BlockSpec((1, tq, ATTN_WIDTH), cur),
                  kv(prev), kv(cur), kv(nxt), kv(prev), kv(cur), kv(nxt),
                  pl.BlockSpec(bias.shape, lambda i, j: (0, 0, 0, 0))],
        out_specs=pl.BlockSpec((1, tq, ATTN_WIDTH), cur),
        out_shape=jax.ShapeDtypeStruct((b, s, ATTN_WIDTH), BF16),
        compiler_params=_params(("parallel", "parallel"), VMEM_LIMIT),
        name="attn",
    )(sink.astype(F32), q, k, k, k, v, v, v, bias)


def _outproj_body(x_ref, f_ref, a_ref, wo_ref, g_ref, wr_ref, x1_ref, h2_ref, aff_ref):
    mix = jnp.concatenate([f_ref[...], a_ref[...]], axis=1)
    x1 = x_ref[...] + jnp.dot(mix, wo_ref[...], preferred_element_type=F32)
    x1_ref[...] = x1
    ms = jnp.mean(x1 * x1, axis=-1, keepdims=True)
    h2 = (x1 * lax.rsqrt(ms + EPS)) * g_ref[...]
    _store_rows_as_tiles(h2_ref, h2)
    hi = h2.astype(BF16)
    lo = (h2 - hi.astype(F32)).astype(BF16)
    nt = (((1,), (1,)), ((), ()))
    o1 = lax.dot_general(wr_ref[...], hi, nt, preferred_element_type=F32)
    o2 = lax.dot_general(wr_ref[:N_EXPERTS], lo, nt, preferred_element_type=F32)
    logits = o1[:N_EXPERTS] + (o1[N_EXPERTS:] + o2)
    e = jnp.exp(logits - jnp.max(logits, axis=0, keepdims=True))
    aff_ref[...] = e / jnp.sum(e, axis=0, keepdims=True)


def _outproj(xf, f, a, w_out, gain, w_router_t, tm):
    n = xf.shape[0]
    row = lambda w: pl.BlockSpec((tm, w), lambda i: (i, 0))
    const = lambda shape: pl.BlockSpec(shape, lambda i: (0, 0))
    return pl.pallas_call(
        _outproj_body,
        grid=(n // tm,),
        in_specs=[row(D_MODEL), row(FOURIER_WIDTH), row(ATTN_WIDTH),
                  const((D_MODEL, D_MODEL)), const((1, D_MODEL)),
                  const((2 * N_EXPERTS, D_MODEL))],
        out_specs=[row(D_MODEL),
                   pl.BlockSpec((tm * ROW_TILE, LANES), lambda i: (i, 0)),
                   pl.BlockSpec((N_EXPERTS, tm), lambda i: (0, i))],
        out_shape=[jax.ShapeDtypeStruct((n, D_MODEL), F32),
                   jax.ShapeDtypeStruct((n * ROW_TILE, LANES), U32),
                   jax.ShapeDtypeStruct((N_EXPERTS, n), F32)],
        compiler_params=_params(("parallel",), VMEM_LIMIT),
        name="outproj",
    )(xf, f, a, w_out, gain, w_router_t)


def _route_body(aff_ref, idx_ref, pos_ref, gate_ref, tok_ref,
                d_a, d_b, n_a, n_b, p_a, p_b, g_a, g_b, *, n, cap):
    e = N_EXPERTS
    ch = PREFIX_CHUNK

    def search(i, tau):
        cand = tau | jnp.left_shift(jnp.int32(1), 30 - i)
        keys = lax.bitcast_convert_type(aff_ref[...], I32)
        cnt = jnp.sum((keys >= cand).astype(I32), axis=1, keepdims=True)
        return jnp.where(cnt >= cap, cand, tau)

    tau = lax.fori_loop(0, 31, search, jnp.zeros((e, 1), I32))
    keys = lax.bitcast_convert_type(aff_ref[...], I32)
    n_gt = jnp.sum((keys > tau).astype(I32), axis=1, keepdims=True)
    need = (cap - n_gt).astype(F32)

    r = lax.broadcasted_iota(I32, (ch, ch), 0)
    c = lax.broadcasted_iota(I32, (ch, ch), 1)
    upper = (r < c).astype(BF16)
    r16 = lax.broadcasted_iota(I32, (e, e), 0)
    c16 = lax.broadcasted_iota(I32, (e, e), 1)
    lower = (c16 < r16).astype(BF16)

    carry_eq = jnp.zeros((e, 1), F32)
    carry_m = jnp.zeros((2 * e, 1), F32)
    for ci in range(n // ch):
        sl = slice(ci * ch, (ci + 1) * ch)
        k = lax.bitcast_convert_type(aff_ref[:, sl], I32)
        gt = k > tau
        eq = (k == tau).astype(F32)
        eq_ex = jnp.dot(eq.astype(BF16), upper, preferred_element_type=F32) + carry_eq
        carry_eq = carry_eq + jnp.sum(eq, axis=1, keepdims=True)
        m = jnp.where(gt | ((eq > 0.0) & (eq_ex < need)), 1.0, 0.0).astype(F32)
        cnt = jnp.sum(m, axis=0, keepdims=True)
        stacked = jnp.concatenate([m, cnt, jnp.zeros((e - 1, ch), F32)], axis=0)
        pre = jnp.dot(stacked.astype(BF16), upper, preferred_element_type=F32) + carry_m
        carry_m = carry_m + jnp.sum(stacked, axis=1, keepdims=True)
        rank = pre[:e]
        tokoff = pre[e:e + 1]
        ex_e = jnp.dot(lower, m.astype(BF16), preferred_element_type=F32)
        tok_ref[0:1, sl] = tokoff.astype(I32)
        tok_ref[1:2, sl] = (tokoff + cnt).astype(I32)
        lane = lax.broadcasted_iota(I32, (e, ch), 1) + ci * ch
        d_a[:, sl] = jnp.where(m > 0.0, lane - rank.astype(I32), 0)
        n_a[:, sl] = lane
        p_a[:, sl] = (tokoff + ex_e).astype(I32)
    g_a[...] = aff_ref[...]

    bufs = [(d_a, n_a, p_a, g_a), (d_b, n_b, p_b, g_b)]
    for b in range(int(math.log2(n))):
        src, dst = bufs[b % 2], bufs[(b + 1) % 2]
        sh = 1 << b
        d = src[0][...]
        moving = (lax.shift_right_logical(d, b) & 1) == 1
        d_in = pltpu.roll(d, n - sh, axis=1)
        arrive = (lax.shift_right_logical(d_in, b) & 1) == 1
        dst[0][...] = jnp.where(arrive, d_in, jnp.where(moving, 0, d))
        for t in (1, 2, 3):
            v = src[t][...]
            dst[t][...] = jnp.where(arrive, pltpu.roll(v, n - sh, axis=1), v)
    fin = bufs[int(math.log2(n)) % 2]
    idx_ref[...] = fin[1][:, :cap]
    pos_ref[...] = fin[2][:, :cap]
    gate_ref[...] = fin[3][:, :cap]


def _route(aff_t, cap):
    e, n = aff_t.shape
    assert n & (n - 1) == 0 and n % PREFIX_CHUNK == 0
    full = lambda shape: pl.BlockSpec(shape, lambda: (0, 0))
    big_i = pltpu.VMEM((e, n), I32)
    return pl.pallas_call(
        functools.partial(_route_body, n=n, cap=cap),
        in_specs=[full((e, n))],
        out_specs=[full((e, cap)), full((e, cap)), full((e, cap)), full((2, n))],
        out_shape=[jax.ShapeDtypeStruct((e, cap), I32),
                   jax.ShapeDtypeStruct((e, cap), I32),
                   jax.ShapeDtypeStruct((e, cap), F32),
                   jax.ShapeDtypeStruct((2, n), I32)],
        scratch_shapes=[big_i, big_i, big_i, big_i, big_i, big_i,
                        pltpu.VMEM((e, n), F32), pltpu.VMEM((e, n), F32)],
        compiler_params=_params(None, VMEM_LIMIT),
        name="route",
    )(aff_t)


def _rows_to_cols(rows):
    r, m = rows.shape
    if r < 8:
        rows = jnp.concatenate([rows, jnp.zeros((8 - r, m), rows.dtype)], axis=0)
    return rows.T[:, :r]


def _ffn_body(idx_cur, idx_nxt, pos_prv, pos_cur, gate_ref, h_hbm, wg_ref, wu_ref, wd_ref, ys_hbm,
              xa, xb, ya, yb, wg_s, wu_s, wd_s, sems, *, tm, spe):
    s = pl.program_id(0)
    last = pl.num_programs(0) - 1
    g_a, g_b, s_a, s_b = sems.at[0], sems.at[1], sems.at[2], sems.at[3]

    def gather(idx_ref, t, buf, sem):
        def one(i):
            pltpu.make_async_copy(_tile_of_row(h_hbm, idx_ref[t, 0, i]),
                                  _tile_of_row(buf, i), sem).start()
        return one

    def scatter(pos_ref, t, buf, sem):
        def one(i):
            pltpu.make_async_copy(_tile_of_row(buf, i),
                                  _tile_of_row(ys_hbm, pos_ref[t, 0, i]), sem).start()
        return one

    def wait_gather(buf, sem):
        pltpu.make_async_copy(h_hbm.at[pl.ds(0, tm * ROW_TILE), :], buf, sem).wait()

    def wait_scatter(buf, sem):
        pltpu.make_async_copy(buf, ys_hbm.at[pl.ds(0, tm * ROW_TILE), :], sem).wait()

    def compute(xbuf, ybuf, t, copies):
        nc = D_EXPERT // FFN_CHUNK
        stages = 3 * nc
        todo = [(fn, i) for fn in copies for i in range(tm)]
        per_stage = -(-len(todo) // stages)

        def issue_some():
            for fn, i in todo[:per_stage]:
                fn(i)
            del todo[:per_stage]

        x = _load_tiles_as_rows(xbuf)
        cols = lambda c: slice(c * FFN_CHUNK, (c + 1) * FFN_CHUNK)

        def up(c):
            g = jnp.dot(x, wg_s[:, cols(c)], preferred_element_type=F32)
            issue_some()
            u = jnp.dot(x, wu_s[:, cols(c)], preferred_element_type=F32)
            issue_some()
            return g, u

        def act(gu):
            g, u = gu
            return (g * jax.nn.sigmoid(g) * u).astype(BF16)

        def down(hid, c):
            part = jnp.dot(hid, wd_s[cols(c), :], preferred_element_type=F32)
            issue_some()
            return part

        gu_next = up(0)
        hid_prev = None
        y = None
        for c in range(nc):
            gu_cur = gu_next
            if c + 1 < nc:
                gu_next = up(c + 1)
            if hid_prev is not None:
                part = down(hid_prev, c - 1)
                y = part if y is None else y + part
            hid_prev = act(gu_cur)
        y = y + down(hid_prev, nc - 1)
        assert not todo
        y = y * _rows_to_cols(gate_ref[t])
        _store_rows_as_tiles(ybuf, y)

    @pl.when(s % spe == 0)
    def _():
        rows = 256
        for w_ref, w_s in ((wg_ref, wg_s), (wu_ref, wu_s), (wd_ref, wd_s)):
            for r in range(0, w_s.shape[0], rows):
                w_s[r:r + rows, :] = w_ref[0, r:r + rows, :].astype(BF16)

    @pl.when(s == 0)
    def _():
        pl.loop(0, tm)(gather(idx_cur, 0, xa, g_a))
        yb[...] = jnp.zeros_like(yb)

    wait_gather(xa, g_a)

    @pl.when(s > 0)
    def _():
        wait_scatter(ya, s_a)

    compute(xa, ya, 0, [gather(idx_cur, 1, xb, g_b), scatter(pos_prv, 1, yb, s_b)])

    wait_gather(xb, g_b)
    wait_scatter(yb, s_b)
    compute(xb, yb, 1, [gather(idx_nxt, 0, xa, g_a), scatter(pos_cur, 0, ya, s_a)])

    @pl.when(s == last)
    def _():
        pl.loop(0, tm)(scatter(pos_cur, 1, yb, s_b))
        wait_gather(xa, g_a)
        wait_scatter(ya, s_a)
        wait_scatter(yb, s_b)


def _ffn(idx_c, pos_c, gate_c, h2, wg, wu, wd, tm):
    e, cap = idx_c.shape
    spe = cap // (2 * tm)
    steps = e * spe
    n_slots = e * cap
    idx3 = idx_c.reshape(2 * steps, 1, tm)
    pos3 = pos_c.reshape(2 * steps, 1, tm)
    gate3 = gate_c.reshape(2 * steps, 1, tm)
    cur = lambda i: (i, 0, 0)
    prv = lambda i: (jnp.maximum(i - 1, 0), 0, 0)
    nxt = lambda i: (jnp.minimum(i + 1, steps - 1), 0, 0)
    smem = lambda f: pl.BlockSpec((2, 1, tm), f, memory_space=pltpu.SMEM)
    wspec = lambda a, b: pl.BlockSpec((1, a, b), lambda i: (i // spe, 0, 0))
    buf = pltpu.VMEM((tm * ROW_TILE, LANES), U32)
    return pl.pallas_call(
        functools.partial(_ffn_body, tm=tm, spe=spe),
        grid=(steps,),
        in_specs=[smem(cur), smem(nxt), smem(prv), smem(cur),
                  pl.BlockSpec((2, 1, tm), cur),
                  pl.BlockSpec(memory_space=pl.ANY),
                  wspec(D_MODEL, D_EXPERT), wspec(D_MODEL, D_EXPERT), wspec(D_EXPERT, D_MODEL)],
        out_specs=pl.BlockSpec(memory_space=pl.ANY),
        out_shape=jax.ShapeDtypeStruct((n_slots * ROW_TILE, LANES), U32),
        scratch_shapes=[buf, buf, buf, buf,
                        pltpu.VMEM((D_MODEL, D_EXPERT), BF16), pltpu.VMEM((D_MODEL, D_EXPERT), BF16),
                        pltpu.VMEM((D_EXPERT, D_MODEL), BF16), pltpu.SemaphoreType.DMA((4,))],
        compiler_params=_params(("arbitrary",), VMEM_LIMIT),
        name="ffn",
    )(idx3, idx3, pos3, pos3, gate3, h2, wg, wu, wd)


def _combine_body(toff_ref, x1_ref, tok_ref, g_ref, ys_hbm, o_ref, win, extra, sems,
                  *, tt, ws, n_slots):
    i = pl.program_id(0)
    last = pl.num_programs(0) - 1
    buf = i % 2

    def window_start(nominal):
        return pl.multiple_of(jnp.minimum(nominal, n_slots - ws), 8)

    def first_fetch(tile, b):
        start = window_start((toff_ref[tile] // 8) * 8)
        return pltpu.make_async_copy(_tiles_of_rows(ys_hbm, start, ws), win.at[b], sems.at[b])

    @pl.when(i == 0)
    def _():
        first_fetch(0, 0).start()

    first_fetch(i, buf).wait()

    @pl.when(i < last)
    def _():
        first_fetch(i + 1, 1 - buf).start()

    hi = toff_ref[i + 1]
    base = (toff_ref[i] // 8) * 8
    nwin = (hi - base + ws - 1) // ws
    tok = _rows_to_cols(tok_ref[...])
    t_lo = tok[:, 0:1]
    as_unsigned = lambda v: lax.bitcast_convert_type(v, jnp.uint32)
    t_cnt = as_unsigned(tok[:, 1:2] - t_lo)
    lane = lax.broadcasted_iota(I32, (tt, ws), 1)

    def selector(start, nominal):
        owned = as_unsigned(lane + (start - t_lo)) < t_cnt
        if nominal is not None:
            owned = owned & (lane >= nominal - start)
        return jnp.where(owned, 1.0, 0.0).astype(BF16)

    def finish(x2_cols):
        ssq = sum(jnp.sum(x * x, axis=-1, keepdims=True) for x in x2_cols)
        scale = lax.rsqrt(ssq * (1.0 / D_MODEL) + EPS)
        for c, x in enumerate(x2_cols):
            w = x.shape[1]
            o_ref[:, c * w:(c + 1) * w] = (x * scale) * g_ref[:, c * w:(c + 1) * w]

    first = win.at[buf]
    sel = selector(window_start(base), None)
    cw = 2 * LANES

    def window_cols(c):
        words = jnp.concatenate(
            [first[pl.ds(2 * (c % 2) + k, ws, stride=ROW_TILE), :] for k in range(2)], axis=1)
        return _unpack_words(words, c >= 2)

    x2_cols = []
    rows_next = window_cols(0)
    for c in range(D_MODEL // cw):
        rows_cur = rows_next
        if (c + 1) * cw < D_MODEL:
            rows_next = window_cols(c + 1)
        x2_cols.append(x1_ref[:, c * cw:(c + 1) * cw]
                       + jnp.dot(sel, rows_cur, preferred_element_type=F32))
    finish(x2_cols)

    @pl.when(nwin > 1)
    def _():
        def later_window(w, acc):
            nominal = base + w * ws
            start = window_start(nominal)
            cp = pltpu.make_async_copy(_tiles_of_rows(ys_hbm, start, ws), extra, sems.at[2])
            cp.start()
            cp.wait()
            rows = _load_tiles_as_rows(extra)
            return acc + jnp.dot(selector(start, nominal), rows, preferred_element_type=F32)

        moe = jnp.dot(sel, _load_tiles_as_rows(first), preferred_element_type=F32)
        moe = lax.fori_loop(1, nwin, later_window, moe)
        finish([x1_ref[...] + moe])


def _combine(x1, ys, tokinfo, gain, tt, ws):
    n = x1.shape[0]
    n_slots = ys.shape[0] // ROW_TILE
    tile_off = jnp.concatenate([tokinfo[0, ::tt], jnp.full((1,), n_slots, I32)])
    return pl.pallas_call(
        functools.partial(_combine_body, tt=tt, ws=ws, n_slots=n_slots),
        grid_spec=pltpu.PrefetchScalarGridSpec(
            num_scalar_prefetch=1,
            grid=(n // tt,),
            in_specs=[pl.BlockSpec((tt, D_MODEL), lambda i, s: (i, 0)),
                      pl.BlockSpec((2, tt), lambda i, s: (0, i)),
                      pl.BlockSpec((1, D_MODEL), lambda i, s: (0, 0)),
                      pl.BlockSpec(memory_space=pl.ANY)],
            out_specs=pl.BlockSpec((tt, D_MODEL), lambda i, s: (i, 0)),
            scratch_shapes=[pltpu.VMEM((2, ws * ROW_TILE, LANES), U32),
                            pltpu.VMEM((ws * ROW_TILE, LANES), U32),
                            pltpu.SemaphoreType.DMA((3,))]),
        out_shape=jax.ShapeDtypeStruct((n, D_MODEL), F32),
        compiler_params=_params(("arbitrary",), VMEM_LIMIT),
        name="combine",
    )(tile_off, x1, tokinfo, gain, ys)


def _tile(n, pref):
    t = pref
    while n % t:
        t //= 2
    return t


def _trunk(x, p):
    b, s, d = x.shape
    n = b * s
    cap = CAPACITY_FACTOR * n // N_EXPERTS
    xf = x.reshape(n, d)
    tm = _tile(n, 512)
    zf, q, k, v = _inproj(xf, p["norm_mix"], p["w_in"], tm, s)
    f = _fourier(zf, p["w_fourier"])
    a = _attention(q.reshape(b, s, ATTN_WIDTH), k.reshape(b, s, KV_WIDTH),
                   v.reshape(b, s, KV_WIDTH), p["bias"], p["sink"], _tile(s, 512))
    x1, h2, aff_t = _outproj(xf, f.reshape(n, FOURIER_WIDTH), a.reshape(n, ATTN_WIDTH),
                             p["w_out"], p["norm_ffn"], p["w_router"], tm)
    idx_c, pos_c, gate_c, tokinfo = _route(aff_t, cap)
    ys = _ffn(idx_c, pos_c, gate_c, h2, p["w_gate"], p["w_up"], p["w_down"], _tile(cap // 2, 512))
    tt = _tile(n, 512)
    ws = min(CAPACITY_FACTOR * tt + tt // 4, N_EXPERTS * cap)
    y = _combine(x1, ys, tokinfo, p["norm_final"], tt, ws)
    return y.reshape(b, s, d)


def kernel(x_prompt, x_sample, w_in, w_fourier, attn_sink, rel_bias, w_out, norm_mix, norm_ffn,
           w_router, w_gate, w_up, w_down, norm_final):
    assert w_in.shape[0] == 1
    wr = w_router[0].T
    wr_hi = wr.astype(BF16)
    wr_lo = (wr - wr_hi.astype(F32)).astype(BF16)
    p = dict(
        norm_mix=norm_mix[0].reshape(1, D_MODEL),
        norm_ffn=norm_ffn[0].reshape(1, D_MODEL),
        norm_final=norm_final.reshape(1, D_MODEL),
        w_in=w_in[0].astype(BF16),
        w_fourier=w_fourier[0].astype(BF16),
        w_out=w_out[0].astype(BF16),
        w_router=jnp.concatenate([wr_hi, wr_lo], axis=0),
        w_gate=w_gate[0], w_up=w_up[0], w_down=w_down[0],
        sink=attn_sink[0],
        bias=_bias_table(rel_bias),
    )
    return _trunk(x_prompt, p), _trunk(x_sample, p)
```

```python
import functools
import math

import numpy as np
import jax
import jax.numpy as jnp
from jax import lax
from jax.experimental import pallas as pl
from jax.experimental.pallas import tpu as pltpu

F32 = jnp.float32
BF16 = jnp.bfloat16
I32 = jnp.int32

D_MODEL = 1024
FOURIER_GROUPS = 4
GROUP_DIM = 128
FOURIER_WIDTH = FOURIER_GROUPS * GROUP_DIM
N_HEADS = 8
N_KV_HEADS = 2
HEAD_DIM = 64
ATTN_WIDTH = N_HEADS * HEAD_DIM
KV_WIDTH = N_KV_HEADS * HEAD_DIM
WINDOW = 128
BLOCK = 128
N_BUCKETS = 32
MAX_DISTANCE = 128
IN_WIDTH = FOURIER_WIDTH + ATTN_WIDTH + 2 * KV_WIDTH
N_EXPERTS = 16
CAPACITY_FACTOR = 2
D_EXPERT = 1024
EPS = 1e-6

LANES = 128
PREFIX_CHUNK = 256
FFT_S2_BLOCK = 16
FFT_ROW_GROUP = 8
STRIP = 32
FFN_CHUNK = 256
VMEM_LIMIT = 56 * 1024 * 1024


U32 = jnp.uint32
HALF = D_MODEL // 2
ROW_TILE = HALF // LANES
HIGH_HALF = np.uint32(0xFFFF0000)


def _params(sem, vmem=None):
    return pltpu.CompilerParams(dimension_semantics=sem, vmem_limit_bytes=vmem)


def _store_rows_as_tiles(ref, val):
    m = val.shape[0]
    bits = lax.bitcast_convert_type(val.astype(BF16).astype(F32), U32)
    words = (bits[:, :HALF] >> 16) | (bits[:, HALF:] & HIGH_HALF)
    for c in range(ROW_TILE):
        ref[pl.ds(c, m, stride=ROW_TILE), :] = words[:, c * LANES:(c + 1) * LANES]


def _unpack_words(words, high):
    bits = (words & HIGH_HALF) if high else (words << 16)
    return lax.bitcast_convert_type(bits, F32).astype(BF16)


def _load_tiles_as_rows(ref):
    m = ref.shape[0] // ROW_TILE
    words = jnp.concatenate(
        [ref[pl.ds(c, m, stride=ROW_TILE), :] for c in range(ROW_TILE)], axis=1)
    return jnp.concatenate([_unpack_words(words, False), _unpack_words(words, True)], axis=1)


def _tiles_of_rows(ref, r, count):
    return ref.at[pl.ds(pl.multiple_of(r * ROW_TILE, ROW_TILE), count * ROW_TILE), :]


def _tile_of_row(ref, r):
    start = r * ROW_TILE
    if not isinstance(r, int):
        start = pl.multiple_of(start, ROW_TILE)
    return ref.at[pl.ds(start, ROW_TILE), :]


def _inproj_body(x_ref, g_ref, w_ref, zf_ref, q_ref, k_ref, v_ref):
    x = x_ref[...]
    ms = jnp.mean(x * x, axis=-1, keepdims=True)
    h = (x * lax.rsqrt(ms + EPS)) * g_ref[...]
    z = jnp.dot(h.astype(BF16), w_ref[...], preferred_element_type=F32)
    o = FOURIER_WIDTH
    g = FFT_ROW_GROUP
    for s1 in range(x.shape[0] // BLOCK):
        for jb in range(BLOCK // g):
            r = s1 * BLOCK + jb * g
            zf_ref[0, jb, s1 * g:(s1 + 1) * g, :] = z[r:r + g, :o]
    q_ref[...] = (z[:, o:o + ATTN_WIDTH] * (HEAD_DIM ** -0.5)).astype(BF16)
    o += ATTN_WIDTH
    k_ref[...] = z[:, o:o + KV_WIDTH].astype(BF16)
    o += KV_WIDTH
    v_ref[...] = z[:, o:o + KV_WIDTH].astype(BF16)


def _inproj(xf, gain, w_in, tm, seq):
    n = xf.shape[0]
    assert tm % BLOCK == 0 and seq % tm == 0
    tiles = seq // tm
    s1_rows = tm // BLOCK * FFT_ROW_GROUP
    nblk = BLOCK // FFT_ROW_GROUP
    row = lambda w: pl.BlockSpec((tm, w), lambda i: (i, 0))
    return pl.pallas_call(
        _inproj_body,
        grid=(n // tm,),
        in_specs=[row(D_MODEL),
                  pl.BlockSpec((1, D_MODEL), lambda i: (0, 0)),
                  pl.BlockSpec((D_MODEL, IN_WIDTH), lambda i: (0, 0))],
        out_specs=[pl.BlockSpec((1, nblk, s1_rows, FOURIER_WIDTH),
                                lambda i: (i // tiles, 0, i % tiles, 0)),
                   row(ATTN_WIDTH), row(KV_WIDTH), row(KV_WIDTH)],
        out_shape=[jax.ShapeDtypeStruct((n // seq, nblk, seq // BLOCK * FFT_ROW_GROUP,
                                         FOURIER_WIDTH), F32),
                   jax.ShapeDtypeStruct((n, ATTN_WIDTH), BF16),
                   jax.ShapeDtypeStruct((n, KV_WIDTH), BF16),
                   jax.ShapeDtypeStruct((n, KV_WIDTH), BF16)],
        compiler_params=_params(("parallel",), VMEM_LIMIT),
        name="inproj",
    )(xf, gain, w_in)


def _dft_tables(s):
    s2 = BLOCK
    s1 = s // s2
    a = np.arange(s1, dtype=np.float64)
    ang1 = 2.0 * np.pi * np.outer(a, a) / s1
    f1 = np.concatenate([np.cos(ang1), -np.sin(ang1)], axis=0)
    b = np.arange(s2, dtype=np.float64)
    ang2 = 2.0 * np.pi * np.outer(b, b) / s2
    angt = 2.0 * np.pi * np.outer(a, b) / s
    c = np.arange(GROUP_DIM, dtype=np.float64)
    angc = 2.0 * np.pi * np.outer(c, c) / GROUP_DIM
    cs = np.concatenate([np.cos(angc), np.sin(angc)], axis=0)
    return dict(
        f1=jnp.asarray(f1, BF16),
        f2r=jnp.asarray(np.cos(ang2), F32), f2i=jnp.asarray(-np.sin(ang2), F32),
        twr=jnp.asarray(np.cos(angt), F32), twi=jnp.asarray(-np.sin(angt), F32),
        cs=jnp.asarray(cs, BF16))


def _dft1_body(f_ref, *refs, nb):
    x_refs, o_ref, stage = refs[:-2], refs[-2], refs[-1]
    g = FFT_ROW_GROUP
    s1 = x_refs[0].shape[2] // g
    per_dot = 4
    for j0 in range(0, nb, per_dot):
        x = jnp.concatenate(
            [ref[0, j // g, pl.ds(j % g, s1, stride=g), :]
             for j in range(j0, j0 + per_dot) for ref in x_refs],
            axis=1).astype(BF16)
        res = jnp.dot(f_ref[...], x, preferred_element_type=F32)
        w = FOURIER_WIDTH
        for jj in range(per_dot):
            stage[0, :, j0 + jj, :] = res[:s1, jj * w:(jj + 1) * w]
            stage[1, :, j0 + jj, :] = res[s1:, jj * w:(jj + 1) * w]
    o_ref[0] = stage[...].astype(BF16)


def _dft1(zf4, f1):
    nb, g = FFT_S2_BLOCK, FFT_ROW_GROUP
    b, s2_groups, rows, w = zf4.shape
    s1, s2 = rows // g, s2_groups * g
    lane_block = lambda k: pl.BlockSpec((1, nb // g, rows, LANES), lambda i, j: (i, j, 0, k))
    nblk = w // LANES
    return pl.pallas_call(
        functools.partial(_dft1_body, nb=nb),
        grid=(b, s2 // nb),
        in_specs=[pl.BlockSpec((2 * s1, s1), lambda i, j: (0, 0))]
                 + [lane_block(k) for k in range(nblk)],
        out_specs=pl.BlockSpec((1, 2, s1, nb, w), lambda i, j: (i, 0, 0, j, 0)),
        out_shape=jax.ShapeDtypeStruct((b, 2, s1, s2, w), BF16),
        scratch_shapes=[pltpu.VMEM((2, s1, nb, w), F32)],
        compiler_params=_params(("parallel", "parallel"), VMEM_LIMIT),
        name="dft1",
    )(f1, *([zf4] * nblk))


def _dft2_body(a_ref, f2r_ref, f2i_ref, twr_ref, twi_ref, cs_ref, wf_ref, o_ref, stage,
               *, kb, scale):
    s2 = BLOCK
    fr = f2r_ref[...]
    fi = f2i_ref[...]
    ys = []
    for j in range(kb):
        tr = twr_ref[0, j:j + 1, :]
        ti = twi_ref[0, j:j + 1, :]
        gr = fr * tr - fi * ti
        gi = fr * ti + fi * tr
        lhs = jnp.concatenate(
            [jnp.concatenate([gr, -gi], axis=1), jnp.concatenate([gi, gr], axis=1)],
            axis=0).astype(BF16)
        rhs = jnp.concatenate([a_ref[0, 0, j], a_ref[0, 1, j]], axis=0)
        ys.append(jnp.dot(lhs, rhs, preferred_element_type=F32))
    cat = jnp.concatenate(
        [jnp.concatenate([y[:s2, g * GROUP_DIM:(g + 1) * GROUP_DIM],
                          y[s2:, g * GROUP_DIM:(g + 1) * GROUP_DIM]], axis=1)
         for g in range(FOURIER_GROUPS) for y in ys], axis=0).astype(BF16)
    re = (jnp.dot(cat, cs_ref[...], preferred_element_type=F32) * scale).astype(BF16)
    rows = kb * s2
    outs = [jnp.dot(re[g * rows:(g + 1) * rows], wf_ref[g], preferred_element_type=F32)
            for g in range(FOURIER_GROUPS)]
    for j in range(kb):
        stage[:, j, :] = jnp.concatenate(
            [o[j * s2:(j + 1) * s2] for o in outs], axis=1)
    o_ref[0] = stage[...].astype(BF16)


def _dft2(a5, tabs, wf, kb, scale):
    b, _, s1, s2, w = a5.shape
    const = lambda shape: pl.BlockSpec(shape, lambda i, j: (0,) * len(shape))
    twr = tabs["twr"].reshape(s1 // kb, kb, s2)
    twi = tabs["twi"].reshape(s1 // kb, kb, s2)
    return pl.pallas_call(
        functools.partial(_dft2_body, kb=kb, scale=scale),
        grid=(b, s1 // kb),
        in_specs=[pl.BlockSpec((1, 2, kb, s2, w), lambda i, j: (i, 0, j, 0, 0)),
                  const((s2, s2)), const((s2, s2)),
                  pl.BlockSpec((1, kb, s2), lambda i, j: (j, 0, 0)),
                  pl.BlockSpec((1, kb, s2), lambda i, j: (j, 0, 0)),
                  const((2 * GROUP_DIM, GROUP_DIM)),
                  const((FOURIER_GROUPS, GROUP_DIM, GROUP_DIM))],
        out_specs=pl.BlockSpec((1, s2, kb, w), lambda i, j: (i, 0, j, 0)),
        out_shape=jax.ShapeDtypeStruct((b, s2, s1, w), BF16),
        scratch_shapes=[pltpu.VMEM((s2, kb, w), F32)],
        compiler_params=_params(("parallel", "parallel"), VMEM_LIMIT),
        name="dft2",
    )(a5, tabs["f2r"], tabs["f2i"], twr, twi, tabs["cs"], wf)


def _fourier(zf, wf_bf16):
    b, s2_blocks, rows, w = zf.shape
    s = s2_blocks * rows
    tabs = _dft_tables(s)
    a = _dft1(zf, tabs["f1"])
    f = _dft2(a, tabs, wf_bf16, kb=16,
              scale=1.0 / math.sqrt(s * GROUP_DIM))
    return f.reshape(b, s, w)


def _bucket_table():
    qi = np.arange(BLOCK)[:, None]
    kj = np.arange(3 * BLOCK)[None, :]
    rel = kj - BLOCK - qi
    nb = N_BUCKETS // 2
    max_exact = nb // 2
    n = np.abs(rel)
    large = max_exact + np.floor(
        np.log(np.maximum(n, 1).astype(np.float64) / max_exact)
        / math.log(MAX_DISTANCE / max_exact) * (nb - max_exact) + 1e-9).astype(np.int64)
    large = np.minimum(large, nb - 1)
    bucket = np.where(rel > 0, nb, 0) + np.where(n < max_exact, n, large)
    return np.where(n <= WINDOW, bucket, -1).astype(np.int32)


def _bias_body(rb_ref, bucket_ref, o_ref):
    bk = bucket_ref[...]
    col = lax.broadcasted_iota(I32, bk.shape, 1)
    for h in range(N_HEADS):
        acc = jnp.full(bk.shape, -jnp.inf, F32)
        for b in range(N_BUCKETS):
            acc = jnp.where(bk == b, rb_ref[b, h], acc)
        o_ref[0, h] = acc
        o_ref[1, h] = jnp.where(col >= BLOCK, acc, -jnp.inf)
        o_ref[2, h] = jnp.where(col < 2 * BLOCK, acc, -jnp.inf)


def _bias_table(rel_bias):
    bucket = jnp.asarray(_bucket_table())
    shape = (3, N_HEADS) + bucket.shape
    return pl.pallas_call(
        _bias_body,
        in_specs=[pl.BlockSpec(memory_space=pltpu.SMEM),
                  pl.BlockSpec(bucket.shape, lambda: (0, 0))],
        out_specs=pl.BlockSpec(shape, lambda: (0, 0, 0, 0)),
        out_shape=jax.ShapeDtypeStruct(shape, F32),
        name="bias_table",
    )(rel_bias.astype(F32), bucket)


def _attn_body(sink_ref, q_ref, kp_ref, kc_ref, kn_ref, vp_ref, vc_ref, vn_ref, bias_ref,
               o_ref, *, tq):
    j = pl.program_id(1)
    last = pl.num_programs(1) - 1
    nsb = tq // BLOCK
    group = N_HEADS // N_KV_HEADS
    gw = group * HEAD_DIM
    kext = jnp.concatenate([kp_ref[0, tq - BLOCK:, :], kc_ref[0], kn_ref[0, :BLOCK, :]], axis=0)
    vext = jnp.concatenate([vp_ref[0, tq - BLOCK:, :], vc_ref[0], vn_ref[0, :BLOCK, :]], axis=0)
    low_half = lax.broadcasted_iota(I32, (1, KV_WIDTH), 1) < HEAD_DIM

    def per_kv_head(ext):
        x = ext.astype(F32)
        xr = pltpu.roll(x, HEAD_DIM, axis=1)
        return (jnp.where(low_half, x, xr).astype(BF16), jnp.where(low_half, xr, x).astype(BF16))

    krep = per_kv_head(kext)
    vrep = per_kv_head(vext)
    head_of_lane = lax.broadcasted_iota(I32, (1, gw), 1) // HEAD_DIM
    nt = (((1,), (1,)), ((), ()))
    def scores(sb, kh):
        rows = slice(sb * BLOCK, (sb + 3) * BLOCK)
        variant = 0
        if sb == 0:
            variant = jnp.where(j == 0, 1, 0)
        elif sb == nsb - 1:
            variant = jnp.where(j == last, 2, 0)
        qblk = q_ref[0, sb * BLOCK:(sb + 1) * BLOCK, kh * gw:(kh + 1) * gw]
        qm = jnp.concatenate(
            [jnp.where(head_of_lane == g, qblk, jnp.zeros_like(qblk)) for g in range(group)],
            axis=0)
        kb = jnp.concatenate([krep[kh][rows], krep[kh][rows]], axis=1)
        s = lax.dot_general(qm, kb, nt, preferred_element_type=F32)
        return s, variant

    def weights(sv, kh):
        s_all, variant = sv
        out = []
        for g in range(group):
            for r0 in range(0, BLOCK, STRIP):
                s = s_all[g * BLOCK + r0:g * BLOCK + r0 + STRIP]
                s = s + bias_ref[variant, kh * group + g, r0:r0 + STRIP, :]
                sink = sink_ref[kh * group + g]
                m = jnp.maximum(jnp.max(s, axis=-1, keepdims=True), sink)
                p = jnp.exp(s - m)
                denom = jnp.sum(p, axis=-1, keepdims=True) + jnp.exp(sink - m)
                out.append((p / denom).astype(BF16))
        return jnp.concatenate(out, axis=0)

    def values(pn, sb, kh):
        rows = slice(sb * BLOCK, (sb + 3) * BLOCK)
        o4 = jnp.dot(pn, vrep[kh][rows], preferred_element_type=F32)
        out = []
        for pair in range(group // 2):
            even = o4[(2 * pair) * BLOCK:(2 * pair + 1) * BLOCK]
            odd = o4[(2 * pair + 1) * BLOCK:(2 * pair + 2) * BLOCK]
            out.append(jnp.where(low_half, even, odd))
        return out

    items = [(sb, kh) for sb in range(nsb) for kh in range(N_KV_HEADS)]
    pieces = {}
    s_next = scores(*items[0])
    pn_prev = None
    for i, (sb, kh) in enumerate(items):
        s_cur = s_next
        if i + 1 < len(items):
            s_next = scores(*items[i + 1])
        if pn_prev is not None:
            pieces[items[i - 1]] = values(pn_prev, *items[i - 1])
        pn_prev = weights(s_cur, kh)
    pieces[items[-1]] = values(pn_prev, *items[-1])
    for sb in range(nsb):
        row = [x for kh in range(N_KV_HEADS) for x in pieces[(sb, kh)]]
        o_ref[0, sb * BLOCK:(sb + 1) * BLOCK, :] = jnp.concatenate(row, axis=1).astype(BF16)


def _attention(q, k, v, bias, sink, tq):
    b, s, _ = q.shape
    nt = s // tq
    assert tq >= 2 * BLOCK
    kv = lambda f: pl.BlockSpec((1, tq, KV_WIDTH), f)
    prev = lambda i, j: (i, jnp.maximum(j - 1, 0), 0)
    cur = lambda i, j: (i, j, 0)
    nxt = lambda i, j: (i, jnp.minimum(j + 1, nt - 1), 0)
    return pl.pallas_call(
        functools.partial(_attn_body, tq=tq),
        grid=(b, nt),
        in_specs=[pl.BlockSpec(memory_space=pltpu.SMEM),
                  pl.BlockSpec((1, tq, ATTN_WIDTH), cur),
                  kv(prev), kv(cur), kv(nxt), kv(prev), kv(cur), kv(nxt),
                  pl.BlockSpec(bias.shape, lambda i, j: (0, 0, 0, 0))],
        out_specs=pl.BlockSpec((1, tq, ATTN_WIDTH), cur),
        out_shape=jax.ShapeDtypeStruct((b, s, ATTN_WIDTH), BF16),
        compiler_params=_params(("parallel", "parallel"), VMEM_LIMIT),
        name="attn",
    )(sink.astype(F32), q, k, k, k, v, v, v, bias)


def _outproj_body(x_ref, f_ref, a_ref, wo_ref, g_ref, wr_ref, x1_ref, h2_ref, aff_ref):
    mix = jnp.concatenate([f_ref[...], a_ref[...]], axis=1)
    x1 = x_ref[...] + jnp.dot(mix, wo_ref[...], preferred_element_type=F32)
    x1_ref[...] = x1
    ms = jnp.mean(x1 * x1, axis=-1, keepdims=True)
    h2 = (x1 * lax.rsqrt(ms + EPS)) * g_ref[...]
    _store_rows_as_tiles(h2_ref, h2)
    hi = h2.astype(BF16)
    lo = (h2 - hi.astype(F32)).astype(BF16)
    nt = (((1,), (1,)), ((), ()))
    o1 = lax.dot_general(wr_ref[...], hi, nt, preferred_element_type=F32)
    o2 = lax.dot_general(wr_ref[:N_EXPERTS], lo, nt, preferred_element_type=F32)
    logits = o1[:N_EXPERTS] + (o1[N_EXPERTS:] + o2)
    e = jnp.exp(logits - jnp.max(logits, axis=0, keepdims=True))
    aff_ref[...] = e / jnp.sum(e, axis=0, keepdims=True)


def _outproj(xf, f, a, w_out, gain, w_router_t, tm):
    n = xf.shape[0]
    row = lambda w: pl.BlockSpec((tm, w), lambda i: (i, 0))
    const = lambda shape: pl.BlockSpec(shape, lambda i: (0, 0))
    return pl.pallas_call(
        _outproj_body,
        grid=(n // tm,),
        in_specs=[row(D_MODEL), row(FOURIER_WIDTH), row(ATTN_WIDTH),
                  const((D_MODEL, D_MODEL)), const((1, D_MODEL)),
                  const((2 * N_EXPERTS, D_MODEL))],
        out_specs=[row(D_MODEL),
                   pl.BlockSpec((tm * ROW_TILE, LANES), lambda i: (i, 0)),
                   pl.BlockSpec((N_EXPERTS, tm), lambda i: (0, i))],
        out_shape=[jax.ShapeDtypeStruct((n, D_MODEL), F32),
                   jax.ShapeDtypeStruct((n * ROW_TILE, LANES), U32),
                   jax.ShapeDtypeStruct((N_EXPERTS, n), F32)],
        compiler_params=_params(("parallel",), VMEM_LIMIT),
        name="outproj",
    )(xf, f, a, w_out, gain, w_router_t)


def _route_body(aff_ref, idx_ref, gate_ref, rank_ref, rsel_ref,
                d_a, d_b, n_a, n_b, g_a, g_b, *, n, cap):
    e = N_EXPERTS
    ch = PREFIX_CHUNK

    def search(i, tau):
        cand = tau | jnp.left_shift(jnp.int32(1), 30 - i)
        keys = lax.bitcast_convert_type(aff_ref[...], I32)
        cnt = jnp.sum((keys >= cand).astype(I32), axis=1, keepdims=True)
        return jnp.where(cnt >= cap, cand, tau)

    tau = lax.fori_loop(0, 31, search, jnp.zeros((e, 1), I32))
    keys = lax.bitcast_convert_type(aff_ref[...], I32)
    n_gt = jnp.sum((keys > tau).astype(I32), axis=1, keepdims=True)
    need = (cap - n_gt).astype(F32)

    r = lax.broadcasted_iota(I32, (ch, ch), 0)
    c = lax.broadcasted_iota(I32, (ch, ch), 1)
    upper = (r < c).astype(BF16)

    carry_eq = jnp.zeros((e, 1), F32)
    carry_m = jnp.zeros((e, 1), F32)
    for ci in range(n // ch):
        sl = slice(ci * ch, (ci + 1) * ch)
        k = lax.bitcast_convert_type(aff_ref[:, sl], I32)
        gt = k > tau
        eq = (k == tau).astype(F32)
        eq_ex = jnp.dot(eq.astype(BF16), upper, preferred_element_type=F32) + carry_eq
        carry_eq = carry_eq + jnp.sum(eq, axis=1, keepdims=True)
        m = jnp.where(gt | ((eq > 0.0) & (eq_ex < need)), 1.0, 0.0).astype(F32)
        rank = (jnp.dot(m.astype(BF16), upper, preferred_element_type=F32)
                + carry_m).astype(I32)
        carry_m = carry_m + jnp.sum(m, axis=1, keepdims=True)
        rank_ref[:, sl] = rank
        rsel_ref[:, sl] = jnp.where(m > 0.0, rank, -1)
        lane = lax.broadcasted_iota(I32, (e, ch), 1) + ci * ch
        d_a[:, sl] = jnp.where(m > 0.0, lane - rank, 0)
        n_a[:, sl] = lane
    g_a[...] = aff_ref[...]

    bufs = [(d_a, n_a, g_a), (d_b, n_b, g_b)]
    for b in range(int(math.log2(n))):
        src, dst = bufs[b % 2], bufs[(b + 1) % 2]
        sh = 1 << b
        d = src[0][...]
        moving = (lax.shift_right_logical(d, b) & 1) == 1
        d_in = pltpu.roll(d, n - sh, axis=1)
        arrive = (lax.shift_right_logical(d_in, b) & 1) == 1
        dst[0][...] = jnp.where(arrive, d_in, jnp.where(moving, 0, d))
        for t in (1, 2):
            v = src[t][...]
            dst[t][...] = jnp.where(arrive, pltpu.roll(v, n - sh, axis=1), v)
    fin = bufs[int(math.log2(n)) % 2]
    idx_ref[...] = fin[1][:, :cap]
    gate_ref[...] = fin[2][:, :cap]


def _route(aff_t, cap):
    e, n = aff_t.shape
    assert n & (n - 1) == 0 and n % PREFIX_CHUNK == 0
    full = lambda shape: pl.BlockSpec(shape, lambda: (0, 0))
    big_i = pltpu.VMEM((e, n), I32)
    return pl.pallas_call(
        functools.partial(_route_body, n=n, cap=cap),
        in_specs=[full((e, n))],
        out_specs=[full((e, cap)), full((e, cap)), full((e, n)), full((e, n))],
        out_shape=[jax.ShapeDtypeStruct((e, cap), I32),
                   jax.ShapeDtypeStruct((e, cap), F32),
                   jax.ShapeDtypeStruct((e, n), I32),
                   jax.ShapeDtypeStruct((e, n), I32)],
        scratch_shapes=[big_i, big_i, big_i, big_i,
                        pltpu.VMEM((e, n), F32), pltpu.VMEM((e, n), F32)],
        compiler_params=_params(None, VMEM_LIMIT),
        name="route",
    )(aff_t)


def _rows_to_cols(rows):
    r, m = rows.shape
    if r < 8:
        rows = jnp.concatenate([rows, jnp.zeros((8 - r, m), rows.dtype)], axis=0)
    return rows.T[:, :r]


def _ffn_body(idx_cur, idx_nxt, gate_ref, h_hbm, wg_ref, wu_ref, wd_ref, ye_ref,
              xa, xb, wg_s, wu_s, wd_s, sems, *, tm, spe):
    s = pl.program_id(0)
    last = pl.num_programs(0) - 1
    g_a, g_b = sems.at[0], sems.at[1]

    def gather(idx_ref, t, buf, sem):
        def one(i):
            pltpu.make_async_copy(_tile_of_row(h_hbm, idx_ref[t, 0, i]),
                                  _tile_of_row(buf, i), sem).start()
        return one

    def wait_gather(buf, sem):
        pltpu.make_async_copy(h_hbm.at[pl.ds(0, tm * ROW_TILE), :], buf, sem).wait()

    def compute(xbuf, t, copies):
        nc = D_EXPERT // FFN_CHUNK
        stages = 3 * nc
        todo = [(fn, i) for fn in copies for i in range(tm)]
        per_stage = -(-len(todo) // stages)

        def issue_some():
            for fn, i in todo[:per_stage]:
                fn(i)
            del todo[:per_stage]

        x = _load_tiles_as_rows(xbuf)
        cols = lambda c: slice(c * FFN_CHUNK, (c + 1) * FFN_CHUNK)

        def up(c):
            g = jnp.dot(x, wg_s[:, cols(c)], preferred_element_type=F32)
            issue_some()
            u = jnp.dot(x, wu_s[:, cols(c)], preferred_element_type=F32)
            issue_some()
            return g, u

        def act(gu):
            g, u = gu
            return (g * jax.nn.sigmoid(g) * u).astype(BF16)

        def down(hid, c):
            part = jnp.dot(hid, wd_s[cols(c), :], preferred_element_type=F32)
            issue_some()
            return part

        gu_next = up(0)
        hid_prev = None
        y = None
        for c in range(nc):
            gu_cur = gu_next
            if c + 1 < nc:
                gu_next = up(c + 1)
            if hid_prev is not None:
                part = down(hid_prev, c - 1)
                y = part if y is None else y + part
            hid_prev = act(gu_cur)
        y = y + down(hid_prev, nc - 1)
        assert not todo
        y = y * _rows_to_cols(gate_ref[t])
        _store_rows_as_tiles(ye_ref.at[pl.ds(t * tm * ROW_TILE, tm * ROW_TILE), :], y)

    @pl.when(s % spe == 0)
    def _():
        rows = 256
        for w_ref, w_s in ((wg_ref, wg_s), (wu_ref, wu_s), (wd_ref, wd_s)):
            for r in range(0, w_s.shape[0], rows):
                w_s[r:r + rows, :] = w_ref[0, r:r + rows, :].astype(BF16)

    @pl.when(s == 0)
    def _():
        pl.loop(0, tm)(gather(idx_cur, 0, xa, g_a))

    wait_gather(xa, g_a)
    compute(xa, 0, [gather(idx_cur, 1, xb, g_b)])

    wait_gather(xb, g_b)
    compute(xb, 1, [gather(idx_nxt, 0, xa, g_a)])

    @pl.when(s == last)
    def _():
        wait_gather(xa, g_a)


def _ffn(idx_c, gate_c, h2, wg, wu, wd, tm):
    e, cap = idx_c.shape
    spe = cap // (2 * tm)
    steps = e * spe
    idx3 = idx_c.reshape(2 * steps, 1, tm)
    gate3 = gate_c.reshape(2 * steps, 1, tm)
    cur = lambda i: (i, 0, 0)
    nxt = lambda i: (jnp.minimum(i + 1, steps - 1), 0, 0)
    smem = lambda f: pl.BlockSpec((2, 1, tm), f, memory_space=pltpu.SMEM)
    wspec = lambda a, b: pl.BlockSpec((1, a, b), lambda i: (i // spe, 0, 0))
    buf = pltpu.VMEM((tm * ROW_TILE, LANES), U32)
    return pl.pallas_call(
        functools.partial(_ffn_body, tm=tm, spe=spe),
        grid=(steps,),
        in_specs=[smem(cur), smem(nxt),
                  pl.BlockSpec((2, 1, tm), cur),
                  pl.BlockSpec(memory_space=pl.ANY),
                  wspec(D_MODEL, D_EXPERT), wspec(D_MODEL, D_EXPERT), wspec(D_EXPERT, D_MODEL)],
        out_specs=pl.BlockSpec((2 * tm * ROW_TILE, LANES), lambda i: (i, 0)),
        out_shape=jax.ShapeDtypeStruct((e * cap * ROW_TILE, LANES), U32),
        scratch_shapes=[buf, buf,
                        pltpu.VMEM((D_MODEL, D_EXPERT), BF16), pltpu.VMEM((D_MODEL, D_EXPERT), BF16),
                        pltpu.VMEM((D_EXPERT, D_MODEL), BF16), pltpu.SemaphoreType.DMA((2,))],
        compiler_params=_params(("arbitrary",), VMEM_LIMIT),
        name="ffn",
    )(idx3, idx3, gate3, h2, wg, wu, wd)


def _combine_body(st_ref, x1_ref, rsel_ref, g_ref, ye_hbm, o_ref, win, extra, sems,
                  *, tt, we, cap, n_tiles):
    i = pl.program_id(0)
    last = pl.num_programs(0) - 1
    buf = i % 2
    n_e = N_EXPERTS
    ws = n_e * we

    def run_start(e, tile):
        return st_ref[e * (n_tiles + 1) + tile]

    def clamp(nominal):
        return jnp.minimum(nominal, cap - we)

    def window_rows(e, start):
        return _tiles_of_rows(ye_hbm, e * cap + start, we)

    def first_fetch(tile, b, e):
        start = clamp((run_start(e, tile) // 8) * 8)
        dst = win.at[b, pl.ds(e * we * ROW_TILE, we * ROW_TILE), :]
        return pltpu.make_async_copy(window_rows(e, start), dst, sems.at[b])

    @pl.when(i == 0)
    def _():
        for e in range(n_e):
            first_fetch(0, 0, e).start()

    for e in range(n_e):
        first_fetch(i, buf, e).wait()

    @pl.when(i < last)
    def _():
        for e in range(n_e):
            first_fetch(i + 1, 1 - buf, e).start()

    rank_of = jnp.concatenate([_rows_to_cols(rsel_ref[0:8, :]),
                               _rows_to_cols(rsel_ref[8:n_e, :])], axis=1)
    lane = lax.broadcasted_iota(I32, (tt, we), 1)

    def selector(e, start, nominal):
        hit = (rank_of[:, e:e + 1] - start) == lane
        if nominal is not None:
            hit = hit & (lane >= nominal - start)
        return jnp.where(hit, 1.0, 0.0).astype(BF16)

    starts = [clamp((run_start(e, i) // 8) * 8) for e in range(n_e)]
    overflow = run_start(0, i + 1) - starts[0] > we
    for e in range(1, n_e):
        overflow = overflow | (run_start(e, i + 1) - starts[e] > we)

    def finish(x2_cols):
        ssq = sum(jnp.sum(x * x, axis=-1, keepdims=True) for x in x2_cols)
        scale = lax.rsqrt(ssq * (1.0 / D_MODEL) + EPS)
        for c, x in enumerate(x2_cols):
            w = x.shape[1]
            o_ref[:, c * w:(c + 1) * w] = (x * scale) * g_ref[:, c * w:(c + 1) * w]

    first = win.at[buf]
    sel = jnp.concatenate([selector(e, starts[e], None) for e in range(n_e)], axis=1)
    cw = 2 * LANES

    def window_cols(c):
        words = jnp.concatenate(
            [first[pl.ds(2 * (c % 2) + k, ws, stride=ROW_TILE), :] for k in range(2)], axis=1)
        return _unpack_words(words, c >= 2)

    x2_cols = []
    rows_next = window_cols(0)
    for c in range(D_MODEL // cw):
        rows_cur = rows_next
        if (c + 1) * cw < D_MODEL:
            rows_next = window_cols(c + 1)
        x2_cols.append(x1_ref[:, c * cw:(c + 1) * cw]
                       + jnp.dot(sel, rows_cur, preferred_element_type=F32))
    finish(x2_cols)

    @pl.when(overflow)
    def _():
        moe = jnp.zeros((tt, D_MODEL), F32)
        for e in range(n_e):
            base = (run_start(e, i) // 8) * 8
            nwin = (run_start(e, i + 1) - base + we - 1) // we

            def window(w, acc, e=e, base=base):
                nominal = base + w * we
                start = clamp(nominal)
                cp = pltpu.make_async_copy(window_rows(e, start), extra, sems.at[2])
                cp.start()
                cp.wait()
                return acc + jnp.dot(selector(e, start, nominal), _load_tiles_as_rows(extra),
                                     preferred_element_type=F32)

            moe = lax.fori_loop(0, nwin, window, moe)
        finish([x1_ref[...] + moe])


def _combine(x1, ye, rank, rsel, gain, tt, we):
    n = x1.shape[0]
    n_e = rank.shape[0]
    cap = ye.shape[0] // ROW_TILE // n_e
    n_tiles = n // tt
    assert cap >= we and cap % 8 == 0
    run_starts = jnp.concatenate([rank[:, ::tt], jnp.full((n_e, 1), cap, I32)], axis=1)
    return pl.pallas_call(
        functools.partial(_combine_body, tt=tt, we=we, cap=cap, n_tiles=n_tiles),
        grid_spec=pltpu.PrefetchScalarGridSpec(
            num_scalar_prefetch=1,
            grid=(n_tiles,),
            in_specs=[pl.BlockSpec((tt, D_MODEL), lambda i, s: (i, 0)),
                      pl.BlockSpec((n_e, tt), lambda i, s: (0, i)),
                      pl.BlockSpec((1, D_MODEL), lambda i, s: (0, 0)),
                      pl.BlockSpec(memory_space=pl.ANY)],
            out_specs=pl.BlockSpec((tt, D_MODEL), lambda i, s: (i, 0)),
            scratch_shapes=[pltpu.VMEM((2, n_e * we * ROW_TILE, LANES), U32),
                            pltpu.VMEM((we * ROW_TILE, LANES), U32),
                            pltpu.SemaphoreType.DMA((3,))]),
        out_shape=jax.ShapeDtypeStruct((n, D_MODEL), F32),
        compiler_params=_params(("arbitrary",), VMEM_LIMIT),
        name="combine",
    )(run_starts.reshape(-1), x1, rsel, gain, ye)


def _tile(n, pref):
    t = pref
    while n % t:
        t //= 2
    return t


def _trunk(x, p):
    b, s, d = x.shape
    n = b * s
    cap = CAPACITY_FACTOR * n // N_EXPERTS
    xf = x.reshape(n, d)
    tm = _tile(n, 512)
    zf, q, k, v = _inproj(xf, p["norm_mix"], p["w_in"], tm, s)
    f = _fourier(zf, p["w_fourier"])
    a = _attention(q.reshape(b, s, ATTN_WIDTH), k.reshape(b, s, KV_WIDTH),
                   v.reshape(b, s, KV_WIDTH), p["bias"], p["sink"], _tile(s, 512))
    x1, h2, aff_t = _outproj(xf, f.reshape(n, FOURIER_WIDTH), a.reshape(n, ATTN_WIDTH),
                             p["w_out"], p["norm_ffn"], p["w_router"], tm)
    idx_c, gate_c, rank, rsel = _route(aff_t, cap)
    ye = _ffn(idx_c, gate_c, h2, p["w_gate"], p["w_up"], p["w_down"], _tile(cap // 2, 512))
    tt = _tile(n, 512)
    we = max(LANES, 2 * CAPACITY_FACTOR * tt // N_EXPERTS)
    y = _combine(x1, ye, rank, rsel, p["norm_final"], tt, we)
    return y.reshape(b, s, d)


def kernel(x_prompt, x_sample, w_in, w_fourier, attn_sink, rel_bias, w_out, norm_mix, norm_ffn,
           w_router, w_gate, w_up, w_down, norm_final):
    assert w_in.shape[0] == 1
    wr = w_router[0].T
    wr_hi = wr.astype(BF16)
    wr_lo = (wr - wr_hi.astype(F32)).astype(BF16)
    p = dict(
        norm_mix=norm_mix[0].reshape(1, D_MODEL),
        norm_ffn=norm_ffn[0].reshape(1, D_MODEL),
        norm_final=norm_final.reshape(1, D_MODEL),
        w_in=w_in[0].astype(BF16),
        w_fourier=w_fourier[0].astype(BF16),
        w_out=w_out[0].astype(BF16),
        w_router=jnp.concatenate([wr_hi, wr_lo], axis=0),
        w_gate=w_gate[0], w_up=w_up[0], w_down=w_down[0],
        sink=attn_sink[0],
        bias=_bias_table(rel_bias),
    )
    return _trunk(x_prompt, p), _trunk(x_sample, p)
```

```python
import functools
import math

import numpy as np
import jax
import jax.numpy as jnp
from jax import lax
from jax.experimental import pallas as pl
from jax.experimental.pallas import tpu as pltpu

F32 = jnp.float32
BF16 = jnp.bfloat16
I32 = jnp.int32

D_MODEL = 1024
FOURIER_GROUPS = 4
GROUP_DIM = 128
FOURIER_WIDTH = FOURIER_GROUPS * GROUP_DIM
N_HEADS = 8
N_KV_HEADS = 2
HEAD_DIM = 64
ATTN_WIDTH = N_HEADS * HEAD_DIM
KV_WIDTH = N_KV_HEADS * HEAD_DIM
WINDOW = 128
BLOCK = 128
N_BUCKETS = 32
MAX_DISTANCE = 128
IN_WIDTH = FOURIER_WIDTH + ATTN_WIDTH + 2 * KV_WIDTH
N_EXPERTS = 16
CAPACITY_FACTOR = 2
D_EXPERT = 1024
EPS = 1e-6

LANES = 128
PREFIX_CHUNK = 256
FFT_S2_BLOCK = 16
FFT_ROW_GROUP = 8
STRIP = 32
FFN_CHUNK = 256
VMEM_LIMIT = 56 * 1024 * 1024


U32 = jnp.uint32
HALF = D_MODEL // 2
ROW_TILE = HALF // LANES
HIGH_HALF = np.uint32(0xFFFF0000)


def _params(sem, vmem=None):
    return pltpu.CompilerParams(dimension_semantics=sem, vmem_limit_bytes=vmem)


def _store_rows_as_tiles(ref, val):
    m = val.shape[0]
    bits = lax.bitcast_convert_type(val.astype(BF16).astype(F32), U32)
    words = (bits[:, :HALF] >> 16) | (bits[:, HALF:] & HIGH_HALF)
    for c in range(ROW_TILE):
        ref[pl.ds(c, m, stride=ROW_TILE), :] = words[:, c * LANES:(c + 1) * LANES]


def _unpack_words(words, high):
    bits = (words & HIGH_HALF) if high else (words << 16)
    return lax.bitcast_convert_type(bits, F32).astype(BF16)


def _load_tiles_as_rows(ref):
    m = ref.shape[0] // ROW_TILE
    words = jnp.concatenate(
        [ref[pl.ds(c, m, stride=ROW_TILE), :] for c in range(ROW_TILE)], axis=1)
    return jnp.concatenate([_unpack_words(words, False), _unpack_words(words, True)], axis=1)


def _tiles_of_rows(ref, r, count):
    return ref.at[pl.ds(pl.multiple_of(r * ROW_TILE, ROW_TILE), count * ROW_TILE), :]


def _tile_of_row(ref, r):
    start = r * ROW_TILE
    if not isinstance(r, int):
        start = pl.multiple_of(start, ROW_TILE)
    return ref.at[pl.ds(start, ROW_TILE), :]


def _inproj_body(x_ref, g_ref, w_ref, zf_ref, q_ref, k_ref, v_ref):
    x = x_ref[...]
    ms = jnp.mean(x * x, axis=-1, keepdims=True)
    h = (x * lax.rsqrt(ms + EPS)) * g_ref[...]
    z = jnp.dot(h.astype(BF16), w_ref[...], preferred_element_type=F32)
    o = FOURIER_WIDTH
    g = FFT_ROW_GROUP
    for s1 in range(x.shape[0] // BLOCK):
        for jb in range(BLOCK // g):
            r = s1 * BLOCK + jb * g
            zf_ref[0, jb, s1 * g:(s1 + 1) * g, :] = z[r:r + g, :o]
    q_ref[...] = (z[:, o:o + ATTN_WIDTH] * (HEAD_DIM ** -0.5)).astype(BF16)
    o += ATTN_WIDTH
    k_ref[...] = z[:, o:o + KV_WIDTH].astype(BF16)
    o += KV_WIDTH
    v_ref[...] = z[:, o:o + KV_WIDTH].astype(BF16)


def _inproj(xf, gain, w_in, tm, seq):
    n = xf.shape[0]
    assert tm % BLOCK == 0 and seq % tm == 0
    tiles = seq // tm
    s1_rows = tm // BLOCK * FFT_ROW_GROUP
    nblk = BLOCK // FFT_ROW_GROUP
    row = lambda w: pl.BlockSpec((tm, w), lambda i: (i, 0))
    return pl.pallas_call(
        _inproj_body,
        grid=(n // tm,),
        in_specs=[row(D_MODEL),
                  pl.BlockSpec((1, D_MODEL), lambda i: (0, 0)),
                  pl.BlockSpec((D_MODEL, IN_WIDTH), lambda i: (0, 0))],
        out_specs=[pl.BlockSpec((1, nblk, s1_rows, FOURIER_WIDTH),
                                lambda i: (i // tiles, 0, i % tiles, 0)),
                   row(ATTN_WIDTH), row(KV_WIDTH), row(KV_WIDTH)],
        out_shape=[jax.ShapeDtypeStruct((n // seq, nblk, seq // BLOCK * FFT_ROW_GROUP,
                                         FOURIER_WIDTH), F32),
                   jax.ShapeDtypeStruct((n, ATTN_WIDTH), BF16),
                   jax.ShapeDtypeStruct((n, KV_WIDTH), BF16),
                   jax.ShapeDtypeStruct((n, KV_WIDTH), BF16)],
        compiler_params=_params(("parallel",), VMEM_LIMIT),
        name="inproj",
    )(xf, gain, w_in)


def _dft_tables(s):
    s2 = BLOCK
    s1 = s // s2
    a = np.arange(s1, dtype=np.float64)
    ang1 = 2.0 * np.pi * np.outer(a, a) / s1
    f1 = np.concatenate([np.cos(ang1), -np.sin(ang1)], axis=0)
    b = np.arange(s2, dtype=np.float64)
    ang2 = 2.0 * np.pi * np.outer(b, b) / s2
    angt = 2.0 * np.pi * np.outer(a, b) / s
    c = np.arange(GROUP_DIM, dtype=np.float64)
    angc = 2.0 * np.pi * np.outer(c, c) / GROUP_DIM
    cs = np.concatenate([np.cos(angc), np.sin(angc)], axis=0)
    return dict(
        f1=jnp.asarray(f1, BF16),
        f2r=jnp.asarray(np.cos(ang2), F32), f2i=jnp.asarray(-np.sin(ang2), F32),
        twr=jnp.asarray(np.cos(angt), F32), twi=jnp.asarray(-np.sin(angt), F32),
        cs=jnp.asarray(cs, BF16))


def _dft1_body(f_ref, *refs, nb):
    x_refs, o_ref, stage = refs[:-2], refs[-2], refs[-1]
    g = FFT_ROW_GROUP
    s1 = x_refs[0].shape[2] // g
    per_dot = 4
    for j0 in range(0, nb, per_dot):
        x = jnp.concatenate(
            [ref[0, j // g, pl.ds(j % g, s1, stride=g), :]
             for j in range(j0, j0 + per_dot) for ref in x_refs],
            axis=1).astype(BF16)
        res = jnp.dot(f_ref[...], x, preferred_element_type=F32)
        w = FOURIER_WIDTH
        for jj in range(per_dot):
            stage[0, :, j0 + jj, :] = res[:s1, jj * w:(jj + 1) * w]
            stage[1, :, j0 + jj, :] = res[s1:, jj * w:(jj + 1) * w]
    o_ref[0] = stage[...].astype(BF16)


def _dft1(zf4, f1):
    nb, g = FFT_S2_BLOCK, FFT_ROW_GROUP
    b, s2_groups, rows, w = zf4.shape
    s1, s2 = rows // g, s2_groups * g
    lane_block = lambda k: pl.BlockSpec((1, nb // g, rows, LANES), lambda i, j: (i, j, 0, k))
    nblk = w // LANES
    return pl.pallas_call(
        functools.partial(_dft1_body, nb=nb),
        grid=(b, s2 // nb),
        in_specs=[pl.BlockSpec((2 * s1, s1), lambda i, j: (0, 0))]
                 + [lane_block(k) for k in range(nblk)],
        out_specs=pl.BlockSpec((1, 2, s1, nb, w), lambda i, j: (i, 0, 0, j, 0)),
        out_shape=jax.ShapeDtypeStruct((b, 2, s1, s2, w), BF16),
        scratch_shapes=[pltpu.VMEM((2, s1, nb, w), F32)],
        compiler_params=_params(("parallel", "parallel"), VMEM_LIMIT),
        name="dft1",
    )(f1, *([zf4] * nblk))


def _dft2_body(a_ref, f2r_ref, f2i_ref, twr_ref, twi_ref, cs_ref, wf_ref, o_ref, stage,
               *, kb, scale):
    s2 = BLOCK
    fr = f2r_ref[...]
    fi = f2i_ref[...]
    ys = []
    for j in range(kb):
        tr = twr_ref[0, j:j + 1, :]
        ti = twi_ref[0, j:j + 1, :]
        gr = fr * tr - fi * ti
        gi = fr * ti + fi * tr
        lhs = jnp.concatenate(
            [jnp.concatenate([gr, -gi], axis=1), jnp.concatenate([gi, gr], axis=1)],
            axis=0).astype(BF16)
        rhs = jnp.concatenate([a_ref[0, 0, j], a_ref[0, 1, j]], axis=0)
        ys.append(jnp.dot(lhs, rhs, preferred_element_type=F32))
    cat = jnp.concatenate(
        [jnp.concatenate([y[:s2, g * GROUP_DIM:(g + 1) * GROUP_DIM],
                          y[s2:, g * GROUP_DIM:(g + 1) * GROUP_DIM]], axis=1)
         for g in range(FOURIER_GROUPS) for y in ys], axis=0).astype(BF16)
    re = (jnp.dot(cat, cs_ref[...], preferred_element_type=F32) * scale).astype(BF16)
    rows = kb * s2
    outs = [jnp.dot(re[g * rows:(g + 1) * rows], wf_ref[g], preferred_element_type=F32)
            for g in range(FOURIER_GROUPS)]
    for j in range(kb):
        stage[:, j, :] = jnp.concatenate(
            [o[j * s2:(j + 1) * s2] for o in outs], axis=1)
    o_ref[0] = stage[...].astype(BF16)


def _dft2(a5, tabs, wf, kb, scale):
    b, _, s1, s2, w = a5.shape
    const = lambda shape: pl.BlockSpec(shape, lambda i, j: (0,) * len(shape))
    twr = tabs["twr"].reshape(s1 // kb, kb, s2)
    twi = tabs["twi"].reshape(s1 // kb, kb, s2)
    return pl.pallas_call(
        functools.partial(_dft2_body, kb=kb, scale=scale),
        grid=(b, s1 // kb),
        in_specs=[pl.BlockSpec((1, 2, kb, s2, w), lambda i, j: (i, 0, j, 0, 0)),
                  const((s2, s2)), const((s2, s2)),
                  pl.BlockSpec((1, kb, s2), lambda i, j: (j, 0, 0)),
                  pl.BlockSpec((1, kb, s2), lambda i, j: (j, 0, 0)),
                  const((2 * GROUP_DIM, GROUP_DIM)),
                  const((FOURIER_GROUPS, GROUP_DIM, GROUP_DIM))],
        out_specs=pl.BlockSpec((1, s2, kb, w), lambda i, j: (i, 0, j, 0)),
        out_shape=jax.ShapeDtypeStruct((b, s2, s1, w), BF16),
        scratch_shapes=[pltpu.VMEM((s2, kb, w), F32)],
        compiler_params=_params(("parallel", "parallel"), VMEM_LIMIT),
        name="dft2",
    )(a5, tabs["f2r"], tabs["f2i"], twr, twi, tabs["cs"], wf)


def _fourier(zf, wf_bf16):
    b, s2_blocks, rows, w = zf.shape
    s = s2_blocks * rows
    tabs = _dft_tables(s)
    a = _dft1(zf, tabs["f1"])
    f = _dft2(a, tabs, wf_bf16, kb=16,
              scale=1.0 / math.sqrt(s * GROUP_DIM))
    return f.reshape(b, s, w)


def _bucket_table():
    qi = np.arange(BLOCK)[:, None]
    kj = np.arange(3 * BLOCK)[None, :]
    rel = kj - BLOCK - qi
    nb = N_BUCKETS // 2
    max_exact = nb // 2
    n = np.abs(rel)
    large = max_exact + np.floor(
        np.log(np.maximum(n, 1).astype(np.float64) / max_exact)
        / math.log(MAX_DISTANCE / max_exact) * (nb - max_exact) + 1e-9).astype(np.int64)
    large = np.minimum(large, nb - 1)
    bucket = np.where(rel > 0, nb, 0) + np.where(n < max_exact, n, large)
    return np.where(n <= WINDOW, bucket, -1).astype(np.int32)


def _bias_body(rb_ref, bucket_ref, o_ref):
    bk = bucket_ref[...]
    col = lax.broadcasted_iota(I32, bk.shape, 1)
    for h in range(N_HEADS):
        acc = jnp.full(bk.shape, -jnp.inf, F32)
        for b in range(N_BUCKETS):
            acc = jnp.where(bk == b, rb_ref[b, h], acc)
        o_ref[0, h] = acc
        o_ref[1, h] = jnp.where(col >= BLOCK, acc, -jnp.inf)
        o_ref[2, h] = jnp.where(col < 2 * BLOCK, acc, -jnp.inf)


def _bias_table(rel_bias):
    bucket = jnp.asarray(_bucket_table())
    shape = (3, N_HEADS) + bucket.shape
    return pl.pallas_call(
        _bias_body,
        in_specs=[pl.BlockSpec(memory_space=pltpu.SMEM),
                  pl.BlockSpec(bucket.shape, lambda: (0, 0))],
        out_specs=pl.BlockSpec(shape, lambda: (0, 0, 0, 0)),
        out_shape=jax.ShapeDtypeStruct(shape, F32),
        name="bias_table",
    )(rel_bias.astype(F32), bucket)


def _attn_body(sink_ref, q_ref, kp_ref, kc_ref, kn_ref, vp_ref, vc_ref, vn_ref, bias_ref,
               o_ref, *, tq):
    j = pl.program_id(1)
    last = pl.num_programs(1) - 1
    nsb = tq // BLOCK
    group = N_HEADS // N_KV_HEADS
    gw = group * HEAD_DIM
    kext = jnp.concatenate([kp_ref[0, tq - BLOCK:, :], kc_ref[0], kn_ref[0, :BLOCK, :]], axis=0)
    vext = jnp.concatenate([vp_ref[0, tq - BLOCK:, :], vc_ref[0], vn_ref[0, :BLOCK, :]], axis=0)
    low_half = lax.broadcasted_iota(I32, (1, KV_WIDTH), 1) < HEAD_DIM

    def per_kv_head(ext):
        x = ext.astype(F32)
        xr = pltpu.roll(x, HEAD_DIM, axis=1)
        return (jnp.where(low_half, x, xr).astype(BF16), jnp.where(low_half, xr, x).astype(BF16))

    krep = per_kv_head(kext)
    vrep = per_kv_head(vext)
    head_of_lane = lax.broadcasted_iota(I32, (1, gw), 1) // HEAD_DIM
    nt = (((1,), (1,)), ((), ()))
    def scores(sb, kh):
        rows = slice(sb * BLOCK, (sb + 3) * BLOCK)
        variant = 0
        if sb == 0:
            variant = jnp.where(j == 0, 1, 0)
        elif sb == nsb - 1:
            variant = jnp.where(j == last, 2, 0)
        qblk = q_ref[0, sb * BLOCK:(sb + 1) * BLOCK, kh * gw:(kh + 1) * gw]
        qm = jnp.concatenate(
            [jnp.where(head_of_lane == g, qblk, jnp.zeros_like(qblk)) for g in range(group)],
            axis=0)
        kb = jnp.concatenate([krep[kh][rows], krep[kh][rows]], axis=1)
        s = lax.dot_general(qm, kb, nt, preferred_element_type=F32)
        return s, variant

    def weights(sv, kh):
        s_all, variant = sv
        out = []
        for g in range(group):
            for r0 in range(0, BLOCK, STRIP):
                s = s_all[g * BLOCK + r0:g * BLOCK + r0 + STRIP]
                s = s + bias_ref[variant, kh * group + g, r0:r0 + STRIP, :]
                sink = sink_ref[kh * group + g]
                m = jnp.maximum(jnp.max(s, axis=-1, keepdims=True), sink)
                p = jnp.exp(s - m)
                denom = jnp.sum(p, axis=-1, keepdims=True) + jnp.exp(sink - m)
                out.append((p / denom).astype(BF16))
        return jnp.concatenate(out, axis=0)

    def values(pn, sb, kh):
        rows = slice(sb * BLOCK, (sb + 3) * BLOCK)
        o4 = jnp.dot(pn, vrep[kh][rows], preferred_element_type=F32)
        out = []
        for pair in range(group // 2):
            even = o4[(2 * pair) * BLOCK:(2 * pair + 1) * BLOCK]
            odd = o4[(2 * pair + 1) * BLOCK:(2 * pair + 2) * BLOCK]
            out.append(jnp.where(low_half, even, odd))
        return out

    items = [(sb, kh) for sb in range(nsb) for kh in range(N_KV_HEADS)]
    pieces = {}
    s_next = scores(*items[0])
    pn_prev = None
    for i, (sb, kh) in enumerate(items):
        s_cur = s_next
        if i + 1 < len(items):
            s_next = scores(*items[i + 1])
        if pn_prev is not None:
            pieces[items[i - 1]] = values(pn_prev, *items[i - 1])
        pn_prev = weights(s_cur, kh)
    pieces[items[-1]] = values(pn_prev, *items[-1])
    for sb in range(nsb):
        row = [x for kh in range(N_KV_HEADS) for x in pieces[(sb, kh)]]
        o_ref[0, sb * BLOCK:(sb + 1) * BLOCK, :] = jnp.concatenate(row, axis=1).astype(BF16)


def _attention(q, k, v, bias, sink, tq):
    b, s, _ = q.shape
    nt = s // tq
    assert tq >= 2 * BLOCK
    kv = lambda f: pl.BlockSpec((1, tq, KV_WIDTH), f)
    prev = lambda i, j: (i, jnp.maximum(j - 1, 0), 0)
    cur = lambda i, j: (i, j, 0)
    nxt = lambda i, j: (i, jnp.minimum(j + 1, nt - 1), 0)
    return pl.pallas_call(
        functools.partial(_attn_body, tq=tq),
        grid=(b, nt),
        in_specs=[pl.BlockSpec(memory_space=pltpu.SMEM),
                  pl.BlockSpec((1, tq, ATTN_WIDTH), cur),
                  kv(prev), kv(cur), kv(nxt), kv(prev), kv(cur), kv(nxt),
                  pl.BlockSpec(bias.shape, lambda i, j: (0, 0, 0, 0))],
        out_specs=pl.BlockSpec((1, tq, ATTN_WIDTH), cur),
        out_shape=jax.ShapeDtypeStruct((b, s, ATTN_WIDTH), BF16),
        compiler_params=_params(("parallel", "parallel"), VMEM_LIMIT),
        name="attn",
    )(sink.astype(F32), q, k, k, k, v, v, v, bias)


def _outproj_body(x_ref, f_ref, a_ref, wo_ref, g_ref, wr_ref, x1_ref, h2_ref, aff_ref):
    mix = jnp.concatenate([f_ref[...], a_ref[...]], axis=1)
    x1 = x_ref[...] + jnp.dot(mix, wo_ref[...], preferred_element_type=F32)
    x1_ref[...] = x1
    ms = jnp.mean(x1 * x1, axis=-1, keepdims=True)
    h2 = (x1 * lax.rsqrt(ms + EPS)) * g_ref[...]
    _store_rows_as_tiles(h2_ref, h2)
    hi = h2.astype(BF16)
    lo = (h2 - hi.astype(F32)).astype(BF16)
    nt = (((1,), (1,)), ((), ()))
    o1 = lax.dot_general(wr_ref[...], hi, nt, preferred_element_type=F32)
    o2 = lax.dot_general(wr_ref[:N_EXPERTS], lo, nt, preferred_element_type=F32)
    logits = o1[:N_EXPERTS] + (o1[N_EXPERTS:] + o2)
    e = jnp.exp(logits - jnp.max(logits, axis=0, keepdims=True))
    aff_ref[...] = e / jnp.sum(e, axis=0, keepdims=True)


def _outproj(xf, f, a, w_out, gain, w_router_t, tm):
    n = xf.shape[0]
    row = lambda w: pl.BlockSpec((tm, w), lambda i: (i, 0))
    const = lambda shape: pl.BlockSpec(shape, lambda i: (0, 0))
    return pl.pallas_call(
        _outproj_body,
        grid=(n // tm,),
        in_specs=[row(D_MODEL), row(FOURIER_WIDTH), row(ATTN_WIDTH),
                  const((D_MODEL, D_MODEL)), const((1, D_MODEL)),
                  const((2 * N_EXPERTS, D_MODEL))],
        out_specs=[row(D_MODEL),
                   pl.BlockSpec((tm * ROW_TILE, LANES), lambda i: (i, 0)),
                   pl.BlockSpec((N_EXPERTS, tm), lambda i: (0, i))],
        out_shape=[jax.ShapeDtypeStruct((n, D_MODEL), F32),
                   jax.ShapeDtypeStruct((n * ROW_TILE, LANES), U32),
                   jax.ShapeDtypeStruct((N_EXPERTS, n), F32)],
        compiler_params=_params(("parallel",), VMEM_LIMIT),
        name="outproj",
    )(xf, f, a, w_out, gain, w_router_t)


def _route_body(aff_ref, idx_ref, gate_ref, rank_ref, rsel_ref,
                d_a, d_b, n_a, n_b, g_a, g_b, *, n, cap):
    e = N_EXPERTS
    ch = PREFIX_CHUNK

    def search(i, tau):
        cand = tau | jnp.left_shift(jnp.int32(1), 30 - i)
        keys = lax.bitcast_convert_type(aff_ref[...], I32)
        cnt = jnp.sum((keys >= cand).astype(I32), axis=1, keepdims=True)
        return jnp.where(cnt >= cap, cand, tau)

    tau = lax.fori_loop(0, 31, search, jnp.zeros((e, 1), I32))
    keys = lax.bitcast_convert_type(aff_ref[...], I32)
    n_gt = jnp.sum((keys > tau).astype(I32), axis=1, keepdims=True)
    need = (cap - n_gt).astype(F32)

    r = lax.broadcasted_iota(I32, (ch, ch), 0)
    c = lax.broadcasted_iota(I32, (ch, ch), 1)
    upper = (r < c).astype(BF16)

    carry_eq = jnp.zeros((e, 1), F32)
    carry_m = jnp.zeros((e, 1), F32)
    for ci in range(n // ch):
        sl = slice(ci * ch, (ci + 1) * ch)
        k = lax.bitcast_convert_type(aff_ref[:, sl], I32)
        gt = k > tau
        eq = (k == tau).astype(F32)
        eq_ex = jnp.dot(eq.astype(BF16), upper, preferred_element_type=F32) + carry_eq
        carry_eq = carry_eq + jnp.sum(eq, axis=1, keepdims=True)
        m = jnp.where(gt | ((eq > 0.0) & (eq_ex < need)), 1.0, 0.0).astype(F32)
        rank = (jnp.dot(m.astype(BF16), upper, preferred_element_type=F32)
                + carry_m).astype(I32)
        carry_m = carry_m + jnp.sum(m, axis=1, keepdims=True)
        rank_ref[:, sl] = rank
        rsel_ref[:, sl] = jnp.where(m > 0.0, rank, -1)
        lane = lax.broadcasted_iota(I32, (e, ch), 1) + ci * ch
        d_a[:, sl] = jnp.where(m > 0.0, lane - rank, 0)
        n_a[:, sl] = lane
    g_a[...] = aff_ref[...]

    bufs = [(d_a, n_a, g_a), (d_b, n_b, g_b)]
    for b in range(int(math.log2(n))):
        src, dst = bufs[b % 2], bufs[(b + 1) % 2]
        sh = 1 << b
        d = src[0][...]
        moving = (lax.shift_right_logical(d, b) & 1) == 1
        d_in = pltpu.roll(d, n - sh, axis=1)
        arrive = (lax.shift_right_logical(d_in, b) & 1) == 1
        dst[0][...] = jnp.where(arrive, d_in, jnp.where(moving, 0, d))
        for t in (1, 2):
            v = src[t][...]
            dst[t][...] = jnp.where(arrive, pltpu.roll(v, n - sh, axis=1), v)
    fin = bufs[int(math.log2(n)) % 2]
    idx_ref[...] = fin[1][:, :cap]
    gate_ref[...] = fin[2][:, :cap]


def _route(aff_t, cap):
    e, n = aff_t.shape
    assert n & (n - 1) == 0 and n % PREFIX_CHUNK == 0
    full = lambda shape: pl.BlockSpec(shape, lambda: (0, 0))
    big_i = pltpu.VMEM((e, n), I32)
    return pl.pallas_call(
        functools.partial(_route_body, n=n, cap=cap),
        in_specs=[full((e, n))],
        out_specs=[full((e, cap)), full((e, cap)), full((e, n)), full((e, n))],
        out_shape=[jax.ShapeDtypeStruct((e, cap), I32),
                   jax.ShapeDtypeStruct((e, cap), F32),
                   jax.ShapeDtypeStruct((e, n), I32),
                   jax.ShapeDtypeStruct((e, n), I32)],
        scratch_shapes=[big_i, big_i, big_i, big_i,
                        pltpu.VMEM((e, n), F32), pltpu.VMEM((e, n), F32)],
        compiler_params=_params(None, VMEM_LIMIT),
        name="route",
    )(aff_t)


def _rows_to_cols(rows):
    r, m = rows.shape
    if r < 8:
        rows = jnp.concatenate([rows, jnp.zeros((8 - r, m), rows.dtype)], axis=0)
    return rows.T[:, :r]


def _ffn_body(idx_cur, idx_nxt, gate_ref, h_hbm, wg_ref, wu_ref, wd_ref, ye_ref,
              xa, xb, wg_s, wu_s, wd_s, sems, *, tm, spe):
    s = pl.program_id(0)
    last = pl.num_programs(0) - 1
    g_a, g_b = sems.at[0], sems.at[1]

    def gather(idx_ref, t, buf, sem):
        def one(i):
            queue = i % 2 if isinstance(i, int) else 0
            pltpu.async_copy(_tile_of_row(h_hbm, idx_ref[t, 0, i]),
                             _tile_of_row(buf, i), sem, priority=queue)
        return one

    def wait_gather(buf, sem):
        pltpu.make_async_copy(h_hbm.at[pl.ds(0, tm * ROW_TILE), :], buf, sem).wait()

    def compute(xbuf, t, copies):
        nc = D_EXPERT // FFN_CHUNK
        stages = 3 * nc
        todo = [(fn, i) for fn in copies for i in range(tm)]
        per_stage = -(-len(todo) // stages)

        def issue_some():
            for fn, i in todo[:per_stage]:
                fn(i)
            del todo[:per_stage]

        x = _load_tiles_as_rows(xbuf)
        cols = lambda c: slice(c * FFN_CHUNK, (c + 1) * FFN_CHUNK)

        def up(c):
            g = jnp.dot(x, wg_s[:, cols(c)], preferred_element_type=F32)
            issue_some()
            u = jnp.dot(x, wu_s[:, cols(c)], preferred_element_type=F32)
            issue_some()
            return g, u

        def act(gu):
            g, u = gu
            return (g * jax.nn.sigmoid(g) * u).astype(BF16)

        def down(hid, c):
            part = jnp.dot(hid, wd_s[cols(c), :], preferred_element_type=F32)
            issue_some()
            return part

        gu_next = up(0)
        hid_prev = None
        y = None
        for c in range(nc):
            gu_cur = gu_next
            if c + 1 < nc:
                gu_next = up(c + 1)
            if hid_prev is not None:
                part = down(hid_prev, c - 1)
                y = part if y is None else y + part
            hid_prev = act(gu_cur)
        y = y + down(hid_prev, nc - 1)
        assert not todo
        y = y * _rows_to_cols(gate_ref[t])
        _store_rows_as_tiles(ye_ref.at[pl.ds(t * tm * ROW_TILE, tm * ROW_TILE), :], y)

    @pl.when(s % spe == 0)
    def _():
        rows = 256
        for w_ref, w_s in ((wg_ref, wg_s), (wu_ref, wu_s), (wd_ref, wd_s)):
            for r in range(0, w_s.shape[0], rows):
                w_s[r:r + rows, :] = w_ref[0, r:r + rows, :].astype(BF16)

    @pl.when(s == 0)
    def _():
        pl.loop(0, tm)(gather(idx_cur, 0, xa, g_a))

    wait_gather(xa, g_a)
    compute(xa, 0, [gather(idx_cur, 1, xb, g_b)])

    wait_gather(xb, g_b)
    compute(xb, 1, [gather(idx_nxt, 0, xa, g_a)])

    @pl.when(s == last)
    def _():
        wait_gather(xa, g_a)


def _ffn(idx_c, gate_c, h2, wg, wu, wd, tm):
    e, cap = idx_c.shape
    spe = cap // (2 * tm)
    steps = e * spe
    idx3 = idx_c.reshape(2 * steps, 1, tm)
    gate3 = gate_c.reshape(2 * steps, 1, tm)
    cur = lambda i: (i, 0, 0)
    nxt = lambda i: (jnp.minimum(i + 1, steps - 1), 0, 0)
    smem = lambda f: pl.BlockSpec((2, 1, tm), f, memory_space=pltpu.SMEM)
    wspec = lambda a, b: pl.BlockSpec((1, a, b), lambda i: (i // spe, 0, 0))
    buf = pltpu.VMEM((tm * ROW_TILE, LANES), U32)
    return pl.pallas_call(
        functools.partial(_ffn_body, tm=tm, spe=spe),
        grid=(steps,),
        in_specs=[smem(cur), smem(nxt),
                  pl.BlockSpec((2, 1, tm), cur),
                  pl.BlockSpec(memory_space=pl.ANY),
                  wspec(D_MODEL, D_EXPERT), wspec(D_MODEL, D_EXPERT), wspec(D_EXPERT, D_MODEL)],
        out_specs=pl.BlockSpec((2 * tm * ROW_TILE, LANES), lambda i: (i, 0)),
        out_shape=jax.ShapeDtypeStruct((e * cap * ROW_TILE, LANES), U32),
        scratch_shapes=[buf, buf,
                        pltpu.VMEM((D_MODEL, D_EXPERT), BF16), pltpu.VMEM((D_MODEL, D_EXPERT), BF16),
                        pltpu.VMEM((D_EXPERT, D_MODEL), BF16), pltpu.SemaphoreType.DMA((2,))],
        compiler_params=_params(("arbitrary",), VMEM_LIMIT),
        name="ffn",
    )(idx3, idx3, gate3, h2, wg, wu, wd)


def _combine_body(st_ref, x1_ref, rsel_ref, g_ref, ye_hbm, o_ref, win, extra, sems,
                  *, tt, we, cap, n_tiles):
    i = pl.program_id(0)
    last = pl.num_programs(0) - 1
    buf = i % 2
    n_e = N_EXPERTS
    ws = n_e * we

    def run_start(e, tile):
        return st_ref[e * (n_tiles + 1) + tile]

    def clamp(nominal):
        return jnp.minimum(nominal, cap - we)

    def window_rows(e, start):
        return _tiles_of_rows(ye_hbm, e * cap + start, we)

    def first_fetch(tile, b, e):
        start = clamp((run_start(e, tile) // 8) * 8)
        dst = win.at[b, pl.ds(e * we * ROW_TILE, we * ROW_TILE), :]
        return pltpu.make_async_copy(window_rows(e, start), dst, sems.at[b])

    @pl.when(i == 0)
    def _():
        for e in range(n_e):
            first_fetch(0, 0, e).start()

    for e in range(n_e):
        first_fetch(i, buf, e).wait()

    @pl.when(i < last)
    def _():
        for e in range(n_e):
            first_fetch(i + 1, 1 - buf, e).start()

    rank_of = jnp.concatenate([_rows_to_cols(rsel_ref[0:8, :]),
                               _rows_to_cols(rsel_ref[8:n_e, :])], axis=1)
    lane = lax.broadcasted_iota(I32, (tt, we), 1)

    def selector(e, start, nominal):
        hit = (rank_of[:, e:e + 1] - start) == lane
        if nominal is not None:
            hit = hit & (lane >= nominal - start)
        return jnp.where(hit, 1.0, 0.0).astype(BF16)

    starts = [clamp((run_start(e, i) // 8) * 8) for e in range(n_e)]
    overflow = run_start(0, i + 1) - starts[0] > we
    for e in range(1, n_e):
        overflow = overflow | (run_start(e, i + 1) - starts[e] > we)

    def finish(x2_cols):
        ssq = sum(jnp.sum(x * x, axis=-1, keepdims=True) for x in x2_cols)
        scale = lax.rsqrt(ssq * (1.0 / D_MODEL) + EPS)
        for c, x in enumerate(x2_cols):
            w = x.shape[1]
            o_ref[:, c * w:(c + 1) * w] = (x * scale) * g_ref[:, c * w:(c + 1) * w]

    first = win.at[buf]
    sel = jnp.concatenate([selector(e, starts[e], None) for e in range(n_e)], axis=1)
    cw = 2 * LANES

    def window_cols(c):
        words = jnp.concatenate(
            [first[pl.ds(2 * (c % 2) + k, ws, stride=ROW_TILE), :] for k in range(2)], axis=1)
        return _unpack_words(words, c >= 2)

    x2_cols = []
    rows_next = window_cols(0)
    for c in range(D_MODEL // cw):
        rows_cur = rows_next
        if (c + 1) * cw < D_MODEL:
            rows_next = window_cols(c + 1)
        x2_cols.append(x1_ref[:, c * cw:(c + 1) * cw]
                       + jnp.dot(sel, rows_cur, preferred_element_type=F32))
    finish(x2_cols)

    @pl.when(overflow)
    def _():
        moe = jnp.zeros((tt, D_MODEL), F32)
        for e in range(n_e):
            base = (run_start(e, i) // 8) * 8
            nwin = (run_start(e, i + 1) - base + we - 1) // we

            def window(w, acc, e=e, base=base):
                nominal = base + w * we
                start = clamp(nominal)
                cp = pltpu.make_async_copy(window_rows(e, start), extra, sems.at[2])
                cp.start()
                cp.wait()
                return acc + jnp.dot(selector(e, start, nominal), _load_tiles_as_rows(extra),
                                     preferred_element_type=F32)

            moe = lax.fori_loop(0, nwin, window, moe)
        finish([x1_ref[...] + moe])


def _combine(x1, ye, rank, rsel, gain, tt, we):
    n = x1.shape[0]
    n_e = rank.shape[0]
    cap = ye.shape[0] // ROW_TILE // n_e
    n_tiles = n // tt
    assert cap >= we and cap % 8 == 0
    run_starts = jnp.concatenate([rank[:, ::tt], jnp.full((n_e, 1), cap, I32)], axis=1)
    return pl.pallas_call(
        functools.partial(_combine_body, tt=tt, we=we, cap=cap, n_tiles=n_tiles),
        grid_spec=pltpu.PrefetchScalarGridSpec(
            num_scalar_prefetch=1,
            grid=(n_tiles,),
            in_specs=[pl.BlockSpec((tt, D_MODEL), lambda i, s: (i, 0)),
                      pl.BlockSpec((n_e, tt), lambda i, s: (0, i)),
                      pl.BlockSpec((1, D_MODEL), lambda i, s: (0, 0)),
                      pl.BlockSpec(memory_space=pl.ANY)],
            out_specs=pl.BlockSpec((tt, D_MODEL), lambda i, s: (i, 0)),
            scratch_shapes=[pltpu.VMEM((2, n_e * we * ROW_TILE, LANES), U32),
                            pltpu.VMEM((we * ROW_TILE, LANES), U32),
                            pltpu.SemaphoreType.DMA((3,))]),
        out_shape=jax.ShapeDtypeStruct((n, D_MODEL), F32),
        compiler_params=_params(("arbitrary",), VMEM_LIMIT),
        name="combine",
    )(run_starts.reshape(-1), x1, rsel, gain, ye)


def _tile(n, pref):
    t = pref
    while n % t:
        t //= 2
    return t


def _trunk(x, p):
    b, s, d = x.shape
    n = b * s
    cap = CAPACITY_FACTOR * n // N_EXPERTS
    xf = x.reshape(n, d)
    tm = _tile(n, 512)
    zf, q, k, v = _inproj(xf, p["norm_mix"], p["w_in"], tm, s)
    f = _fourier(zf, p["w_fourier"])
    a = _attention(q.reshape(b, s, ATTN_WIDTH), k.reshape(b, s, KV_WIDTH),
                   v.reshape(b, s, KV_WIDTH), p["bias"], p["sink"], _tile(s, 512))
    x1, h2, aff_t = _outproj(xf, f.reshape(n, FOURIER_WIDTH), a.reshape(n, ATTN_WIDTH),
                             p["w_out"], p["norm_ffn"], p["w_router"], tm)
    idx_c, gate_c, rank, rsel = _route(aff_t, cap)
    ye = _ffn(idx_c, gate_c, h2, p["w_gate"], p["w_up"], p["w_down"], _tile(cap // 2, 512))
    tt = _tile(n, 512)
    we = max(LANES, 2 * CAPACITY_FACTOR * tt // N_EXPERTS)
    y = _combine(x1, ye, rank, rsel, p["norm_final"], tt, we)
    return y.reshape(b, s, d)


def kernel(x_prompt, x_sample, w_in, w_fourier, attn_sink, rel_bias, w_out, norm_mix, norm_ffn,
           w_router, w_gate, w_up, w_down, norm_final):
    assert w_in.shape[0] == 1
    wr = w_router[0].T
    wr_hi = wr.astype(BF16)
    wr_lo = (wr - wr_hi.astype(F32)).astype(BF16)
    p = dict(
        norm_mix=norm_mix[0].reshape(1, D_MODEL),
        norm_ffn=norm_ffn[0].reshape(1, D_MODEL),
        norm_final=norm_final.reshape(1, D_MODEL),
        w_in=w_in[0].astype(BF16),
        w_fourier=w_fourier[0].astype(BF16),
        w_out=w_out[0].astype(BF16),
        w_router=jnp.concatenate([wr_hi, wr_lo], axis=0),
        w_gate=w_gate[0], w_up=w_up[0], w_down=w_down[0],
        sink=attn_sink[0],
        bias=_bias_table(rel_bias),
    )
    return _trunk(x_prompt, p), _trunk(x_sample, p)
```

```python
import functools
import math

import numpy as np
import jax
import jax.numpy as jnp
from jax import lax
from jax.experimental import pallas as pl
from jax.experimental.pallas import tpu as pltpu

F32 = jnp.float32
BF16 = jnp.bfloat16
I32 = jnp.int32

D_MODEL = 1024
FOURIER_GROUPS = 4
GROUP_DIM = 128
FOURIER_WIDTH = FOURIER_GROUPS * GROUP_DIM
N_HEADS = 8
N_KV_HEADS = 2
HEAD_DIM = 64
ATTN_WIDTH = N_HEADS * HEAD_DIM
KV_WIDTH = N_KV_HEADS * HEAD_DIM
WINDOW = 128
BLOCK = 128
N_BUCKETS = 32
MAX_DISTANCE = 128
IN_WIDTH = FOURIER_WIDTH + ATTN_WIDTH + 2 * KV_WIDTH
N_EXPERTS = 16
CAPACITY_FACTOR = 2
D_EXPERT = 1024
EPS = 1e-6

LANES = 128
PREFIX_CHUNK = 256
FFT_S2_BLOCK = 16
FFT_ROW_GROUP = 8
STRIP = 32
FFN_CHUNK = 256
VMEM_LIMIT = 56 * 1024 * 1024


U32 = jnp.uint32
HALF = D_MODEL // 2
ROW_TILE = HALF // LANES
HIGH_HALF = np.uint32(0xFFFF0000)


def _params(sem, vmem=None):
    return pltpu.CompilerParams(dimension_semantics=sem, vmem_limit_bytes=vmem)


def _store_rows_as_tiles(ref, val):
    m = val.shape[0]
    bits = lax.bitcast_convert_type(val.astype(BF16).astype(F32), U32)
    words = (bits[:, :HALF] >> 16) | (bits[:, HALF:] & HIGH_HALF)
    for c in range(ROW_TILE):
        ref[pl.ds(c, m, stride=ROW_TILE), :] = words[:, c * LANES:(c + 1) * LANES]


def _unpack_words(words, high):
    bits = (words & HIGH_HALF) if high else (words << 16)
    return lax.bitcast_convert_type(bits, F32).astype(BF16)


def _load_tiles_as_rows(ref):
    m = ref.shape[0] // ROW_TILE
    words = jnp.concatenate(
        [ref[pl.ds(c, m, stride=ROW_TILE), :] for c in range(ROW_TILE)], axis=1)
    return jnp.concatenate([_unpack_words(words, False), _unpack_words(words, True)], axis=1)


def _tiles_of_rows(ref, r, count):
    return ref.at[pl.ds(pl.multiple_of(r * ROW_TILE, ROW_TILE), count * ROW_TILE), :]


def _tile_of_row(ref, r):
    start = r * ROW_TILE
    if not isinstance(r, int):
        start = pl.multiple_of(start, ROW_TILE)
    return ref.at[pl.ds(start, ROW_TILE), :]


def _inproj_body(x_ref, g_ref, w_ref, zf_ref, q_ref, k_ref, v_ref):
    x = x_ref[...]
    ms = jnp.mean(x * x, axis=-1, keepdims=True)
    h = (x * lax.rsqrt(ms + EPS)) * g_ref[...]
    z = jnp.dot(h.astype(BF16), w_ref[...], preferred_element_type=F32)
    o = FOURIER_WIDTH
    g = FFT_ROW_GROUP
    for s1 in range(x.shape[0] // BLOCK):
        for jb in range(BLOCK // g):
            r = s1 * BLOCK + jb * g
            zf_ref[0, jb, s1 * g:(s1 + 1) * g, :] = z[r:r + g, :o]
    q_ref[...] = (z[:, o:o + ATTN_WIDTH] * (HEAD_DIM ** -0.5)).astype(BF16)
    o += ATTN_WIDTH
    k_ref[...] = z[:, o:o + KV_WIDTH].astype(BF16)
    o += KV_WIDTH
    v_ref[...] = z[:, o:o + KV_WIDTH].astype(BF16)


def _inproj(xf, gain, w_in, tm, seq):
    n = xf.shape[0]
    assert tm % BLOCK == 0 and seq % tm == 0
    tiles = seq // tm
    s1_rows = tm // BLOCK * FFT_ROW_GROUP
    nblk = BLOCK // FFT_ROW_GROUP
    row = lambda w: pl.BlockSpec((tm, w), lambda i: (i, 0))
    return pl.pallas_call(
        _inproj_body,
        grid=(n // tm,),
        in_specs=[row(D_MODEL),
                  pl.BlockSpec((1, D_MODEL), lambda i: (0, 0)),
                  pl.BlockSpec((D_MODEL, IN_WIDTH), lambda i: (0, 0))],
        out_specs=[pl.BlockSpec((1, nblk, s1_rows, FOURIER_WIDTH),
                                lambda i: (i // tiles, 0, i % tiles, 0)),
                   row(ATTN_WIDTH), row(KV_WIDTH), row(KV_WIDTH)],
        out_shape=[jax.ShapeDtypeStruct((n // seq, nblk, seq // BLOCK * FFT_ROW_GROUP,
                                         FOURIER_WIDTH), F32),
                   jax.ShapeDtypeStruct((n, ATTN_WIDTH), BF16),
                   jax.ShapeDtypeStruct((n, KV_WIDTH), BF16),
                   jax.ShapeDtypeStruct((n, KV_WIDTH), BF16)],
        compiler_params=_params(("parallel",), VMEM_LIMIT),
        name="inproj",
    )(xf, gain, w_in)


def _dft_tables(s):
    s2 = BLOCK
    s1 = s // s2
    a = np.arange(s1, dtype=np.float64)
    ang1 = 2.0 * np.pi * np.outer(a, a) / s1
    f1 = np.concatenate([np.cos(ang1), -np.sin(ang1)], axis=0)
    b = np.arange(s2, dtype=np.float64)
    ang2 = 2.0 * np.pi * np.outer(b, b) / s2
    angt = 2.0 * np.pi * np.outer(a, b) / s
    c = np.arange(GROUP_DIM, dtype=np.float64)
    angc = 2.0 * np.pi * np.outer(c, c) / GROUP_DIM
    cs = np.concatenate([np.cos(angc), np.sin(angc)], axis=0)
    return dict(
        f1=jnp.asarray(f1, BF16),
        f2r=jnp.asarray(np.cos(ang2), F32), f2i=jnp.asarray(-np.sin(ang2), F32),
        twr=jnp.asarray(np.cos(angt), F32), twi=jnp.asarray(-np.sin(angt), F32),
        cs=jnp.asarray(cs, BF16))


def _dft1_body(f_ref, *refs, nb):
    x_refs, o_ref, stage = refs[:-2], refs[-2], refs[-1]
    g = FFT_ROW_GROUP
    s1 = x_refs[0].shape[2] // g
    per_dot = 4
    for j0 in range(0, nb, per_dot):
        x = jnp.concatenate(
            [ref[0, j // g, pl.ds(j % g, s1, stride=g), :]
             for j in range(j0, j0 + per_dot) for ref in x_refs],
            axis=1).astype(BF16)
        res = jnp.dot(f_ref[...], x, preferred_element_type=F32)
        w = FOURIER_WIDTH
        for jj in range(per_dot):
            stage[0, :, j0 + jj, :] = res[:s1, jj * w:(jj + 1) * w]
            stage[1, :, j0 + jj, :] = res[s1:, jj * w:(jj + 1) * w]
    o_ref[0] = stage[...].astype(BF16)


def _dft1(zf4, f1):
    nb, g = FFT_S2_BLOCK, FFT_ROW_GROUP
    b, s2_groups, rows, w = zf4.shape
    s1, s2 = rows // g, s2_groups * g
    lane_block = lambda k: pl.BlockSpec((1, nb // g, rows, LANES), lambda i, j: (i, j, 0, k))
    nblk = w // LANES
    return pl.pallas_call(
        functools.partial(_dft1_body, nb=nb),
        grid=(b, s2 // nb),
        in_specs=[pl.BlockSpec((2 * s1, s1), lambda i, j: (0, 0))]
                 + [lane_block(k) for k in range(nblk)],
        out_specs=pl.BlockSpec((1, 2, s1, nb, w), lambda i, j: (i, 0, 0, j, 0)),
        out_shape=jax.ShapeDtypeStruct((b, 2, s1, s2, w), BF16),
        scratch_shapes=[pltpu.VMEM((2, s1, nb, w), F32)],
        compiler_params=_params(("parallel", "parallel"), VMEM_LIMIT),
        name="dft1",
    )(f1, *([zf4] * nblk))


def _dft2_body(a_ref, f2r_ref, f2i_ref, twr_ref, twi_ref, cs_ref, wf_ref, o_ref, stage,
               *, kb, scale):
    s2 = BLOCK
    fr = f2r_ref[...]
    fi = f2i_ref[...]
    ys = []
    for j in range(kb):
        tr = twr_ref[0, j:j + 1, :]
        ti = twi_ref[0, j:j + 1, :]
        gr = fr * tr - fi * ti
        gi = fr * ti + fi * tr
        lhs = jnp.concatenate(
            [jnp.concatenate([gr, -gi], axis=1), jnp.concatenate([gi, gr], axis=1)],
            axis=0).astype(BF16)
        rhs = jnp.concatenate([a_ref[0, 0, j], a_ref[0, 1, j]], axis=0)
        ys.append(jnp.dot(lhs, rhs, preferred_element_type=F32))
    cat = jnp.concatenate(
        [jnp.concatenate([y[:s2, g * GROUP_DIM:(g + 1) * GROUP_DIM],
                          y[s2:, g * GROUP_DIM:(g + 1) * GROUP_DIM]], axis=1)
         for g in range(FOURIER_GROUPS) for y in ys], axis=0).astype(BF16)
    re = (jnp.dot(cat, cs_ref[...], preferred_element_type=F32) * scale).astype(BF16)
    rows = kb * s2
    outs = [jnp.dot(re[g * rows:(g + 1) * rows], wf_ref[g], preferred_element_type=F32)
            for g in range(FOURIER_GROUPS)]
    for j in range(kb):
        stage[:, j, :] = jnp.concatenate(
            [o[j * s2:(j + 1) * s2] for o in outs], axis=1)
    o_ref[0] = stage[...].astype(BF16)


def _dft2(a5, tabs, wf, kb, scale):
    b, _, s1, s2, w = a5.shape
    const = lambda shape: pl.BlockSpec(shape, lambda i, j: (0,) * len(shape))
    twr = tabs["twr"].reshape(s1 // kb, kb, s2)
    twi = tabs["twi"].reshape(s1 // kb, kb, s2)
    return pl.pallas_call(
        functools.partial(_dft2_body, kb=kb, scale=scale),
        grid=(b, s1 // kb),
        in_specs=[pl.BlockSpec((1, 2, kb, s2, w), lambda i, j: (i, 0, j, 0, 0)),
                  const((s2, s2)), const((s2, s2)),
                  pl.BlockSpec((1, kb, s2), lambda i, j: (j, 0, 0)),
                  pl.BlockSpec((1, kb, s2), lambda i, j: (j, 0, 0)),
                  const((2 * GROUP_DIM, GROUP_DIM)),
                  const((FOURIER_GROUPS, GROUP_DIM, GROUP_DIM))],
        out_specs=pl.BlockSpec((1, s2, kb, w), lambda i, j: (i, 0, j, 0)),
        out_shape=jax.ShapeDtypeStruct((b, s2, s1, w), BF16),
        scratch_shapes=[pltpu.VMEM((s2, kb, w), F32)],
        compiler_params=_params(("parallel", "parallel"), VMEM_LIMIT),
        name="dft2",
    )(a5, tabs["f2r"], tabs["f2i"], twr, twi, tabs["cs"], wf)


def _fourier(zf, wf_bf16):
    b, s2_blocks, rows, w = zf.shape
    s = s2_blocks * rows
    tabs = _dft_tables(s)
    a = _dft1(zf, tabs["f1"])
    f = _dft2(a, tabs, wf_bf16, kb=16,
              scale=1.0 / math.sqrt(s * GROUP_DIM))
    return f.reshape(b, s, w)


def _bucket_table():
    qi = np.arange(BLOCK)[:, None]
    kj = np.arange(3 * BLOCK)[None, :]
    rel = kj - BLOCK - qi
    nb = N_BUCKETS // 2
    max_exact = nb // 2
    n = np.abs(rel)
    large = max_exact + np.floor(
        np.log(np.maximum(n, 1).astype(np.float64) / max_exact)
        / math.log(MAX_DISTANCE / max_exact) * (nb - max_exact) + 1e-9).astype(np.int64)
    large = np.minimum(large, nb - 1)
    bucket = np.where(rel > 0, nb, 0) + np.where(n < max_exact, n, large)
    return np.where(n <= WINDOW, bucket, -1).astype(np.int32)


def _bias_body(rb_ref, bucket_ref, o_ref):
    bk = bucket_ref[...]
    col = lax.broadcasted_iota(I32, bk.shape, 1)
    for h in range(N_HEADS):
        acc = jnp.full(bk.shape, -jnp.inf, F32)
        for b in range(N_BUCKETS):
            acc = jnp.where(bk == b, rb_ref[b, h], acc)
        o_ref[0, h] = acc
        o_ref[1, h] = jnp.where(col >= BLOCK, acc, -jnp.inf)
        o_ref[2, h] = jnp.where(col < 2 * BLOCK, acc, -jnp.inf)


def _bias_table(rel_bias):
    bucket = jnp.asarray(_bucket_table())
    shape = (3, N_HEADS) + bucket.shape
    return pl.pallas_call(
        _bias_body,
        in_specs=[pl.BlockSpec(memory_space=pltpu.SMEM),
                  pl.BlockSpec(bucket.shape, lambda: (0, 0))],
        out_specs=pl.BlockSpec(shape, lambda: (0, 0, 0, 0)),
        out_shape=jax.ShapeDtypeStruct(shape, F32),
        name="bias_table",
    )(rel_bias.astype(F32), bucket)


def _attn_body(sink_ref, q_ref, kp_ref, kc_ref, kn_ref, vp_ref, vc_ref, vn_ref, bias_ref,
               o_ref, *, tq):
    j = pl.program_id(1)
    last = pl.num_programs(1) - 1
    nsb = tq // BLOCK
    group = N_HEADS // N_KV_HEADS
    gw = group * HEAD_DIM
    kext = jnp.concatenate([kp_ref[0, tq - BLOCK:, :], kc_ref[0], kn_ref[0, :BLOCK, :]], axis=0)
    vext = jnp.concatenate([vp_ref[0, tq - BLOCK:, :], vc_ref[0], vn_ref[0, :BLOCK, :]], axis=0)
    low_half = lax.broadcasted_iota(I32, (1, KV_WIDTH), 1) < HEAD_DIM

    def per_kv_head(ext):
        x = ext.astype(F32)
        xr = pltpu.roll(x, HEAD_DIM, axis=1)
        return (jnp.where(low_half, x, xr).astype(BF16), jnp.where(low_half, xr, x).astype(BF16))

    krep = per_kv_head(kext)
    vrep = per_kv_head(vext)
    head_of_lane = lax.broadcasted_iota(I32, (1, gw), 1) // HEAD_DIM
    nt = (((1,), (1,)), ((), ()))
    def scores(sb, kh):
        rows = slice(sb * BLOCK, (sb + 3) * BLOCK)
        variant = 0
        if sb == 0:
            variant = jnp.where(j == 0, 1, 0)
        elif sb == nsb - 1:
            variant = jnp.where(j == last, 2, 0)
        qblk = q_ref[0, sb * BLOCK:(sb + 1) * BLOCK, kh * gw:(kh + 1) * gw]
        qm = jnp.concatenate(
            [jnp.where(head_of_lane == g, qblk, jnp.zeros_like(qblk)) for g in range(group)],
            axis=0)
        kb = jnp.concatenate([krep[kh][rows], krep[kh][rows]], axis=1)
        s = lax.dot_general(qm, kb, nt, preferred_element_type=F32)
        return s, variant

    def weights(sv, kh):
        s_all, variant = sv
        out = []
        for g in range(group):
            for r0 in range(0, BLOCK, STRIP):
                s = s_all[g * BLOCK + r0:g * BLOCK + r0 + STRIP]
                s = s + bias_ref[variant, kh * group + g, r0:r0 + STRIP, :]
                sink = sink_ref[kh * group + g]
                m = jnp.maximum(jnp.max(s, axis=-1, keepdims=True), sink)
                p = jnp.exp(s - m)
                denom = jnp.sum(p, axis=-1, keepdims=True) + jnp.exp(sink - m)
                out.append((p / denom).astype(BF16))
        return jnp.concatenate(out, axis=0)

    def values(pn, sb, kh):
        rows = slice(sb * BLOCK, (sb + 3) * BLOCK)
        o4 = jnp.dot(pn, vrep[kh][rows], preferred_element_type=F32)
        out = []
        for pair in range(group // 2):
            even = o4[(2 * pair) * BLOCK:(2 * pair + 1) * BLOCK]
            odd = o4[(2 * pair + 1) * BLOCK:(2 * pair + 2) * BLOCK]
            out.append(jnp.where(low_half, even, odd))
        return out

    items = [(sb, kh) for sb in range(nsb) for kh in range(N_KV_HEADS)]
    pieces = {}
    s_next = scores(*items[0])
    pn_prev = None
    for i, (sb, kh) in enumerate(items):
        s_cur = s_next
        if i + 1 < len(items):
            s_next = scores(*items[i + 1])
        if pn_prev is not None:
            pieces[items[i - 1]] = values(pn_prev, *items[i - 1])
        pn_prev = weights(s_cur, kh)
    pieces[items[-1]] = values(pn_prev, *items[-1])
    for sb in range(nsb):
        row = [x for kh in range(N_KV_HEADS) for x in pieces[(sb, kh)]]
        o_ref[0, sb * BLOCK:(sb + 1) * BLOCK, :] = jnp.concatenate(row, axis=1).astype(BF16)


def _attention(q, k, v, bias, sink, tq):
    b, s, _ = q.shape
    nt = s // tq
    assert tq >= 2 * BLOCK
    kv = lambda f: pl.BlockSpec((1, tq, KV_WIDTH), f)
    prev = lambda i, j: (i, jnp.maximum(j - 1, 0), 0)
    cur = lambda i, j: (i, j, 0)
    nxt = lambda i, j: (i, jnp.minimum(j + 1, nt - 1), 0)
    return pl.pallas_call(
        functools.partial(_attn_body, tq=tq),
        grid=(b, nt),
        in_specs=[pl.BlockSpec(memory_space=pltpu.SMEM),
                  pl.BlockSpec((1, tq, ATTN_WIDTH), cur),
                  kv(prev), kv(cur), kv(nxt), kv(prev), kv(cur), kv(nxt),
                  pl.BlockSpec(bias.shape, lambda i, j: (0, 0, 0, 0))],
        out_specs=pl.BlockSpec((1, tq, ATTN_WIDTH), cur),
        out_shape=jax.ShapeDtypeStruct((b, s, ATTN_WIDTH), BF16),
        compiler_params=_params(("parallel", "parallel"), VMEM_LIMIT),
        name="attn",
    )(sink.astype(F32), q, k, k, k, v, v, v, bias)


def _outproj_body(x_ref, f_ref, a_ref, wo_ref, g_ref, wr_ref, x1_ref, h2_ref, aff_ref):
    mix = jnp.concatenate([f_ref[...], a_ref[...]], axis=1)
    x1 = x_ref[...] + jnp.dot(mix, wo_ref[...], preferred_element_type=F32)
    x1_ref[...] = x1
    ms = jnp.mean(x1 * x1, axis=-1, keepdims=True)
    h2 = (x1 * lax.rsqrt(ms + EPS)) * g_ref[...]
    _store_rows_as_tiles(h2_ref, h2)
    hi = h2.astype(BF16)
    lo = (h2 - hi.astype(F32)).astype(BF16)
    nt = (((1,), (1,)), ((), ()))
    o1 = lax.dot_general(wr_ref[...], hi, nt, preferred_element_type=F32)
    o2 = lax.dot_general(wr_ref[:N_EXPERTS], lo, nt, preferred_element_type=F32)
    logits = o1[:N_EXPERTS] + (o1[N_EXPERTS:] + o2)
    e = jnp.exp(logits - jnp.max(logits, axis=0, keepdims=True))
    aff_ref[...] = e / jnp.sum(e, axis=0, keepdims=True)


def _outproj(xf, f, a, w_out, gain, w_router_t, tm):
    n = xf.shape[0]
    row = lambda w: pl.BlockSpec((tm, w), lambda i: (i, 0))
    const = lambda shape: pl.BlockSpec(shape, lambda i: (0, 0))
    return pl.pallas_call(
        _outproj_body,
        grid=(n // tm,),
        in_specs=[row(D_MODEL), row(FOURIER_WIDTH), row(ATTN_WIDTH),
                  const((D_MODEL, D_MODEL)), const((1, D_MODEL)),
                  const((2 * N_EXPERTS, D_MODEL))],
        out_specs=[row(D_MODEL),
                   pl.BlockSpec((tm * ROW_TILE, LANES), lambda i: (i, 0)),
                   pl.BlockSpec((N_EXPERTS, tm), lambda i: (0, i))],
        out_shape=[jax.ShapeDtypeStruct((n, D_MODEL), F32),
                   jax.ShapeDtypeStruct((n * ROW_TILE, LANES), U32),
                   jax.ShapeDtypeStruct((N_EXPERTS, n), F32)],
        compiler_params=_params(("parallel",), VMEM_LIMIT),
        name="outproj",
    )(xf, f, a, w_out, gain, w_router_t)


def _route_body(aff_ref, idx_ref, gate_ref, rank_ref, rsel_ref,
                d_a, d_b, n_a, n_b, g_a, g_b, *, n, cap):
    e = N_EXPERTS
    ch = PREFIX_CHUNK

    def search(i, tau):
        cand = tau | jnp.left_shift(jnp.int32(1), 30 - i)
        keys = lax.bitcast_convert_type(aff_ref[...], I32)
        cnt = jnp.sum((keys >= cand).astype(I32), axis=1, keepdims=True)
        return jnp.where(cnt >= cap, cand, tau)

    tau = lax.fori_loop(0, 31, search, jnp.zeros((e, 1), I32))
    keys = lax.bitcast_convert_type(aff_ref[...], I32)
    n_gt = jnp.sum((keys > tau).astype(I32), axis=1, keepdims=True)
    need = (cap - n_gt).astype(F32)

    r = lax.broadcasted_iota(I32, (ch, ch), 0)
    c = lax.broadcasted_iota(I32, (ch, ch), 1)
    upper = (r < c).astype(BF16)

    carry_eq = jnp.zeros((e, 1), F32)
    carry_m = jnp.zeros((e, 1), F32)
    for ci in range(n // ch):
        sl = slice(ci * ch, (ci + 1) * ch)
        k = lax.bitcast_convert_type(aff_ref[:, sl], I32)
        gt = k > tau
        eq = (k == tau).astype(F32)
        eq_ex = jnp.dot(eq.astype(BF16), upper, preferred_element_type=F32) + carry_eq
        carry_eq = carry_eq + jnp.sum(eq, axis=1, keepdims=True)
        m = jnp.where(gt | ((eq > 0.0) & (eq_ex < need)), 1.0, 0.0).astype(F32)
        rank = (jnp.dot(m.astype(BF16), upper, preferred_element_type=F32)
                + carry_m).astype(I32)
        carry_m = carry_m + jnp.sum(m, axis=1, keepdims=True)
        rank_ref[:, sl] = rank
        rsel_ref[:, sl] = jnp.where(m > 0.0, rank, -1)
        lane = lax.broadcasted_iota(I32, (e, ch), 1) + ci * ch
        d_a[:, sl] = jnp.where(m > 0.0, lane - rank, 0)
        n_a[:, sl] = lane
    g_a[...] = aff_ref[...]

    bufs = [(d_a, n_a, g_a), (d_b, n_b, g_b)]
    for b in range(int(math.log2(n))):
        src, dst = bufs[b % 2], bufs[(b + 1) % 2]
        sh = 1 << b
        d = src[0][...]
        moving = (lax.shift_right_logical(d, b) & 1) == 1
        d_in = pltpu.roll(d, n - sh, axis=1)
        arrive = (lax.shift_right_logical(d_in, b) & 1) == 1
        dst[0][...] = jnp.where(arrive, d_in, jnp.where(moving, 0, d))
        for t in (1, 2):
            v = src[t][...]
            dst[t][...] = jnp.where(arrive, pltpu.roll(v, n - sh, axis=1), v)
    fin = bufs[int(math.log2(n)) % 2]
    idx_ref[...] = fin[1][:, :cap]
    gate_ref[...] = fin[2][:, :cap]


def _route(aff_t, cap):
    e, n = aff_t.shape
    assert n & (n - 1) == 0 and n % PREFIX_CHUNK == 0
    full = lambda shape: pl.BlockSpec(shape, lambda: (0, 0))
    big_i = pltpu.VMEM((e, n), I32)
    return pl.pallas_call(
        functools.partial(_route_body, n=n, cap=cap),
        in_specs=[full((e, n))],
        out_specs=[full((e, cap)), full((e, cap)), full((e, n)), full((e, n))],
        out_shape=[jax.ShapeDtypeStruct((e, cap), I32),
                   jax.ShapeDtypeStruct((e, cap), F32),
                   jax.ShapeDtypeStruct((e, n), I32),
                   jax.ShapeDtypeStruct((e, n), I32)],
        scratch_shapes=[big_i, big_i, big_i, big_i,
                        pltpu.VMEM((e, n), F32), pltpu.VMEM((e, n), F32)],
        compiler_params=_params(None, VMEM_LIMIT),
        name="route",
    )(aff_t)


def _rows_to_cols(rows):
    r, m = rows.shape
    if r < 8:
        rows = jnp.concatenate([rows, jnp.zeros((8 - r, m), rows.dtype)], axis=0)
    return rows.T[:, :r]


def _ffn_body(idx_cur, idx_nxt, gate_ref, h_hbm, wg_ref, wu_ref, wd_ref, ye_ref,
              xa, xb, wg_s, wu_s, wd_s, sems, *, tm, spe):
    s = pl.program_id(0)
    last = pl.num_programs(0) - 1
    g_a, g_b = sems.at[0], sems.at[1]

    def gather(idx_ref, t, buf, sem):
        def one(i):
            queue = i % 2 if isinstance(i, int) else 0
            pltpu.async_copy(_tile_of_row(h_hbm, idx_ref[t, 0, i]),
                             _tile_of_row(buf, i), sem, priority=queue)
        return one

    def wait_gather(buf, sem):
        pltpu.make_async_copy(h_hbm.at[pl.ds(0, tm * ROW_TILE), :], buf, sem).wait()

    def compute(xbuf, t, copies):
        nc = D_EXPERT // FFN_CHUNK
        stages = 3 * nc
        todo = [(fn, i) for fn in copies for i in range(tm)]
        per_stage = -(-len(todo) // stages)

        def issue_some():
            for fn, i in todo[:per_stage]:
                fn(i)
            del todo[:per_stage]

        x = _load_tiles_as_rows(xbuf)
        cols = lambda c: slice(c * FFN_CHUNK, (c + 1) * FFN_CHUNK)

        def up(c):
            g = jnp.dot(x, wg_s[:, cols(c)], preferred_element_type=F32)
            issue_some()
            u = jnp.dot(x, wu_s[:, cols(c)], preferred_element_type=F32)
            issue_some()
            return g, u

        def act(gu):
            g, u = gu
            return (g * jax.nn.sigmoid(g) * u).astype(BF16)

        def down(hid, c):
            part = jnp.dot(hid, wd_s[cols(c), :], preferred_element_type=F32)
            issue_some()
            return part

        gu_next = up(0)
        hid_prev = None
        y = None
        for c in range(nc):
            gu_cur = gu_next
            if c + 1 < nc:
                gu_next = up(c + 1)
            if hid_prev is not None:
                part = down(hid_prev, c - 1)
                y = part if y is None else y + part
            hid_prev = act(gu_cur)
        y = y + down(hid_prev, nc - 1)
        assert not todo
        y = y * _rows_to_cols(gate_ref[t])
        _store_rows_as_tiles(ye_ref.at[pl.ds(t * tm * ROW_TILE, tm * ROW_TILE), :], y)

    @pl.when(s % spe == 0)
    def _():
        rows = 256
        for w_ref, w_s in ((wg_ref, wg_s), (wu_ref, wu_s), (wd_ref, wd_s)):
            for r in range(0, w_s.shape[0], rows):
                w_s[r:r + rows, :] = w_ref[0, r:r + rows, :].astype(BF16)

    @pl.when(s == 0)
    def _():
        pl.loop(0, tm)(gather(idx_cur, 0, xa, g_a))

    wait_gather(xa, g_a)
    compute(xa, 0, [gather(idx_cur, 1, xb, g_b)])

    wait_gather(xb, g_b)
    compute(xb, 1, [gather(idx_nxt, 0, xa, g_a)])

    @pl.when(s == last)
    def _():
        wait_gather(xa, g_a)


def _ffn(idx_c, gate_c, h2, wg, wu, wd, tm):
    e, cap = idx_c.shape
    spe = cap // (2 * tm)
    steps = e * spe
    idx3 = idx_c.reshape(2 * steps, 1, tm)
    gate3 = gate_c.reshape(2 * steps, 1, tm)
    cur = lambda i: (i, 0, 0)
    nxt = lambda i: (jnp.minimum(i + 1, steps - 1), 0, 0)
    smem = lambda f: pl.BlockSpec((2, 1, tm), f, memory_space=pltpu.SMEM)
    wspec = lambda a, b: pl.BlockSpec((1, a, b), lambda i: (i // spe, 0, 0))
    buf = pltpu.VMEM((tm * ROW_TILE, LANES), U32)
    return pl.pallas_call(
        functools.partial(_ffn_body, tm=tm, spe=spe),
        grid=(steps,),
        in_specs=[smem(cur), smem(nxt),
                  pl.BlockSpec((2, 1, tm), cur),
                  pl.BlockSpec(memory_space=pl.ANY),
                  wspec(D_MODEL, D_EXPERT), wspec(D_MODEL, D_EXPERT), wspec(D_EXPERT, D_MODEL)],
        out_specs=pl.BlockSpec((2 * tm * ROW_TILE, LANES), lambda i: (i, 0)),
        out_shape=jax.ShapeDtypeStruct((e * cap * ROW_TILE, LANES), U32),
        scratch_shapes=[buf, buf,
                        pltpu.VMEM((D_MODEL, D_EXPERT), BF16), pltpu.VMEM((D_MODEL, D_EXPERT), BF16),
                        pltpu.VMEM((D_EXPERT, D_MODEL), BF16), pltpu.SemaphoreType.DMA((2,))],
        compiler_params=_params(("arbitrary",), VMEM_LIMIT),
        name="ffn",
    )(idx3, idx3, gate3, h2, wg, wu, wd)


def _combine_body(st_ref, x1_ref, rsel_ref, g_ref, ye_hbm, o_ref, win, extra, sems,
                  *, tt, we, cap, n_tiles):
    i = pl.program_id(0)
    last = pl.num_programs(0) - 1
    buf = i % 2
    n_e = N_EXPERTS
    ws = n_e * we

    def run_start(e, tile):
        return st_ref[e * (n_tiles + 1) + tile]

    def clamp(nominal):
        return jnp.minimum(nominal, cap - we)

    def window_rows(e, start):
        return _tiles_of_rows(ye_hbm, e * cap + start, we)

    def first_fetch(tile, b, e):
        start = clamp((run_start(e, tile) // 8) * 8)
        dst = win.at[b, pl.ds(e * we * ROW_TILE, we * ROW_TILE), :]
        return pltpu.make_async_copy(window_rows(e, start), dst, sems.at[b])

    @pl.when(i == 0)
    def _():
        for e in range(n_e):
            first_fetch(0, 0, e).start()

    for e in range(n_e):
        first_fetch(i, buf, e).wait()

    @pl.when(i < last)
    def _():
        for e in range(n_e):
            first_fetch(i + 1, 1 - buf, e).start()

    rank_of = jnp.concatenate([_rows_to_cols(rsel_ref[0:8, :]),
                               _rows_to_cols(rsel_ref[8:n_e, :])], axis=1)
    lane = lax.broadcasted_iota(I32, (tt, we), 1)

    def selector(e, start, nominal):
        hit = (rank_of[:, e:e + 1] - start) == lane
        if nominal is not None:
            hit = hit & (lane >= nominal - start)
        return jnp.where(hit, 1.0, 0.0).astype(BF16)

    starts = [clamp((run_start(e, i) // 8) * 8) for e in range(n_e)]
    overflow = run_start(0, i + 1) - starts[0] > we
    for e in range(1, n_e):
        overflow = overflow | (run_start(e, i + 1) - starts[e] > we)

    def finish(x2_cols):
        ssq = sum(jnp.sum(x * x, axis=-1, keepdims=True) for x in x2_cols)
        scale = lax.rsqrt(ssq * (1.0 / D_MODEL) + EPS)
        for c, x in enumerate(x2_cols):
            w = x.shape[1]
            o_ref[:, c * w:(c + 1) * w] = (x * scale) * g_ref[:, c * w:(c + 1) * w]

    first = win.at[buf]
    sel = jnp.concatenate([selector(e, starts[e], None) for e in range(n_e)], axis=1)
    cw = 2 * LANES

    def window_cols(c):
        words = jnp.concatenate(
            [first[pl.ds(2 * (c % 2) + k, ws, stride=ROW_TILE), :] for k in range(2)], axis=1)
        return _unpack_words(words, c >= 2)

    x2_cols = []
    rows_next = window_cols(0)
    for c in range(D_MODEL // cw):
        rows_cur = rows_next
        if (c + 1) * cw < D_MODEL:
            rows_next = window_cols(c + 1)
        x2_cols.append(x1_ref[:, c * cw:(c + 1) * cw]
                       + jnp.dot(sel, rows_cur, preferred_element_type=F32))
    finish(x2_cols)

    @pl.when(overflow)
    def _():
        moe = jnp.zeros((tt, D_MODEL), F32)
        for e in range(n_e):
            base = (run_start(e, i) // 8) * 8
            nwin = (run_start(e, i + 1) - base + we - 1) // we

            def window(w, acc, e=e, base=base):
                nominal = base + w * we
                start = clamp(nominal)
                cp = pltpu.make_async_copy(window_rows(e, start), extra, sems.at[2])
                cp.start()
                cp.wait()
                return acc + jnp.dot(selector(e, start, nominal), _load_tiles_as_rows(extra),
                                     preferred_element_type=F32)

            moe = lax.fori_loop(0, nwin, window, moe)
        finish([x1_ref[...] + moe])


def _combine(x1, ye, rank, rsel, gain, tt, we):
    n = x1.shape[0]
    n_e = rank.shape[0]
    cap = ye.shape[0] // ROW_TILE // n_e
    n_tiles = n // tt
    assert cap >= we and cap % 8 == 0
    run_starts = jnp.concatenate([rank[:, ::tt], jnp.full((n_e, 1), cap, I32)], axis=1)
    return pl.pallas_call(
        functools.partial(_combine_body, tt=tt, we=we, cap=cap, n_tiles=n_tiles),
        grid_spec=pltpu.PrefetchScalarGridSpec(
            num_scalar_prefetch=1,
            grid=(n_tiles,),
            in_specs=[pl.BlockSpec((tt, D_MODEL), lambda i, s: (i, 0)),
                      pl.BlockSpec((n_e, tt), lambda i, s: (0, i)),
                      pl.BlockSpec((1, D_MODEL), lambda i, s: (0, 0)),
                      pl.BlockSpec(memory_space=pl.ANY)],
            out_specs=pl.BlockSpec((tt, D_MODEL), lambda i, s: (i, 0)),
            scratch_shapes=[pltpu.VMEM((2, n_e * we * ROW_TILE, LANES), U32),
                            pltpu.VMEM((we * ROW_TILE, LANES), U32),
                            pltpu.SemaphoreType.DMA((3,))]),
        out_shape=jax.ShapeDtypeStruct((n, D_MODEL), F32),
        compiler_params=_params(("arbitrary",), VMEM_LIMIT),
        name="combine",
    )(run_starts.reshape(-1), x1, rsel, gain, ye)


def _tile(n, pref):
    t = pref
    while n % t:
        t //= 2
    return t


def _trunk(x, p):
    b, s, d = x.shape
    n = b * s
    cap = CAPACITY_FACTOR * n // N_EXPERTS
    xf = x.reshape(n, d)
    tm = _tile(n, 512)
    zf, q, k, v = _inproj(xf, p["norm_mix"], p["w_in"], _tile(s, 1024), s)
    f = _fourier(zf, p["w_fourier"])
    a = _attention(q.reshape(b, s, ATTN_WIDTH), k.reshape(b, s, KV_WIDTH),
                   v.reshape(b, s, KV_WIDTH), p["bias"], p["sink"], _tile(s, 1024))
    x1, h2, aff_t = _outproj(xf, f.reshape(n, FOURIER_WIDTH), a.reshape(n, ATTN_WIDTH),
                             p["w_out"], p["norm_ffn"], p["w_router"], tm)
    idx_c, gate_c, rank, rsel = _route(aff_t, cap)
    ye = _ffn(idx_c, gate_c, h2, p["w_gate"], p["w_up"], p["w_down"], _tile(cap // 2, 512))
    tt = _tile(n, 512)
    we = max(LANES, 2 * CAPACITY_FACTOR * tt // N_EXPERTS)
    y = _combine(x1, ye, rank, rsel, p["norm_final"], tt, we)
    return y.reshape(b, s, d)


def kernel(x_prompt, x_sample, w_in, w_fourier, attn_sink, rel_bias, w_out, norm_mix, norm_ffn,
           w_router, w_gate, w_up, w_down, norm_final):
    assert w_in.shape[0] == 1
    wr = w_router[0].T
    wr_hi = wr.astype(BF16)
    wr_lo = (wr - wr_hi.astype(F32)).astype(BF16)
    p = dict(
        norm_mix=norm_mix[0].reshape(1, D_MODEL),
        norm_ffn=norm_ffn[0].reshape(1, D_MODEL),
        norm_final=norm_final.reshape(1, D_MODEL),
        w_in=w_in[0].astype(BF16),
        w_fourier=w_fourier[0].astype(BF16),
        w_out=w_out[0].astype(BF16),
        w_router=jnp.concatenate([wr_hi, wr_lo], axis=0),
        w_gate=w_gate[0], w_up=w_up[0], w_down=w_down[0],
        sink=attn_sink[0],
        bias=_bias_table(rel_bias),
    )
    return _trunk(x_prompt, p), _trunk(x_sample, p)
```

```python
import functools
import math

import numpy as np
import jax
import jax.numpy as jnp
from jax import lax
from jax.experimental import pallas as pl
from jax.experimental.pallas import tpu as pltpu

F32 = jnp.float32
BF16 = jnp.bfloat16
I32 = jnp.int32

D_MODEL = 1024
FOURIER_GROUPS = 4
GROUP_DIM = 128
FOURIER_WIDTH = FOURIER_GROUPS * GROUP_DIM
N_HEADS = 8
N_KV_HEADS = 2
HEAD_DIM = 64
ATTN_WIDTH = N_HEADS * HEAD_DIM
KV_WIDTH = N_KV_HEADS * HEAD_DIM
WINDOW = 128
BLOCK = 128
N_BUCKETS = 32
MAX_DISTANCE = 128
IN_WIDTH = FOURIER_WIDTH + ATTN_WIDTH + 2 * KV_WIDTH
N_EXPERTS = 16
CAPACITY_FACTOR = 2
D_EXPERT = 1024
EPS = 1e-6

LANES = 128
PREFIX_CHUNK = 256
FFT_S2_BLOCK = 16
FFT_ROW_GROUP = 8
STRIP = 32
FFN_CHUNK = 256
VMEM_LIMIT = 56 * 1024 * 1024


U32 = jnp.uint32
HALF = D_MODEL // 2
ROW_TILE = HALF // LANES
HIGH_HALF = np.uint32(0xFFFF0000)


def _params(sem, vmem=None):
    return pltpu.CompilerParams(dimension_semantics=sem, vmem_limit_bytes=vmem)


def _store_rows_as_tiles(ref, val):
    m = val.shape[0]
    bits = lax.bitcast_convert_type(val.astype(BF16).astype(F32), U32)
    words = (bits[:, :HALF] >> 16) | (bits[:, HALF:] & HIGH_HALF)
    for c in range(ROW_TILE):
        ref[pl.ds(c, m, stride=ROW_TILE), :] = words[:, c * LANES:(c + 1) * LANES]


def _unpack_words(words, high):
    bits = (words & HIGH_HALF) if high else (words << 16)
    return lax.bitcast_convert_type(bits, F32).astype(BF16)


def _load_tiles_as_rows(ref):
    m = ref.shape[0] // ROW_TILE
    words = jnp.concatenate(
        [ref[pl.ds(c, m, stride=ROW_TILE), :] for c in range(ROW_TILE)], axis=1)
    return jnp.concatenate([_unpack_words(words, False), _unpack_words(words, True)], axis=1)


def _tiles_of_rows(ref, r, count):
    return ref.at[pl.ds(pl.multiple_of(r * ROW_TILE, ROW_TILE), count * ROW_TILE), :]


def _tile_of_row(ref, r):
    start = r * ROW_TILE
    if not isinstance(r, int):
        start = pl.multiple_of(start, ROW_TILE)
    return ref.at[pl.ds(start, ROW_TILE), :]


def _inproj_body(x_ref, g_ref, w_ref, zf_ref, q_ref, k_ref, v_ref):
    x = x_ref[...]
    ms = jnp.mean(x * x, axis=-1, keepdims=True)
    h = (x * lax.rsqrt(ms + EPS)) * g_ref[...]
    z = jnp.dot(h.astype(BF16), w_ref[...], preferred_element_type=F32)
    o = FOURIER_WIDTH
    g = FFT_ROW_GROUP
    for s1 in range(x.shape[0] // BLOCK):
        for jb in range(BLOCK // g):
            r = s1 * BLOCK + jb * g
            zf_ref[0, jb, s1 * g:(s1 + 1) * g, :] = z[r:r + g, :o]
    q_ref[...] = (z[:, o:o + ATTN_WIDTH] * (HEAD_DIM ** -0.5)).astype(BF16)
    o += ATTN_WIDTH
    k_ref[...] = z[:, o:o + KV_WIDTH].astype(BF16)
    o += KV_WIDTH
    v_ref[...] = z[:, o:o + KV_WIDTH].astype(BF16)


def _inproj(xf, gain, w_in, tm, seq):
    n = xf.shape[0]
    assert tm % BLOCK == 0 and seq % tm == 0
    tiles = seq // tm
    s1_rows = tm // BLOCK * FFT_ROW_GROUP
    nblk = BLOCK // FFT_ROW_GROUP
    row = lambda w: pl.BlockSpec((tm, w), lambda i: (i, 0))
    return pl.pallas_call(
        _inproj_body,
        grid=(n // tm,),
        in_specs=[row(D_MODEL),
                  pl.BlockSpec((1, D_MODEL), lambda i: (0, 0)),
                  pl.BlockSpec((D_MODEL, IN_WIDTH), lambda i: (0, 0))],
        out_specs=[pl.BlockSpec((1, nblk, s1_rows, FOURIER_WIDTH),
                                lambda i: (i // tiles, 0, i % tiles, 0)),
                   row(ATTN_WIDTH), row(KV_WIDTH), row(KV_WIDTH)],
        out_shape=[jax.ShapeDtypeStruct((n // seq, nblk, seq // BLOCK * FFT_ROW_GROUP,
                                         FOURIER_WIDTH), F32),
                   jax.ShapeDtypeStruct((n, ATTN_WIDTH), BF16),
                   jax.ShapeDtypeStruct((n, KV_WIDTH), BF16),
                   jax.ShapeDtypeStruct((n, KV_WIDTH), BF16)],
        compiler_params=_params(("parallel",), VMEM_LIMIT),
        name="inproj",
    )(xf, gain, w_in)


def _dft_tables(s):
    s2 = BLOCK
    s1 = s // s2
    a = np.arange(s1, dtype=np.float64)
    ang1 = 2.0 * np.pi * np.outer(a, a) / s1
    f1 = np.concatenate([np.cos(ang1), -np.sin(ang1)], axis=0)
    b = np.arange(s2, dtype=np.float64)
    ang2 = 2.0 * np.pi * np.outer(b, b) / s2
    angt = 2.0 * np.pi * np.outer(a, b) / s
    c = np.arange(GROUP_DIM, dtype=np.float64)
    angc = 2.0 * np.pi * np.outer(c, c) / GROUP_DIM
    cs = np.concatenate([np.cos(angc), np.sin(angc)], axis=0)
    return dict(
        f1=jnp.asarray(f1, BF16),
        f2r=jnp.asarray(np.cos(ang2), F32), f2i=jnp.asarray(-np.sin(ang2), F32),
        twr=jnp.asarray(np.cos(angt), F32), twi=jnp.asarray(-np.sin(angt), F32),
        cs=jnp.asarray(cs, BF16))


def _dft1_body(f_ref, *refs, nb):
    x_refs, o_ref, stage = refs[:-2], refs[-2], refs[-1]
    g = FFT_ROW_GROUP
    s1 = x_refs[0].shape[2] // g
    per_dot = 4
    for j0 in range(0, nb, per_dot):
        x = jnp.concatenate(
            [ref[0, j // g, pl.ds(j % g, s1, stride=g), :]
             for j in range(j0, j0 + per_dot) for ref in x_refs],
            axis=1).astype(BF16)
        res = jnp.dot(f_ref[...], x, preferred_element_type=F32)
        w = FOURIER_WIDTH
        for jj in range(per_dot):
            stage[0, :, j0 + jj, :] = res[:s1, jj * w:(jj + 1) * w]
            stage[1, :, j0 + jj, :] = res[s1:, jj * w:(jj + 1) * w]
    o_ref[0] = stage[...].astype(BF16)


def _dft1(zf4, f1):
    nb, g = FFT_S2_BLOCK, FFT_ROW_GROUP
    b, s2_groups, rows, w = zf4.shape
    s1, s2 = rows // g, s2_groups * g
    lane_block = lambda k: pl.BlockSpec((1, nb // g, rows, LANES), lambda i, j: (i, j, 0, k))
    nblk = w // LANES
    return pl.pallas_call(
        functools.partial(_dft1_body, nb=nb),
        grid=(b, s2 // nb),
        in_specs=[pl.BlockSpec((2 * s1, s1), lambda i, j: (0, 0))]
                 + [lane_block(k) for k in range(nblk)],
        out_specs=pl.BlockSpec((1, 2, s1, nb, w), lambda i, j: (i, 0, 0, j, 0)),
        out_shape=jax.ShapeDtypeStruct((b, 2, s1, s2, w), BF16),
        scratch_shapes=[pltpu.VMEM((2, s1, nb, w), F32)],
        compiler_params=_params(("parallel", "parallel"), VMEM_LIMIT),
        name="dft1",
    )(f1, *([zf4] * nblk))


def _dft2_body(a_ref, f2r_ref, f2i_ref, twr_ref, twi_ref, cs_ref, wf_ref, o_ref, stage,
               *, kb, scale):
    s2 = BLOCK
    fr = f2r_ref[...]
    fi = f2i_ref[...]
    ys = []
    for j in range(kb):
        tr = twr_ref[0, j:j + 1, :]
        ti = twi_ref[0, j:j + 1, :]
        gr = fr * tr - fi * ti
        gi = fr * ti + fi * tr
        lhs = jnp.concatenate(
            [jnp.concatenate([gr, -gi], axis=1), jnp.concatenate([gi, gr], axis=1)],
            axis=0).astype(BF16)
        rhs = jnp.concatenate([a_ref[0, 0, j], a_ref[0, 1, j]], axis=0)
        ys.append(jnp.dot(lhs, rhs, preferred_element_type=F32))
    cat = jnp.concatenate(
        [jnp.concatenate([y[:s2, g * GROUP_DIM:(g + 1) * GROUP_DIM],
                          y[s2:, g * GROUP_DIM:(g + 1) * GROUP_DIM]], axis=1)
         for g in range(FOURIER_GROUPS) for y in ys], axis=0).astype(BF16)
    re = (jnp.dot(cat, cs_ref[...], preferred_element_type=F32) * scale).astype(BF16)
    rows = kb * s2
    outs = [jnp.dot(re[g * rows:(g + 1) * rows], wf_ref[g], preferred_element_type=F32)
            for g in range(FOURIER_GROUPS)]
    for j in range(kb):
        stage[:, j, :] = jnp.concatenate(
            [o[j * s2:(j + 1) * s2] for o in outs], axis=1)
    o_ref[0] = stage[...].astype(BF16)


def _dft2(a5, tabs, wf, kb, scale):
    b, _, s1, s2, w = a5.shape
    const = lambda shape: pl.BlockSpec(shape, lambda i, j: (0,) * len(shape))
    twr = tabs["twr"].reshape(s1 // kb, kb, s2)
    twi = tabs["twi"].reshape(s1 // kb, kb, s2)
    return pl.pallas_call(
        functools.partial(_dft2_body, kb=kb, scale=scale),
        grid=(b, s1 // kb),
        in_specs=[pl.BlockSpec((1, 2, kb, s2, w), lambda i, j: (i, 0, j, 0, 0)),
                  const((s2, s2)), const((s2, s2)),
                  pl.BlockSpec((1, kb, s2), lambda i, j: (j, 0, 0)),
                  pl.BlockSpec((1, kb, s2), lambda i, j: (j, 0, 0)),
                  const((2 * GROUP_DIM, GROUP_DIM)),
                  const((FOURIER_GROUPS, GROUP_DIM, GROUP_DIM))],
        out_specs=pl.BlockSpec((1, s2, kb, w), lambda i, j: (i, 0, j, 0)),
        out_shape=jax.ShapeDtypeStruct((b, s2, s1, w), BF16),
        scratch_shapes=[pltpu.VMEM((s2, kb, w), F32)],
        compiler_params=_params(("parallel", "parallel"), VMEM_LIMIT),
        name="dft2",
    )(a5, tabs["f2r"], tabs["f2i"], twr, twi, tabs["cs"], wf)


def _fourier(zf, wf_bf16):
    b, s2_blocks, rows, w = zf.shape
    s = s2_blocks * rows
    tabs = _dft_tables(s)
    a = _dft1(zf, tabs["f1"])
    f = _dft2(a, tabs, wf_bf16, kb=16,
              scale=1.0 / math.sqrt(s * GROUP_DIM))
    return f.reshape(b, s, w)


def _bucket_table():
    qi = np.arange(BLOCK)[:, None]
    kj = np.arange(3 * BLOCK)[None, :]
    rel = kj - BLOCK - qi
    nb = N_BUCKETS // 2
    max_exact = nb // 2
    n = np.abs(rel)
    large = max_exact + np.floor(
        np.log(np.maximum(n, 1).astype(np.float64) / max_exact)
        / math.log(MAX_DISTANCE / max_exact) * (nb - max_exact) + 1e-9).astype(np.int64)
    large = np.minimum(large, nb - 1)
    bucket = np.where(rel > 0, nb, 0) + np.where(n < max_exact, n, large)
    return np.where(n <= WINDOW, bucket, -1).astype(np.int32)


def _bias_body(rb_ref, bucket_ref, o_ref):
    bk = bucket_ref[...]
    col = lax.broadcasted_iota(I32, bk.shape, 1)
    for h in range(N_HEADS):
        acc = jnp.full(bk.shape, -jnp.inf, F32)
        for b in range(N_BUCKETS):
            acc = jnp.where(bk == b, rb_ref[b, h], acc)
        o_ref[0, h] = acc
        o_ref[1, h] = jnp.where(col >= BLOCK, acc, -jnp.inf)
        o_ref[2, h] = jnp.where(col < 2 * BLOCK, acc, -jnp.inf)


def _bias_table(rel_bias):
    bucket = jnp.asarray(_bucket_table())
    shape = (3, N_HEADS) + bucket.shape
    return pl.pallas_call(
        _bias_body,
        in_specs=[pl.BlockSpec(memory_space=pltpu.SMEM),
                  pl.BlockSpec(bucket.shape, lambda: (0, 0))],
        out_specs=pl.BlockSpec(shape, lambda: (0, 0, 0, 0)),
        out_shape=jax.ShapeDtypeStruct(shape, F32),
        name="bias_table",
    )(rel_bias.astype(F32), bucket)


def _attn_body(sink_ref, q_ref, kp_ref, kc_ref, kn_ref, vp_ref, vc_ref, vn_ref, bias_ref,
               o_ref, *, tq):
    j = pl.program_id(1)
    last = pl.num_programs(1) - 1
    nsb = tq // BLOCK
    group = N_HEADS // N_KV_HEADS
    gw = group * HEAD_DIM
    kext = jnp.concatenate([kp_ref[0, tq - BLOCK:, :], kc_ref[0], kn_ref[0, :BLOCK, :]], axis=0)
    vext = jnp.concatenate([vp_ref[0, tq - BLOCK:, :], vc_ref[0], vn_ref[0, :BLOCK, :]], axis=0)
    low_half = lax.broadcasted_iota(I32, (1, KV_WIDTH), 1) < HEAD_DIM

    def per_kv_head(ext):
        x = ext.astype(F32)
        xr = pltpu.roll(x, HEAD_DIM, axis=1)
        return (jnp.where(low_half, x, xr).astype(BF16), jnp.where(low_half, xr, x).astype(BF16))

    krep = per_kv_head(kext)
    vrep = per_kv_head(vext)
    head_of_lane = lax.broadcasted_iota(I32, (1, gw), 1) // HEAD_DIM
    nt = (((1,), (1,)), ((), ()))
    def scores(sb, kh):
        rows = slice(sb * BLOCK, (sb + 3) * BLOCK)
        variant = 0
        if sb == 0:
            variant = jnp.where(j == 0, 1, 0)
        elif sb == nsb - 1:
            variant = jnp.where(j == last, 2, 0)
        qblk = q_ref[0, sb * BLOCK:(sb + 1) * BLOCK, kh * gw:(kh + 1) * gw]
        qm = jnp.concatenate(
            [jnp.where(head_of_lane == g, qblk, jnp.zeros_like(qblk)) for g in range(group)],
            axis=0)
        kb = jnp.concatenate([krep[kh][rows], krep[kh][rows]], axis=1)
        s = lax.dot_general(qm, kb, nt, preferred_element_type=F32)
        return s, variant

    def weights(sv, kh):
        s_all, variant = sv
        out = []
        for g in range(group):
            for r0 in range(0, BLOCK, STRIP):
                s = s_all[g * BLOCK + r0:g * BLOCK + r0 + STRIP]
                s = s + bias_ref[variant, kh * group + g, r0:r0 + STRIP, :]
                sink = sink_ref[kh * group + g]
                m = jnp.maximum(jnp.max(s, axis=-1, keepdims=True), sink)
                p = jnp.exp(s - m)
                denom = jnp.sum(p, axis=-1, keepdims=True) + jnp.exp(sink - m)
                out.append((p / denom).astype(BF16))
        return jnp.concatenate(out, axis=0)

    def values(pn, sb, kh):
        rows = slice(sb * BLOCK, (sb + 3) * BLOCK)
        o4 = jnp.dot(pn, vrep[kh][rows], preferred_element_type=F32)
        out = []
        for pair in range(group // 2):
            even = o4[(2 * pair) * BLOCK:(2 * pair + 1) * BLOCK]
            odd = o4[(2 * pair + 1) * BLOCK:(2 * pair + 2) * BLOCK]
            out.append(jnp.where(low_half, even, odd))
        return out

    items = [(sb, kh) for sb in range(nsb) for kh in range(N_KV_HEADS)]
    pieces = {}
    s_next = scores(*items[0])
    pn_prev = None
    for i, (sb, kh) in enumerate(items):
        s_cur = s_next
        if i + 1 < len(items):
            s_next = scores(*items[i + 1])
        if pn_prev is not None:
            pieces[items[i - 1]] = values(pn_prev, *items[i - 1])
        pn_prev = weights(s_cur, kh)
    pieces[items[-1]] = values(pn_prev, *items[-1])
    for sb in range(nsb):
        row = [x for kh in range(N_KV_HEADS) for x in pieces[(sb, kh)]]
        o_ref[0, sb * BLOCK:(sb + 1) * BLOCK, :] = jnp.concatenate(row, axis=1).astype(BF16)


def _attention(q, k, v, bias, sink, tq):
    b, s, _ = q.shape
    nt = s // tq
    assert tq >= 2 * BLOCK
    kv = lambda f: pl.BlockSpec((1, tq, KV_WIDTH), f)
    prev = lambda i, j: (i, jnp.maximum(j - 1, 0), 0)
    cur = lambda i, j: (i, j, 0)
    nxt = lambda i, j: (i, jnp.minimum(j + 1, nt - 1), 0)
    return pl.pallas_call(
        functools.partial(_attn_body, tq=tq),
        grid=(b, nt),
        in_specs=[pl.BlockSpec(memory_space=pltpu.SMEM),
                  pl.BlockSpec((1, tq, ATTN_WIDTH), cur),
                  kv(prev), kv(cur), kv(nxt), kv(prev), kv(cur), kv(nxt),
                  pl.BlockSpec(bias.shape, lambda i, j: (0, 0, 0, 0))],
        out_specs=pl.BlockSpec((1, tq, ATTN_WIDTH), cur),
        out_shape=jax.ShapeDtypeStruct((b, s, ATTN_WIDTH), BF16),
        compiler_params=_params(("parallel", "parallel"), VMEM_LIMIT),
        name="attn",
    )(sink.astype(F32), q, k, k, k, v, v, v, bias)


def _outproj_body(x_ref, f_ref, a_ref, wo_ref, g_ref, wr_ref, x1_ref, h2_ref, aff_ref):
    mix = jnp.concatenate([f_ref[...], a_ref[...]], axis=1)
    x1 = x_ref[...] + jnp.dot(mix, wo_ref[...], preferred_element_type=F32)
    x1_ref[...] = x1
    ms = jnp.mean(x1 * x1, axis=-1, keepdims=True)
    h2 = (x1 * lax.rsqrt(ms + EPS)) * g_ref[...]
    _store_rows_as_tiles(h2_ref, h2)
    hi = h2.astype(BF16)
    lo = (h2 - hi.astype(F32)).astype(BF16)
    nt = (((1,), (1,)), ((), ()))
    o1 = lax.dot_general(wr_ref[...], hi, nt, preferred_element_type=F32)
    o2 = lax.dot_general(wr_ref[:N_EXPERTS], lo, nt, preferred_element_type=F32)
    logits = o1[:N_EXPERTS] + (o1[N_EXPERTS:] + o2)
    e = jnp.exp(logits - jnp.max(logits, axis=0, keepdims=True))
    aff_ref[...] = e / jnp.sum(e, axis=0, keepdims=True)


def _outproj(xf, f, a, w_out, gain, w_router_t, tm):
    n = xf.shape[0]
    row = lambda w: pl.BlockSpec((tm, w), lambda i: (i, 0))
    const = lambda shape: pl.BlockSpec(shape, lambda i: (0, 0))
    return pl.pallas_call(
        _outproj_body,
        grid=(n // tm,),
        in_specs=[row(D_MODEL), row(FOURIER_WIDTH), row(ATTN_WIDTH),
                  const((D_MODEL, D_MODEL)), const((1, D_MODEL)),
                  const((2 * N_EXPERTS, D_MODEL))],
        out_specs=[row(D_MODEL),
                   pl.BlockSpec((tm * ROW_TILE, LANES), lambda i: (i, 0)),
                   pl.BlockSpec((N_EXPERTS, tm), lambda i: (0, i))],
        out_shape=[jax.ShapeDtypeStruct((n, D_MODEL), F32),
                   jax.ShapeDtypeStruct((n * ROW_TILE, LANES), U32),
                   jax.ShapeDtypeStruct((N_EXPERTS, n), F32)],
        compiler_params=_params(("parallel",), VMEM_LIMIT),
        name="outproj",
    )(xf, f, a, w_out, gain, w_router_t)


def _route_body(aff_ref, idx_ref, gate_ref, rank_ref, rsel_ref,
                d_a, d_b, n_a, n_b, g_a, g_b, *, n, cap):
    e = N_EXPERTS
    ch = PREFIX_CHUNK

    def search(i, tau):
        cand = tau | jnp.left_shift(jnp.int32(1), 30 - i)
        keys = lax.bitcast_convert_type(aff_ref[...], I32)
        cnt = jnp.sum((keys >= cand).astype(I32), axis=1, keepdims=True)
        return jnp.where(cnt >= cap, cand, tau)

    tau = lax.fori_loop(0, 31, search, jnp.zeros((e, 1), I32))
    keys = lax.bitcast_convert_type(aff_ref[...], I32)
    n_gt = jnp.sum((keys > tau).astype(I32), axis=1, keepdims=True)
    need = (cap - n_gt).astype(F32)

    r = lax.broadcasted_iota(I32, (ch, ch), 0)
    c = lax.broadcasted_iota(I32, (ch, ch), 1)
    upper = (r < c).astype(BF16)

    carry_eq = jnp.zeros((e, 1), F32)
    carry_m = jnp.zeros((e, 1), F32)
    for ci in range(n // ch):
        sl = slice(ci * ch, (ci + 1) * ch)
        k = lax.bitcast_convert_type(aff_ref[:, sl], I32)
        gt = k > tau
        eq = (k == tau).astype(F32)
        eq_ex = jnp.dot(eq.astype(BF16), upper, preferred_element_type=F32) + carry_eq
        carry_eq = carry_eq + jnp.sum(eq, axis=1, keepdims=True)
        m = jnp.where(gt | ((eq > 0.0) & (eq_ex < need)), 1.0, 0.0).astype(F32)
        rank = (jnp.dot(m.astype(BF16), upper, preferred_element_type=F32)
                + carry_m).astype(I32)
        carry_m = carry_m + jnp.sum(m, axis=1, keepdims=True)
        rank_ref[:, sl] = rank
        rsel_ref[:, sl] = jnp.where(m > 0.0, rank, -1)
        lane = lax.broadcasted_iota(I32, (e, ch), 1) + ci * ch
        d_a[:, sl] = jnp.where(m > 0.0, lane - rank, 0)
        n_a[:, sl] = lane
    g_a[...] = aff_ref[...]

    bufs = [(d_a, n_a, g_a), (d_b, n_b, g_b)]
    for b in range(int(math.log2(n))):
        src, dst = bufs[b % 2], bufs[(b + 1) % 2]
        sh = 1 << b
        d = src[0][...]
        moving = (lax.shift_right_logical(d, b) & 1) == 1
        d_in = pltpu.roll(d, n - sh, axis=1)
        arrive = (lax.shift_right_logical(d_in, b) & 1) == 1
        dst[0][...] = jnp.where(arrive, d_in, jnp.where(moving, 0, d))
        for t in (1, 2):
            v = src[t][...]
            dst[t][...] = jnp.where(arrive, pltpu.roll(v, n - sh, axis=1), v)
    fin = bufs[int(math.log2(n)) % 2]
    idx_ref[...] = fin[1][:, :cap]
    gate_ref[...] = fin[2][:, :cap]


def _route(aff_t, cap):
    e, n = aff_t.shape
    assert n & (n - 1) == 0 and n % PREFIX_CHUNK == 0
    full = lambda shape: pl.BlockSpec(shape, lambda: (0, 0))
    big_i = pltpu.VMEM((e, n), I32)
    return pl.pallas_call(
        functools.partial(_route_body, n=n, cap=cap),
        in_specs=[full((e, n))],
        out_specs=[full((e, cap)), full((e, cap)), full((e, n)), full((e, n))],
        out_shape=[jax.ShapeDtypeStruct((e, cap), I32),
                   jax.ShapeDtypeStruct((e, cap), F32),
                   jax.ShapeDtypeStruct((e, n), I32),
                   jax.ShapeDtypeStruct((e, n), I32)],
        scratch_shapes=[big_i, big_i, big_i, big_i,
                        pltpu.VMEM((e, n), F32), pltpu.VMEM((e, n), F32)],
        compiler_params=_params(None, VMEM_LIMIT),
        name="route",
    )(aff_t)


def _rows_to_cols(rows):
    r, m = rows.shape
    if r < 8:
        rows = jnp.concatenate([rows, jnp.zeros((8 - r, m), rows.dtype)], axis=0)
    return rows.T[:, :r]


def _ffn_body(idx_cur, idx_nxt, gate_ref, h_hbm, wg_ref, wu_ref, wd_ref, ye_ref,
              xa, xb, wg_s, wu_s, wd_s, sems, *, tm, spe):
    s = pl.program_id(0)
    last = pl.num_programs(0) - 1
    g_a, g_b = sems.at[0], sems.at[1]

    def gather(idx_ref, t, buf, sem):
        def one(i):
            queue = i % 2 if isinstance(i, int) else 0
            pltpu.async_copy(_tile_of_row(h_hbm, idx_ref[t, 0, i]),
                             _tile_of_row(buf, i), sem, priority=queue)
        return one

    def wait_gather(buf, sem):
        pltpu.make_async_copy(h_hbm.at[pl.ds(0, tm * ROW_TILE), :], buf, sem).wait()

    def compute(xbuf, t, copies):
        nc = D_EXPERT // FFN_CHUNK
        stages = 3 * nc
        todo = [(fn, i) for fn in copies for i in range(tm)]
        per_stage = -(-len(todo) // stages)

        def issue_some():
            for fn, i in todo[:per_stage]:
                fn(i)
            del todo[:per_stage]

        x = _load_tiles_as_rows(xbuf)
        cols = lambda c: slice(c * FFN_CHUNK, (c + 1) * FFN_CHUNK)

        def up(c):
            g = jnp.dot(x, wg_s[:, cols(c)], preferred_element_type=F32)
            issue_some()
            u = jnp.dot(x, wu_s[:, cols(c)], preferred_element_type=F32)
            issue_some()
            return g, u

        def act(gu):
            g, u = gu
            return (g * jax.nn.sigmoid(g) * u).astype(BF16)

        def down(hid, c):
            part = jnp.dot(hid, wd_s[cols(c), :], preferred_element_type=F32)
            issue_some()
            return part

        gu_next = up(0)
        hid_prev = None
        y = None
        for c in range(nc):
            gu_cur = gu_next
            if c + 1 < nc:
                gu_next = up(c + 1)
            if hid_prev is not None:
                part = down(hid_prev, c - 1)
                y = part if y is None else y + part
            hid_prev = act(gu_cur)
        y = y + down(hid_prev, nc - 1)
        assert not todo
        y = y * _rows_to_cols(gate_ref[t])
        _store_rows_as_tiles(ye_ref.at[pl.ds(t * tm * ROW_TILE, tm * ROW_TILE), :], y)

    @pl.when(s % spe == 0)
    def _():
        rows = 256
        for w_ref, w_s in ((wg_ref, wg_s), (wu_ref, wu_s), (wd_ref, wd_s)):
            for r in range(0, w_s.shape[0], rows):
                w_s[r:r + rows, :] = w_ref[0, r:r + rows, :].astype(BF16)

    @pl.when(s == 0)
    def _():
        pl.loop(0, tm)(gather(idx_cur, 0, xa, g_a))

    wait_gather(xa, g_a)
    compute(xa, 0, [gather(idx_cur, 1, xb, g_b)])

    wait_gather(xb, g_b)
    compute(xb, 1, [gather(idx_nxt, 0, xa, g_a)])

    @pl.when(s == last)
    def _():
        wait_gather(xa, g_a)


def _ffn(idx_c, gate_c, h2, wg, wu, wd, tm):
    e, cap = idx_c.shape
    spe = cap // (2 * tm)
    steps = e * spe
    idx3 = idx_c.reshape(2 * steps, 1, tm)
    gate3 = gate_c.reshape(2 * steps, 1, tm)
    cur = lambda i: (i, 0, 0)
    nxt = lambda i: (jnp.minimum(i + 1, steps - 1), 0, 0)
    smem = lambda f: pl.BlockSpec((2, 1, tm), f, memory_space=pltpu.SMEM)
    wspec = lambda a, b: pl.BlockSpec((1, a, b), lambda i: (i // spe, 0, 0))
    buf = pltpu.VMEM((tm * ROW_TILE, LANES), U32)
    return pl.pallas_call(
        functools.partial(_ffn_body, tm=tm, spe=spe),
        grid=(steps,),
        in_specs=[smem(cur), smem(nxt),
                  pl.BlockSpec((2, 1, tm), cur),
                  pl.BlockSpec(memory_space=pl.ANY),
                  wspec(D_MODEL, D_EXPERT), wspec(D_MODEL, D_EXPERT), wspec(D_EXPERT, D_MODEL)],
        out_specs=pl.BlockSpec((2 * tm * ROW_TILE, LANES), lambda i: (i, 0)),
        out_shape=jax.ShapeDtypeStruct((e * cap * ROW_TILE, LANES), U32),
        scratch_shapes=[buf, buf,
                        pltpu.VMEM((D_MODEL, D_EXPERT), BF16), pltpu.VMEM((D_MODEL, D_EXPERT), BF16),
                        pltpu.VMEM((D_EXPERT, D_MODEL), BF16), pltpu.SemaphoreType.DMA((2,))],
        compiler_params=_params(("arbitrary",), VMEM_LIMIT),
        name="ffn",
    )(idx3, idx3, gate3, h2, wg, wu, wd)


def _combine_body(st_ref, x1_ref, rsel_ref, g_ref, ye_hbm, o_ref, win, extra, sems,
                  *, tt, we, cap, n_tiles):
    i = pl.program_id(0)
    last = pl.num_programs(0) - 1
    buf = i % 2
    n_e = N_EXPERTS
    ws = n_e * we

    def run_start(e, tile):
        return st_ref[e * (n_tiles + 1) + tile]

    def clamp(nominal):
        return jnp.minimum(nominal, cap - we)

    def window_rows(e, start):
        return _tiles_of_rows(ye_hbm, e * cap + start, we)

    def first_fetch(tile, b, e):
        start = clamp((run_start(e, tile) // 8) * 8)
        dst = win.at[b, pl.ds(e * we * ROW_TILE, we * ROW_TILE), :]
        return pltpu.make_async_copy(window_rows(e, start), dst, sems.at[b])

    @pl.when(i == 0)
    def _():
        for e in range(n_e):
            first_fetch(0, 0, e).start()

    for e in range(n_e):
        first_fetch(i, buf, e).wait()

    @pl.when(i < last)
    def _():
        for e in range(n_e):
            first_fetch(i + 1, 1 - buf, e).start()

    rank_of = jnp.concatenate([_rows_to_cols(rsel_ref[0:8, :]),
                               _rows_to_cols(rsel_ref[8:n_e, :])], axis=1)
    lane = lax.broadcasted_iota(I32, (tt, we), 1)

    def selector(e, start, nominal):
        hit = (rank_of[:, e:e + 1] - start) == lane
        if nominal is not None:
            hit = hit & (lane >= nominal - start)
        return jnp.where(hit, 1.0, 0.0).astype(BF16)

    starts = [clamp((run_start(e, i) // 8) * 8) for e in range(n_e)]
    overflow = run_start(0, i + 1) - starts[0] > we
    for e in range(1, n_e):
        overflow = overflow | (run_start(e, i + 1) - starts[e] > we)

    def finish(x2_cols):
        ssq = sum(jnp.sum(x * x, axis=-1, keepdims=True) for x in x2_cols)
        scale = lax.rsqrt(ssq * (1.0 / D_MODEL) + EPS)
        for c, x in enumerate(x2_cols):
            w = x.shape[1]
            o_ref[:, c * w:(c + 1) * w] = (x * scale) * g_ref[:, c * w:(c + 1) * w]

    first = win.at[buf]
    sel = jnp.concatenate([selector(e, starts[e], None) for e in range(n_e)], axis=1)
    cw = 2 * LANES

    def window_cols(c):
        words = jnp.concatenate(
            [first[pl.ds(2 * (c % 2) + k, ws, stride=ROW_TILE), :] for k in range(2)], axis=1)
        return _unpack_words(words, c >= 2)

    x2_cols = []
    rows_next = window_cols(0)
    for c in range(D_MODEL // cw):
        rows_cur = rows_next
        if (c + 1) * cw < D_MODEL:
            rows_next = window_cols(c + 1)
        x2_cols.append(x1_ref[:, c * cw:(c + 1) * cw]
                       + jnp.dot(sel, rows_cur, preferred_element_type=F32))
    finish(x2_cols)

    @pl.when(overflow)
    def _():
        moe = jnp.zeros((tt, D_MODEL), F32)
        for e in range(n_e):
            base = (run_start(e, i) // 8) * 8
            nwin = (run_start(e, i + 1) - base + we - 1) // we

            def window(w, acc, e=e, base=base):
                nominal = base + w * we
                start = clamp(nominal)
                cp = pltpu.make_async_copy(window_rows(e, start), extra, sems.at[2])
                cp.start()
                cp.wait()
                return acc + jnp.dot(selector(e, start, nominal), _load_tiles_as_rows(extra),
                                     preferred_element_type=F32)

            moe = lax.fori_loop(0, nwin, window, moe)
        finish([x1_ref[...] + moe])


def _combine(x1, ye, rank, rsel, gain, tt, we):
    n = x1.shape[0]
    n_e = rank.shape[0]
    cap = ye.shape[0] // ROW_TILE // n_e
    n_tiles = n // tt
    assert cap >= we and cap % 8 == 0
    run_starts = jnp.concatenate([rank[:, ::tt], jnp.full((n_e, 1), cap, I32)], axis=1)
    return pl.pallas_call(
        functools.partial(_combine_body, tt=tt, we=we, cap=cap, n_tiles=n_tiles),
        grid_spec=pltpu.PrefetchScalarGridSpec(
            num_scalar_prefetch=1,
            grid=(n_tiles,),
            in_specs=[pl.BlockSpec((tt, D_MODEL), lambda i, s: (i, 0)),
                      pl.BlockSpec((n_e, tt), lambda i, s: (0, i)),
                      pl.BlockSpec((1, D_MODEL), lambda i, s: (0, 0)),
                      pl.BlockSpec(memory_space=pl.ANY)],
            out_specs=pl.BlockSpec((tt, D_MODEL), lambda i, s: (i, 0)),
            scratch_shapes=[pltpu.VMEM((2, n_e * we * ROW_TILE, LANES), U32),
                            pltpu.VMEM((we * ROW_TILE, LANES), U32),
                            pltpu.SemaphoreType.DMA((3,))]),
        out_shape=jax.ShapeDtypeStruct((n, D_MODEL), F32),
        compiler_params=_params(("arbitrary",), VMEM_LIMIT),
        name="combine",
    )(run_starts.reshape(-1), x1, rsel, gain, ye)


def _tile(n, pref):
    t = pref
    while n % t:
        t //= 2
    return t


def _trunk(x, p):
    b, s, d = x.shape
    n = b * s
    cap = CAPACITY_FACTOR * n // N_EXPERTS
    xf = x.reshape(n, d)
    tm = _tile(n, 1024)
    zf, q, k, v = _inproj(xf, p["norm_mix"], p["w_in"], _tile(s, 1024), s)
    f = _fourier(zf, p["w_fourier"])
    a = _attention(q.reshape(b, s, ATTN_WIDTH), k.reshape(b, s, KV_WIDTH),
                   v.reshape(b, s, KV_WIDTH), p["bias"], p["sink"], _tile(s, 1024))
    x1, h2, aff_t = _outproj(xf, f.reshape(n, FOURIER_WIDTH), a.reshape(n, ATTN_WIDTH),
                             p["w_out"], p["norm_ffn"], p["w_router"], tm)
    idx_c, gate_c, rank, rsel = _route(aff_t, cap)
    ye = _ffn(idx_c, gate_c, h2, p["w_gate"], p["w_up"], p["w_down"], _tile(cap // 2, 512))
    tt = _tile(n, 512)
    we = max(LANES, 2 * CAPACITY_FACTOR * tt // N_EXPERTS)
    y = _combine(x1, ye, rank, rsel, p["norm_final"], tt, we)
    return y.reshape(b, s, d)


def kernel(x_prompt, x_sample, w_in, w_fourier, attn_sink, rel_bias, w_out, norm_mix, norm_ffn,
           w_router, w_gate, w_up, w_down, norm_final):
    assert w_in.shape[0] == 1
    wr = w_router[0].T
    wr_hi = wr.astype(BF16)
    wr_lo = (wr - wr_hi.astype(F32)).astype(BF16)
    p = dict(
        norm_mix=norm_mix[0].reshape(1, D_MODEL),
        norm_ffn=norm_ffn[0].reshape(1, D_MODEL),
        norm_final=norm_final.reshape(1, D_MODEL),
        w_in=w_in[0].astype(BF16),
        w_fourier=w_fourier[0].astype(BF16),
        w_out=w_out[0].astype(BF16),
        w_router=jnp.concatenate([wr_hi, wr_lo], axis=0),
        w_gate=w_gate[0], w_up=w_up[0], w_down=w_down[0],
        sink=attn_sink[0],
        bias=_bias_table(rel_bias),
    )
    return _trunk(x_prompt, p), _trunk(x_sample, p)
```

```python
import functools
import math

import numpy as np
import jax
import jax.numpy as jnp
from jax import lax
from jax.experimental import pallas as pl
from jax.experimental.pallas import tpu as pltpu

F32 = jnp.float32
BF16 = jnp.bfloat16
I32 = jnp.int32

D_MODEL = 1024
FOURIER_GROUPS = 4
GROUP_DIM = 128
FOURIER_WIDTH = FOURIER_GROUPS * GROUP_DIM
N_HEADS = 8
N_KV_HEADS = 2
HEAD_DIM = 64
ATTN_WIDTH = N_HEADS * HEAD_DIM
KV_WIDTH = N_KV_HEADS * HEAD_DIM
WINDOW = 128
BLOCK = 128
N_BUCKETS = 32
MAX_DISTANCE = 128
IN_WIDTH = FOURIER_WIDTH + ATTN_WIDTH + 2 * KV_WIDTH
N_EXPERTS = 16
CAPACITY_FACTOR = 2
D_EXPERT = 1024
EPS = 1e-6

LANES = 128
PREFIX_CHUNK = 256
FFT_S2_BLOCK = 16
FFT_ROW_GROUP = 8
STRIP = 32
FFN_CHUNK = 256
VMEM_LIMIT = 56 * 1024 * 1024


U32 = jnp.uint32
HALF = D_MODEL // 2
ROW_TILE = HALF // LANES
HIGH_HALF = np.uint32(0xFFFF0000)


def _params(sem, vmem=None):
    return pltpu.CompilerParams(dimension_semantics=sem, vmem_limit_bytes=vmem)


def _store_rows_as_tiles(ref, val):
    m = val.shape[0]
    bits = lax.bitcast_convert_type(val.astype(BF16).astype(F32), U32)
    words = (bits[:, :HALF] >> 16) | (bits[:, HALF:] & HIGH_HALF)
    for c in range(ROW_TILE):
        ref[pl.ds(c, m, stride=ROW_TILE), :] = words[:, c * LANES:(c + 1) * LANES]


def _unpack_words(words, high):
    bits = (words & HIGH_HALF) if high else (words << 16)
    return lax.bitcast_convert_type(bits, F32).astype(BF16)


def _load_tiles_as_rows(ref):
    m = ref.shape[0] // ROW_TILE
    words = jnp.concatenate(
        [ref[pl.ds(c, m, stride=ROW_TILE), :] for c in range(ROW_TILE)], axis=1)
    return jnp.concatenate([_unpack_words(words, False), _unpack_words(words, True)], axis=1)


def _tiles_of_rows(ref, r, count):
    return ref.at[pl.ds(pl.multiple_of(r * ROW_TILE, ROW_TILE), count * ROW_TILE), :]


def _tile_of_row(ref, r):
    start = r * ROW_TILE
    if not isinstance(r, int):
        start = pl.multiple_of(start, ROW_TILE)
    return ref.at[pl.ds(start, ROW_TILE), :]


def _inproj_body(x_ref, g_ref, w_ref, zf_ref, q_ref, k_ref, v_ref):
    x = x_ref[...]
    ms = jnp.mean(x * x, axis=-1, keepdims=True)
    h = (x * lax.rsqrt(ms + EPS)) * g_ref[...]
    z = jnp.dot(h.astype(BF16), w_ref[...], preferred_element_type=F32)
    o = FOURIER_WIDTH
    g = FFT_ROW_GROUP
    for s1 in range(x.shape[0] // BLOCK):
        for jb in range(BLOCK // g):
            r = s1 * BLOCK + jb * g
            zf_ref[0, jb, s1 * g:(s1 + 1) * g, :] = z[r:r + g, :o]
    q_ref[...] = (z[:, o:o + ATTN_WIDTH] * (HEAD_DIM ** -0.5)).astype(BF16)
    o += ATTN_WIDTH
    k_ref[...] = z[:, o:o + KV_WIDTH].astype(BF16)
    o += KV_WIDTH
    v_ref[...] = z[:, o:o + KV_WIDTH].astype(BF16)


def _inproj(xf, gain, w_in, tm, seq):
    n = xf.shape[0]
    assert tm % BLOCK == 0 and seq % tm == 0
    tiles = seq // tm
    s1_rows = tm // BLOCK * FFT_ROW_GROUP
    nblk = BLOCK // FFT_ROW_GROUP
    row = lambda w: pl.BlockSpec((tm, w), lambda i: (i, 0))
    return pl.pallas_call(
        _inproj_body,
        grid=(n // tm,),
        in_specs=[row(D_MODEL),
                  pl.BlockSpec((1, D_MODEL), lambda i: (0, 0)),
                  pl.BlockSpec((D_MODEL, IN_WIDTH), lambda i: (0, 0))],
        out_specs=[pl.BlockSpec((1, nblk, s1_rows, FOURIER_WIDTH),
                                lambda i: (i // tiles, 0, i % tiles, 0)),
                   row(ATTN_WIDTH), row(KV_WIDTH), row(KV_WIDTH)],
        out_shape=[jax.ShapeDtypeStruct((n // seq, nblk, seq // BLOCK * FFT_ROW_GROUP,
                                         FOURIER_WIDTH), F32),
                   jax.ShapeDtypeStruct((n, ATTN_WIDTH), BF16),
                   jax.ShapeDtypeStruct((n, KV_WIDTH), BF16),
                   jax.ShapeDtypeStruct((n, KV_WIDTH), BF16)],
        compiler_params=_params(("parallel",), VMEM_LIMIT),
        name="inproj",
    )(xf, gain, w_in)


def _dft_tables(s):
    s2 = BLOCK
    s1 = s // s2
    a = np.arange(s1, dtype=np.float64)
    ang1 = 2.0 * np.pi * np.outer(a, a) / s1
    f1 = np.concatenate([np.cos(ang1), -np.sin(ang1)], axis=0)
    b = np.arange(s2, dtype=np.float64)
    ang2 = 2.0 * np.pi * np.outer(b, b) / s2
    angt = 2.0 * np.pi * np.outer(a, b) / s
    c = np.arange(GROUP_DIM, dtype=np.float64)
    angc = 2.0 * np.pi * np.outer(c, c) / GROUP_DIM
    cs = np.concatenate([np.cos(angc), np.sin(angc)], axis=0)
    return dict(
        f1=jnp.asarray(f1, BF16),
        f2r=jnp.asarray(np.cos(ang2), F32), f2i=jnp.asarray(-np.sin(ang2), F32),
        twr=jnp.asarray(np.cos(angt), F32), twi=jnp.asarray(-np.sin(angt), F32),
        cs=jnp.asarray(cs, BF16))


def _dft1_body(f_ref, *refs, nb):
    x_refs, o_ref, stage = refs[:-2], refs[-2], refs[-1]
    g = FFT_ROW_GROUP
    s1 = x_refs[0].shape[2] // g
    per_dot = 4
    for j0 in range(0, nb, per_dot):
        x = jnp.concatenate(
            [ref[0, j // g, pl.ds(j % g, s1, stride=g), :]
             for j in range(j0, j0 + per_dot) for ref in x_refs],
            axis=1).astype(BF16)
        res = jnp.dot(f_ref[...], x, preferred_element_type=F32)
        w = FOURIER_WIDTH
        for jj in range(per_dot):
            stage[0, :, j0 + jj, :] = res[:s1, jj * w:(jj + 1) * w]
            stage[1, :, j0 + jj, :] = res[s1:, jj * w:(jj + 1) * w]
    o_ref[0] = stage[...].astype(BF16)


def _dft1(zf4, f1):
    nb, g = FFT_S2_BLOCK, FFT_ROW_GROUP
    b, s2_groups, rows, w = zf4.shape
    s1, s2 = rows // g, s2_groups * g
    lane_block = lambda k: pl.BlockSpec((1, nb // g, rows, LANES), lambda i, j: (i, j, 0, k))
    nblk = w // LANES
    return pl.pallas_call(
        functools.partial(_dft1_body, nb=nb),
        grid=(b, s2 // nb),
        in_specs=[pl.BlockSpec((2 * s1, s1), lambda i, j: (0, 0))]
                 + [lane_block(k) for k in range(nblk)],
        out_specs=pl.BlockSpec((1, 2, s1, nb, w), lambda i, j: (i, 0, 0, j, 0)),
        out_shape=jax.ShapeDtypeStruct((b, 2, s1, s2, w), BF16),
        scratch_shapes=[pltpu.VMEM((2, s1, nb, w), F32)],
        compiler_params=_params(("parallel", "parallel"), VMEM_LIMIT),
        name="dft1",
    )(f1, *([zf4] * nblk))


def _dft2_body(a_ref, f2r_ref, f2i_ref, twr_ref, twi_ref, cs_ref, wf_ref, o_ref, stage,
               *, kb, scale):
    s2 = BLOCK
    fr = f2r_ref[...]
    fi = f2i_ref[...]
    ys = []
    for j in range(kb):
        tr = twr_ref[0, j:j + 1, :]
        ti = twi_ref[0, j:j + 1, :]
        gr = fr * tr - fi * ti
        gi = fr * ti + fi * tr
        lhs = jnp.concatenate(
            [jnp.concatenate([gr, -gi], axis=1), jnp.concatenate([gi, gr], axis=1)],
            axis=0).astype(BF16)
        rhs = jnp.concatenate([a_ref[0, 0, j], a_ref[0, 1, j]], axis=0)
        ys.append(jnp.dot(lhs, rhs, preferred_element_type=F32))
    cat = jnp.concatenate(
        [jnp.concatenate([y[:s2, g * GROUP_DIM:(g + 1) * GROUP_DIM],
                          y[s2:, g * GROUP_DIM:(g + 1) * GROUP_DIM]], axis=1)
         for g in range(FOURIER_GROUPS) for y in ys], axis=0).astype(BF16)
    re = (jnp.dot(cat, cs_ref[...], preferred_element_type=F32) * scale).astype(BF16)
    rows = kb * s2
    outs = [jnp.dot(re[g * rows:(g + 1) * rows], wf_ref[g], preferred_element_type=F32)
            for g in range(FOURIER_GROUPS)]
    for j in range(kb):
        stage[:, j, :] = jnp.concatenate(
            [o[j * s2:(j + 1) * s2] for o in outs], axis=1)
    o_ref[0] = stage[...].astype(BF16)


def _dft2(a5, tabs, wf, kb, scale):
    b, _, s1, s2, w = a5.shape
    const = lambda shape: pl.BlockSpec(shape, lambda i, j: (0,) * len(shape))
    twr = tabs["twr"].reshape(s1 // kb, kb, s2)
    twi = tabs["twi"].reshape(s1 // kb, kb, s2)
    return pl.pallas_call(
        functools.partial(_dft2_body, kb=kb, scale=scale),
        grid=(b, s1 // kb),
        in_specs=[pl.BlockSpec((1, 2, kb, s2, w), lambda i, j: (i, 0, j, 0, 0)),
                  const((s2, s2)), const((s2, s2)),
                  pl.BlockSpec((1, kb, s2), lambda i, j: (j, 0, 0)),
                  pl.BlockSpec((1, kb, s2), lambda i, j: (j, 0, 0)),
                  const((2 * GROUP_DIM, GROUP_DIM)),
                  const((FOURIER_GROUPS, GROUP_DIM, GROUP_DIM))],
        out_specs=pl.BlockSpec((1, s2, kb, w), lambda i, j: (i, 0, j, 0)),
        out_shape=jax.ShapeDtypeStruct((b, s2, s1, w), BF16),
        scratch_shapes=[pltpu.VMEM((s2, kb, w), F32)],
        compiler_params=_params(("parallel", "parallel"), VMEM_LIMIT),
        name="dft2",
    )(a5, tabs["f2r"], tabs["f2i"], twr, twi, tabs["cs"], wf)


def _fourier(zf, wf_bf16):
    b, s2_blocks, rows, w = zf.shape
    s = s2_blocks * rows
    tabs = _dft_tables(s)
    a = _dft1(zf, tabs["f1"])
    f = _dft2(a, tabs, wf_bf16, kb=16,
              scale=1.0 / math.sqrt(s * GROUP_DIM))
    return f.reshape(b, s, w)


def _bucket_table():
    qi = np.arange(BLOCK)[:, None]
    kj = np.arange(3 * BLOCK)[None, :]
    rel = kj - BLOCK - qi
    nb = N_BUCKETS // 2
    max_exact = nb // 2
    n = np.abs(rel)
    large = max_exact + np.floor(
        np.log(np.maximum(n, 1).astype(np.float64) / max_exact)
        / math.log(MAX_DISTANCE / max_exact) * (nb - max_exact) + 1e-9).astype(np.int64)
    large = np.minimum(large, nb - 1)
    bucket = np.where(rel > 0, nb, 0) + np.where(n < max_exact, n, large)
    return np.where(n <= WINDOW, bucket, -1).astype(np.int32)


def _bias_body(rb_ref, bucket_ref, o_ref):
    bk = bucket_ref[...]
    col = lax.broadcasted_iota(I32, bk.shape, 1)
    for h in range(N_HEADS):
        acc = jnp.full(bk.shape, -jnp.inf, F32)
        for b in range(N_BUCKETS):
            acc = jnp.where(bk == b, rb_ref[b, h], acc)
        o_ref[0, h] = acc
        o_ref[1, h] = jnp.where(col >= BLOCK, acc, -jnp.inf)
        o_ref[2, h] = jnp.where(col < 2 * BLOCK, acc, -jnp.inf)


def _bias_table(rel_bias):
    bucket = jnp.asarray(_bucket_table())
    shape = (3, N_HEADS) + bucket.shape
    return pl.pallas_call(
        _bias_body,
        in_specs=[pl.BlockSpec(memory_space=pltpu.SMEM),
                  pl.BlockSpec(bucket.shape, lambda: (0, 0))],
        out_specs=pl.BlockSpec(shape, lambda: (0, 0, 0, 0)),
        out_shape=jax.ShapeDtypeStruct(shape, F32),
        name="bias_table",
    )(rel_bias.astype(F32), bucket)


def _attn_body(sink_ref, q_ref, kp_ref, kc_ref, kn_ref, vp_ref, vc_ref, vn_ref, bias_ref,
               o_ref, *, tq):
    j = pl.program_id(1)
    last = pl.num_programs(1) - 1
    nsb = tq // BLOCK
    group = N_HEADS // N_KV_HEADS
    gw = group * HEAD_DIM
    kext = jnp.concatenate([kp_ref[0, tq - BLOCK:, :], kc_ref[0], kn_ref[0, :BLOCK, :]], axis=0)
    vext = jnp.concatenate([vp_ref[0, tq - BLOCK:, :], vc_ref[0], vn_ref[0, :BLOCK, :]], axis=0)
    low_half = lax.broadcasted_iota(I32, (1, KV_WIDTH), 1) < HEAD_DIM

    def per_kv_head(ext):
        x = ext.astype(F32)
        xr = pltpu.roll(x, HEAD_DIM, axis=1)
        return (jnp.where(low_half, x, xr).astype(BF16), jnp.where(low_half, xr, x).astype(BF16))

    krep = per_kv_head(kext)
    vrep = per_kv_head(vext)
    head_of_lane = lax.broadcasted_iota(I32, (1, gw), 1) // HEAD_DIM
    nt = (((1,), (1,)), ((), ()))
    def scores(sb, kh):
        rows = slice(sb * BLOCK, (sb + 3) * BLOCK)
        variant = 0
        if sb == 0:
            variant = jnp.where(j == 0, 1, 0)
        elif sb == nsb - 1:
            variant = jnp.where(j == last, 2, 0)
        qblk = q_ref[0, sb * BLOCK:(sb + 1) * BLOCK, kh * gw:(kh + 1) * gw]
        qm = jnp.concatenate(
            [jnp.where(head_of_lane == g, qblk, jnp.zeros_like(qblk)) for g in range(group)],
            axis=0)
        kb = jnp.concatenate([krep[kh][rows], krep[kh][rows]], axis=1)
        s = lax.dot_general(qm, kb, nt, preferred_element_type=F32)
        return s, variant

    def weights(sv, kh):
        s_all, variant = sv
        out = []
        for g in range(group):
            for r0 in range(0, BLOCK, STRIP):
                s = s_all[g * BLOCK + r0:g * BLOCK + r0 + STRIP]
                s = s + bias_ref[variant, kh * group + g, r0:r0 + STRIP, :]
                sink = sink_ref[kh * group + g]
                m = jnp.maximum(jnp.max(s, axis=-1, keepdims=True), sink)
                p = jnp.exp(s - m)
                denom = jnp.sum(p, axis=-1, keepdims=True) + jnp.exp(sink - m)
                out.append((p / denom).astype(BF16))
        return jnp.concatenate(out, axis=0)

    def values(pn, sb, kh):
        rows = slice(sb * BLOCK, (sb + 3) * BLOCK)
        o4 = jnp.dot(pn, vrep[kh][rows], preferred_element_type=F32)
        out = []
        for pair in range(group // 2):
            even = o4[(2 * pair) * BLOCK:(2 * pair + 1) * BLOCK]
            odd = o4[(2 * pair + 1) * BLOCK:(2 * pair + 2) * BLOCK]
            out.append(jnp.where(low_half, even, odd))
        return out

    items = [(sb, kh) for sb in range(nsb) for kh in range(N_KV_HEADS)]
    pieces = {}
    s_next = scores(*items[0])
    pn_prev = None
    for i, (sb, kh) in enumerate(items):
        s_cur = s_next
        if i + 1 < len(items):
            s_next = scores(*items[i + 1])
        if pn_prev is not None:
            pieces[items[i - 1]] = values(pn_prev, *items[i - 1])
        pn_prev = weights(s_cur, kh)
    pieces[items[-1]] = values(pn_prev, *items[-1])
    for sb in range(nsb):
        row = [x for kh in range(N_KV_HEADS) for x in pieces[(sb, kh)]]
        o_ref[0, sb * BLOCK:(sb + 1) * BLOCK, :] = jnp.concatenate(row, axis=1).astype(BF16)


def _attention(q, k, v, bias, sink, tq):
    b, s, _ = q.shape
    nt = s // tq
    assert tq >= 2 * BLOCK
    kv = lambda f: pl.BlockSpec((1, tq, KV_WIDTH), f)
    prev = lambda i, j: (i, jnp.maximum(j - 1, 0), 0)
    cur = lambda i, j: (i, j, 0)
    nxt = lambda i, j: (i, jnp.minimum(j + 1, nt - 1), 0)
    return pl.pallas_call(
        functools.partial(_attn_body, tq=tq),
        grid=(b, nt),
        in_specs=[pl.BlockSpec(memory_space=pltpu.SMEM),
                  pl.BlockSpec((1, tq, ATTN_WIDTH), cur),
                  kv(prev), kv(cur), kv(nxt), kv(prev), kv(cur), kv(nxt),
                  pl.BlockSpec(bias.shape, lambda i, j: (0, 0, 0, 0))],
        out_specs=pl.BlockSpec((1, tq, ATTN_WIDTH), cur),
        out_shape=jax.ShapeDtypeStruct((b, s, ATTN_WIDTH), BF16),
        compiler_params=_params(("parallel", "parallel"), VMEM_LIMIT),
        name="attn",
    )(sink.astype(F32), q, k, k, k, v, v, v, bias)


def _outproj_body(x_ref, f_ref, a_ref, wo_ref, g_ref, wr_ref, x1_ref, h2_ref, aff_ref):
    mix = jnp.concatenate([f_ref[...], a_ref[...]], axis=1)
    x1 = x_ref[...] + jnp.dot(mix, wo_ref[...], preferred_element_type=F32)
    x1_ref[...] = x1
    ms = jnp.mean(x1 * x1, axis=-1, keepdims=True)
    h2 = (x1 * lax.rsqrt(ms + EPS)) * g_ref[...]
    _store_rows_as_tiles(h2_ref, h2)
    hi = h2.astype(BF16)
    lo = (h2 - hi.astype(F32)).astype(BF16)
    nt = (((1,), (1,)), ((), ()))
    o1 = lax.dot_general(wr_ref[...], hi, nt, preferred_element_type=F32)
    o2 = lax.dot_general(wr_ref[:N_EXPERTS], lo, nt, preferred_element_type=F32)
    logits = o1[:N_EXPERTS] + (o1[N_EXPERTS:] + o2)
    e = jnp.exp(logits - jnp.max(logits, axis=0, keepdims=True))
    aff_ref[...] = e / jnp.sum(e, axis=0, keepdims=True)


def _outproj(xf, f, a, w_out, gain, w_router_t, tm):
    n = xf.shape[0]
    row = lambda w: pl.BlockSpec((tm, w), lambda i: (i, 0))
    const = lambda shape: pl.BlockSpec(shape, lambda i: (0, 0))
    return pl.pallas_call(
        _outproj_body,
        grid=(n // tm,),
        in_specs=[row(D_MODEL), row(FOURIER_WIDTH), row(ATTN_WIDTH),
                  const((D_MODEL, D_MODEL)), const((1, D_MODEL)),
                  const((2 * N_EXPERTS, D_MODEL))],
        out_specs=[row(D_MODEL),
                   pl.BlockSpec((tm * ROW_TILE, LANES), lambda i: (i, 0)),
                   pl.BlockSpec((N_EXPERTS, tm), lambda i: (0, i))],
        out_shape=[jax.ShapeDtypeStruct((n, D_MODEL), F32),
                   jax.ShapeDtypeStruct((n * ROW_TILE, LANES), U32),
                   jax.ShapeDtypeStruct((N_EXPERTS, n), F32)],
        compiler_params=_params(("parallel",), VMEM_LIMIT),
        name="outproj",
    )(xf, f, a, w_out, gain, w_router_t)


def _route_body(aff_ref, idx_ref, gate_ref, rank_ref, rsel_ref,
                d_a, d_b, n_a, n_b, g_a, g_b, *, n, cap):
    e = N_EXPERTS
    ch = PREFIX_CHUNK

    def search(i, tau):
        cand = tau | jnp.left_shift(jnp.int32(1), 30 - i)
        keys = lax.bitcast_convert_type(aff_ref[...], I32)
        cnt = jnp.sum((keys >= cand).astype(I32), axis=1, keepdims=True)
        return jnp.where(cnt >= cap, cand, tau)

    tau = lax.fori_loop(0, 31, search, jnp.zeros((e, 1), I32))
    keys = lax.bitcast_convert_type(aff_ref[...], I32)
    n_gt = jnp.sum((keys > tau).astype(I32), axis=1, keepdims=True)
    need = (cap - n_gt).astype(F32)

    r = lax.broadcasted_iota(I32, (ch, ch), 0)
    c = lax.broadcasted_iota(I32, (ch, ch), 1)
    upper = (r < c).astype(BF16)

    carry_eq = jnp.zeros((e, 1), F32)
    carry_m = jnp.zeros((e, 1), F32)
    for ci in range(n // ch):
        sl = slice(ci * ch, (ci + 1) * ch)
        k = lax.bitcast_convert_type(aff_ref[:, sl], I32)
        gt = k > tau
        eq = (k == tau).astype(F32)
        eq_ex = jnp.dot(eq.astype(BF16), upper, preferred_element_type=F32) + carry_eq
        carry_eq = carry_eq + jnp.sum(eq, axis=1, keepdims=True)
        m = jnp.where(gt | ((eq > 0.0) & (eq_ex < need)), 1.0, 0.0).astype(F32)
        rank = (jnp.dot(m.astype(BF16), upper, preferred_element_type=F32)
                + carry_m).astype(I32)
        carry_m = carry_m + jnp.sum(m, axis=1, keepdims=True)
        rank_ref[:, sl] = rank
        rsel_ref[:, sl] = jnp.where(m > 0.0, rank, -1)
        lane = lax.broadcasted_iota(I32, (e, ch), 1) + ci * ch
        d_a[:, sl] = jnp.where(m > 0.0, lane - rank, 0)
        n_a[:, sl] = lane
    g_a[...] = aff_ref[...]

    bufs = [(d_a, n_a, g_a), (d_b, n_b, g_b)]
    for b in range(int(math.log2(n))):
        src, dst = bufs[b % 2], bufs[(b + 1) % 2]
        sh = 1 << b
        d = src[0][...]
        moving = (lax.shift_right_logical(d, b) & 1) == 1
        d_in = pltpu.roll(d, n - sh, axis=1)
        arrive = (lax.shift_right_logical(d_in, b) & 1) == 1
        dst[0][...] = jnp.where(arrive, d_in, jnp.where(moving, 0, d))
        for t in (1, 2):
            v = src[t][...]
            dst[t][...] = jnp.where(arrive, pltpu.roll(v, n - sh, axis=1), v)
    fin = bufs[int(math.log2(n)) % 2]
    idx_ref[...] = fin[1][:, :cap]
    gate_ref[...] = fin[2][:, :cap]


def _route(aff_t, cap):
    e, n = aff_t.shape
    assert n & (n - 1) == 0 and n % PREFIX_CHUNK == 0
    full = lambda shape: pl.BlockSpec(shape, lambda: (0, 0))
    big_i = pltpu.VMEM((e, n), I32)
    return pl.pallas_call(
        functools.partial(_route_body, n=n, cap=cap),
        in_specs=[full((e, n))],
        out_specs=[full((e, cap)), full((e, cap)), full((e, n)), full((e, n))],
        out_shape=[jax.ShapeDtypeStruct((e, cap), I32),
                   jax.ShapeDtypeStruct((e, cap), F32),
                   jax.ShapeDtypeStruct((e, n), I32),
                   jax.ShapeDtypeStruct((e, n), I32)],
        scratch_shapes=[big_i, big_i, big_i, big_i,
                        pltpu.VMEM((e, n), F32), pltpu.VMEM((e, n), F32)],
        compiler_params=_params(None, VMEM_LIMIT),
        name="route",
    )(aff_t)


def _rows_to_cols(rows):
    r, m = rows.shape
    if r < 8:
        rows = jnp.concatenate([rows, jnp.zeros((8 - r, m), rows.dtype)], axis=0)
    return rows.T[:, :r]


def _ffn_body(idx_cur, idx_nxt, gate_ref, h_hbm, wg_ref, wu_ref, wd_ref, ye_ref,
              xa, xb, wg_s, wu_s, wd_s, sems, *, tm, spe):
    s = pl.program_id(0)
    last = pl.num_programs(0) - 1
    g_a, g_b = sems.at[0], sems.at[1]

    def gather(idx_ref, t, buf, sem):
        def one(i):
            queue = i % 2 if isinstance(i, int) else 0
            pltpu.async_copy(_tile_of_row(h_hbm, idx_ref[t, 0, i]),
                             _tile_of_row(buf, i), sem, priority=queue)
        return one

    def wait_gather(buf, sem):
        pltpu.make_async_copy(h_hbm.at[pl.ds(0, tm * ROW_TILE), :], buf, sem).wait()

    def compute(xbuf, t, copies):
        nc = D_EXPERT // FFN_CHUNK
        stages = 3 * nc
        todo = [(fn, i) for fn in copies for i in range(tm)]
        per_stage = -(-len(todo) // stages)

        def issue_some():
            for fn, i in todo[:per_stage]:
                fn(i)
            del todo[:per_stage]

        x = _load_tiles_as_rows(xbuf)
        cols = lambda c: slice(c * FFN_CHUNK, (c + 1) * FFN_CHUNK)

        def up(c):
            g = jnp.dot(x, wg_s[:, cols(c)], preferred_element_type=F32)
            issue_some()
            u = jnp.dot(x, wu_s[:, cols(c)], preferred_element_type=F32)
            issue_some()
            return g, u

        def act(gu):
            g, u = gu
            return (g * jax.nn.sigmoid(g) * u).astype(BF16)

        def down(hid, c):
            part = jnp.dot(hid, wd_s[cols(c), :], preferred_element_type=F32)
            issue_some()
            return part

        gu_next = up(0)
        hid_prev = None
        y = None
        for c in range(nc):
            gu_cur = gu_next
            if c + 1 < nc:
                gu_next = up(c + 1)
            if hid_prev is not None:
                part = down(hid_prev, c - 1)
                y = part if y is None else y + part
            hid_prev = act(gu_cur)
        y = y + down(hid_prev, nc - 1)
        assert not todo
        y = y * _rows_to_cols(gate_ref[t])
        _store_rows_as_tiles(ye_ref.at[pl.ds(t * tm * ROW_TILE, tm * ROW_TILE), :], y)

    @pl.when(s % spe == 0)
    def _():
        rows = 256
        for w_ref, w_s in ((wg_ref, wg_s), (wu_ref, wu_s), (wd_ref, wd_s)):
            for r in range(0, w_s.shape[0], rows):
                w_s[r:r + rows, :] = w_ref[0, r:r + rows, :].astype(BF16)

    @pl.when(s == 0)
    def _():
        pl.loop(0, tm)(gather(idx_cur, 0, xa, g_a))

    wait_gather(xa, g_a)
    compute(xa, 0, [gather(idx_cur, 1, xb, g_b)])

    wait_gather(xb, g_b)
    compute(xb, 1, [gather(idx_nxt, 0, xa, g_a)])

    @pl.when(s == last)
    def _():
        wait_gather(xa, g_a)


def _ffn(idx_c, gate_c, h2, wg, wu, wd, tm):
    e, cap = idx_c.shape
    spe = cap // (2 * tm)
    steps = e * spe
    idx3 = idx_c.reshape(2 * steps, 1, tm)
    gate3 = gate_c.reshape(2 * steps, 1, tm)
    cur = lambda i: (i, 0, 0)
    nxt = lambda i: (jnp.minimum(i + 1, steps - 1), 0, 0)
    smem = lambda f: pl.BlockSpec((2, 1, tm), f, memory_space=pltpu.SMEM)
    wspec = lambda a, b: pl.BlockSpec((1, a, b), lambda i: (i // spe, 0, 0))
    buf = pltpu.VMEM((tm * ROW_TILE, LANES), U32)
    return pl.pallas_call(
        functools.partial(_ffn_body, tm=tm, spe=spe),
        grid=(steps,),
        in_specs=[smem(cur), smem(nxt),
                  pl.BlockSpec((2, 1, tm), cur),
                  pl.BlockSpec(memory_space=pl.ANY),
                  wspec(D_MODEL, D_EXPERT), wspec(D_MODEL, D_EXPERT), wspec(D_EXPERT, D_MODEL)],
        out_specs=pl.BlockSpec((2 * tm * ROW_TILE, LANES), lambda i: (i, 0)),
        out_shape=jax.ShapeDtypeStruct((e * cap * ROW_TILE, LANES), U32),
        scratch_shapes=[buf, buf,
                        pltpu.VMEM((D_MODEL, D_EXPERT), BF16), pltpu.VMEM((D_MODEL, D_EXPERT), BF16),
                        pltpu.VMEM((D_EXPERT, D_MODEL), BF16), pltpu.SemaphoreType.DMA((2,))],
        compiler_params=_params(("arbitrary",), VMEM_LIMIT),
        name="ffn",
    )(idx3, idx3, gate3, h2, wg, wu, wd)


def _combine_body(st_ref, x1_ref, rsel_ref, g_ref, ye_hbm, o_ref, win, extra, sems,
                  *, tt, we, cap, n_tiles):
    i = pl.program_id(0)
    last = pl.num_programs(0) - 1
    buf = i % 2
    n_e = N_EXPERTS
    ws = n_e * we

    def run_start(e, tile):
        return st_ref[e * (n_tiles + 1) + tile]

    def clamp(nominal):
        return jnp.minimum(nominal, cap - we)

    def window_rows(e, start):
        return _tiles_of_rows(ye_hbm, e * cap + start, we)

    def first_fetch(tile, b, e):
        start = clamp((run_start(e, tile) // 8) * 8)
        dst = win.at[b, pl.ds(e * we * ROW_TILE, we * ROW_TILE), :]
        return pltpu.make_async_copy(window_rows(e, start), dst, sems.at[b])

    @pl.when(i == 0)
    def _():
        for e in range(n_e):
            first_fetch(0, 0, e).start()

    for e in range(n_e):
        first_fetch(i, buf, e).wait()

    @pl.when(i < last)
    def _():
        for e in range(n_e):
            first_fetch(i + 1, 1 - buf, e).start()

    rank_of = jnp.concatenate([_rows_to_cols(rsel_ref[0:8, :]),
                               _rows_to_cols(rsel_ref[8:n_e, :])], axis=1)
    lane = lax.broadcasted_iota(I32, (tt, we), 1)

    def selector(e, start, nominal):
        hit = (rank_of[:, e:e + 1] - start) == lane
        if nominal is not None:
            hit = hit & (lane >= nominal - start)
        return jnp.where(hit, 1.0, 0.0).astype(BF16)

    starts = [clamp((run_start(e, i) // 8) * 8) for e in range(n_e)]
    overflow = run_start(0, i + 1) - starts[0] > we
    for e in range(1, n_e):
        overflow = overflow | (run_start(e, i + 1) - starts[e] > we)

    def finish(x2_cols):
        ssq = sum(jnp.sum(x * x, axis=-1, keepdims=True) for x in x2_cols)
        scale = lax.rsqrt(ssq * (1.0 / D_MODEL) + EPS)
        for c, x in enumerate(x2_cols):
            w = x.shape[1]
            o_ref[:, c * w:(c + 1) * w] = (x * scale) * g_ref[:, c * w:(c + 1) * w]

    first = win.at[buf]
    sel = jnp.concatenate([selector(e, starts[e], None) for e in range(n_e)], axis=1)
    cw = 2 * LANES

    def window_words(b):
        return jnp.concatenate(
            [first[pl.ds(2 * b + k, ws, stride=ROW_TILE), :] for k in range(2)], axis=1)

    nblk = HALF // cw
    x2_cols = [None] * (2 * nblk)
    words_next = window_words(0)
    for b in range(nblk):
        words = words_next
        if b + 1 < nblk:
            words_next = window_words(b + 1)
        for high in (False, True):
            c = b + nblk * high
            x2_cols[c] = (x1_ref[:, c * cw:(c + 1) * cw]
                          + jnp.dot(sel, _unpack_words(words, high), preferred_element_type=F32))
    finish(x2_cols)

    @pl.when(overflow)
    def _():
        moe = jnp.zeros((tt, D_MODEL), F32)
        for e in range(n_e):
            base = (run_start(e, i) // 8) * 8
            nwin = (run_start(e, i + 1) - base + we - 1) // we

            def window(w, acc, e=e, base=base):
                nominal = base + w * we
                start = clamp(nominal)
                cp = pltpu.make_async_copy(window_rows(e, start), extra, sems.at[2])
                cp.start()
                cp.wait()
                return acc + jnp.dot(selector(e, start, nominal), _load_tiles_as_rows(extra),
                                     preferred_element_type=F32)

            moe = lax.fori_loop(0, nwin, window, moe)
        finish([x1_ref[...] + moe])


def _combine(x1, ye, rank, rsel, gain, tt, we):
    n = x1.shape[0]
    n_e = rank.shape[0]
    cap = ye.shape[0] // ROW_TILE // n_e
    n_tiles = n // tt
    assert cap >= we and cap % 8 == 0
    run_starts = jnp.concatenate([rank[:, ::tt], jnp.full((n_e, 1), cap, I32)], axis=1)
    return pl.pallas_call(
        functools.partial(_combine_body, tt=tt, we=we, cap=cap, n_tiles=n_tiles),
        grid_spec=pltpu.PrefetchScalarGridSpec(
            num_scalar_prefetch=1,
            grid=(n_tiles,),
            in_specs=[pl.BlockSpec((tt, D_MODEL), lambda i, s: (i, 0)),
                      pl.BlockSpec((n_e, tt), lambda i, s: (0, i)),
                      pl.BlockSpec((1, D_MODEL), lambda i, s: (0, 0)),
                      pl.BlockSpec(memory_space=pl.ANY)],
            out_specs=pl.BlockSpec((tt, D_MODEL), lambda i, s: (i, 0)),
            scratch_shapes=[pltpu.VMEM((2, n_e * we * ROW_TILE, LANES), U32),
                            pltpu.VMEM((we * ROW_TILE, LANES), U32),
                            pltpu.SemaphoreType.DMA((3,))]),
        out_shape=jax.ShapeDtypeStruct((n, D_MODEL), F32),
        compiler_params=_params(("arbitrary",), VMEM_LIMIT),
        name="combine",
    )(run_starts.reshape(-1), x1, rsel, gain, ye)


def _tile(n, pref):
    t = pref
    while n % t:
        t //= 2
    return t


def _trunk(x, p):
    b, s, d = x.shape
    n = b * s
    cap = CAPACITY_FACTOR * n // N_EXPERTS
    xf = x.reshape(n, d)
    tm = _tile(n, 1024)
    zf, q, k, v = _inproj(xf, p["norm_mix"], p["w_in"], _tile(s, 1024), s)
    f = _fourier(zf, p["w_fourier"])
    a = _attention(q.reshape(b, s, ATTN_WIDTH), k.reshape(b, s, KV_WIDTH),
                   v.reshape(b, s, KV_WIDTH), p["bias"], p["sink"], _tile(s, 1024))
    x1, h2, aff_t = _outproj(xf, f.reshape(n, FOURIER_WIDTH), a.reshape(n, ATTN_WIDTH),
                             p["w_out"], p["norm_ffn"], p["w_router"], tm)
    idx_c, gate_c, rank, rsel = _route(aff_t, cap)
    ye = _ffn(idx_c, gate_c, h2, p["w_gate"], p["w_up"], p["w_down"], _tile(cap // 2, 512))
    tt = _tile(n, 512)
    we = max(LANES, 2 * CAPACITY_FACTOR * tt // N_EXPERTS)
    y = _combine(x1, ye, rank, rsel, p["norm_final"], tt, we)
    return y.reshape(b, s, d)


def kernel(x_prompt, x_sample, w_in, w_fourier, attn_sink, rel_bias, w_out, norm_mix, norm_ffn,
           w_router, w_gate, w_up, w_down, norm_final):
    assert w_in.shape[0] == 1
    wr = w_router[0].T
    wr_hi = wr.astype(BF16)
    wr_lo = (wr - wr_hi.astype(F32)).astype(BF16)
    p = dict(
        norm_mix=norm_mix[0].reshape(1, D_MODEL),
        norm_ffn=norm_ffn[0].reshape(1, D_MODEL),
        norm_final=norm_final.reshape(1, D_MODEL),
        w_in=w_in[0].astype(BF16),
        w_fourier=w_fourier[0].astype(BF16),
        w_out=w_out[0].astype(BF16),
        w_router=jnp.concatenate([wr_hi, wr_lo], axis=0),
        w_gate=w_gate[0], w_up=w_up[0], w_down=w_down[0],
        sink=attn_sink[0],
        bias=_bias_table(rel_bias),
    )
    return _trunk(x_prompt, p), _trunk(x_sample, p)
```

```python
import functools
import math

import numpy as np
import jax
import jax.numpy as jnp
from jax import lax
from jax.experimental import pallas as pl
from jax.experimental.pallas import tpu as pltpu

F32 = jnp.float32
BF16 = jnp.bfloat16
I32 = jnp.int32

D_MODEL = 1024
FOURIER_GROUPS = 4
GROUP_DIM = 128
FOURIER_WIDTH = FOURIER_GROUPS * GROUP_DIM
N_HEADS = 8
N_KV_HEADS = 2
HEAD_DIM = 64
ATTN_WIDTH = N_HEADS * HEAD_DIM
KV_WIDTH = N_KV_HEADS * HEAD_DIM
WINDOW = 128
BLOCK = 128
N_BUCKETS = 32
MAX_DISTANCE = 128
IN_WIDTH = FOURIER_WIDTH + ATTN_WIDTH + 2 * KV_WIDTH
N_EXPERTS = 16
CAPACITY_FACTOR = 2
D_EXPERT = 1024
EPS = 1e-6

LANES = 128
PREFIX_CHUNK = 256
FFT_S2_BLOCK = 16
FFT_ROW_GROUP = 8
STRIP = 32
FFN_CHUNK = 256
VMEM_LIMIT = 56 * 1024 * 1024


U32 = jnp.uint32
HALF = D_MODEL // 2
ROW_TILE = HALF // LANES
HIGH_HALF = np.uint32(0xFFFF0000)


def _params(sem, vmem=None):
    return pltpu.CompilerParams(dimension_semantics=sem, vmem_limit_bytes=vmem)


def _store_rows_as_tiles(ref, val):
    m = val.shape[0]
    bits = lax.bitcast_convert_type(val.astype(BF16).astype(F32), U32)
    words = (bits[:, :HALF] >> 16) | (bits[:, HALF:] & HIGH_HALF)
    for c in range(ROW_TILE):
        ref[pl.ds(c, m, stride=ROW_TILE), :] = words[:, c * LANES:(c + 1) * LANES]


def _unpack_words(words, high):
    bits = (words & HIGH_HALF) if high else (words << 16)
    return lax.bitcast_convert_type(bits, F32).astype(BF16)


def _load_tiles_as_rows(ref):
    m = ref.shape[0] // ROW_TILE
    words = jnp.concatenate(
        [ref[pl.ds(c, m, stride=ROW_TILE), :] for c in range(ROW_TILE)], axis=1)
    return jnp.concatenate([_unpack_words(words, False), _unpack_words(words, True)], axis=1)


def _tiles_of_rows(ref, r, count):
    return ref.at[pl.ds(pl.multiple_of(r * ROW_TILE, ROW_TILE), count * ROW_TILE), :]


def _tile_of_row(ref, r):
    start = r * ROW_TILE
    if not isinstance(r, int):
        start = pl.multiple_of(start, ROW_TILE)
    return ref.at[pl.ds(start, ROW_TILE), :]


def _inproj_body(x_ref, g_ref, w_ref, zf_ref, q_ref, k_ref, v_ref):
    x = x_ref[...]
    ms = jnp.mean(x * x, axis=-1, keepdims=True)
    h = (x * lax.rsqrt(ms + EPS)) * g_ref[...]
    z = jnp.dot(h.astype(BF16), w_ref[...], preferred_element_type=F32)
    o = FOURIER_WIDTH
    g = FFT_ROW_GROUP
    for s1 in range(x.shape[0] // BLOCK):
        for jb in range(BLOCK // g):
            r = s1 * BLOCK + jb * g
            zf_ref[0, jb, s1 * g:(s1 + 1) * g, :] = z[r:r + g, :o]
    q_ref[...] = (z[:, o:o + ATTN_WIDTH] * (HEAD_DIM ** -0.5)).astype(BF16)
    o += ATTN_WIDTH
    k_ref[...] = z[:, o:o + KV_WIDTH].astype(BF16)
    o += KV_WIDTH
    v_ref[...] = z[:, o:o + KV_WIDTH].astype(BF16)


def _inproj(xf, gain, w_in, tm, seq):
    n = xf.shape[0]
    assert tm % BLOCK == 0 and seq % tm == 0
    tiles = seq // tm
    s1_rows = tm // BLOCK * FFT_ROW_GROUP
    nblk = BLOCK // FFT_ROW_GROUP
    row = lambda w: pl.BlockSpec((tm, w), lambda i: (i, 0))
    return pl.pallas_call(
        _inproj_body,
        grid=(n // tm,),
        in_specs=[row(D_MODEL),
                  pl.BlockSpec((1, D_MODEL), lambda i: (0, 0)),
                  pl.BlockSpec((D_MODEL, IN_WIDTH), lambda i: (0, 0))],
        out_specs=[pl.BlockSpec((1, nblk, s1_rows, FOURIER_WIDTH),
                                lambda i: (i // tiles, 0, i % tiles, 0)),
                   row(ATTN_WIDTH), row(KV_WIDTH), row(KV_WIDTH)],
        out_shape=[jax.ShapeDtypeStruct((n // seq, nblk, seq // BLOCK * FFT_ROW_GROUP,
                                         FOURIER_WIDTH), F32),
                   jax.ShapeDtypeStruct((n, ATTN_WIDTH), BF16),
                   jax.ShapeDtypeStruct((n, KV_WIDTH), BF16),
                   jax.ShapeDtypeStruct((n, KV_WIDTH), BF16)],
        compiler_params=_params(("parallel",), VMEM_LIMIT),
        name="inproj",
    )(xf, gain, w_in)


def _dft_tables(s):
    s2 = BLOCK
    s1 = s // s2
    a = np.arange(s1, dtype=np.float64)
    ang1 = 2.0 * np.pi * np.outer(a, a) / s1
    f1 = np.concatenate([np.cos(ang1), -np.sin(ang1)], axis=0)
    b = np.arange(s2, dtype=np.float64)
    ang2 = 2.0 * np.pi * np.outer(b, b) / s2
    angt = 2.0 * np.pi * np.outer(a, b) / s
    c = np.arange(GROUP_DIM, dtype=np.float64)
    angc = 2.0 * np.pi * np.outer(c, c) / GROUP_DIM
    cs = np.concatenate([np.cos(angc), np.sin(angc)], axis=0)
    return dict(
        f1=jnp.asarray(f1, BF16),
        f2r=jnp.asarray(np.cos(ang2), F32), f2i=jnp.asarray(-np.sin(ang2), F32),
        twr=jnp.asarray(np.cos(angt), F32), twi=jnp.asarray(-np.sin(angt), F32),
        cs=jnp.asarray(cs, BF16))


def _dft1_body(f_ref, *refs, nb):
    x_refs, o_ref, stage = refs[:-2], refs[-2], refs[-1]
    g = FFT_ROW_GROUP
    s1 = x_refs[0].shape[2] // g
    per_dot = 4
    for j0 in range(0, nb, per_dot):
        x = jnp.concatenate(
            [ref[0, j // g, pl.ds(j % g, s1, stride=g), :]
             for j in range(j0, j0 + per_dot) for ref in x_refs],
            axis=1).astype(BF16)
        res = jnp.dot(f_ref[...], x, preferred_element_type=F32)
        w = FOURIER_WIDTH
        for jj in range(per_dot):
            stage[0, :, j0 + jj, :] = res[:s1, jj * w:(jj + 1) * w]
            stage[1, :, j0 + jj, :] = res[s1:, jj * w:(jj + 1) * w]
    o_ref[0] = stage[...].astype(BF16)


def _dft1(zf4, f1):
    nb, g = FFT_S2_BLOCK, FFT_ROW_GROUP
    b, s2_groups, rows, w = zf4.shape
    s1, s2 = rows // g, s2_groups * g
    lane_block = lambda k: pl.BlockSpec((1, nb // g, rows, LANES), lambda i, j: (i, j, 0, k))
    nblk = w // LANES
    return pl.pallas_call(
        functools.partial(_dft1_body, nb=nb),
        grid=(b, s2 // nb),
        in_specs=[pl.BlockSpec((2 * s1, s1), lambda i, j: (0, 0))]
                 + [lane_block(k) for k in range(nblk)],
        out_specs=pl.BlockSpec((1, 2, s1, nb, w), lambda i, j: (i, 0, 0, j, 0)),
        out_shape=jax.ShapeDtypeStruct((b, 2, s1, s2, w), BF16),
        scratch_shapes=[pltpu.VMEM((2, s1, nb, w), F32)],
        compiler_params=_params(("parallel", "parallel"), VMEM_LIMIT),
        name="dft1",
    )(f1, *([zf4] * nblk))


def _dft2_body(a_ref, f2r_ref, f2i_ref, twr_ref, twi_ref, cs_ref, wf_ref, o_ref, stage,
               *, kb, scale):
    s2 = BLOCK
    fr = f2r_ref[...]
    fi = f2i_ref[...]
    ys = []
    for j in range(kb):
        tr = twr_ref[0, j:j + 1, :]
        ti = twi_ref[0, j:j + 1, :]
        gr = fr * tr - fi * ti
        gi = fr * ti + fi * tr
        lhs = jnp.concatenate(
            [jnp.concatenate([gr, -gi], axis=1), jnp.concatenate([gi, gr], axis=1)],
            axis=0).astype(BF16)
        rhs = jnp.concatenate([a_ref[0, 0, j], a_ref[0, 1, j]], axis=0)
        ys.append(jnp.dot(lhs, rhs, preferred_element_type=F32))
    cat = jnp.concatenate(
        [jnp.concatenate([y[:s2, g * GROUP_DIM:(g + 1) * GROUP_DIM],
                          y[s2:, g * GROUP_DIM:(g + 1) * GROUP_DIM]], axis=1)
         for g in range(FOURIER_GROUPS) for y in ys], axis=0).astype(BF16)
    re = (jnp.dot(cat, cs_ref[...], preferred_element_type=F32) * scale).astype(BF16)
    rows = kb * s2
    outs = [jnp.dot(re[g * rows:(g + 1) * rows], wf_ref[g], preferred_element_type=F32)
            for g in range(FOURIER_GROUPS)]
    for j in range(kb):
        stage[:, j, :] = jnp.concatenate(
            [o[j * s2:(j + 1) * s2] for o in outs], axis=1)
    o_ref[0] = stage[...].astype(BF16)


def _dft2(a5, tabs, wf, kb, scale):
    b, _, s1, s2, w = a5.shape
    const = lambda shape: pl.BlockSpec(shape, lambda i, j: (0,) * len(shape))
    twr = tabs["twr"].reshape(s1 // kb, kb, s2)
    twi = tabs["twi"].reshape(s1 // kb, kb, s2)
    return pl.pallas_call(
        functools.partial(_dft2_body, kb=kb, scale=scale),
        grid=(b, s1 // kb),
        in_specs=[pl.BlockSpec((1, 2, kb, s2, w), lambda i, j: (i, 0, j, 0, 0)),
                  const((s2, s2)), const((s2, s2)),
                  pl.BlockSpec((1, kb, s2), lambda i, j: (j, 0, 0)),
                  pl.BlockSpec((1, kb, s2), lambda i, j: (j, 0, 0)),
                  const((2 * GROUP_DIM, GROUP_DIM)),
                  const((FOURIER_GROUPS, GROUP_DIM, GROUP_DIM))],
        out_specs=pl.BlockSpec((1, s2, kb, w), lambda i, j: (i, 0, j, 0)),
        out_shape=jax.ShapeDtypeStruct((b, s2, s1, w), BF16),
        scratch_shapes=[pltpu.VMEM((s2, kb, w), F32)],
        compiler_params=_params(("parallel", "parallel"), VMEM_LIMIT),
        name="dft2",
    )(a5, tabs["f2r"], tabs["f2i"], twr, twi, tabs["cs"], wf)


def _fourier(zf, wf_bf16):
    b, s2_blocks, rows, w = zf.shape
    s = s2_blocks * rows
    tabs = _dft_tables(s)
    a = _dft1(zf, tabs["f1"])
    f = _dft2(a, tabs, wf_bf16, kb=16,
              scale=1.0 / math.sqrt(s * GROUP_DIM))
    return f.reshape(b, s, w)


def _bucket_table():
    qi = np.arange(BLOCK)[:, None]
    kj = np.arange(3 * BLOCK)[None, :]
    rel = kj - BLOCK - qi
    nb = N_BUCKETS // 2
    max_exact = nb // 2
    n = np.abs(rel)
    large = max_exact + np.floor(
        np.log(np.maximum(n, 1).astype(np.float64) / max_exact)
        / math.log(MAX_DISTANCE / max_exact) * (nb - max_exact) + 1e-9).astype(np.int64)
    large = np.minimum(large, nb - 1)
    bucket = np.where(rel > 0, nb, 0) + np.where(n < max_exact, n, large)
    return np.where(n <= WINDOW, bucket, -1).astype(np.int32)


def _bias_body(rb_ref, bucket_ref, o_ref):
    bk = bucket_ref[...]
    col = lax.broadcasted_iota(I32, bk.shape, 1)
    for h in range(N_HEADS):
        acc = jnp.full(bk.shape, -jnp.inf, F32)
        for b in range(N_BUCKETS):
            acc = jnp.where(bk == b, rb_ref[b, h], acc)
        o_ref[0, h] = acc
        o_ref[1, h] = jnp.where(col >= BLOCK, acc, -jnp.inf)
        o_ref[2, h] = jnp.where(col < 2 * BLOCK, acc, -jnp.inf)


def _bias_table(rel_bias):
    bucket = jnp.asarray(_bucket_table())
    shape = (3, N_HEADS) + bucket.shape
    return pl.pallas_call(
        _bias_body,
        in_specs=[pl.BlockSpec(memory_space=pltpu.SMEM),
                  pl.BlockSpec(bucket.shape, lambda: (0, 0))],
        out_specs=pl.BlockSpec(shape, lambda: (0, 0, 0, 0)),
        out_shape=jax.ShapeDtypeStruct(shape, F32),
        name="bias_table",
    )(rel_bias.astype(F32), bucket)


def _attn_body(sink_ref, q_ref, kp_ref, kc_ref, kn_ref, vp_ref, vc_ref, vn_ref, bias_ref,
               o_ref, *, tq):
    j = pl.program_id(1)
    last = pl.num_programs(1) - 1
    nsb = tq // BLOCK
    group = N_HEADS // N_KV_HEADS
    gw = group * HEAD_DIM
    kext = jnp.concatenate([kp_ref[0, tq - BLOCK:, :], kc_ref[0], kn_ref[0, :BLOCK, :]], axis=0)
    vext = jnp.concatenate([vp_ref[0, tq - BLOCK:, :], vc_ref[0], vn_ref[0, :BLOCK, :]], axis=0)
    low_half = lax.broadcasted_iota(I32, (1, KV_WIDTH), 1) < HEAD_DIM

    def per_kv_head(ext):
        x = ext.astype(F32)
        xr = pltpu.roll(x, HEAD_DIM, axis=1)
        return (jnp.where(low_half, x, xr).astype(BF16), jnp.where(low_half, xr, x).astype(BF16))

    krep = per_kv_head(kext)
    vrep = per_kv_head(vext)
    head_of_lane = lax.broadcasted_iota(I32, (1, gw), 1) // HEAD_DIM
    nt = (((1,), (1,)), ((), ()))
    def scores(sb, kh):
        rows = slice(sb * BLOCK, (sb + 3) * BLOCK)
        variant = 0
        if sb == 0:
            variant = jnp.where(j == 0, 1, 0)
        elif sb == nsb - 1:
            variant = jnp.where(j == last, 2, 0)
        qblk = q_ref[0, sb * BLOCK:(sb + 1) * BLOCK, kh * gw:(kh + 1) * gw]
        qm = jnp.concatenate(
            [jnp.where(head_of_lane == g, qblk, jnp.zeros_like(qblk)) for g in range(group)],
            axis=0)
        kb = jnp.concatenate([krep[kh][rows], krep[kh][rows]], axis=1)
        s = lax.dot_general(qm, kb, nt, preferred_element_type=F32)
        return s, variant

    def weights(sv, kh):
        s_all, variant = sv
        out, inv = [], []
        for g in range(group):
            for r0 in range(0, BLOCK, STRIP):
                s = s_all[g * BLOCK + r0:g * BLOCK + r0 + STRIP]
                s = s + bias_ref[variant, kh * group + g, r0:r0 + STRIP, :]
                sink = sink_ref[kh * group + g]
                m = jnp.maximum(jnp.max(s, axis=-1, keepdims=True), sink)
                p = jnp.exp(s - m)
                denom = jnp.sum(p, axis=-1, keepdims=True) + jnp.exp(sink - m)
                out.append(p.astype(BF16))
                inv.append(1.0 / denom)
        return jnp.concatenate(out, axis=0), inv

    def values(pw, sb, kh):
        p, inv = pw
        rows = slice(sb * BLOCK, (sb + 3) * BLOCK)
        o4 = jnp.dot(p, vrep[kh][rows], preferred_element_type=F32)
        o4 = jnp.concatenate(
            [o4[k * STRIP:(k + 1) * STRIP] * inv[k] for k in range(len(inv))], axis=0)
        out = []
        for pair in range(group // 2):
            even = o4[(2 * pair) * BLOCK:(2 * pair + 1) * BLOCK]
            odd = o4[(2 * pair + 1) * BLOCK:(2 * pair + 2) * BLOCK]
            out.append(jnp.where(low_half, even, odd))
        return out

    items = [(sb, kh) for sb in range(nsb) for kh in range(N_KV_HEADS)]
    pieces = {}
    s_next = scores(*items[0])
    pn_prev = None
    for i, (sb, kh) in enumerate(items):
        s_cur = s_next
        if i + 1 < len(items):
            s_next = scores(*items[i + 1])
        if pn_prev is not None:
            pieces[items[i - 1]] = values(pn_prev, *items[i - 1])
        pn_prev = weights(s_cur, kh)
    pieces[items[-1]] = values(pn_prev, *items[-1])
    for sb in range(nsb):
        row = [x for kh in range(N_KV_HEADS) for x in pieces[(sb, kh)]]
        o_ref[0, sb * BLOCK:(sb + 1) * BLOCK, :] = jnp.concatenate(row, axis=1).astype(BF16)


def _attention(q, k, v, bias, sink, tq):
    b, s, _ = q.shape
    nt = s // tq
    assert tq >= 2 * BLOCK
    kv = lambda f: pl.BlockSpec((1, tq, KV_WIDTH), f)
    prev = lambda i, j: (i, jnp.maximum(j - 1, 0), 0)
    cur = lambda i, j: (i, j, 0)
    nxt = lambda i, j: (i, jnp.minimum(j + 1, nt - 1), 0)
    return pl.pallas_call(
        functools.partial(_attn_body, tq=tq),
        grid=(b, nt),
        in_specs=[pl.BlockSpec(memory_space=pltpu.SMEM),
                  pl.BlockSpec((1, tq, ATTN_WIDTH), cur),
                  kv(prev), kv(cur), kv(nxt), kv(prev), kv(cur), kv(nxt),
                  pl.BlockSpec(bias.shape, lambda i, j: (0, 0, 0, 0))],
        out_specs=pl.BlockSpec((1, tq, ATTN_WIDTH), cur),
        out_shape=jax.ShapeDtypeStruct((b, s, ATTN_WIDTH), BF16),
        compiler_params=_params(("parallel", "parallel"), VMEM_LIMIT),
        name="attn",
    )(sink.astype(F32), q, k, k, k, v, v, v, bias)


def _outproj_body(x_ref, f_ref, a_ref, wo_ref, g_ref, wr_ref, x1_ref, h2_ref, aff_ref):
    mix = jnp.concatenate([f_ref[...], a_ref[...]], axis=1)
    x1 = x_ref[...] + jnp.dot(mix, wo_ref[...], preferred_element_type=F32)
    x1_ref[...] = x1
    ms = jnp.mean(x1 * x1, axis=-1, keepdims=True)
    h2 = (x1 * lax.rsqrt(ms + EPS)) * g_ref[...]
    _store_rows_as_tiles(h2_ref, h2)
    hi = h2.astype(BF16)
    lo = (h2 - hi.astype(F32)).astype(BF16)
    nt = (((1,), (1,)), ((), ()))
    o1 = lax.dot_general(wr_ref[...], hi, nt, preferred_element_type=F32)
    o2 = lax.dot_general(wr_ref[:N_EXPERTS], lo, nt, preferred_element_type=F32)
    logits = o1[:N_EXPERTS] + (o1[N_EXPERTS:] + o2)
    e = jnp.exp(logits - jnp.max(logits, axis=0, keepdims=True))
    aff_ref[...] = e / jnp.sum(e, axis=0, keepdims=True)


def _outproj(xf, f, a, w_out, gain, w_router_t, tm):
    n = xf.shape[0]
    row = lambda w: pl.BlockSpec((tm, w), lambda i: (i, 0))
    const = lambda shape: pl.BlockSpec(shape, lambda i: (0, 0))
    return pl.pallas_call(
        _outproj_body,
        grid=(n // tm,),
        in_specs=[row(D_MODEL), row(FOURIER_WIDTH), row(ATTN_WIDTH),
                  const((D_MODEL, D_MODEL)), const((1, D_MODEL)),
                  const((2 * N_EXPERTS, D_MODEL))],
        out_specs=[row(D_MODEL),
                   pl.BlockSpec((tm * ROW_TILE, LANES), lambda i: (i, 0)),
                   pl.BlockSpec((N_EXPERTS, tm), lambda i: (0, i))],
        out_shape=[jax.ShapeDtypeStruct((n, D_MODEL), F32),
                   jax.ShapeDtypeStruct((n * ROW_TILE, LANES), U32),
                   jax.ShapeDtypeStruct((N_EXPERTS, n), F32)],
        compiler_params=_params(("parallel",), VMEM_LIMIT),
        name="outproj",
    )(xf, f, a, w_out, gain, w_router_t)


def _route_body(aff_ref, idx_ref, gate_ref, rank_ref, rsel_ref,
                d_a, d_b, n_a, n_b, g_a, g_b, *, n, cap):
    e = N_EXPERTS
    ch = PREFIX_CHUNK

    def search(i, tau):
        cand = tau | jnp.left_shift(jnp.int32(1), 30 - i)
        keys = lax.bitcast_convert_type(aff_ref[...], I32)
        cnt = jnp.sum((keys >= cand).astype(I32), axis=1, keepdims=True)
        return jnp.where(cnt >= cap, cand, tau)

    tau = lax.fori_loop(0, 31, search, jnp.zeros((e, 1), I32))
    keys = lax.bitcast_convert_type(aff_ref[...], I32)
    n_gt = jnp.sum((keys > tau).astype(I32), axis=1, keepdims=True)
    need = (cap - n_gt).astype(F32)

    r = lax.broadcasted_iota(I32, (ch, ch), 0)
    c = lax.broadcasted_iota(I32, (ch, ch), 1)
    upper = (r < c).astype(BF16)

    carry_eq = jnp.zeros((e, 1), F32)
    carry_m = jnp.zeros((e, 1), F32)
    for ci in range(n // ch):
        sl = slice(ci * ch, (ci + 1) * ch)
        k = lax.bitcast_convert_type(aff_ref[:, sl], I32)
        gt = k > tau
        eq = (k == tau).astype(F32)
        eq_ex = jnp.dot(eq.astype(BF16), upper, preferred_element_type=F32) + carry_eq
        carry_eq = carry_eq + jnp.sum(eq, axis=1, keepdims=True)
        m = jnp.where(gt | ((eq > 0.0) & (eq_ex < need)), 1.0, 0.0).astype(F32)
        rank = (jnp.dot(m.astype(BF16), upper, preferred_element_type=F32)
                + carry_m).astype(I32)
        carry_m = carry_m + jnp.sum(m, axis=1, keepdims=True)
        rank_ref[:, sl] = rank
        rsel_ref[:, sl] = jnp.where(m > 0.0, rank, -1)
        lane = lax.broadcasted_iota(I32, (e, ch), 1) + ci * ch
        d_a[:, sl] = jnp.where(m > 0.0, lane - rank, 0)
        n_a[:, sl] = lane
    g_a[...] = aff_ref[...]

    bufs = [(d_a, n_a, g_a), (d_b, n_b, g_b)]
    for b in range(int(math.log2(n))):
        src, dst = bufs[b % 2], bufs[(b + 1) % 2]
        sh = 1 << b
        d = src[0][...]
        moving = (lax.shift_right_logical(d, b) & 1) == 1
        d_in = pltpu.roll(d, n - sh, axis=1)
        arrive = (lax.shift_right_logical(d_in, b) & 1) == 1
        dst[0][...] = jnp.where(arrive, d_in, jnp.where(moving, 0, d))
        for t in (1, 2):
            v = src[t][...]
            dst[t][...] = jnp.where(arrive, pltpu.roll(v, n - sh, axis=1), v)
    fin = bufs[int(math.log2(n)) % 2]
    idx_ref[...] = fin[1][:, :cap]
    gate_ref[...] = fin[2][:, :cap]


def _route(aff_t, cap):
    e, n = aff_t.shape
    assert n & (n - 1) == 0 and n % PREFIX_CHUNK == 0
    full = lambda shape: pl.BlockSpec(shape, lambda: (0, 0))
    big_i = pltpu.VMEM((e, n), I32)
    return pl.pallas_call(
        functools.partial(_route_body, n=n, cap=cap),
        in_specs=[full((e, n))],
        out_specs=[full((e, cap)), full((e, cap)), full((e, n)), full((e, n))],
        out_shape=[jax.ShapeDtypeStruct((e, cap), I32),
                   jax.ShapeDtypeStruct((e, cap), F32),
                   jax.ShapeDtypeStruct((e, n), I32),
                   jax.ShapeDtypeStruct((e, n), I32)],
        scratch_shapes=[big_i, big_i, big_i, big_i,
                        pltpu.VMEM((e, n), F32), pltpu.VMEM((e, n), F32)],
        compiler_params=_params(None, VMEM_LIMIT),
        name="route",
    )(aff_t)


def _rows_to_cols(rows):
    r, m = rows.shape
    if r < 8:
        rows = jnp.concatenate([rows, jnp.zeros((8 - r, m), rows.dtype)], axis=0)
    return rows.T[:, :r]


def _ffn_body(idx_cur, idx_nxt, gate_ref, h_hbm, wg_ref, wu_ref, wd_ref, ye_ref,
              xa, xb, wg_s, wu_s, wd_s, sems, *, tm, spe):
    s = pl.program_id(0)
    last = pl.num_programs(0) - 1
    g_a, g_b = sems.at[0], sems.at[1]

    def gather(idx_ref, t, buf, sem):
        def one(i):
            queue = i % 2 if isinstance(i, int) else 0
            pltpu.async_copy(_tile_of_row(h_hbm, idx_ref[t, 0, i]),
                             _tile_of_row(buf, i), sem, priority=queue)
        return one

    def wait_gather(buf, sem):
        pltpu.make_async_copy(h_hbm.at[pl.ds(0, tm * ROW_TILE), :], buf, sem).wait()

    def compute(xbuf, t, copies):
        nc = D_EXPERT // FFN_CHUNK
        stages = 3 * nc
        todo = [(fn, i) for fn in copies for i in range(tm)]
        per_stage = -(-len(todo) // stages)

        def issue_some():
            for fn, i in todo[:per_stage]:
                fn(i)
            del todo[:per_stage]

        x = _load_tiles_as_rows(xbuf)
        cols = lambda c: slice(c * FFN_CHUNK, (c + 1) * FFN_CHUNK)

        def up(c):
            g = jnp.dot(x, wg_s[:, cols(c)], preferred_element_type=F32)
            issue_some()
            u = jnp.dot(x, wu_s[:, cols(c)], preferred_element_type=F32)
            issue_some()
            return g, u

        def act(gu):
            g, u = gu
            return (g * jax.nn.sigmoid(g) * u).astype(BF16)

        def down(hid, c):
            part = jnp.dot(hid, wd_s[cols(c), :], preferred_element_type=F32)
            issue_some()
            return part

        gu_next = up(0)
        hid_prev = None
        y = None
        for c in range(nc):
            gu_cur = gu_next
            if c + 1 < nc:
                gu_next = up(c + 1)
            if hid_prev is not None:
                part = down(hid_prev, c - 1)
                y = part if y is None else y + part
            hid_prev = act(gu_cur)
        y = y + down(hid_prev, nc - 1)
        assert not todo
        y = y * _rows_to_cols(gate_ref[t])
        _store_rows_as_tiles(ye_ref.at[pl.ds(t * tm * ROW_TILE, tm * ROW_TILE), :], y)

    @pl.when(s % spe == 0)
    def _():
        rows = 256
        for w_ref, w_s in ((wg_ref, wg_s), (wu_ref, wu_s), (wd_ref, wd_s)):
            for r in range(0, w_s.shape[0], rows):
                w_s[r:r + rows, :] = w_ref[0, r:r + rows, :].astype(BF16)

    @pl.when(s == 0)
    def _():
        pl.loop(0, tm)(gather(idx_cur, 0, xa, g_a))

    wait_gather(xa, g_a)
    compute(xa, 0, [gather(idx_cur, 1, xb, g_b)])

    wait_gather(xb, g_b)
    compute(xb, 1, [gather(idx_nxt, 0, xa, g_a)])

    @pl.when(s == last)
    def _():
        wait_gather(xa, g_a)


def _ffn(idx_c, gate_c, h2, wg, wu, wd, tm):
    e, cap = idx_c.shape
    spe = cap // (2 * tm)
    steps = e * spe
    idx3 = idx_c.reshape(2 * steps, 1, tm)
    gate3 = gate_c.reshape(2 * steps, 1, tm)
    cur = lambda i: (i, 0, 0)
    nxt = lambda i: (jnp.minimum(i + 1, steps - 1), 0, 0)
    smem = lambda f: pl.BlockSpec((2, 1, tm), f, memory_space=pltpu.SMEM)
    wspec = lambda a, b: pl.BlockSpec((1, a, b), lambda i: (i // spe, 0, 0))
    buf = pltpu.VMEM((tm * ROW_TILE, LANES), U32)
    return pl.pallas_call(
        functools.partial(_ffn_body, tm=tm, spe=spe),
        grid=(steps,),
        in_specs=[smem(cur), smem(nxt),
                  pl.BlockSpec((2, 1, tm), cur),
                  pl.BlockSpec(memory_space=pl.ANY),
                  wspec(D_MODEL, D_EXPERT), wspec(D_MODEL, D_EXPERT), wspec(D_EXPERT, D_MODEL)],
        out_specs=pl.BlockSpec((2 * tm * ROW_TILE, LANES), lambda i: (i, 0)),
        out_shape=jax.ShapeDtypeStruct((e * cap * ROW_TILE, LANES), U32),
        scratch_shapes=[buf, buf,
                        pltpu.VMEM((D_MODEL, D_EXPERT), BF16), pltpu.VMEM((D_MODEL, D_EXPERT), BF16),
                        pltpu.VMEM((D_EXPERT, D_MODEL), BF16), pltpu.SemaphoreType.DMA((2,))],
        compiler_params=_params(("arbitrary",), VMEM_LIMIT),
        name="ffn",
    )(idx3, idx3, gate3, h2, wg, wu, wd)


def _combine_body(st_ref, x1_ref, rsel_ref, g_ref, ye_hbm, o_ref, win, extra, sems,
                  *, tt, we, cap, n_tiles):
    i = pl.program_id(0)
    last = pl.num_programs(0) - 1
    buf = i % 2
    n_e = N_EXPERTS
    ws = n_e * we

    def run_start(e, tile):
        return st_ref[e * (n_tiles + 1) + tile]

    def clamp(nominal):
        return jnp.minimum(nominal, cap - we)

    def window_rows(e, start):
        return _tiles_of_rows(ye_hbm, e * cap + start, we)

    def first_fetch(tile, b, e):
        start = clamp((run_start(e, tile) // 8) * 8)
        dst = win.at[b, pl.ds(e * we * ROW_TILE, we * ROW_TILE), :]
        return pltpu.make_async_copy(window_rows(e, start), dst, sems.at[b])

    @pl.when(i == 0)
    def _():
        for e in range(n_e):
            first_fetch(0, 0, e).start()

    for e in range(n_e):
        first_fetch(i, buf, e).wait()

    @pl.when(i < last)
    def _():
        for e in range(n_e):
            first_fetch(i + 1, 1 - buf, e).start()

    rank_of = jnp.concatenate([_rows_to_cols(rsel_ref[0:8, :]),
                               _rows_to_cols(rsel_ref[8:n_e, :])], axis=1)
    lane = lax.broadcasted_iota(I32, (tt, we), 1)

    def selector(e, start, nominal):
        hit = (rank_of[:, e:e + 1] - start) == lane
        if nominal is not None:
            hit = hit & (lane >= nominal - start)
        return jnp.where(hit, 1.0, 0.0).astype(BF16)

    starts = [clamp((run_start(e, i) // 8) * 8) for e in range(n_e)]
    overflow = run_start(0, i + 1) - starts[0] > we
    for e in range(1, n_e):
        overflow = overflow | (run_start(e, i + 1) - starts[e] > we)

    def finish(x2_cols):
        ssq = sum(jnp.sum(x * x, axis=-1, keepdims=True) for x in x2_cols)
        scale = lax.rsqrt(ssq * (1.0 / D_MODEL) + EPS)
        for c, x in enumerate(x2_cols):
            w = x.shape[1]
            o_ref[:, c * w:(c + 1) * w] = (x * scale) * g_ref[:, c * w:(c + 1) * w]

    first = win.at[buf]
    sel = jnp.concatenate([selector(e, starts[e], None) for e in range(n_e)], axis=1)
    cw = 2 * LANES

    def window_words(b):
        return jnp.concatenate(
            [first[pl.ds(2 * b + k, ws, stride=ROW_TILE), :] for k in range(2)], axis=1)

    nblk = HALF // cw
    x2_cols = [None] * (2 * nblk)
    words_next = window_words(0)
    for b in range(nblk):
        words = words_next
        if b + 1 < nblk:
            words_next = window_words(b + 1)
        for high in (False, True):
            c = b + nblk * high
            x2_cols[c] = (x1_ref[:, c * cw:(c + 1) * cw]
                          + jnp.dot(sel, _unpack_words(words, high), preferred_element_type=F32))
    finish(x2_cols)

    @pl.when(overflow)
    def _():
        moe = jnp.zeros((tt, D_MODEL), F32)
        for e in range(n_e):
            base = (run_start(e, i) // 8) * 8
            nwin = (run_start(e, i + 1) - base + we - 1) // we

            def window(w, acc, e=e, base=base):
                nominal = base + w * we
                start = clamp(nominal)
                cp = pltpu.make_async_copy(window_rows(e, start), extra, sems.at[2])
                cp.start()
                cp.wait()
                return acc + jnp.dot(selector(e, start, nominal), _load_tiles_as_rows(extra),
                                     preferred_element_type=F32)

            moe = lax.fori_loop(0, nwin, window, moe)
        finish([x1_ref[...] + moe])


def _combine(x1, ye, rank, rsel, gain, tt, we):
    n = x1.shape[0]
    n_e = rank.shape[0]
    cap = ye.shape[0] // ROW_TILE // n_e
    n_tiles = n // tt
    assert cap >= we and cap % 8 == 0
    run_starts = jnp.concatenate([rank[:, ::tt], jnp.full((n_e, 1), cap, I32)], axis=1)
    return pl.pallas_call(
        functools.partial(_combine_body, tt=tt, we=we, cap=cap, n_tiles=n_tiles),
        grid_spec=pltpu.PrefetchScalarGridSpec(
            num_scalar_prefetch=1,
            grid=(n_tiles,),
            in_specs=[pl.BlockSpec((tt, D_MODEL), lambda i, s: (i, 0)),
                      pl.BlockSpec((n_e, tt), lambda i, s: (0, i)),
                      pl.BlockSpec((1, D_MODEL), lambda i, s: (0, 0)),
                      pl.BlockSpec(memory_space=pl.ANY)],
            out_specs=pl.BlockSpec((tt, D_MODEL), lambda i, s: (i, 0)),
            scratch_shapes=[pltpu.VMEM((2, n_e * we * ROW_TILE, LANES), U32),
                            pltpu.VMEM((we * ROW_TILE, LANES), U32),
                            pltpu.SemaphoreType.DMA((3,))]),
        out_shape=jax.ShapeDtypeStruct((n, D_MODEL), F32),
        compiler_params=_params(("arbitrary",), VMEM_LIMIT),
        name="combine",
    )(run_starts.reshape(-1), x1, rsel, gain, ye)


def _tile(n, pref):
    t = pref
    while n % t:
        t //= 2
    return t


def _trunk(x, p):
    b, s, d = x.shape
    n = b * s
    cap = CAPACITY_FACTOR * n // N_EXPERTS
    xf = x.reshape(n, d)
    tm = _tile(n, 1024)
    zf, q, k, v = _inproj(xf, p["norm_mix"], p["w_in"], _tile(s, 1024), s)
    f = _fourier(zf, p["w_fourier"])
    a = _attention(q.reshape(b, s, ATTN_WIDTH), k.reshape(b, s, KV_WIDTH),
                   v.reshape(b, s, KV_WIDTH), p["bias"], p["sink"], _tile(s, 1024))
    x1, h2, aff_t = _outproj(xf, f.reshape(n, FOURIER_WIDTH), a.reshape(n, ATTN_WIDTH),
                             p["w_out"], p["norm_ffn"], p["w_router"], tm)
    idx_c, gate_c, rank, rsel = _route(aff_t, cap)
    ye = _ffn(idx_c, gate_c, h2, p["w_gate"], p["w_up"], p["w_down"], _tile(cap // 2, 512))
    tt = _tile(n, 512)
    we = max(LANES, 2 * CAPACITY_FACTOR * tt // N_EXPERTS)
    y = _combine(x1, ye, rank, rsel, p["norm_final"], tt, we)
    return y.reshape(b, s, d)


def kernel(x_prompt, x_sample, w_in, w_fourier, attn_sink, rel_bias, w_out, norm_mix, norm_ffn,
           w_router, w_gate, w_up, w_down, norm_final):
    assert w_in.shape[0] == 1
    wr = w_router[0].T
    wr_hi = wr.astype(BF16)
    wr_lo = (wr - wr_hi.astype(F32)).astype(BF16)
    p = dict(
        norm_mix=norm_mix[0].reshape(1, D_MODEL),
        norm_ffn=norm_ffn[0].reshape(1, D_MODEL),
        norm_final=norm_final.reshape(1, D_MODEL),
        w_in=w_in[0].astype(BF16),
        w_fourier=w_fourier[0].astype(BF16),
        w_out=w_out[0].astype(BF16),
        w_router=jnp.concatenate([wr_hi, wr_lo], axis=0),
        w_gate=w_gate[0], w_up=w_up[0], w_down=w_down[0],
        sink=attn_sink[0],
        bias=_bias_table(rel_bias),
    )
    return _trunk(x_prompt, p), _trunk(x_sample, p)
```

```python
import functools
import math

import numpy as np
import jax
import jax.numpy as jnp
from jax import lax
from jax.experimental import pallas as pl
from jax.experimental.pallas import tpu as pltpu

F32 = jnp.float32
BF16 = jnp.bfloat16
I32 = jnp.int32

D_MODEL = 1024
FOURIER_GROUPS = 4
GROUP_DIM = 128
FOURIER_WIDTH = FOURIER_GROUPS * GROUP_DIM
N_HEADS = 8
N_KV_HEADS = 2
HEAD_DIM = 64
ATTN_WIDTH = N_HEADS * HEAD_DIM
KV_WIDTH = N_KV_HEADS * HEAD_DIM
WINDOW = 128
BLOCK = 128
N_BUCKETS = 32
MAX_DISTANCE = 128
IN_WIDTH = FOURIER_WIDTH + ATTN_WIDTH + 2 * KV_WIDTH
N_EXPERTS = 16
CAPACITY_FACTOR = 2
D_EXPERT = 1024
EPS = 1e-6

LANES = 128
PREFIX_CHUNK = 256
PREFIX_GROUP = 8
FFT_S2_BLOCK = 16
FFT_ROW_GROUP = 8
STRIP = 32
FFN_CHUNK = 256
VMEM_LIMIT = 56 * 1024 * 1024


U32 = jnp.uint32
HALF = D_MODEL // 2
ROW_TILE = HALF // LANES
HIGH_HALF = np.uint32(0xFFFF0000)


def _params(sem, vmem=None):
    return pltpu.CompilerParams(dimension_semantics=sem, vmem_limit_bytes=vmem)


def _store_rows_as_tiles(ref, val):
    m = val.shape[0]
    bits = lax.bitcast_convert_type(val.astype(BF16).astype(F32), U32)
    words = (bits[:, :HALF] >> 16) | (bits[:, HALF:] & HIGH_HALF)
    for c in range(ROW_TILE):
        ref[pl.ds(c, m, stride=ROW_TILE), :] = words[:, c * LANES:(c + 1) * LANES]


def _unpack_words(words, high):
    bits = (words & HIGH_HALF) if high else (words << 16)
    return lax.bitcast_convert_type(bits, F32).astype(BF16)


def _load_tiles_as_rows(ref):
    m = ref.shape[0] // ROW_TILE
    words = jnp.concatenate(
        [ref[pl.ds(c, m, stride=ROW_TILE), :] for c in range(ROW_TILE)], axis=1)
    return jnp.concatenate([_unpack_words(words, False), _unpack_words(words, True)], axis=1)


def _tiles_of_rows(ref, r, count):
    return ref.at[pl.ds(pl.multiple_of(r * ROW_TILE, ROW_TILE), count * ROW_TILE), :]


def _tile_of_row(ref, r):
    start = r * ROW_TILE
    if not isinstance(r, int):
        start = pl.multiple_of(start, ROW_TILE)
    return ref.at[pl.ds(start, ROW_TILE), :]


def _inproj_body(x_ref, g_ref, w_ref, zf_ref, q_ref, k_ref, v_ref):
    x = x_ref[...]
    ms = jnp.mean(x * x, axis=-1, keepdims=True)
    h = (x * lax.rsqrt(ms + EPS)) * g_ref[...]
    z = jnp.dot(h.astype(BF16), w_ref[...], preferred_element_type=F32)
    o = FOURIER_WIDTH
    g = FFT_ROW_GROUP
    for s1 in range(x.shape[0] // BLOCK):
        for jb in range(BLOCK // g):
            r = s1 * BLOCK + jb * g
            zf_ref[0, jb, s1 * g:(s1 + 1) * g, :] = z[r:r + g, :o]
    q_ref[...] = (z[:, o:o + ATTN_WIDTH] * (HEAD_DIM ** -0.5)).astype(BF16)
    o += ATTN_WIDTH
    k_ref[...] = z[:, o:o + KV_WIDTH].astype(BF16)
    o += KV_WIDTH
    v_ref[...] = z[:, o:o + KV_WIDTH].astype(BF16)


def _inproj(xf, gain, w_in, tm, seq):
    n = xf.shape[0]
    assert tm % BLOCK == 0 and seq % tm == 0
    tiles = seq // tm
    s1_rows = tm // BLOCK * FFT_ROW_GROUP
    nblk = BLOCK // FFT_ROW_GROUP
    row = lambda w: pl.BlockSpec((tm, w), lambda i: (i, 0))
    return pl.pallas_call(
        _inproj_body,
        grid=(n // tm,),
        in_specs=[row(D_MODEL),
                  pl.BlockSpec((1, D_MODEL), lambda i: (0, 0)),
                  pl.BlockSpec((D_MODEL, IN_WIDTH), lambda i: (0, 0))],
        out_specs=[pl.BlockSpec((1, nblk, s1_rows, FOURIER_WIDTH),
                                lambda i: (i // tiles, 0, i % tiles, 0)),
                   row(ATTN_WIDTH), row(KV_WIDTH), row(KV_WIDTH)],
        out_shape=[jax.ShapeDtypeStruct((n // seq, nblk, seq // BLOCK * FFT_ROW_GROUP,
                                         FOURIER_WIDTH), F32),
                   jax.ShapeDtypeStruct((n, ATTN_WIDTH), BF16),
                   jax.ShapeDtypeStruct((n, KV_WIDTH), BF16),
                   jax.ShapeDtypeStruct((n, KV_WIDTH), BF16)],
        compiler_params=_params(("parallel",), VMEM_LIMIT),
        name="inproj",
    )(xf, gain, w_in)


def _dft_tables(s):
    s2 = BLOCK
    s1 = s // s2
    a = np.arange(s1, dtype=np.float64)
    ang1 = 2.0 * np.pi * np.outer(a, a) / s1
    f1 = np.concatenate([np.cos(ang1), -np.sin(ang1)], axis=0)
    b = np.arange(s2, dtype=np.float64)
    ang2 = 2.0 * np.pi * np.outer(b, b) / s2
    angt = 2.0 * np.pi * np.outer(a, b) / s
    c = np.arange(GROUP_DIM, dtype=np.float64)
    angc = 2.0 * np.pi * np.outer(c, c) / GROUP_DIM
    cs = np.concatenate([np.cos(angc), np.sin(angc)], axis=0)
    return dict(
        f1=jnp.asarray(f1, BF16),
        f2r=jnp.asarray(np.cos(ang2), F32), f2i=jnp.asarray(-np.sin(ang2), F32),
        twr=jnp.asarray(np.cos(angt), F32), twi=jnp.asarray(-np.sin(angt), F32),
        cs=jnp.asarray(cs, BF16))


def _dft1_body(f_ref, *refs, nb):
    x_refs, o_ref, stage = refs[:-2], refs[-2], refs[-1]
    g = FFT_ROW_GROUP
    s1 = x_refs[0].shape[2] // g
    per_dot = 4
    for j0 in range(0, nb, per_dot):
        x = jnp.concatenate(
            [ref[0, j // g, pl.ds(j % g, s1, stride=g), :]
             for j in range(j0, j0 + per_dot) for ref in x_refs],
            axis=1).astype(BF16)
        res = jnp.dot(f_ref[...], x, preferred_element_type=F32)
        w = FOURIER_WIDTH
        for jj in range(per_dot):
            stage[0, :, j0 + jj, :] = res[:s1, jj * w:(jj + 1) * w]
            stage[1, :, j0 + jj, :] = res[s1:, jj * w:(jj + 1) * w]
    o_ref[0] = stage[...].astype(BF16)


def _dft1(zf4, f1):
    nb, g = FFT_S2_BLOCK, FFT_ROW_GROUP
    b, s2_groups, rows, w = zf4.shape
    s1, s2 = rows // g, s2_groups * g
    lane_block = lambda k: pl.BlockSpec((1, nb // g, rows, LANES), lambda i, j: (i, j, 0, k))
    nblk = w // LANES
    return pl.pallas_call(
        functools.partial(_dft1_body, nb=nb),
        grid=(b, s2 // nb),
        in_specs=[pl.BlockSpec((2 * s1, s1), lambda i, j: (0, 0))]
                 + [lane_block(k) for k in range(nblk)],
        out_specs=pl.BlockSpec((1, 2, s1, nb, w), lambda i, j: (i, 0, 0, j, 0)),
        out_shape=jax.ShapeDtypeStruct((b, 2, s1, s2, w), BF16),
        scratch_shapes=[pltpu.VMEM((2, s1, nb, w), F32)],
        compiler_params=_params(("parallel", "parallel"), VMEM_LIMIT),
        name="dft1",
    )(f1, *([zf4] * nblk))


def _dft2_body(a_ref, f2r_ref, f2i_ref, twr_ref, twi_ref, cs_ref, wf_ref, o_ref, stage,
               *, kb, scale):
    s2 = BLOCK
    fr = f2r_ref[...]
    fi = f2i_ref[...]
    ys = []
    for j in range(kb):
        tr = twr_ref[0, j:j + 1, :]
        ti = twi_ref[0, j:j + 1, :]
        gr = fr * tr - fi * ti
        gi = fr * ti + fi * tr
        lhs = jnp.concatenate(
            [jnp.concatenate([gr, -gi], axis=1), jnp.concatenate([gi, gr], axis=1)],
            axis=0).astype(BF16)
        rhs = jnp.concatenate([a_ref[0, 0, j], a_ref[0, 1, j]], axis=0)
        ys.append(jnp.dot(lhs, rhs, preferred_element_type=F32))
    cat = jnp.concatenate(
        [jnp.concatenate([y[:s2, g * GROUP_DIM:(g + 1) * GROUP_DIM],
                          y[s2:, g * GROUP_DIM:(g + 1) * GROUP_DIM]], axis=1)
         for g in range(FOURIER_GROUPS) for y in ys], axis=0).astype(BF16)
    re = (jnp.dot(cat, cs_ref[...], preferred_element_type=F32) * scale).astype(BF16)
    rows = kb * s2
    outs = [jnp.dot(re[g * rows:(g + 1) * rows], wf_ref[g], preferred_element_type=F32)
            for g in range(FOURIER_GROUPS)]
    for j in range(kb):
        stage[:, j, :] = jnp.concatenate(
            [o[j * s2:(j + 1) * s2] for o in outs], axis=1)
    o_ref[0] = stage[...].astype(BF16)


def _dft2(a5, tabs, wf, kb, scale):
    b, _, s1, s2, w = a5.shape
    const = lambda shape: pl.BlockSpec(shape, lambda i, j: (0,) * len(shape))
    twr = tabs["twr"].reshape(s1 // kb, kb, s2)
    twi = tabs["twi"].reshape(s1 // kb, kb, s2)
    return pl.pallas_call(
        functools.partial(_dft2_body, kb=kb, scale=scale),
        grid=(b, s1 // kb),
        in_specs=[pl.BlockSpec((1, 2, kb, s2, w), lambda i, j: (i, 0, j, 0, 0)),
                  const((s2, s2)), const((s2, s2)),
                  pl.BlockSpec((1, kb, s2), lambda i, j: (j, 0, 0)),
                  pl.BlockSpec((1, kb, s2), lambda i, j: (j, 0, 0)),
                  const((2 * GROUP_DIM, GROUP_DIM)),
                  const((FOURIER_GROUPS, GROUP_DIM, GROUP_DIM))],
        out_specs=pl.BlockSpec((1, s2, kb, w), lambda i, j: (i, 0, j, 0)),
        out_shape=jax.ShapeDtypeStruct((b, s2, s1, w), BF16),
        scratch_shapes=[pltpu.VMEM((s2, kb, w), F32)],
        compiler_params=_params(("parallel", "parallel"), VMEM_LIMIT),
        name="dft2",
    )(a5, tabs["f2r"], tabs["f2i"], twr, twi, tabs["cs"], wf)


def _fourier(zf, wf_bf16):
    b, s2_blocks, rows, w = zf.shape
    s = s2_blocks * rows
    tabs = _dft_tables(s)
    a = _dft1(zf, tabs["f1"])
    f = _dft2(a, tabs, wf_bf16, kb=16,
              scale=1.0 / math.sqrt(s * GROUP_DIM))
    return f.reshape(b, s, w)


def _bucket_table():
    qi = np.arange(BLOCK)[:, None]
    kj = np.arange(3 * BLOCK)[None, :]
    rel = kj - BLOCK - qi
    nb = N_BUCKETS // 2
    max_exact = nb // 2
    n = np.abs(rel)
    large = max_exact + np.floor(
        np.log(np.maximum(n, 1).astype(np.float64) / max_exact)
        / math.log(MAX_DISTANCE / max_exact) * (nb - max_exact) + 1e-9).astype(np.int64)
    large = np.minimum(large, nb - 1)
    bucket = np.where(rel > 0, nb, 0) + np.where(n < max_exact, n, large)
    return np.where(n <= WINDOW, bucket, -1).astype(np.int32)


def _bias_body(rb_ref, bucket_ref, o_ref):
    bk = bucket_ref[...]
    col = lax.broadcasted_iota(I32, bk.shape, 1)
    for h in range(N_HEADS):
        acc = jnp.full(bk.shape, -jnp.inf, F32)
        for b in range(N_BUCKETS):
            acc = jnp.where(bk == b, rb_ref[b, h], acc)
        o_ref[0, h] = acc
        o_ref[1, h] = jnp.where(col >= BLOCK, acc, -jnp.inf)
        o_ref[2, h] = jnp.where(col < 2 * BLOCK, acc, -jnp.inf)


def _bias_table(rel_bias):
    bucket = jnp.asarray(_bucket_table())
    shape = (3, N_HEADS) + bucket.shape
    return pl.pallas_call(
        _bias_body,
        in_specs=[pl.BlockSpec(memory_space=pltpu.SMEM),
                  pl.BlockSpec(bucket.shape, lambda: (0, 0))],
        out_specs=pl.BlockSpec(shape, lambda: (0, 0, 0, 0)),
        out_shape=jax.ShapeDtypeStruct(shape, F32),
        name="bias_table",
    )(rel_bias.astype(F32), bucket)


def _attn_body(sink_ref, q_ref, kp_ref, kc_ref, kn_ref, vp_ref, vc_ref, vn_ref, bias_ref,
               o_ref, *, tq):
    j = pl.program_id(1)
    last = pl.num_programs(1) - 1
    nsb = tq // BLOCK
    group = N_HEADS // N_KV_HEADS
    gw = group * HEAD_DIM
    kext = jnp.concatenate([kp_ref[0, tq - BLOCK:, :], kc_ref[0], kn_ref[0, :BLOCK, :]], axis=0)
    vext = jnp.concatenate([vp_ref[0, tq - BLOCK:, :], vc_ref[0], vn_ref[0, :BLOCK, :]], axis=0)
    low_half = lax.broadcasted_iota(I32, (1, KV_WIDTH), 1) < HEAD_DIM

    def per_kv_head(ext):
        x = ext.astype(F32)
        xr = pltpu.roll(x, HEAD_DIM, axis=1)
        return (jnp.where(low_half, x, xr).astype(BF16), jnp.where(low_half, xr, x).astype(BF16))

    krep = per_kv_head(kext)
    vrep = per_kv_head(vext)
    head_of_lane = lax.broadcasted_iota(I32, (1, gw), 1) // HEAD_DIM
    nt = (((1,), (1,)), ((), ()))
    def scores(sb, kh):
        rows = slice(sb * BLOCK, (sb + 3) * BLOCK)
        variant = 0
        if sb == 0:
            variant = jnp.where(j == 0, 1, 0)
        elif sb == nsb - 1:
            variant = jnp.where(j == last, 2, 0)
        qblk = q_ref[0, sb * BLOCK:(sb + 1) * BLOCK, kh * gw:(kh + 1) * gw]
        qm = jnp.concatenate(
            [jnp.where(head_of_lane == g, qblk, jnp.zeros_like(qblk)) for g in range(group)],
            axis=0)
        kb = jnp.concatenate([krep[kh][rows], krep[kh][rows]], axis=1)
        s = lax.dot_general(qm, kb, nt, preferred_element_type=F32)
        return s, variant

    def weights(sv, kh):
        s_all, variant = sv
        out, inv = [], []
        for g in range(group):
            for r0 in range(0, BLOCK, STRIP):
                s = s_all[g * BLOCK + r0:g * BLOCK + r0 + STRIP]
                s = s + bias_ref[variant, kh * group + g, r0:r0 + STRIP, :]
                sink = sink_ref[kh * group + g]
                m = jnp.maximum(jnp.max(s, axis=-1, keepdims=True), sink)
                p = jnp.exp(s - m)
                denom = jnp.sum(p, axis=-1, keepdims=True) + jnp.exp(sink - m)
                out.append(p.astype(BF16))
                inv.append(1.0 / denom)
        return jnp.concatenate(out, axis=0), inv

    def values(pw, sb, kh):
        p, inv = pw
        rows = slice(sb * BLOCK, (sb + 3) * BLOCK)
        o4 = jnp.dot(p, vrep[kh][rows], preferred_element_type=F32)
        o4 = jnp.concatenate(
            [o4[k * STRIP:(k + 1) * STRIP] * inv[k] for k in range(len(inv))], axis=0)
        out = []
        for pair in range(group // 2):
            even = o4[(2 * pair) * BLOCK:(2 * pair + 1) * BLOCK]
            odd = o4[(2 * pair + 1) * BLOCK:(2 * pair + 2) * BLOCK]
            out.append(jnp.where(low_half, even, odd))
        return out

    items = [(sb, kh) for sb in range(nsb) for kh in range(N_KV_HEADS)]
    pieces = {}
    s_next = scores(*items[0])
    pn_prev = None
    for i, (sb, kh) in enumerate(items):
        s_cur = s_next
        if i + 1 < len(items):
            s_next = scores(*items[i + 1])
        if pn_prev is not None:
            pieces[items[i - 1]] = values(pn_prev, *items[i - 1])
        pn_prev = weights(s_cur, kh)
    pieces[items[-1]] = values(pn_prev, *items[-1])
    for sb in range(nsb):
        row = [x for kh in range(N_KV_HEADS) for x in pieces[(sb, kh)]]
        o_ref[0, sb * BLOCK:(sb + 1) * BLOCK, :] = jnp.concatenate(row, axis=1).astype(BF16)


def _attention(q, k, v, bias, sink, tq):
    b, s, _ = q.shape
    nt = s // tq
    assert tq >= 2 * BLOCK
    kv = lambda f: pl.BlockSpec((1, tq, KV_WIDTH), f)
    prev = lambda i, j: (i, jnp.maximum(j - 1, 0), 0)
    cur = lambda i, j: (i, j, 0)
    nxt = lambda i, j: (i, jnp.minimum(j + 1, nt - 1), 0)
    return pl.pallas_call(
        functools.partial(_attn_body, tq=tq),
        grid=(b, nt),
        in_specs=[pl.BlockSpec(memory_space=pltpu.SMEM),
                  pl.BlockSpec((1, tq, ATTN_WIDTH), cur),
                  kv(prev), kv(cur), kv(nxt), kv(prev), kv(cur), kv(nxt),
                  pl.BlockSpec(bias.shape, lambda i, j: (0, 0, 0, 0))],
        out_specs=pl.BlockSpec((1, tq, ATTN_WIDTH), cur),
        out_shape=jax.ShapeDtypeStruct((b, s, ATTN_WIDTH), BF16),
        compiler_params=_params(("parallel", "parallel"), VMEM_LIMIT),
        name="attn",
    )(sink.astype(F32), q, k, k, k, v, v, v, bias)


def _outproj_body(x_ref, f_ref, a_ref, wo_ref, g_ref, wr_ref, x1_ref, h2_ref, aff_ref):
    mix = jnp.concatenate([f_ref[...], a_ref[...]], axis=1)
    x1 = x_ref[...] + jnp.dot(mix, wo_ref[...], preferred_element_type=F32)
    x1_ref[...] = x1
    ms = jnp.mean(x1 * x1, axis=-1, keepdims=True)
    h2 = (x1 * lax.rsqrt(ms + EPS)) * g_ref[...]
    _store_rows_as_tiles(h2_ref, h2)
    hi = h2.astype(BF16)
    lo = (h2 - hi.astype(F32)).astype(BF16)
    nt = (((1,), (1,)), ((), ()))
    o1 = lax.dot_general(wr_ref[...], hi, nt, preferred_element_type=F32)
    o2 = lax.dot_general(wr_ref[:N_EXPERTS], lo, nt, preferred_element_type=F32)
    logits = o1[:N_EXPERTS] + (o1[N_EXPERTS:] + o2)
    e = jnp.exp(logits - jnp.max(logits, axis=0, keepdims=True))
    aff_ref[...] = e / jnp.sum(e, axis=0, keepdims=True)


def _outproj(xf, f, a, w_out, gain, w_router_t, tm):
    n = xf.shape[0]
    row = lambda w: pl.BlockSpec((tm, w), lambda i: (i, 0))
    const = lambda shape: pl.BlockSpec(shape, lambda i: (0, 0))
    return pl.pallas_call(
        _outproj_body,
        grid=(n // tm,),
        in_specs=[row(D_MODEL), row(FOURIER_WIDTH), row(ATTN_WIDTH),
                  const((D_MODEL, D_MODEL)), const((1, D_MODEL)),
                  const((2 * N_EXPERTS, D_MODEL))],
        out_specs=[row(D_MODEL),
                   pl.BlockSpec((tm * ROW_TILE, LANES), lambda i: (i, 0)),
                   pl.BlockSpec((N_EXPERTS, tm), lambda i: (0, i))],
        out_shape=[jax.ShapeDtypeStruct((n, D_MODEL), F32),
                   jax.ShapeDtypeStruct((n * ROW_TILE, LANES), U32),
                   jax.ShapeDtypeStruct((N_EXPERTS, n), F32)],
        compiler_params=_params(("parallel",), VMEM_LIMIT),
        name="outproj",
    )(xf, f, a, w_out, gain, w_router_t)


def _route_body(aff_ref, idx_ref, gate_ref, rank_ref, rsel_ref,
                d_a, d_b, n_a, n_b, g_a, g_b, *, n, cap):
    e = N_EXPERTS
    ch = PREFIX_CHUNK

    def search(i, tau):
        cand = tau | jnp.left_shift(jnp.int32(1), 30 - i)
        keys = lax.bitcast_convert_type(aff_ref[...], I32)
        cnt = jnp.sum((keys >= cand).astype(I32), axis=1, keepdims=True)
        return jnp.where(cnt >= cap, cand, tau)

    tau = lax.fori_loop(0, 31, search, jnp.zeros((e, 1), I32))
    keys = lax.bitcast_convert_type(aff_ref[...], I32)
    n_gt = jnp.sum((keys > tau).astype(I32), axis=1, keepdims=True)
    need = (cap - n_gt).astype(F32)

    r = lax.broadcasted_iota(I32, (ch, ch), 0)
    c = lax.broadcasted_iota(I32, (ch, ch), 1)
    upper = (r < c).astype(BF16)

    carry_eq = jnp.zeros((e, 1), F32)
    carry_m = jnp.zeros((e, 1), F32)
    grp = PREFIX_GROUP
    assert (n // ch) % grp == 0

    def group_prefix(vals, carry):
        pre = jnp.dot(jnp.concatenate(vals, axis=0).astype(BF16), upper,
                      preferred_element_type=F32)
        out = []
        for g, v in enumerate(vals):
            out.append(pre[g * e:(g + 1) * e] + carry)
            carry = carry + jnp.sum(v, axis=1, keepdims=True)
        return out, carry

    for c0 in range(0, n // ch, grp):
        sls = [slice((c0 + g) * ch, (c0 + g + 1) * ch) for g in range(grp)]
        ks = [lax.bitcast_convert_type(aff_ref[:, sl], I32) for sl in sls]
        eqs = [(k == tau).astype(F32) for k in ks]
        eq_exs, carry_eq = group_prefix(eqs, carry_eq)
        ms = [jnp.where((k > tau) | ((eq > 0.0) & (eq_ex < need)), 1.0, 0.0).astype(F32)
              for k, eq, eq_ex in zip(ks, eqs, eq_exs)]
        ranks, carry_m = group_prefix(ms, carry_m)
        for g, sl in enumerate(sls):
            m, rank = ms[g], ranks[g].astype(I32)
            rank_ref[:, sl] = rank
            rsel_ref[:, sl] = jnp.where(m > 0.0, rank, -1)
            lane = lax.broadcasted_iota(I32, (e, ch), 1) + (c0 + g) * ch
            d_a[:, sl] = jnp.where(m > 0.0, lane - rank, 0)
            n_a[:, sl] = lane
    g_a[...] = aff_ref[...]

    bufs = [(d_a, n_a, g_a), (d_b, n_b, g_b)]
    for b in range(int(math.log2(n))):
        src, dst = bufs[b % 2], bufs[(b + 1) % 2]
        sh = 1 << b
        d = src[0][...]
        moving = (lax.shift_right_logical(d, b) & 1) == 1
        d_in = pltpu.roll(d, n - sh, axis=1)
        arrive = (lax.shift_right_logical(d_in, b) & 1) == 1
        dst[0][...] = jnp.where(arrive, d_in, jnp.where(moving, 0, d))
        for t in (1, 2):
            v = src[t][...]
            dst[t][...] = jnp.where(arrive, pltpu.roll(v, n - sh, axis=1), v)
    fin = bufs[int(math.log2(n)) % 2]
    idx_ref[...] = fin[1][:, :cap]
    gate_ref[...] = fin[2][:, :cap]


def _route(aff_t, cap):
    e, n = aff_t.shape
    assert n & (n - 1) == 0 and n % PREFIX_CHUNK == 0
    full = lambda shape: pl.BlockSpec(shape, lambda: (0, 0))
    big_i = pltpu.VMEM((e, n), I32)
    return pl.pallas_call(
        functools.partial(_route_body, n=n, cap=cap),
        in_specs=[full((e, n))],
        out_specs=[full((e, cap)), full((e, cap)), full((e, n)), full((e, n))],
        out_shape=[jax.ShapeDtypeStruct((e, cap), I32),
                   jax.ShapeDtypeStruct((e, cap), F32),
                   jax.ShapeDtypeStruct((e, n), I32),
                   jax.ShapeDtypeStruct((e, n), I32)],
        scratch_shapes=[big_i, big_i, big_i, big_i,
                        pltpu.VMEM((e, n), F32), pltpu.VMEM((e, n), F32)],
        compiler_params=_params(None, VMEM_LIMIT),
        name="route",
    )(aff_t)


def _rows_to_cols(rows):
    r, m = rows.shape
    if r < 8:
        rows = jnp.concatenate([rows, jnp.zeros((8 - r, m), rows.dtype)], axis=0)
    return rows.T[:, :r]


def _ffn_body(idx_cur, idx_nxt, gate_ref, h_hbm, wg_ref, wu_ref, wd_ref, ye_ref,
              xa, xb, wg_s, wu_s, wd_s, sems, *, tm, spe):
    s = pl.program_id(0)
    last = pl.num_programs(0) - 1
    g_a, g_b = sems.at[0], sems.at[1]

    def gather(idx_ref, t, buf, sem):
        def one(i):
            queue = i % 2 if isinstance(i, int) else 0
            pltpu.async_copy(_tile_of_row(h_hbm, idx_ref[t, 0, i]),
                             _tile_of_row(buf, i), sem, priority=queue)
        return one

    def wait_gather(buf, sem):
        pltpu.make_async_copy(h_hbm.at[pl.ds(0, tm * ROW_TILE), :], buf, sem).wait()

    def compute(xbuf, t, copies):
        nc = D_EXPERT // FFN_CHUNK
        stages = 3 * nc
        todo = [(fn, i) for fn in copies for i in range(tm)]
        per_stage = -(-len(todo) // stages)

        def issue_some():
            for fn, i in todo[:per_stage]:
                fn(i)
            del todo[:per_stage]

        x = _load_tiles_as_rows(xbuf)
        cols = lambda c: slice(c * FFN_CHUNK, (c + 1) * FFN_CHUNK)

        def up(c):
            g = jnp.dot(x, wg_s[:, cols(c)], preferred_element_type=F32)
            issue_some()
            u = jnp.dot(x, wu_s[:, cols(c)], preferred_element_type=F32)
            issue_some()
            return g, u

        def act(gu):
            g, u = gu
            return (g * jax.nn.sigmoid(g) * u).astype(BF16)

        def down(hid, c):
            part = jnp.dot(hid, wd_s[cols(c), :], preferred_element_type=F32)
            issue_some()
            return part

        gu_next = up(0)
        hid_prev = None
        y = None
        for c in range(nc):
            gu_cur = gu_next
            if c + 1 < nc:
                gu_next = up(c + 1)
            if hid_prev is not None:
                part = down(hid_prev, c - 1)
                y = part if y is None else y + part
            hid_prev = act(gu_cur)
        y = y + down(hid_prev, nc - 1)
        assert not todo
        y = y * _rows_to_cols(gate_ref[t])
        _store_rows_as_tiles(ye_ref.at[pl.ds(t * tm * ROW_TILE, tm * ROW_TILE), :], y)

    @pl.when(s % spe == 0)
    def _():
        rows = 256
        for w_ref, w_s in ((wg_ref, wg_s), (wu_ref, wu_s), (wd_ref, wd_s)):
            for r in range(0, w_s.shape[0], rows):
                w_s[r:r + rows, :] = w_ref[0, r:r + rows, :].astype(BF16)

    @pl.when(s == 0)
    def _():
        pl.loop(0, tm)(gather(idx_cur, 0, xa, g_a))

    wait_gather(xa, g_a)
    compute(xa, 0, [gather(idx_cur, 1, xb, g_b)])

    wait_gather(xb, g_b)
    compute(xb, 1, [gather(idx_nxt, 0, xa, g_a)])

    @pl.when(s == last)
    def _():
        wait_gather(xa, g_a)


def _ffn(idx_c, gate_c, h2, wg, wu, wd, tm):
    e, cap = idx_c.shape
    spe = cap // (2 * tm)
    steps = e * spe
    idx3 = idx_c.reshape(2 * steps, 1, tm)
    gate3 = gate_c.reshape(2 * steps, 1, tm)
    cur = lambda i: (i, 0, 0)
    nxt = lambda i: (jnp.minimum(i + 1, steps - 1), 0, 0)
    smem = lambda f: pl.BlockSpec((2, 1, tm), f, memory_space=pltpu.SMEM)
    wspec = lambda a, b: pl.BlockSpec((1, a, b), lambda i: (i // spe, 0, 0))
    buf = pltpu.VMEM((tm * ROW_TILE, LANES), U32)
    return pl.pallas_call(
        functools.partial(_ffn_body, tm=tm, spe=spe),
        grid=(steps,),
        in_specs=[smem(cur), smem(nxt),
                  pl.BlockSpec((2, 1, tm), cur),
                  pl.BlockSpec(memory_space=pl.ANY),
                  wspec(D_MODEL, D_EXPERT), wspec(D_MODEL, D_EXPERT), wspec(D_EXPERT, D_MODEL)],
        out_specs=pl.BlockSpec((2 * tm * ROW_TILE, LANES), lambda i: (i, 0)),
        out_shape=jax.ShapeDtypeStruct((e * cap * ROW_TILE, LANES), U32),
        scratch_shapes=[buf, buf,
                        pltpu.VMEM((D_MODEL, D_EXPERT), BF16), pltpu.VMEM((D_MODEL, D_EXPERT), BF16),
                        pltpu.VMEM((D_EXPERT, D_MODEL), BF16), pltpu.SemaphoreType.DMA((2,))],
        compiler_params=_params(("arbitrary",), VMEM_LIMIT),
        name="ffn",
    )(idx3, idx3, gate3, h2, wg, wu, wd)


def _combine_body(st_ref, x1_ref, rsel_ref, g_ref, ye_hbm, o_ref, win, extra, sems,
                  *, tt, we, cap, n_tiles):
    i = pl.program_id(0)
    last = pl.num_programs(0) - 1
    buf = i % 2
    n_e = N_EXPERTS
    ws = n_e * we

    def run_start(e, tile):
        return st_ref[e * (n_tiles + 1) + tile]

    def clamp(nominal):
        return jnp.minimum(nominal, cap - we)

    def window_rows(e, start):
        return _tiles_of_rows(ye_hbm, e * cap + start, we)

    def first_fetch(tile, b, e):
        start = clamp((run_start(e, tile) // 8) * 8)
        dst = win.at[b, pl.ds(e * we * ROW_TILE, we * ROW_TILE), :]
        return pltpu.make_async_copy(window_rows(e, start), dst, sems.at[b])

    @pl.when(i == 0)
    def _():
        for e in range(n_e):
            first_fetch(0, 0, e).start()

    for e in range(n_e):
        first_fetch(i, buf, e).wait()

    @pl.when(i < last)
    def _():
        for e in range(n_e):
            first_fetch(i + 1, 1 - buf, e).start()

    rank_of = jnp.concatenate([_rows_to_cols(rsel_ref[0:8, :]),
                               _rows_to_cols(rsel_ref[8:n_e, :])], axis=1)
    lane = lax.broadcasted_iota(I32, (tt, we), 1)

    def selector(e, start, nominal):
        hit = (rank_of[:, e:e + 1] - start) == lane
        if nominal is not None:
            hit = hit & (lane >= nominal - start)
        return jnp.where(hit, 1.0, 0.0).astype(BF16)

    starts = [clamp((run_start(e, i) // 8) * 8) for e in range(n_e)]
    overflow = run_start(0, i + 1) - starts[0] > we
    for e in range(1, n_e):
        overflow = overflow | (run_start(e, i + 1) - starts[e] > we)

    def finish(x2_cols):
        ssq = sum(jnp.sum(x * x, axis=-1, keepdims=True) for x in x2_cols)
        scale = lax.rsqrt(ssq * (1.0 / D_MODEL) + EPS)
        for c, x in enumerate(x2_cols):
            w = x.shape[1]
            o_ref[:, c * w:(c + 1) * w] = (x * scale) * g_ref[:, c * w:(c + 1) * w]

    first = win.at[buf]
    sel = jnp.concatenate([selector(e, starts[e], None) for e in range(n_e)], axis=1)
    cw = 2 * LANES

    def window_words(b):
        return jnp.concatenate(
            [first[pl.ds(2 * b + k, ws, stride=ROW_TILE), :] for k in range(2)], axis=1)

    nblk = HALF // cw
    x2_cols = [None] * (2 * nblk)
    words_next = window_words(0)
    for b in range(nblk):
        words = words_next
        if b + 1 < nblk:
            words_next = window_words(b + 1)
        for high in (False, True):
            c = b + nblk * high
            x2_cols[c] = (x1_ref[:, c * cw:(c + 1) * cw]
                          + jnp.dot(sel, _unpack_words(words, high), preferred_element_type=F32))
    finish(x2_cols)

    @pl.when(overflow)
    def _():
        moe = jnp.zeros((tt, D_MODEL), F32)
        for e in range(n_e):
            base = (run_start(e, i) // 8) * 8
            nwin = (run_start(e, i + 1) - base + we - 1) // we

            def window(w, acc, e=e, base=base):
                nominal = base + w * we
                start = clamp(nominal)
                cp = pltpu.make_async_copy(window_rows(e, start), extra, sems.at[2])
                cp.start()
                cp.wait()
                return acc + jnp.dot(selector(e, start, nominal), _load_tiles_as_rows(extra),
                                     preferred_element_type=F32)

            moe = lax.fori_loop(0, nwin, window, moe)
        finish([x1_ref[...] + moe])


def _combine(x1, ye, rank, rsel, gain, tt, we):
    n = x1.shape[0]
    n_e = rank.shape[0]
    cap = ye.shape[0] // ROW_TILE // n_e
    n_tiles = n // tt
    assert cap >= we and cap % 8 == 0
    run_starts = jnp.concatenate([rank[:, ::tt], jnp.full((n_e, 1), cap, I32)], axis=1)
    return pl.pallas_call(
        functools.partial(_combine_body, tt=tt, we=we, cap=cap, n_tiles=n_tiles),
        grid_spec=pltpu.PrefetchScalarGridSpec(
            num_scalar_prefetch=1,
            grid=(n_tiles,),
            in_specs=[pl.BlockSpec((tt, D_MODEL), lambda i, s: (i, 0)),
                      pl.BlockSpec((n_e, tt), lambda i, s: (0, i)),
                      pl.BlockSpec((1, D_MODEL), lambda i, s: (0, 0)),
                      pl.BlockSpec(memory_space=pl.ANY)],
            out_specs=pl.BlockSpec((tt, D_MODEL), lambda i, s: (i, 0)),
            scratch_shapes=[pltpu.VMEM((2, n_e * we * ROW_TILE, LANES), U32),
                            pltpu.VMEM((we * ROW_TILE, LANES), U32),
                            pltpu.SemaphoreType.DMA((3,))]),
        out_shape=jax.ShapeDtypeStruct((n, D_MODEL), F32),
        compiler_params=_params(("arbitrary",), VMEM_LIMIT),
        name="combine",
    )(run_starts.reshape(-1), x1, rsel, gain, ye)


def _tile(n, pref):
    t = pref
    while n % t:
        t //= 2
    return t


def _trunk(x, p):
    b, s, d = x.shape
    n = b * s
    cap = CAPACITY_FACTOR * n // N_EXPERTS
    xf = x.reshape(n, d)
    tm = _tile(n, 1024)
    zf, q, k, v = _inproj(xf, p["norm_mix"], p["w_in"], _tile(s, 1024), s)
    f = _fourier(zf, p["w_fourier"])
    a = _attention(q.reshape(b, s, ATTN_WIDTH), k.reshape(b, s, KV_WIDTH),
                   v.reshape(b, s, KV_WIDTH), p["bias"], p["sink"], _tile(s, 1024))
    x1, h2, aff_t = _outproj(xf, f.reshape(n, FOURIER_WIDTH), a.reshape(n, ATTN_WIDTH),
                             p["w_out"], p["norm_ffn"], p["w_router"], tm)
    idx_c, gate_c, rank, rsel = _route(aff_t, cap)
    ye = _ffn(idx_c, gate_c, h2, p["w_gate"], p["w_up"], p["w_down"], _tile(cap // 2, 512))
    tt = _tile(n, 512)
    we = max(LANES, 2 * CAPACITY_FACTOR * tt // N_EXPERTS)
    y = _combine(x1, ye, rank, rsel, p["norm_final"], tt, we)
    return y.reshape(b, s, d)


def kernel(x_prompt, x_sample, w_in, w_fourier, attn_sink, rel_bias, w_out, norm_mix, norm_ffn,
           w_router, w_gate, w_up, w_down, norm_final):
    assert w_in.shape[0] == 1
    wr = w_router[0].T
    wr_hi = wr.astype(BF16)
    wr_lo = (wr - wr_hi.astype(F32)).astype(BF16)
    p = dict(
        norm_mix=norm_mix[0].reshape(1, D_MODEL),
        norm_ffn=norm_ffn[0].reshape(1, D_MODEL),
        norm_final=norm_final.reshape(1, D_MODEL),
        w_in=w_in[0].astype(BF16),
        w_fourier=w_fourier[0].astype(BF16),
        w_out=w_out[0].astype(BF16),
        w_router=jnp.concatenate([wr_hi, wr_lo], axis=0),
        w_gate=w_gate[0], w_up=w_up[0], w_down=w_down[0],
        sink=attn_sink[0],
        bias=_bias_table(rel_bias),
    )
    return _trunk(x_prompt, p), _trunk(x_sample, p)
```
